```python
import math
import jax, jax.numpy as jnp
from jax import lax
import numpy as np

D_MODEL = 1024
BATCH = 2
SEQ = 8192
DEPTH = 1

ATTN_WIDTH = D_MODEL // 2
SSM_WIDTH = D_MODEL - ATTN_WIDTH
N_ATTN_HEADS = 4
ATTN_HEAD_DIM = ATTN_WIDTH // (2 * N_ATTN_HEADS)
V_HEAD_DIM = 2 * ATTN_HEAD_DIM
Q_BLOCK = 128
SSM_GROUP = 16
N_SSM_GROUPS = SSM_WIDTH // SSM_GROUP
SSM_STATE = 64
IN_WIDTH = 3 * ATTN_WIDTH + SSM_WIDTH
N_EXPERTS = 256
TOP_K = 8
N_EXPERT_GROUPS = 8
TOPK_GROUPS = 4
EXPERT_DIM = D_MODEL // 4
SHARED_DIM = EXPERT_DIM
ROUTED_SCALE = 2.5
EXPERT_BLOCK = 128
NORM_EPS = 1e-6
SUBLN_EPS = 1e-5
N_ADA = 6

kernel_name = "hybrid_diffattn_s5_moe_adaln"


def rms_norm(x, g, eps):
    xf = x.astype(jnp.float32)
    y = xf * lax.rsqrt(jnp.mean(xf * xf, axis=-1, keepdims=True) + eps)
    return (y * g.astype(jnp.float32)).astype(x.dtype)


def diff_attention(q, k, v, lam, lam_init, subln_g):
    bsz, seq = q.shape[0], q.shape[1]
    nb = seq // Q_BLOCK
    qb = (q * (ATTN_HEAD_DIM ** -0.5)).reshape(
        bsz, nb, Q_BLOCK, N_ATTN_HEADS, 2, ATTN_HEAD_DIM).swapaxes(0, 1)
    kpos = jnp.arange(seq)

    def block(args):
        qi, i = args
        s = jnp.einsum('bqhtd,bkhtd->bhtqk', qi, k, preferred_element_type=jnp.float32)
        qpos = i * Q_BLOCK + jnp.arange(Q_BLOCK)
        s = jnp.where(kpos[None, :] <= qpos[:, None], s, -jnp.inf)
        p = jax.nn.softmax(s, axis=-1)
        a = p[:, :, 0] - lam * p[:, :, 1]
        return jnp.einsum('bhqk,bkhe->bqhe', a.astype(v.dtype), v)

    o = lax.map(block, (qb, jnp.arange(nb)))
    o = o.swapaxes(0, 1).reshape(bsz, seq, N_ATTN_HEADS, V_HEAD_DIM)
    o = rms_norm(o, subln_g, SUBLN_EPS) * (1.0 - lam_init)
    return o.reshape(bsz, seq, ATTN_WIDTH)


def s5_mixer(u, a_re, a_im, log_step, b_re, b_im, c_re, c_im, d_skip, w_glu, b_glu):
    bsz, seq, _ = u.shape
    f32 = jnp.float32
    uf = u.astype(f32)
    ug = uf.reshape(bsz, seq, N_SSM_GROUPS, SSM_GROUP)
    lam = lax.complex(jnp.minimum(a_re.astype(f32), -1e-4), a_im.astype(f32))
    delta = jnp.exp(log_step.astype(f32))[:, None]
    lam_bar = jnp.exp(lam * delta)
    b_bar = ((lam_bar - 1.0) / lam)[:, :, None] * lax.complex(b_re.astype(f32), b_im.astype(f32))
    bu = jnp.einsum('blgc,gpc->blgp', ug.astype(jnp.complex64), b_bar)
    a = jnp.broadcast_to(lam_bar, bu.shape)

    def combine(e1, e2):
        a1, b1 = e1
        a2, b2 = e2
        return a1 * a2, a2 * b1 + b2

    _, states = lax.associative_scan(combine, (a, bu), axis=1)
    cmat = lax.complex(c_re.astype(f32), c_im.astype(f32))
    y = jnp.einsum('gcp,blgp->blgc', cmat, states).real.reshape(bsz, seq, SSM_WIDTH)
    y = y + d_skip.astype(f32) * uf
    g = jax.nn.gelu(y)
    out = g * jax.nn.sigmoid(g @ w_glu.astype(f32) + b_glu.astype(f32))
    return out.astype(u.dtype)


def moe(h, w_router, router_bias, w_gate, w_up, w_down, ws_gate, ws_up, ws_down):
    bsz, seq, dm = h.shape
    t = h.reshape(-1, dm)
    n_tok = t.shape[0]
    f32 = jnp.float32
    scores = jax.nn.sigmoid(jnp.dot(t, w_router, preferred_element_type=f32))
    sel = scores + router_bias.astype(f32)
    group_score = lax.top_k(sel.reshape(n_tok, N_EXPERT_GROUPS, -1), 2)[0].sum(-1)
    _, gidx = lax.top_k(group_score, TOPK_GROUPS)
    gmask = jnp.any(gidx[:, :, None] == jnp.arange(N_EXPERT_GROUPS)[None, None, :], axis=1)
    emask = jnp.repeat(gmask, N_EXPERTS // N_EXPERT_GROUPS, axis=1)
    _, idx = lax.top_k(jnp.where(emask, sel, -jnp.inf), TOP_K)
    w = jnp.take_along_axis(scores, idx, axis=-1)
    w = w / jnp.sum(w, axis=-1, keepdims=True) * ROUTED_SCALE

    nk = n_tok * TOP_K
    flat_e = idx.reshape(-1)
    flat_tok = jnp.arange(nk, dtype=jnp.int32) // TOP_K
    flat_w = w.reshape(-1)
    order = jnp.argsort(flat_e)
    se = flat_e[order]
    counts = jnp.bincount(flat_e, length=N_EXPERTS)
    pcounts = (counts + EXPERT_BLOCK - 1) // EXPERT_BLOCK * EXPERT_BLOCK
    starts = jnp.cumsum(counts) - counts
    pends = jnp.cumsum(pcounts)
    pstarts = pends - pcounts
    dest = pstarts[se] + jnp.arange(nk) - starts[se]
    n_blocks = -(-(nk + N_EXPERTS * (EXPERT_BLOCK - 1)) // EXPERT_BLOCK)
    n_rows = n_blocks * EXPERT_BLOCK
    buf_tok = jnp.zeros((n_rows,), jnp.int32).at[dest].set(flat_tok[order])
    buf_w = jnp.zeros((n_rows,), f32).at[dest].set(flat_w[order])
    blk_e = jnp.minimum(jnp.searchsorted(pends, jnp.arange(n_blocks) * EXPERT_BLOCK, side='right'),
                        N_EXPERTS - 1)

    def expert_block(args):
        tok, e = args
        xb = t[tok]
        hb = jax.nn.silu(xb @ w_gate[e]) * (xb @ w_up[e])
        return hb @ w_down[e]

    out = lax.map(expert_block, (buf_tok.reshape(n_blocks, EXPERT_BLOCK), blk_e))
    out = out.reshape(n_rows, dm) * buf_w[:, None].astype(out.dtype)
    routed = jax.ops.segment_sum(out, buf_tok, num_segments=n_tok)
    shared = (jax.nn.silu(t @ ws_gate) * (t @ ws_up)) @ ws_down
    return (routed + shared).reshape(bsz, seq, dm)


def setup_inputs(seed: int = 0) -> dict:
    key = jax.random.key(seed)
    ks = jax.random.split(key, 32)
    f32 = jnp.float32
    nrm = lambda k, shape, s: (jax.random.normal(k, shape, f32) * s)
    L_ = DEPTH
    a_im = jnp.broadcast_to(jnp.pi * jnp.arange(SSM_STATE, dtype=f32), (L_, N_SSM_GROUPS, SSM_STATE))
    return {
        "x": nrm(ks[0], (BATCH, SEQ, D_MODEL), 1.0),
        "c": nrm(ks[1], (BATCH, D_MODEL), 1.0),
        "w_ada": nrm(ks[2], (L_, D_MODEL, N_ADA * D_MODEL), 0.5 * D_MODEL ** -0.5),
        "b_ada": nrm(ks[3], (L_, N_ADA * D_MODEL), 0.02),
        "norm1_g": 1.0 + nrm(ks[4], (L_, D_MODEL), 0.02),
        "w_in": nrm(ks[5], (L_, D_MODEL, IN_WIDTH), D_MODEL ** -0.5),
        "w_out": nrm(ks[6], (L_, D_MODEL, D_MODEL), D_MODEL ** -0.5),
        "lambda_q1": nrm(ks[7], (L_, ATTN_HEAD_DIM), 0.1),
        "lambda_k1": nrm(ks[8], (L_, ATTN_HEAD_DIM), 0.1),
        "lambda_q2": nrm(ks[9], (L_, ATTN_HEAD_DIM), 0.1),
        "lambda_k2": nrm(ks[10], (L_, ATTN_HEAD_DIM), 0.1),
        "subln_g": 1.0 + nrm(ks[11], (L_, V_HEAD_DIM), 0.02),
        "ssm_a_re": -0.5 + nrm(ks[12], (L_, N_SSM_GROUPS, SSM_STATE), 0.01),
        "ssm_a_im": a_im + nrm(ks[13], (L_, N_SSM_GROUPS, SSM_STATE), 0.01),
        "ssm_log_step": jax.random.uniform(ks[14], (L_, N_SSM_GROUPS), f32, math.log(1e-3), math.log(1e-1)),
        "ssm_b_re": nrm(ks[15], (L_, N_SSM_GROUPS, SSM_STATE, SSM_GROUP), (2 * SSM_GROUP) ** -0.5),
        "ssm_b_im": nrm(ks[16], (L_, N_SSM_GROUPS, SSM_STATE, SSM_GROUP), (2 * SSM_GROUP) ** -0.5),
        "ssm_c_re": nrm(ks[17], (L_, N_SSM_GROUPS, SSM_GROUP, SSM_STATE), (2 * SSM_STATE) ** -0.5),
        "ssm_c_im": nrm(ks[18], (L_, N_SSM_GROUPS, SSM_GROUP, SSM_STATE), (2 * SSM_STATE) ** -0.5),
        "ssm_d": nrm(ks[19], (L_, SSM_WIDTH), 1.0),
        "w_glu": nrm(ks[20], (L_, SSM_WIDTH, SSM_WIDTH), SSM_WIDTH ** -0.5),
        "b_glu": nrm(ks[21], (L_, SSM_WIDTH), 0.02),
        "norm2_g": 1.0 + nrm(ks[22], (L_, D_MODEL), 0.02),
        "w_router": nrm(ks[23], (L_, D_MODEL, N_EXPERTS), D_MODEL ** -0.5),
        "router_bias": nrm(ks[24], (L_, N_EXPERTS), 0.01),
        "w_gate": nrm(ks[25], (L_, N_EXPERTS, D_MODEL, EXPERT_DIM), D_MODEL ** -0.5),
        "w_up": nrm(ks[26], (L_, N_EXPERTS, D_MODEL, EXPERT_DIM), D_MODEL ** -0.5),
        "w_down": nrm(ks[27], (L_, N_EXPERTS, EXPERT_DIM, D_MODEL), EXPERT_DIM ** -0.5),
        "ws_gate": nrm(ks[28], (L_, D_MODEL, SHARED_DIM), D_MODEL ** -0.5),
        "ws_up": nrm(ks[29], (L_, D_MODEL, SHARED_DIM), D_MODEL ** -0.5),
        "ws_down": nrm(ks[30], (L_, SHARED_DIM, D_MODEL), SHARED_DIM ** -0.5),
        "final_g": 1.0 + nrm(ks[31], (D_MODEL,), 0.02),
    }


def reference(x, c, w_ada, b_ada, norm1_g, w_in, w_out, lambda_q1, lambda_k1, lambda_q2,
              lambda_k2, subln_g, ssm_a_re, ssm_a_im, ssm_log_step, ssm_b_re, ssm_b_im,
              ssm_c_re, ssm_c_im, ssm_d, w_glu, b_glu, norm2_g, w_router, router_bias,
              w_gate, w_up, w_down, ws_gate, ws_up, ws_down, final_g):
    bsz, seq, _ = x.shape
    f32 = jnp.float32
    cond = jax.nn.silu(c)
    for l in range(DEPTH):
        mod = (cond @ w_ada[l] + b_ada[l])[:, None, :]
        shift1, scale1, gate1, shift2, scale2, gate2 = jnp.split(mod, N_ADA, axis=-1)

        h = rms_norm(x, norm1_g[l], NORM_EPS) * (1.0 + scale1) + shift1
        proj = h @ w_in[l]
        q, k, v, u = jnp.split(proj, [ATTN_WIDTH, 2 * ATTN_WIDTH, 3 * ATTN_WIDTH], axis=-1)
        q = q.reshape(bsz, seq, N_ATTN_HEADS, 2, ATTN_HEAD_DIM)
        k = k.reshape(bsz, seq, N_ATTN_HEADS, 2, ATTN_HEAD_DIM)
        v = v.reshape(bsz, seq, N_ATTN_HEADS, V_HEAD_DIM)
        lam_init = 0.8 - 0.6 * math.exp(-0.3 * l)
        lam = (jnp.exp(jnp.sum(lambda_q1[l].astype(f32) * lambda_k1[l].astype(f32)))
               - jnp.exp(jnp.sum(lambda_q2[l].astype(f32) * lambda_k2[l].astype(f32))) + lam_init)
        attn_out = diff_attention(q, k, v, lam, lam_init, subln_g[l])
        ssm_out = s5_mixer(u, ssm_a_re[l], ssm_a_im[l], ssm_log_step[l], ssm_b_re[l], ssm_b_im[l],
                           ssm_c_re[l], ssm_c_im[l], ssm_d[l], w_glu[l], b_glu[l])
        mix = jnp.concatenate([attn_out, ssm_out], axis=-1) @ w_out[l]
        x = x + gate1 * mix

        h2 = rms_norm(x, norm2_g[l], NORM_EPS) * (1.0 + scale2) + shift2
        x = x + gate2 * moe(h2, w_router[l], router_bias[l], w_gate[l], w_up[l], w_down[l],
                            ws_gate[l], ws_up[l], ws_down[l])
    return rms_norm(x, final_g, NORM_EPS)
```

```python
import functools
import math

import jax
import jax.numpy as jnp
from jax import lax
from jax.experimental import pallas as pl
from jax.experimental.pallas import tpu as pltpu

F32 = jnp.float32
BF16 = jnp.bfloat16

N_ATTN_HEADS = 4
ATTN_HEAD_DIM = 64
V_HEAD_DIM = 128
SSM_GROUP = 16
N_SSM_GROUPS = 32
SSM_STATE = 64
N_EXPERTS = 256
TOP_K = 8
N_EXPERT_GROUPS = 8
TOPK_GROUPS = 4
ROUTED_SCALE = 2.5
NORM_EPS = 1e-6
SUBLN_EPS = 1e-5
N_ADA = 6
LAM_INIT = 0.8 - 0.6 * math.exp(-0.3 * 0)

LANES = 128
MOE_BLOCK = 256
NEG_BIG = -1e30
VMEM_LIMIT = 48 * 1024 * 1024


def _split_bf16(a):
    hi = a.astype(BF16)
    lo = (a - hi.astype(F32)).astype(BF16)
    return hi, lo


def _dot(a, b):
    return jnp.dot(a, b, preferred_element_type=F32)


def _dot_nt(a, b):
    return lax.dot_general(a, b, (((1,), (1,)), ((), ())), preferred_element_type=F32)


def _dot3(a, b):
    ah, al = _split_bf16(a)
    bh, bl = _split_bf16(b)
    return _dot(ah, bh) + _dot(ah, bl) + _dot(al, bh)


def _dot3_nt(a, b):
    ah, al = _split_bf16(a)
    bh, bl = _split_bf16(b)
    return _dot_nt(ah, bh) + _dot_nt(ah, bl) + _dot_nt(al, bh)


def _silu(x):
    return x * jax.nn.sigmoid(x)


def _gelu_tanh(x):
    c = math.sqrt(2.0 / math.pi)
    return 0.5 * x * (1.0 + jnp.tanh(c * (x + 0.044715 * (x * x * x))))


def _mod_kernel(c_ref, w_ref, b_ref, o_ref):
    cond = _silu(c_ref[...])
    o_ref[...] = _dot3(cond, w_ref[...]) + b_ref[...]


def _mod_call(c, w_ada, b_ada):
    bsz, dm = c.shape
    n_out = w_ada.shape[1]
    tn = 1024
    return pl.pallas_call(
        _mod_kernel,
        grid=(n_out // tn,),
        in_specs=[
            pl.BlockSpec((bsz, dm), lambda j: (0, 0)),
            pl.BlockSpec((dm, tn), lambda j: (0, j)),
            pl.BlockSpec((1, tn), lambda j: (0, j)),
        ],
        out_specs=pl.BlockSpec((bsz, tn), lambda j: (0, j)),
        out_shape=jax.ShapeDtypeStruct((bsz, n_out), F32),
        compiler_params=pltpu.CompilerParams(vmem_limit_bytes=VMEM_LIMIT),
        name="mod",
    )(c, w_ada, b_ada.reshape(1, n_out))


def _inproj_kernel(x_ref, mod_ref, g_ref, wqkv_ref, wut_ref, q_ref, k_ref, v_ref, ut_ref, *, aw):
    x = x_ref[...]
    ms = jnp.mean(x * x, axis=-1, keepdims=True)
    h = x * lax.rsqrt(ms + NORM_EPS) * g_ref[...]
    h = h * (1.0 + mod_ref[1:2, :]) + mod_ref[0:1, :]
    hb = h.astype(BF16)
    qkv = _dot(hb, wqkv_ref[...])
    q_ref[...] = (qkv[:, :aw] * (ATTN_HEAD_DIM ** -0.5)).astype(BF16)
    k_ref[...] = qkv[:, aw:2 * aw].astype(BF16)
    v_ref[...] = qkv[:, 2 * aw:].astype(BF16)
    ut_ref[...] = _dot_nt(wut_ref[...], hb)


def _inproj_call(x, mod3, norm_g, wqkv, wut, tm=512):
    bsz, seq, dm = x.shape
    aw = wqkv.shape[1] // 3
    sw = wut.shape[0]
    row = lambda b, i: (b, i, 0)
    return pl.pallas_call(
        functools.partial(_inproj_kernel, aw=aw),
        grid=(bsz, seq // tm),
        in_specs=[
            pl.BlockSpec((None, tm, dm), row),
            pl.BlockSpec((None, N_ADA, dm), lambda b, i: (b, 0, 0)),
            pl.BlockSpec((1, dm), lambda b, i: (0, 0)),
            pl.BlockSpec(wqkv.shape, lambda b, i: (0, 0)),
            pl.BlockSpec(wut.shape, lambda b, i: (0, 0)),
        ],
        out_specs=[
            pl.BlockSpec((None, tm, aw), row),
            pl.BlockSpec((None, tm, aw), row),
            pl.BlockSpec((None, tm, aw), row),
            pl.BlockSpec((None, sw, tm), lambda b, i: (b, 0, i)),
        ],
        out_shape=[
            jax.ShapeDtypeStruct((bsz, seq, aw), BF16),
            jax.ShapeDtypeStruct((bsz, seq, aw), BF16),
            jax.ShapeDtypeStruct((bsz, seq, aw), BF16),
            jax.ShapeDtypeStruct((bsz, sw, seq), F32),
        ],
        compiler_params=pltpu.CompilerParams(
            dimension_semantics=("parallel", "parallel"), vmem_limit_bytes=VMEM_LIMIT),
        name="inproj",
    )(x, mod3, norm_g.reshape(1, dm), wqkv, wut)


def _attn_kernel(lam_ref, q_ref, k_ref, v_ref, g_ref, o_ref, m_sc, l_sc, acc_sc, *, tq):
    i = pl.program_id(2)
    q = q_ref[...]
    lane = lax.broadcasted_iota(jnp.int32, q.shape, 1)
    zero = jnp.zeros_like(q)
    q2 = jnp.concatenate([jnp.where(lane < ATTN_HEAD_DIM, q, zero),
                          jnp.where(lane >= ATTN_HEAD_DIM, q, zero)], axis=0)
    m_sc[...] = jnp.full(m_sc.shape, NEG_BIG, F32)
    l_sc[...] = jnp.zeros(l_sc.shape, F32)
    acc_sc[...] = jnp.zeros(acc_sc.shape, F32)

    def step(j, masked):
        start = pl.multiple_of(j * tq, tq)
        kt = k_ref[pl.ds(start, tq), :]
        vt = v_ref[pl.ds(start, tq), :]
        s = _dot_nt(q2, kt)
        if masked:
            r = lax.broadcasted_iota(jnp.int32, s.shape, 0)
            c = lax.broadcasted_iota(jnp.int32, s.shape, 1)
            s = jnp.where(c <= (r & (tq - 1)), s, NEG_BIG)
        m_prev = m_sc[...]
        m_new = jnp.maximum(m_prev, jnp.max(s, axis=-1, keepdims=True))
        alpha = jnp.exp(m_prev - m_new)
        p = jnp.exp(s - pltpu.repeat(m_new, tq // LANES, axis=1))
        l_sc[...] = alpha * l_sc[...] + jnp.sum(p, axis=-1, keepdims=True)
        acc_sc[...] = alpha * acc_sc[...] + _dot(p.astype(BF16), vt)
        m_sc[...] = m_new

    def body(j, carry):
        step(j, False)
        return carry

    lax.fori_loop(0, i, body, 0)
    step(i, True)

    o_all = acc_sc[...] / l_sc[...]
    o = o_all[:tq] - lam_ref[0] * o_all[tq:]
    ms = jnp.mean(o * o, axis=-1, keepdims=True)
    o = o * lax.rsqrt(ms + SUBLN_EPS) * g_ref[...] * (1.0 - LAM_INIT)
    o_ref[...] = o.astype(o_ref.dtype)


def _attn_call(lam, q, k, v, subln_g, tq=512):
    bsz, seq, aw = q.shape
    nh = aw // V_HEAD_DIM
    qmap = lambda b, h, i: (b, i, h)
    kvmap = lambda b, h, i: (b, 0, h)
    return pl.pallas_call(
        functools.partial(_attn_kernel, tq=tq),
        grid=(bsz, nh, seq // tq),
        in_specs=[
            pl.BlockSpec(memory_space=pltpu.SMEM),
            pl.BlockSpec((None, tq, V_HEAD_DIM), qmap),
            pl.BlockSpec((None, seq, V_HEAD_DIM), kvmap),
            pl.BlockSpec((None, seq, V_HEAD_DIM), kvmap),
            pl.BlockSpec((1, V_HEAD_DIM), lambda b, h, i: (0, 0)),
        ],
        out_specs=pl.BlockSpec((None, tq, V_HEAD_DIM), qmap),
        out_shape=jax.ShapeDtypeStruct((bsz, seq, aw), BF16),
        scratch_shapes=[pltpu.VMEM((2 * tq, V_HEAD_DIM), F32)] * 3,
        compiler_params=pltpu.CompilerParams(
            dimension_semantics=("parallel", "parallel", "parallel"), vmem_limit_bytes=VMEM_LIMIT),
        name="attn",
    )(lam, q, k, v, subln_g.reshape(1, V_HEAD_DIM))


def _ssm_operators(a_re, a_im, log_step, b_re, b_im, c_re, c_im):
    t = LANES
    hi = lax.Precision.HIGHEST
    lam = lax.complex(jnp.minimum(a_re, -1e-4), a_im)
    delta = jnp.exp(log_step)[:, None]
    lam_bar = jnp.exp(lam * delta)
    bbar = ((lam_bar - 1.0) / lam)[:, :, None] * lax.complex(b_re, b_im)
    cmat = lax.complex(c_re, c_im)
    ld = lam * delta
    tau = jnp.arange(t + 1, dtype=F32)
    pw = jnp.exp(ld[:, :, None] * tau)
    cb = cmat[:, None, :, :] * jnp.swapaxes(bbar, 1, 2)[:, :, None, :]
    g = a_re.shape[0]
    cb = cb.reshape(g, SSM_GROUP * SSM_GROUP, SSM_STATE)
    cbcat = jnp.concatenate([cb.real, -cb.imag], axis=-1)
    pcat = jnp.concatenate([pw.real[:, :, :t], pw.imag[:, :, :t]], axis=1)
    kmat = jnp.einsum('gxp,gpt->gxt', cbcat, pcat, precision=hi)
    prev = pw[:, :, t - 1::-1][:, :, :t]
    ws = bbar.transpose(0, 2, 1)[:, :, None, :] * prev.transpose(0, 2, 1)[:, None, :, :]
    wstate = jnp.concatenate([ws.real, ws.imag], axis=-1).reshape(g, SSM_GROUP * t, 2 * SSM_STATE)
    wc = cmat.transpose(0, 2, 1)[:, :, :, None] * pw[:, :, None, 1:]
    wcarry = jnp.concatenate([wc.real, -wc.imag], axis=1).reshape(g, 2 * SSM_STATE, SSM_GROUP * t)
    rows = []
    for i in range(6):
        d = jnp.exp(ld * float(t * (1 << i)))
        rows.append(jnp.concatenate([d.real, d.real], axis=-1))
        rows.append(jnp.concatenate([-d.imag, d.imag], axis=-1))
    rows += [jnp.zeros_like(rows[0])] * 4
    dpow = jnp.stack(rows, axis=1)
    return kmat, wstate.astype(BF16), wcarry.astype(BF16), dpow


def _ssm_kernel(u_ref, k_ref, ws_ref, wc_ref, dp_ref, y_ref, m_sc):
    bsz, nch, n_chunk, t = u_ref.shape
    row = lax.broadcasted_iota(jnp.int32, (t, t), 0)
    col = lax.broadcasted_iota(jnp.int32, (t, t), 1)
    causal = col >= row

    def build(ci, carry):
        r0 = pl.multiple_of(ci * t, t)
        for co in range(nch):
            kr = k_ref[pl.ds(ci * nch + co, 1), :]
            kb = jnp.broadcast_to(kr, (t, t))
            kb = pltpu.roll(kb, 0, 1, stride=1, stride_axis=0)
            m_sc[pl.ds(r0, t), co * t:(co + 1) * t] = jnp.where(causal, kb, 0.0).astype(BF16)
        return carry

    lax.fori_loop(0, nch, build, 0)

    uflat = jnp.concatenate(
        [jnp.concatenate([u_ref[b, ci] for b in range(bsz)], axis=0) for ci in range(nch)],
        axis=1).astype(BF16)
    y = _dot(uflat, m_sc[...])
    z = _dot(uflat, ws_ref[...])
    kidx = lax.broadcasted_iota(jnp.int32, z.shape, 0) & (n_chunk - 1)
    half = z.shape[1] // 2
    shift = 1
    i = 0
    while shift < n_chunk:
        zs = jnp.where(kidx >= shift, pltpu.roll(z, shift, 0), 0.0)
        z = z + zs * dp_ref[2 * i:2 * i + 1, :] + pltpu.roll(zs, half, 1) * dp_ref[2 * i + 1:2 * i + 2, :]
        shift *= 2
        i += 1
    xin = jnp.where(kidx >= 1, pltpu.roll(z, 1, 0), 0.0)
    xh, xl = _split_bf16(xin)
    wc = wc_ref[...]
    y = y + _dot(xh, wc) + _dot(xl, wc)
    for b in range(bsz):
        for co in range(nch):
            y_ref[b, co] = y[b * n_chunk:(b + 1) * n_chunk, co * t:(co + 1) * t]


def _ssm_call(ut, kmat, wstate, wcarry, dpow):
    bsz, sw, seq = ut.shape
    n_groups = sw // SSM_GROUP
    n_chunk = seq // LANES
    assert n_chunk & (n_chunk - 1) == 0 and n_chunk <= 64
    u4 = ut.reshape(bsz, sw, n_chunk, LANES)
    blk = (bsz, SSM_GROUP, n_chunk, LANES)
    gmap = lambda g: (0, g, 0, 0)
    pmap = lambda g: (g, 0, 0)
    y4 = pl.pallas_call(
        _ssm_kernel,
        grid=(n_groups,),
        in_specs=[
            pl.BlockSpec(blk, gmap),
            pl.BlockSpec((None,) + kmat.shape[1:], pmap),
            pl.BlockSpec((None,) + wstate.shape[1:], pmap),
            pl.BlockSpec((None,) + wcarry.shape[1:], pmap),
            pl.BlockSpec((None,) + dpow.shape[1:], pmap),
        ],
        out_specs=pl.BlockSpec(blk, gmap),
        out_shape=jax.ShapeDtypeStruct(u4.shape, F32),
        scratch_shapes=[pltpu.VMEM((SSM_GROUP * LANES, SSM_GROUP * LANES), BF16)],
        compiler_params=pltpu.CompilerParams(
            dimension_semantics=("parallel",), vmem_limit_bytes=VMEM_LIMIT),
        name="ssm",
    )(u4, kmat, wstate, wcarry, dpow)
    return y4.reshape(bsz, sw, seq)


def _mid_kernel(x_ref, attn_ref, yt_ref, ut_ref, mod_ref, dsk_ref, wglut_ref, bglu_ref, wo1_ref, wo2_ref,
                g2_ref, wrt_ref, wsg_ref, wsu_ref, wsd_ref, xp_ref, h2_ref, lg_ref):
    gt = _gelu_tanh(yt_ref[...] + dsk_ref[...] * ut_ref[...])
    zt = _dot(wglut_ref[...], gt.astype(BF16)) + bglu_ref[...]
    st = gt * jax.nn.sigmoid(zt)
    s = st.T.astype(BF16)
    mix = _dot(attn_ref[...], wo1_ref[...]) + _dot(s, wo2_ref[...])
    x1 = x_ref[...] + mod_ref[2:3, :] * mix
    ms = jnp.mean(x1 * x1, axis=-1, keepdims=True)
    h2 = x1 * lax.rsqrt(ms + NORM_EPS) * g2_ref[...]
    h2 = h2 * (1.0 + mod_ref[4:5, :]) + mod_ref[3:4, :]
    hb = h2.astype(BF16)
    h2_ref[...] = hb
    lg_ref[...] = jax.nn.sigmoid(_dot3_nt(wrt_ref[...], h2))
    sh = _silu(_dot(hb, wsg_ref[...])) * _dot(hb, wsu_ref[...])
    shared = _dot(sh.astype(BF16), wsd_ref[...])
    xp_ref[...] = x1 + mod_ref[5:6, :] * shared


def _mid_call(x, attn, yt, ut, mod3, dsk, wglut, bglu, wo1, wo2, g2, wrt, wsg, wsu, wsd, tm=512):
    bsz, seq, dm = x.shape
    aw = attn.shape[2]
    sw = yt.shape[1]
    ne = wrt.shape[0]
    row = lambda b, i: (b, i, 0)
    colm = lambda b, i: (b, 0, i)
    full = lambda a: pl.BlockSpec(a.shape, lambda b, i: (0,) * a.ndim)
    dsk = dsk.reshape(sw, 1)
    bglu = bglu.reshape(sw, 1)
    g2 = g2.reshape(1, dm)
    return pl.pallas_call(
        _mid_kernel,
        grid=(bsz, seq // tm),
        in_specs=[
            pl.BlockSpec((None, tm, dm), row),
            pl.BlockSpec((None, tm, aw), row),
            pl.BlockSpec((None, sw, tm), colm),
            pl.BlockSpec((None, sw, tm), colm),
            pl.BlockSpec((None, N_ADA, dm), lambda b, i: (b, 0, 0)),
            full(dsk), full(wglut), full(bglu), full(wo1), full(wo2), full(g2), full(wrt),
            full(wsg), full(wsu), full(wsd),
        ],
        out_specs=[
            pl.BlockSpec((None, tm, dm), row),
            pl.BlockSpec((None, tm, dm), row),
            pl.BlockSpec((None, ne, tm), colm),
        ],
        out_shape=[
            jax.ShapeDtypeStruct((bsz, seq, dm), F32),
            jax.ShapeDtypeStruct((bsz, seq, dm), BF16),
            jax.ShapeDtypeStruct((bsz, ne, seq), F32),
        ],
        compiler_params=pltpu.CompilerParams(
            dimension_semantics=("parallel", "parallel"), vmem_limit_bytes=VMEM_LIMIT),
        name="mid",
    )(x, attn, yt, ut, mod3, dsk, wglut, bglu, wo1, wo2, g2, wrt, wsg, wsu, wsd)


def _moe_kernel(be_ref, xs_ref, wg_ref, wu_ref, wd_ref, o_ref, wgb, wub, wdb):
    i = pl.program_id(0)
    changed = jnp.logical_or(i == 0, be_ref[i] != be_ref[jnp.maximum(i - 1, 0)])

    @pl.when(changed)
    def _():
        wgb[...] = wg_ref[...].astype(BF16)
        wub[...] = wu_ref[...].astype(BF16)
        wdb[...] = wd_ref[...].astype(BF16)

    xb = xs_ref[...]
    hb = _silu(_dot(xb, wgb[...])) * _dot(xb, wub[...])
    o_ref[...] = _dot(hb.astype(BF16), wdb[...]).astype(o_ref.dtype)


def _moe_call(blk_e, xs, w_gate, w_up, w_down):
    n_rows, dm = xs.shape
    de = w_gate.shape[2]
    n_blocks = n_rows // MOE_BLOCK
    grid_spec = pltpu.PrefetchScalarGridSpec(
        num_scalar_prefetch=1,
        grid=(n_blocks,),
        in_specs=[
            pl.BlockSpec((MOE_BLOCK, dm), lambda i, be: (i, 0)),
            pl.BlockSpec((None, dm, de), lambda i, be: (be[i], 0, 0)),
            pl.BlockSpec((None, dm, de), lambda i, be: (be[i], 0, 0)),
            pl.BlockSpec((None, de, dm), lambda i, be: (be[i], 0, 0)),
        ],
        out_specs=pl.BlockSpec((MOE_BLOCK, dm), lambda i, be: (i, 0)),
        scratch_shapes=[pltpu.VMEM((dm, de), BF16), pltpu.VMEM((dm, de), BF16), pltpu.VMEM((de, dm), BF16)],
    )
    return pl.pallas_call(
        _moe_kernel,
        grid_spec=grid_spec,
        out_shape=jax.ShapeDtypeStruct((n_rows, dm), BF16),
        compiler_params=pltpu.CompilerParams(
            dimension_semantics=("arbitrary",), vmem_limit_bytes=VMEM_LIMIT),
        name="moe",
    )(blk_e, xs, w_gate, w_up, w_down)


def _final_kernel(xp_ref, r_ref, mod_ref, g_ref, o_ref):
    x = xp_ref[...] + mod_ref[5:6, :] * r_ref[...]
    ms = jnp.mean(x * x, axis=-1, keepdims=True)
    o_ref[...] = x * lax.rsqrt(ms + NORM_EPS) * g_ref[...]


def _final_call(xp, routed, mod3, final_g, tm=1024):
    bsz, seq, dm = xp.shape
    row = lambda b, i: (b, i, 0)
    return pl.pallas_call(
        _final_kernel,
        grid=(bsz, seq // tm),
        in_specs=[
            pl.BlockSpec((None, tm, dm), row),
            pl.BlockSpec((None, tm, dm), row),
            pl.BlockSpec((None, N_ADA, dm), lambda b, i: (b, 0, 0)),
            pl.BlockSpec((1, dm), lambda b, i: (0, 0)),
        ],
        out_specs=pl.BlockSpec((None, tm, dm), row),
        out_shape=jax.ShapeDtypeStruct((bsz, seq, dm), F32),
        compiler_params=pltpu.CompilerParams(
            dimension_semantics=("parallel", "parallel"), vmem_limit_bytes=VMEM_LIMIT),
        name="final",
    )(xp, routed, mod3, final_g.reshape(1, dm))


def _route(scores, router_bias):
    n_tok = scores.shape[0]
    sel = scores + router_bias
    group_score = lax.top_k(sel.reshape(n_tok, N_EXPERT_GROUPS, -1), 2)[0].sum(-1)
    _, gidx = lax.top_k(group_score, TOPK_GROUPS)
    gmask = jnp.any(gidx[:, :, None] == jnp.arange(N_EXPERT_GROUPS)[None, None, :], axis=1)
    emask = jnp.repeat(gmask, N_EXPERTS // N_EXPERT_GROUPS, axis=1)
    _, idx = lax.top_k(jnp.where(emask, sel, -jnp.inf), TOP_K)
    w = jnp.take_along_axis(scores, idx, axis=-1)
    w = w / jnp.sum(w, axis=-1, keepdims=True) * ROUTED_SCALE

    nk = n_tok * TOP_K
    flat_e = idx.reshape(-1)
    order = jnp.argsort(flat_e)
    se = flat_e[order]
    counts = jnp.bincount(flat_e, length=N_EXPERTS)
    pcounts = (counts + MOE_BLOCK - 1) // MOE_BLOCK * MOE_BLOCK
    starts = jnp.cumsum(counts) - counts
    pends = jnp.cumsum(pcounts)
    pstarts = pends - pcounts
    dest = (pstarts[se] + jnp.arange(nk) - starts[se]).astype(jnp.int32)
    n_blocks = -(-(nk + N_EXPERTS * (MOE_BLOCK - 1)) // MOE_BLOCK)
    n_rows = n_blocks * MOE_BLOCK
    buf_tok = jnp.zeros((n_rows,), jnp.int32).at[dest].set((order // TOP_K).astype(jnp.int32))
    pos = jnp.zeros((nk,), jnp.int32).at[order].set(dest).reshape(n_tok, TOP_K)
    blk_e = jnp.minimum(jnp.searchsorted(pends, jnp.arange(n_blocks) * MOE_BLOCK, side='right'),
                        N_EXPERTS - 1).astype(jnp.int32)
    return w, buf_tok, pos, blk_e


def kernel(x, c, w_ada, b_ada, norm1_g, w_in, w_out, lambda_q1, lambda_k1, lambda_q2, lambda_k2, subln_g,
           ssm_a_re, ssm_a_im, ssm_log_step, ssm_b_re, ssm_b_im, ssm_c_re, ssm_c_im, ssm_d, w_glu, b_glu,
           norm2_g, w_router, router_bias, w_gate, w_up, w_down, ws_gate, ws_up, ws_down, final_g):
    bsz, seq, dm = x.shape
    n_tok = bsz * seq
    aw = N_ATTN_HEADS * V_HEAD_DIM

    mod3 = _mod_call(c, w_ada[0], b_ada[0]).reshape(bsz, N_ADA, dm)

    wqkv = w_in[0][:, :3 * aw].astype(BF16)
    wut = w_in[0][:, 3 * aw:].T.astype(BF16)
    q, k, v, ut = _inproj_call(x, mod3, norm1_g[0], wqkv, wut)

    lam = (jnp.exp(jnp.sum(lambda_q1[0] * lambda_k1[0])) - jnp.exp(jnp.sum(lambda_q2[0] * lambda_k2[0]))
           + LAM_INIT).reshape(1)
    attn = _attn_call(lam, q, k, v, subln_g[0])

    kmat, wstate, wcarry, dpow = _ssm_operators(ssm_a_re[0], ssm_a_im[0], ssm_log_step[0], ssm_b_re[0],
                                                ssm_b_im[0], ssm_c_re[0], ssm_c_im[0])
    yt = _ssm_call(ut, kmat, wstate, wcarry, dpow)

    xp, h2, logits_t = _mid_call(
        x, attn, yt, ut, mod3, ssm_d[0], w_glu[0].T.astype(BF16), b_glu[0],
        w_out[0][:aw].astype(BF16), w_out[0][aw:].astype(BF16), norm2_g[0], w_router[0].T,
        ws_gate[0].astype(BF16), ws_up[0].astype(BF16), ws_down[0].astype(BF16))

    scores = jnp.swapaxes(logits_t, 1, 2).reshape(n_tok, N_EXPERTS)
    w, buf_tok, pos, blk_e = _route(scores, router_bias[0])
    xs = jnp.take(h2.reshape(n_tok, dm), buf_tok, axis=0)
    out = _moe_call(blk_e, xs, w_gate[0], w_up[0], w_down[0])
    routed = jnp.sum(jnp.take(out, pos, axis=0).astype(F32) * w[:, :, None], axis=1)

    return _final_call(xp, routed.reshape(bsz, seq, dm), mod3, final_g)
```

```python
import functools
import math

import jax
import jax.numpy as jnp
from jax import lax
from jax.experimental import pallas as pl
from jax.experimental.pallas import tpu as pltpu
from jax.experimental.pallas import tpu_sc as plsc

F32 = jnp.float32
BF16 = jnp.bfloat16

N_ATTN_HEADS = 4
ATTN_HEAD_DIM = 64
V_HEAD_DIM = 128
SSM_GROUP = 16
N_SSM_GROUPS = 32
SSM_STATE = 64
N_EXPERTS = 256
TOP_K = 8
N_EXPERT_GROUPS = 8
TOPK_GROUPS = 4
ROUTED_SCALE = 2.5
NORM_EPS = 1e-6
SUBLN_EPS = 1e-5
N_ADA = 6
LAM_INIT = 0.8 - 0.6 * math.exp(-0.3 * 0)

LANES = 128
MOE_BLOCK = 256
NEG_BIG = -1e30
VMEM_LIMIT = 48 * 1024 * 1024


def _split_bf16(a):
    hi = a.astype(BF16)
    lo = (a - hi.astype(F32)).astype(BF16)
    return hi, lo


def _dot(a, b):
    return jnp.dot(a, b, preferred_element_type=F32)


def _dot_nt(a, b):
    return lax.dot_general(a, b, (((1,), (1,)), ((), ())), preferred_element_type=F32)


def _dot3(a, b):
    ah, al = _split_bf16(a)
    bh, bl = _split_bf16(b)
    return _dot(ah, bh) + _dot(ah, bl) + _dot(al, bh)


def _dot3_nt(a, b):
    ah, al = _split_bf16(a)
    bh, bl = _split_bf16(b)
    return _dot_nt(ah, bh) + _dot_nt(ah, bl) + _dot_nt(al, bh)


def _silu(x):
    return x * jax.nn.sigmoid(x)


def _gelu_tanh(x):
    c = math.sqrt(2.0 / math.pi)
    return 0.5 * x * (1.0 + jnp.tanh(c * (x + 0.044715 * (x * x * x))))


def _mod_kernel(c_ref, w_ref, b_ref, o_ref):
    cond = _silu(c_ref[...])
    o_ref[...] = _dot3(cond, w_ref[...]) + b_ref[...]


def _mod_call(c, w_ada, b_ada):
    bsz, dm = c.shape
    n_out = w_ada.shape[1]
    tn = 1024
    return pl.pallas_call(
        _mod_kernel,
        grid=(n_out // tn,),
        in_specs=[
            pl.BlockSpec((bsz, dm), lambda j: (0, 0)),
            pl.BlockSpec((dm, tn), lambda j: (0, j)),
            pl.BlockSpec((1, tn), lambda j: (0, j)),
        ],
        out_specs=pl.BlockSpec((bsz, tn), lambda j: (0, j)),
        out_shape=jax.ShapeDtypeStruct((bsz, n_out), F32),
        compiler_params=pltpu.CompilerParams(vmem_limit_bytes=VMEM_LIMIT),
        name="mod",
    )(c, w_ada, b_ada.reshape(1, n_out))


def _inproj_kernel(x_ref, mod_ref, g_ref, wqkv_ref, wut_ref, q_ref, k_ref, v_ref, ut_ref, *, aw):
    x = x_ref[...]
    ms = jnp.mean(x * x, axis=-1, keepdims=True)
    h = x * lax.rsqrt(ms + NORM_EPS) * g_ref[...]
    h = h * (1.0 + mod_ref[1:2, :]) + mod_ref[0:1, :]
    hb = h.astype(BF16)
    qkv = _dot(hb, wqkv_ref[...])
    q_ref[...] = (qkv[:, :aw] * (ATTN_HEAD_DIM ** -0.5)).astype(BF16)
    k_ref[...] = qkv[:, aw:2 * aw].astype(BF16)
    v_ref[...] = qkv[:, 2 * aw:].astype(BF16)
    ut_ref[...] = _dot_nt(wut_ref[...], hb)


def _inproj_call(x, mod3, norm_g, wqkv, wut, tm=512):
    bsz, seq, dm = x.shape
    aw = wqkv.shape[1] // 3
    sw = wut.shape[0]
    row = lambda b, i: (b, i, 0)
    return pl.pallas_call(
        functools.partial(_inproj_kernel, aw=aw),
        grid=(bsz, seq // tm),
        in_specs=[
            pl.BlockSpec((None, tm, dm), row),
            pl.BlockSpec((None, N_ADA, dm), lambda b, i: (b, 0, 0)),
            pl.BlockSpec((1, dm), lambda b, i: (0, 0)),
            pl.BlockSpec(wqkv.shape, lambda b, i: (0, 0)),
            pl.BlockSpec(wut.shape, lambda b, i: (0, 0)),
        ],
        out_specs=[
            pl.BlockSpec((None, tm, aw), row),
            pl.BlockSpec((None, tm, aw), row),
            pl.BlockSpec((None, tm, aw), row),
            pl.BlockSpec((None, sw, tm), lambda b, i: (b, 0, i)),
        ],
        out_shape=[
            jax.ShapeDtypeStruct((bsz, seq, aw), BF16),
            jax.ShapeDtypeStruct((bsz, seq, aw), BF16),
            jax.ShapeDtypeStruct((bsz, seq, aw), BF16),
            jax.ShapeDtypeStruct((bsz, sw, seq), F32),
        ],
        compiler_params=pltpu.CompilerParams(
            dimension_semantics=("parallel", "parallel"), vmem_limit_bytes=VMEM_LIMIT),
        name="inproj",
    )(x, mod3, norm_g.reshape(1, dm), wqkv, wut)


def _attn_kernel(lam_ref, q_ref, k_ref, v_ref, g_ref, o_ref, m_sc, l_sc, acc_sc, *, tq):
    i = pl.program_id(2)
    q = q_ref[...]
    lane = lax.broadcasted_iota(jnp.int32, q.shape, 1)
    zero = jnp.zeros_like(q)
    q2 = jnp.concatenate([jnp.where(lane < ATTN_HEAD_DIM, q, zero),
                          jnp.where(lane >= ATTN_HEAD_DIM, q, zero)], axis=0)
    m_sc[...] = jnp.full(m_sc.shape, NEG_BIG, F32)
    l_sc[...] = jnp.zeros(l_sc.shape, F32)
    acc_sc[...] = jnp.zeros(acc_sc.shape, F32)

    def step(j, masked):
        start = pl.multiple_of(j * tq, tq)
        kt = k_ref[pl.ds(start, tq), :]
        vt = v_ref[pl.ds(start, tq), :]
        s = _dot_nt(q2, kt)
        if masked:
            r = lax.broadcasted_iota(jnp.int32, s.shape, 0)
            c = lax.broadcasted_iota(jnp.int32, s.shape, 1)
            s = jnp.where(c <= (r & (tq - 1)), s, NEG_BIG)
        m_prev = m_sc[...]
        m_new = jnp.maximum(m_prev, jnp.max(s, axis=-1, keepdims=True))
        alpha = jnp.exp(m_prev - m_new)
        p = jnp.exp(s - jnp.concatenate([m_new] * (tq // LANES), axis=1))
        l_sc[...] = alpha * l_sc[...] + jnp.sum(p, axis=-1, keepdims=True)
        acc_sc[...] = alpha * acc_sc[...] + _dot(p.astype(BF16), vt)
        m_sc[...] = m_new

    def body(j, carry):
        step(j, False)
        return carry

    lax.fori_loop(0, i, body, 0)
    step(i, True)

    o_all = acc_sc[...] / l_sc[...]
    o = o_all[:tq] - lam_ref[0] * o_all[tq:]
    ms = jnp.mean(o * o, axis=-1, keepdims=True)
    o = o * lax.rsqrt(ms + SUBLN_EPS) * g_ref[...] * (1.0 - LAM_INIT)
    o_ref[...] = o.astype(o_ref.dtype)


def _attn_call(lam, q, k, v, subln_g, tq=512):
    bsz, seq, aw = q.shape
    nh = aw // V_HEAD_DIM
    qmap = lambda b, h, i: (b, i, h)
    kvmap = lambda b, h, i: (b, 0, h)
    return pl.pallas_call(
        functools.partial(_attn_kernel, tq=tq),
        grid=(bsz, nh, seq // tq),
        in_specs=[
            pl.BlockSpec(memory_space=pltpu.SMEM),
            pl.BlockSpec((None, tq, V_HEAD_DIM), qmap),
            pl.BlockSpec((None, seq, V_HEAD_DIM), kvmap),
            pl.BlockSpec((None, seq, V_HEAD_DIM), kvmap),
            pl.BlockSpec((1, V_HEAD_DIM), lambda b, h, i: (0, 0)),
        ],
        out_specs=pl.BlockSpec((None, tq, V_HEAD_DIM), qmap),
        out_shape=jax.ShapeDtypeStruct((bsz, seq, aw), BF16),
        scratch_shapes=[pltpu.VMEM((2 * tq, V_HEAD_DIM), F32)] * 3,
        compiler_params=pltpu.CompilerParams(
            dimension_semantics=("parallel", "parallel", "parallel"), vmem_limit_bytes=VMEM_LIMIT),
        name="attn",
    )(lam, q, k, v, subln_g.reshape(1, V_HEAD_DIM))


def _ssm_operators(a_re, a_im, log_step, b_re, b_im, c_re, c_im):
    t = LANES
    hi = lax.Precision.HIGHEST
    lam = lax.complex(jnp.minimum(a_re, -1e-4), a_im)
    delta = jnp.exp(log_step)[:, None]
    lam_bar = jnp.exp(lam * delta)
    bbar = ((lam_bar - 1.0) / lam)[:, :, None] * lax.complex(b_re, b_im)
    cmat = lax.complex(c_re, c_im)
    ld = lam * delta
    tau = jnp.arange(t + 1, dtype=F32)
    pw = jnp.exp(ld[:, :, None] * tau)
    cb = cmat[:, None, :, :] * jnp.swapaxes(bbar, 1, 2)[:, :, None, :]
    g = a_re.shape[0]
    cb = cb.reshape(g, SSM_GROUP * SSM_GROUP, SSM_STATE)
    cbcat = jnp.concatenate([cb.real, -cb.imag], axis=-1)
    pcat = jnp.concatenate([pw.real[:, :, :t], pw.imag[:, :, :t]], axis=1)
    kmat = jnp.einsum('gxp,gpt->gxt', cbcat, pcat, precision=hi)
    prev = pw[:, :, t - 1::-1][:, :, :t]
    ws = bbar.transpose(0, 2, 1)[:, :, None, :] * prev.transpose(0, 2, 1)[:, None, :, :]
    wstate = jnp.concatenate([ws.real, ws.imag], axis=-1).reshape(g, SSM_GROUP * t, 2 * SSM_STATE)
    wc = cmat.transpose(0, 2, 1)[:, :, :, None] * pw[:, :, None, 1:]
    wcarry = jnp.concatenate([wc.real, -wc.imag], axis=1).reshape(g, 2 * SSM_STATE, SSM_GROUP * t)
    rows = []
    for i in range(6):
        d = jnp.exp(ld * float(t * (1 << i)))
        rows.append(jnp.concatenate([d.real, d.real], axis=-1))
        rows.append(jnp.concatenate([-d.imag, d.imag], axis=-1))
    rows += [jnp.zeros_like(rows[0])] * 4
    dpow = jnp.stack(rows, axis=1)
    return kmat, wstate.astype(BF16), wcarry.astype(BF16), dpow


def _ssm_kernel(u_ref, k_ref, ws_ref, wc_ref, dp_ref, y_ref, m_sc):
    bsz, nch, n_chunk, t = u_ref.shape
    row = lax.broadcasted_iota(jnp.int32, (t, t), 0)
    col = lax.broadcasted_iota(jnp.int32, (t, t), 1)
    causal = col >= row

    def build(ci, carry):
        r0 = pl.multiple_of(ci * t, t)
        for co in range(nch):
            kr = k_ref[pl.ds(ci * nch + co, 1), :]
            kb = jnp.broadcast_to(kr, (t, t))
            kb = pltpu.roll(kb, 0, 1, stride=1, stride_axis=0)
            m_sc[pl.ds(r0, t), co * t:(co + 1) * t] = jnp.where(causal, kb, 0.0).astype(BF16)
        return carry

    lax.fori_loop(0, nch, build, 0)

    uflat = jnp.concatenate(
        [jnp.concatenate([u_ref[b, ci] for b in range(bsz)], axis=0) for ci in range(nch)],
        axis=1).astype(BF16)
    y = _dot(uflat, m_sc[...])
    z = _dot(uflat, ws_ref[...])
    kidx = lax.broadcasted_iota(jnp.int32, z.shape, 0) & (n_chunk - 1)
    half = z.shape[1] // 2
    shift = 1
    i = 0
    while shift < n_chunk:
        zs = jnp.where(kidx >= shift, pltpu.roll(z, shift, 0), 0.0)
        z = z + zs * dp_ref[2 * i:2 * i + 1, :] + pltpu.roll(zs, half, 1) * dp_ref[2 * i + 1:2 * i + 2, :]
        shift *= 2
        i += 1
    xin = jnp.where(kidx >= 1, pltpu.roll(z, 1, 0), 0.0)
    xh, xl = _split_bf16(xin)
    wc = wc_ref[...]
    y = y + _dot(xh, wc) + _dot(xl, wc)
    for b in range(bsz):
        for co in range(nch):
            y_ref[b, co] = y[b * n_chunk:(b + 1) * n_chunk, co * t:(co + 1) * t]


def _ssm_call(ut, kmat, wstate, wcarry, dpow):
    bsz, sw, seq = ut.shape
    n_groups = sw // SSM_GROUP
    n_chunk = seq // LANES
    assert n_chunk & (n_chunk - 1) == 0 and n_chunk <= 64
    u4 = ut.reshape(bsz, sw, n_chunk, LANES)
    blk = (bsz, SSM_GROUP, n_chunk, LANES)
    gmap = lambda g: (0, g, 0, 0)
    pmap = lambda g: (g, 0, 0)
    y4 = pl.pallas_call(
        _ssm_kernel,
        grid=(n_groups,),
        in_specs=[
            pl.BlockSpec(blk, gmap),
            pl.BlockSpec((None,) + kmat.shape[1:], pmap),
            pl.BlockSpec((None,) + wstate.shape[1:], pmap),
            pl.BlockSpec((None,) + wcarry.shape[1:], pmap),
            pl.BlockSpec((None,) + dpow.shape[1:], pmap),
        ],
        out_specs=pl.BlockSpec(blk, gmap),
        out_shape=jax.ShapeDtypeStruct(u4.shape, F32),
        scratch_shapes=[pltpu.VMEM((SSM_GROUP * LANES, SSM_GROUP * LANES), BF16)],
        compiler_params=pltpu.CompilerParams(
            dimension_semantics=("parallel",), vmem_limit_bytes=VMEM_LIMIT),
        name="ssm",
    )(u4, kmat, wstate, wcarry, dpow)
    return y4.reshape(bsz, sw, seq)


def _mid_kernel(x_ref, attn_ref, yt_ref, ut_ref, mod_ref, dsk_ref, wglut_ref, bglu_ref, wo1_ref, wo2_ref,
                g2_ref, wrt_ref, wsg_ref, wsu_ref, wsd_ref, xp_ref, h2_ref, lg_ref):
    gt = _gelu_tanh(yt_ref[...] + dsk_ref[...] * ut_ref[...])
    zt = _dot(wglut_ref[...], gt.astype(BF16)) + bglu_ref[...]
    st = gt * jax.nn.sigmoid(zt)
    s = st.T.astype(BF16)
    mix = _dot(attn_ref[...], wo1_ref[...]) + _dot(s, wo2_ref[...])
    x1 = x_ref[...] + mod_ref[2:3, :] * mix
    ms = jnp.mean(x1 * x1, axis=-1, keepdims=True)
    h2 = x1 * lax.rsqrt(ms + NORM_EPS) * g2_ref[...]
    h2 = h2 * (1.0 + mod_ref[4:5, :]) + mod_ref[3:4, :]
    hb = h2.astype(BF16)
    bits = lax.bitcast_convert_type(hb.astype(F32), jnp.uint32)
    half = bits.shape[1] // 2
    h2_ref[...] = (bits[:, :half] >> 16) | (bits[:, half:] & jnp.uint32(0xFFFF0000))
    lg_ref[...] = jax.nn.sigmoid(_dot3_nt(wrt_ref[...], h2))
    sh = _silu(_dot(hb, wsg_ref[...])) * _dot(hb, wsu_ref[...])
    shared = _dot(sh.astype(BF16), wsd_ref[...])
    xp_ref[...] = x1 + mod_ref[5:6, :] * shared


def _mid_call(x, attn, yt, ut, mod3, dsk, wglut, bglu, wo1, wo2, g2, wrt, wsg, wsu, wsd, tm=512):
    bsz, seq, dm = x.shape
    aw = attn.shape[2]
    sw = yt.shape[1]
    ne = wrt.shape[0]
    row = lambda b, i: (b, i, 0)
    colm = lambda b, i: (b, 0, i)
    full = lambda a: pl.BlockSpec(a.shape, lambda b, i: (0,) * a.ndim)
    dsk = dsk.reshape(sw, 1)
    bglu = bglu.reshape(sw, 1)
    g2 = g2.reshape(1, dm)
    return pl.pallas_call(
        _mid_kernel,
        grid=(bsz, seq // tm),
        in_specs=[
            pl.BlockSpec((None, tm, dm), row),
            pl.BlockSpec((None, tm, aw), row),
            pl.BlockSpec((None, sw, tm), colm),
            pl.BlockSpec((None, sw, tm), colm),
            pl.BlockSpec((None, N_ADA, dm), lambda b, i: (b, 0, 0)),
            full(dsk), full(wglut), full(bglu), full(wo1), full(wo2), full(g2), full(wrt),
            full(wsg), full(wsu), full(wsd),
        ],
        out_specs=[
            pl.BlockSpec((None, tm, dm), row),
            pl.BlockSpec((None, tm, dm // 2), row),
            pl.BlockSpec((None, ne, tm), colm),
        ],
        out_shape=[
            jax.ShapeDtypeStruct((bsz, seq, dm), F32),
            jax.ShapeDtypeStruct((bsz, seq, dm // 2), jnp.uint32),
            jax.ShapeDtypeStruct((bsz, ne, seq), F32),
        ],
        compiler_params=pltpu.CompilerParams(
            dimension_semantics=("parallel", "parallel"), vmem_limit_bytes=VMEM_LIMIT),
        name="mid",
    )(x, attn, yt, ut, mod3, dsk, wglut, bglu, wo1, wo2, g2, wrt, wsg, wsu, wsd)


def _moe_kernel(be_ref, nv_ref, xs_ref, wg_ref, wu_ref, wd_ref, o_ref, wgb, wub, wdb):
    i = pl.program_id(0)
    changed = jnp.logical_or(i == 0, be_ref[i] != be_ref[jnp.maximum(i - 1, 0)])

    @pl.when(changed)
    def _():
        wgb[...] = wg_ref[...].astype(BF16)
        wub[...] = wu_ref[...].astype(BF16)
        wdb[...] = wd_ref[...].astype(BF16)

    xu = xs_ref[...]
    valid = lax.broadcasted_iota(jnp.int32, xu.shape, 0) < nv_ref[i]
    xa = jnp.where(valid, lax.bitcast_convert_type(xu << 16, F32), 0.0).astype(BF16)
    xb = jnp.where(valid, lax.bitcast_convert_type(xu & jnp.uint32(0xFFFF0000), F32), 0.0).astype(BF16)
    half = xu.shape[1]
    gate = _dot(xa, wgb[:half, :]) + _dot(xb, wgb[half:, :])
    up = _dot(xa, wub[:half, :]) + _dot(xb, wub[half:, :])
    hb = _silu(gate) * up
    o_ref[...] = _dot(hb.astype(BF16), wdb[...]).astype(o_ref.dtype)


def _moe_call(blk_e, n_valid, xs, w_gate, w_up, w_down):
    n_rows = xs.shape[0]
    dm, de = w_gate.shape[1:]
    n_blocks = n_rows // MOE_BLOCK
    grid_spec = pltpu.PrefetchScalarGridSpec(
        num_scalar_prefetch=2,
        grid=(n_blocks,),
        in_specs=[
            pl.BlockSpec((MOE_BLOCK, dm // 2), lambda i, be, nv: (i, 0)),
            pl.BlockSpec((None, dm, de), lambda i, be, nv: (be[i], 0, 0)),
            pl.BlockSpec((None, dm, de), lambda i, be, nv: (be[i], 0, 0)),
            pl.BlockSpec((None, de, dm), lambda i, be, nv: (be[i], 0, 0)),
        ],
        out_specs=pl.BlockSpec((MOE_BLOCK, dm), lambda i, be, nv: (i, 0)),
        scratch_shapes=[pltpu.VMEM((dm, de), BF16), pltpu.VMEM((dm, de), BF16), pltpu.VMEM((de, dm), BF16)],
    )
    return pl.pallas_call(
        _moe_kernel,
        grid_spec=grid_spec,
        out_shape=jax.ShapeDtypeStruct((n_rows, dm), BF16),
        compiler_params=pltpu.CompilerParams(
            dimension_semantics=("arbitrary",), vmem_limit_bytes=VMEM_LIMIT),
        name="moe",
    )(blk_e, n_valid, xs, w_gate, w_up, w_down)


def _final_kernel(xp_ref, r_ref, w_ref, mod_ref, g_ref, o_ref):
    w = w_ref[...]
    routed = w[:, 0:1] * r_ref[0].astype(F32)
    for k in range(1, r_ref.shape[0]):
        routed = routed + w[:, k:k + 1] * r_ref[k].astype(F32)
    x = xp_ref[...] + mod_ref[5:6, :] * routed
    ms = jnp.mean(x * x, axis=-1, keepdims=True)
    o_ref[...] = x * lax.rsqrt(ms + NORM_EPS) * g_ref[...]


def _final_call(xp, rows, w, mod3, final_g, tm=256):
    bsz, seq, dm = xp.shape
    kk = rows.shape[1]
    row = lambda b, i: (b, i, 0)
    return pl.pallas_call(
        _final_kernel,
        grid=(bsz, seq // tm),
        in_specs=[
            pl.BlockSpec((None, tm, dm), row),
            pl.BlockSpec((None, kk, tm, dm), lambda b, i: (b, 0, i, 0)),
            pl.BlockSpec((None, tm, kk), row),
            pl.BlockSpec((None, N_ADA, dm), lambda b, i: (b, 0, 0)),
            pl.BlockSpec((1, dm), lambda b, i: (0, 0)),
        ],
        out_specs=pl.BlockSpec((None, tm, dm), row),
        out_shape=jax.ShapeDtypeStruct((bsz, seq, dm), F32),
        compiler_params=pltpu.CompilerParams(
            dimension_semantics=("parallel", "parallel"), vmem_limit_bytes=VMEM_LIMIT),
        name="final",
    )(xp, rows, w, mod3, final_g.reshape(1, dm))


def _route_kernel(sc_ref, bias_ref, idx_ref, w_ref, rank_ref, cnt_ref, tri_sc, carry_sc):
    ne, tn = sc_ref.shape
    gsz = ne // N_EXPERT_GROUPS
    neg = -jnp.inf
    first = jnp.logical_and(pl.program_id(0) == 0, pl.program_id(1) == 0)

    @pl.when(first)
    def _():
        r = lax.broadcasted_iota(jnp.int32, (tn, tn), 0)
        c = lax.broadcasted_iota(jnp.int32, (tn, tn), 1)
        tri_sc[...] = jnp.where(r < c, 1.0, 0.0).astype(BF16)
        carry_sc[...] = jnp.zeros(carry_sc.shape, F32)

    s = sc_ref[...]
    sel = s + bias_ref[...]
    gs = []
    for g in range(N_EXPERT_GROUPS):
        blk = sel[g * gsz:(g + 1) * gsz]
        m1 = jnp.max(blk, axis=0, keepdims=True)
        eq = blk == m1
        n_eq = jnp.sum(jnp.where(eq, 1.0, 0.0), axis=0, keepdims=True)
        m2 = jnp.max(jnp.where(eq, neg, blk), axis=0, keepdims=True)
        gs.append(m1 + jnp.where(n_eq >= 2.0, m1, m2))
    gs = jnp.concatenate(gs, axis=0)
    gi = lax.broadcasted_iota(jnp.int32, gs.shape, 0)
    beaten = jnp.zeros(gs.shape, F32)
    for gp in range(N_EXPERT_GROUPS):
        other = gs[gp:gp + 1]
        wins = jnp.where(other > gs, 1.0, jnp.where(other == gs, jnp.where(gi > gp, 1.0, 0.0), 0.0))
        beaten = beaten + wins
    gadd = jnp.where(beaten < float(TOPK_GROUPS), 0.0, neg)
    cur = jnp.concatenate(
        [sel[g * gsz:(g + 1) * gsz] + gadd[g:g + 1] for g in range(N_EXPERT_GROUPS)], axis=0)

    eidx = lax.broadcasted_iota(jnp.int32, (ne, tn), 0).astype(F32)
    chosen = jnp.zeros((ne, tn), F32)
    idx_rows, s_rows = [], []
    for _ in range(TOP_K):
        m = jnp.max(cur, axis=0, keepdims=True)
        ik = jnp.min(jnp.where(cur == m, eidx, float(ne)), axis=0, keepdims=True)
        oh = eidx == ik
        s_rows.append(jnp.sum(jnp.where(oh, s, 0.0), axis=0, keepdims=True))
        cur = jnp.where(oh, neg, cur)
        chosen = chosen + jnp.where(oh, 1.0, 0.0)
        idx_rows.append(ik)
    sk = jnp.concatenate(s_rows, axis=0)
    w_ref[...] = sk / jnp.sum(sk, axis=0, keepdims=True) * ROUTED_SCALE
    idx_ref[...] = jnp.concatenate(idx_rows, axis=0).astype(jnp.int32)

    before = _dot(chosen.astype(BF16), tri_sc[...]) + carry_sc[...]
    rank_rows = [jnp.sum(jnp.where(eidx == ik, before, 0.0), axis=0, keepdims=True) for ik in idx_rows]
    rank_ref[...] = jnp.concatenate(rank_rows, axis=0).astype(jnp.int32)
    carry_sc[...] = carry_sc[...] + jnp.sum(chosen, axis=1, keepdims=True)
    cnt_ref[...] = carry_sc[...]


def _route_call(scores_t, router_bias, tn=512):
    bsz, ne, seq = scores_t.shape
    tok = lambda b, i: (b, 0, i)
    return pl.pallas_call(
        _route_kernel,
        grid=(bsz, seq // tn),
        in_specs=[
            pl.BlockSpec((None, ne, tn), tok),
            pl.BlockSpec((ne, 1), lambda b, i: (0, 0)),
        ],
        out_specs=[
            pl.BlockSpec((None, TOP_K, tn), tok),
            pl.BlockSpec((None, TOP_K, tn), tok),
            pl.BlockSpec((None, TOP_K, tn), tok),
            pl.BlockSpec((ne, 1), lambda b, i: (0, 0)),
        ],
        out_shape=[
            jax.ShapeDtypeStruct((bsz, TOP_K, seq), jnp.int32),
            jax.ShapeDtypeStruct((bsz, TOP_K, seq), F32),
            jax.ShapeDtypeStruct((bsz, TOP_K, seq), jnp.int32),
            jax.ShapeDtypeStruct((ne, 1), F32),
        ],
        scratch_shapes=[pltpu.VMEM((tn, tn), BF16), pltpu.VMEM((ne, 1), F32)],
        compiler_params=pltpu.CompilerParams(
            dimension_semantics=("arbitrary", "arbitrary"), vmem_limit_bytes=VMEM_LIMIT),
        name="route",
    )(scores_t, router_bias.reshape(ne, 1))


def _dispatch_tables(idx, rank, counts):
    ne = counts.shape[0]
    n_tok = idx.shape[0] * idx.shape[2]
    counts = counts.astype(jnp.int32)
    pcounts = (counts + MOE_BLOCK - 1) // MOE_BLOCK * MOE_BLOCK
    pends = jnp.cumsum(pcounts)
    pstarts = pends - pcounts
    n_blocks = -(-(n_tok * TOP_K + ne * (MOE_BLOCK - 1)) // MOE_BLOCK)
    blk_start = jnp.arange(n_blocks, dtype=jnp.int32) * MOE_BLOCK
    blk_e = jnp.minimum(jnp.sum(pends[None, :] <= blk_start[:, None], axis=1), ne - 1).astype(jnp.int32)
    onehot = blk_e[:, None] == jnp.arange(ne, dtype=jnp.int32)[None, :]
    used = jnp.sum(jnp.where(onehot, (counts - (blk_start[:, None] - pstarts[None, :])), 0), axis=1)
    n_valid = jnp.clip(used, 0, MOE_BLOCK).astype(jnp.int32)
    sel = idx[..., None] == jnp.arange(ne, dtype=jnp.int32)
    dest = jnp.sum(jnp.where(sel, pstarts, 0), axis=-1).astype(jnp.int32) + rank
    return dest, blk_e, n_valid, n_blocks


def _scatter_rows(rows, dest, n_out):
    n_tok, width = rows.shape
    k_slots = dest.shape[0]
    win = LANES
    mesh = plsc.VectorSubcoreMesh(core_axis_name="c", subcore_axis_name="s")

    @pl.kernel(out_type=jax.ShapeDtypeStruct((n_out, width), rows.dtype), mesh=mesh, scratch_types=[])
    def scatter(rows_hbm, dest_hbm, out_hbm):
        def body(rows_vmem, idx_vmem):
            pltpu.sync_copy(rows_vmem, out_hbm.at[idx_vmem.at[0]])

        pltpu.emit_pipeline(
            body,
            grid=(n_tok // win, k_slots),
            in_specs=[
                pl.BlockSpec((win, width), lambda i, k: (i, 0)),
                pl.BlockSpec((1, win), lambda i, k: (k, i)),
            ],
            out_specs=[],
            core_axis_name=("c", "s"),
            dimension_semantics=(pltpu.PARALLEL, pltpu.ARBITRARY),
        )(rows_hbm, dest_hbm)

    return scatter(rows, dest)


def kernel(x, c, w_ada, b_ada, norm1_g, w_in, w_out, lambda_q1, lambda_k1, lambda_q2, lambda_k2, subln_g,
           ssm_a_re, ssm_a_im, ssm_log_step, ssm_b_re, ssm_b_im, ssm_c_re, ssm_c_im, ssm_d, w_glu, b_glu,
           norm2_g, w_router, router_bias, w_gate, w_up, w_down, ws_gate, ws_up, ws_down, final_g):
    bsz, seq, dm = x.shape
    n_tok = bsz * seq
    aw = N_ATTN_HEADS * V_HEAD_DIM

    mod3 = _mod_call(c, w_ada[0], b_ada[0]).reshape(bsz, N_ADA, dm)

    wqkv = w_in[0][:, :3 * aw].astype(BF16)
    wut = w_in[0][:, 3 * aw:].T.astype(BF16)
    q, k, v, ut = _inproj_call(x, mod3, norm1_g[0], wqkv, wut)

    lam = (jnp.exp(jnp.sum(lambda_q1[0] * lambda_k1[0])) - jnp.exp(jnp.sum(lambda_q2[0] * lambda_k2[0]))
           + LAM_INIT).reshape(1)
    attn = _attn_call(lam, q, k, v, subln_g[0])

    kmat, wstate, wcarry, dpow = _ssm_operators(ssm_a_re[0], ssm_a_im[0], ssm_log_step[0], ssm_b_re[0],
                                                ssm_b_im[0], ssm_c_re[0], ssm_c_im[0])
    yt = _ssm_call(ut, kmat, wstate, wcarry, dpow)

    xp, h2, scores_t = _mid_call(
        x, attn, yt, ut, mod3, ssm_d[0], w_glu[0].T.astype(BF16), b_glu[0],
        w_out[0][:aw].astype(BF16), w_out[0][aw:].astype(BF16), norm2_g[0], w_router[0].T,
        ws_gate[0].astype(BF16), ws_up[0].astype(BF16), ws_down[0].astype(BF16))

    idx, w, rank, counts = _route_call(scores_t, router_bias[0])
    dest, blk_e, n_valid, n_blocks = _dispatch_tables(idx, rank, counts.reshape(-1))
    n_rows = n_blocks * MOE_BLOCK

    dest_k = jnp.swapaxes(dest, 0, 1).reshape(TOP_K, n_tok)
    dest_half = jnp.stack([2 * dest_k, 2 * dest_k + 1], axis=-1).reshape(TOP_K, 2 * n_tok)
    xs = _scatter_rows(h2.reshape(2 * n_tok, dm // 4), dest_half, 2 * n_rows).reshape(n_rows, dm // 2)

    out = _moe_call(blk_e, n_valid, xs, w_gate[0], w_up[0], w_down[0])
    rows = jnp.take(out, dest, axis=0)
    return _final_call(xp, rows, jnp.swapaxes(w, 1, 2), mod3, final_g)
```

```python
import functools
import math

import jax
import jax.numpy as jnp
from jax import lax
from jax.experimental import pallas as pl
from jax.experimental.pallas import tpu as pltpu
from jax.experimental.pallas import tpu_sc as plsc

F32 = jnp.float32
BF16 = jnp.bfloat16

N_ATTN_HEADS = 4
ATTN_HEAD_DIM = 64
V_HEAD_DIM = 128
SSM_GROUP = 16
N_SSM_GROUPS = 32
SSM_STATE = 64
N_EXPERTS = 256
TOP_K = 8
N_EXPERT_GROUPS = 8
TOPK_GROUPS = 4
ROUTED_SCALE = 2.5
NORM_EPS = 1e-6
SUBLN_EPS = 1e-5
N_ADA = 6
LAM_INIT = 0.8 - 0.6 * math.exp(-0.3 * 0)

LANES = 128
MOE_BLOCK = 256
NEG_BIG = -1e30
VMEM_LIMIT = 48 * 1024 * 1024


def _split_bf16(a):
    hi = a.astype(BF16)
    lo = (a - hi.astype(F32)).astype(BF16)
    return hi, lo


def _dot(a, b):
    return jnp.dot(a, b, preferred_element_type=F32)


def _dot_nt(a, b):
    return lax.dot_general(a, b, (((1,), (1,)), ((), ())), preferred_element_type=F32)


def _dot3(a, b):
    ah, al = _split_bf16(a)
    bh, bl = _split_bf16(b)
    return _dot(ah, bh) + _dot(ah, bl) + _dot(al, bh)


def _dot3_nt(a, b):
    ah, al = _split_bf16(a)
    bh, bl = _split_bf16(b)
    return _dot_nt(ah, bh) + _dot_nt(ah, bl) + _dot_nt(al, bh)


def _silu(x):
    return x * jax.nn.sigmoid(x)


def _gelu_tanh(x):
    c = math.sqrt(2.0 / math.pi)
    return 0.5 * x * (1.0 + jnp.tanh(c * (x + 0.044715 * (x * x * x))))


def _mod_kernel(c_ref, w_ref, b_ref, o_ref):
    cond = _silu(c_ref[...])
    o_ref[...] = _dot3(cond, w_ref[...]) + b_ref[...]


def _mod_call(c, w_ada, b_ada):
    bsz, dm = c.shape
    n_out = w_ada.shape[1]
    tn = 1024
    return pl.pallas_call(
        _mod_kernel,
        grid=(n_out // tn,),
        in_specs=[
            pl.BlockSpec((bsz, dm), lambda j: (0, 0)),
            pl.BlockSpec((dm, tn), lambda j: (0, j)),
            pl.BlockSpec((1, tn), lambda j: (0, j)),
        ],
        out_specs=pl.BlockSpec((bsz, tn), lambda j: (0, j)),
        out_shape=jax.ShapeDtypeStruct((bsz, n_out), F32),
        compiler_params=pltpu.CompilerParams(vmem_limit_bytes=VMEM_LIMIT),
        name="mod",
    )(c, w_ada, b_ada.reshape(1, n_out))


def _inproj_kernel(x_ref, mod_ref, g_ref, wqkv_ref, wut_ref, q_ref, k_ref, v_ref, ut_ref, *, aw):
    x = x_ref[...]
    ms = jnp.mean(x * x, axis=-1, keepdims=True)
    h = x * lax.rsqrt(ms + NORM_EPS) * g_ref[...]
    h = h * (1.0 + mod_ref[1:2, :]) + mod_ref[0:1, :]
    hb = h.astype(BF16)
    qkv = _dot(hb, wqkv_ref[...])
    q_ref[...] = (qkv[:, :aw] * (ATTN_HEAD_DIM ** -0.5)).astype(BF16)
    k_ref[...] = qkv[:, aw:2 * aw].astype(BF16)
    v_ref[...] = qkv[:, 2 * aw:].astype(BF16)
    ut_ref[...] = _dot_nt(wut_ref[...], hb)


def _inproj_call(x, mod3, norm_g, wqkv, wut, tm=512):
    bsz, seq, dm = x.shape
    aw = wqkv.shape[1] // 3
    sw = wut.shape[0]
    row = lambda b, i: (b, i, 0)
    return pl.pallas_call(
        functools.partial(_inproj_kernel, aw=aw),
        grid=(bsz, seq // tm),
        in_specs=[
            pl.BlockSpec((None, tm, dm), row),
            pl.BlockSpec((None, N_ADA, dm), lambda b, i: (b, 0, 0)),
            pl.BlockSpec((1, dm), lambda b, i: (0, 0)),
            pl.BlockSpec(wqkv.shape, lambda b, i: (0, 0)),
            pl.BlockSpec(wut.shape, lambda b, i: (0, 0)),
        ],
        out_specs=[
            pl.BlockSpec((None, tm, aw), row),
            pl.BlockSpec((None, tm, aw), row),
            pl.BlockSpec((None, tm, aw), row),
            pl.BlockSpec((None, sw, tm), lambda b, i: (b, 0, i)),
        ],
        out_shape=[
            jax.ShapeDtypeStruct((bsz, seq, aw), BF16),
            jax.ShapeDtypeStruct((bsz, seq, aw), BF16),
            jax.ShapeDtypeStruct((bsz, seq, aw), BF16),
            jax.ShapeDtypeStruct((bsz, sw, seq), F32),
        ],
        compiler_params=pltpu.CompilerParams(
            dimension_semantics=("parallel", "parallel"), vmem_limit_bytes=VMEM_LIMIT),
        name="inproj",
    )(x, mod3, norm_g.reshape(1, dm), wqkv, wut)


def _attn_kernel(lam_ref, q_ref, k_ref, v_ref, g_ref, o_ref, m_sc, l_sc, acc_sc, *, tq):
    i = pl.program_id(2)
    q = q_ref[...]
    lane = lax.broadcasted_iota(jnp.int32, q.shape, 1)
    zero = jnp.zeros_like(q)
    q2 = jnp.concatenate([jnp.where(lane < ATTN_HEAD_DIM, q, zero),
                          jnp.where(lane >= ATTN_HEAD_DIM, q, zero)], axis=0)
    m_sc[...] = jnp.full(m_sc.shape, NEG_BIG, F32)
    l_sc[...] = jnp.zeros(l_sc.shape, F32)
    acc_sc[...] = jnp.zeros(acc_sc.shape, F32)

    def step(j, masked):
        start = pl.multiple_of(j * tq, tq)
        kt = k_ref[pl.ds(start, tq), :]
        vt = v_ref[pl.ds(start, tq), :]
        s = _dot_nt(q2, kt)
        if masked:
            r = lax.broadcasted_iota(jnp.int32, s.shape, 0)
            c = lax.broadcasted_iota(jnp.int32, s.shape, 1)
            s = jnp.where(c <= (r & (tq - 1)), s, NEG_BIG)
        m_prev = m_sc[...]
        m_new = jnp.maximum(m_prev, jnp.max(s, axis=-1, keepdims=True))
        alpha = jnp.exp(m_prev - m_new)
        p = jnp.exp(s - jnp.concatenate([m_new] * (tq // LANES), axis=1))
        l_sc[...] = alpha * l_sc[...] + jnp.sum(p, axis=-1, keepdims=True)
        acc_sc[...] = alpha * acc_sc[...] + _dot(p.astype(BF16), vt)
        m_sc[...] = m_new

    def body(j, carry):
        step(j, False)
        return carry

    lax.fori_loop(0, i, body, 0)
    step(i, True)

    o_all = acc_sc[...] / l_sc[...]
    o = o_all[:tq] - lam_ref[0] * o_all[tq:]
    ms = jnp.mean(o * o, axis=-1, keepdims=True)
    o = o * lax.rsqrt(ms + SUBLN_EPS) * g_ref[...] * (1.0 - LAM_INIT)
    o_ref[...] = o.astype(o_ref.dtype)


def _attn_call(lam, q, k, v, subln_g, tq=512):
    bsz, seq, aw = q.shape
    nh = aw // V_HEAD_DIM
    qmap = lambda b, h, i: (b, i, h)
    kvmap = lambda b, h, i: (b, 0, h)
    return pl.pallas_call(
        functools.partial(_attn_kernel, tq=tq),
        grid=(bsz, nh, seq // tq),
        in_specs=[
            pl.BlockSpec(memory_space=pltpu.SMEM),
            pl.BlockSpec((None, tq, V_HEAD_DIM), qmap),
            pl.BlockSpec((None, seq, V_HEAD_DIM), kvmap),
            pl.BlockSpec((None, seq, V_HEAD_DIM), kvmap),
            pl.BlockSpec((1, V_HEAD_DIM), lambda b, h, i: (0, 0)),
        ],
        out_specs=pl.BlockSpec((None, tq, V_HEAD_DIM), qmap),
        out_shape=jax.ShapeDtypeStruct((bsz, seq, aw), BF16),
        scratch_shapes=[pltpu.VMEM((2 * tq, V_HEAD_DIM), F32)] * 3,
        compiler_params=pltpu.CompilerParams(
            dimension_semantics=("parallel", "parallel", "parallel"), vmem_limit_bytes=VMEM_LIMIT),
        name="attn",
    )(lam, q, k, v, subln_g.reshape(1, V_HEAD_DIM))


def _ssm_operators(a_re, a_im, log_step, b_re, b_im, c_re, c_im):
    t = LANES
    hi = lax.Precision.HIGHEST
    lam = lax.complex(jnp.minimum(a_re, -1e-4), a_im)
    delta = jnp.exp(log_step)[:, None]
    lam_bar = jnp.exp(lam * delta)
    bbar = ((lam_bar - 1.0) / lam)[:, :, None] * lax.complex(b_re, b_im)
    cmat = lax.complex(c_re, c_im)
    ld = lam * delta
    tau = jnp.arange(t + 1, dtype=F32)
    pw = jnp.exp(ld[:, :, None] * tau)
    cb = cmat[:, None, :, :] * jnp.swapaxes(bbar, 1, 2)[:, :, None, :]
    g = a_re.shape[0]
    cb = cb.reshape(g, SSM_GROUP * SSM_GROUP, SSM_STATE)
    cbcat = jnp.concatenate([cb.real, -cb.imag], axis=-1)
    pcat = jnp.concatenate([pw.real[:, :, :t], pw.imag[:, :, :t]], axis=1)
    kmat = jnp.einsum('gxp,gpt->gxt', cbcat, pcat, precision=hi)
    prev = pw[:, :, t - 1::-1][:, :, :t]
    ws = bbar.transpose(0, 2, 1)[:, :, None, :] * prev.transpose(0, 2, 1)[:, None, :, :]
    wstate = jnp.concatenate([ws.real, ws.imag], axis=-1).reshape(g, SSM_GROUP * t, 2 * SSM_STATE)
    wc = cmat.transpose(0, 2, 1)[:, :, :, None] * pw[:, :, None, 1:]
    wcarry = jnp.concatenate([wc.real, -wc.imag], axis=1).reshape(g, 2 * SSM_STATE, SSM_GROUP * t)
    rows = []
    for i in range(6):
        d = jnp.exp(ld * float(t * (1 << i)))
        rows.append(jnp.concatenate([d.real, d.real], axis=-1))
        rows.append(jnp.concatenate([-d.imag, d.imag], axis=-1))
    rows += [jnp.zeros_like(rows[0])] * 4
    dpow = jnp.stack(rows, axis=1)
    return kmat, wstate.astype(BF16), wcarry.astype(BF16), dpow


def _ssm_kernel(u_ref, k_ref, ws_ref, wc_ref, dp_ref, y_ref, m_sc):
    bsz, nch, n_chunk, t = u_ref.shape
    row = lax.broadcasted_iota(jnp.int32, (t, t), 0)
    col = lax.broadcasted_iota(jnp.int32, (t, t), 1)
    causal = col >= row

    def build(ci, carry):
        r0 = pl.multiple_of(ci * t, t)
        for co in range(nch):
            kr = k_ref[pl.ds(ci * nch + co, 1), :]
            kb = jnp.broadcast_to(kr, (t, t))
            kb = pltpu.roll(kb, 0, 1, stride=1, stride_axis=0)
            m_sc[pl.ds(r0, t), co * t:(co + 1) * t] = jnp.where(causal, kb, 0.0).astype(BF16)
        return carry

    lax.fori_loop(0, nch, build, 0)

    uflat = jnp.concatenate(
        [jnp.concatenate([u_ref[b, ci] for b in range(bsz)], axis=0) for ci in range(nch)],
        axis=1).astype(BF16)
    y = _dot(uflat, m_sc[...])
    z = _dot(uflat, ws_ref[...])
    kidx = lax.broadcasted_iota(jnp.int32, z.shape, 0) & (n_chunk - 1)
    half = z.shape[1] // 2
    shift = 1
    i = 0
    while shift < n_chunk:
        zs = jnp.where(kidx >= shift, pltpu.roll(z, shift, 0), 0.0)
        z = z + zs * dp_ref[2 * i:2 * i + 1, :] + pltpu.roll(zs, half, 1) * dp_ref[2 * i + 1:2 * i + 2, :]
        shift *= 2
        i += 1
    xin = jnp.where(kidx >= 1, pltpu.roll(z, 1, 0), 0.0)
    xh, xl = _split_bf16(xin)
    wc = wc_ref[...]
    y = y + _dot(xh, wc) + _dot(xl, wc)
    for b in range(bsz):
        for co in range(nch):
            y_ref[b, co] = y[b * n_chunk:(b + 1) * n_chunk, co * t:(co + 1) * t]


def _ssm_call(ut, kmat, wstate, wcarry, dpow):
    bsz, sw, seq = ut.shape
    n_groups = sw // SSM_GROUP
    n_chunk = seq // LANES
    assert n_chunk & (n_chunk - 1) == 0 and n_chunk <= 64
    u4 = ut.reshape(bsz, sw, n_chunk, LANES)
    blk = (bsz, SSM_GROUP, n_chunk, LANES)
    gmap = lambda g: (0, g, 0, 0)
    pmap = lambda g: (g, 0, 0)
    y4 = pl.pallas_call(
        _ssm_kernel,
        grid=(n_groups,),
        in_specs=[
            pl.BlockSpec(blk, gmap),
            pl.BlockSpec((None,) + kmat.shape[1:], pmap),
            pl.BlockSpec((None,) + wstate.shape[1:], pmap),
            pl.BlockSpec((None,) + wcarry.shape[1:], pmap),
            pl.BlockSpec((None,) + dpow.shape[1:], pmap),
        ],
        out_specs=pl.BlockSpec(blk, gmap),
        out_shape=jax.ShapeDtypeStruct(u4.shape, F32),
        scratch_shapes=[pltpu.VMEM((SSM_GROUP * LANES, SSM_GROUP * LANES), BF16)],
        compiler_params=pltpu.CompilerParams(
            dimension_semantics=("parallel",), vmem_limit_bytes=VMEM_LIMIT),
        name="ssm",
    )(u4, kmat, wstate, wcarry, dpow)
    return y4.reshape(bsz, sw, seq)


def _mid_kernel(x_ref, attn_ref, yt_ref, ut_ref, mod_ref, dsk_ref, wglut_ref, bglu_ref, wo1_ref, wo2_ref,
                g2_ref, wrt_ref, wsg_ref, wsu_ref, wsd_ref, xp_ref, h2_ref, lg_ref):
    gt = _gelu_tanh(yt_ref[...] + dsk_ref[...] * ut_ref[...])
    zt = _dot(wglut_ref[...], gt.astype(BF16)) + bglu_ref[...]
    st = gt * jax.nn.sigmoid(zt)
    s = st.T.astype(BF16)
    mix = _dot(attn_ref[...], wo1_ref[...]) + _dot(s, wo2_ref[...])
    x1 = x_ref[...] + mod_ref[2:3, :] * mix
    ms = jnp.mean(x1 * x1, axis=-1, keepdims=True)
    h2 = x1 * lax.rsqrt(ms + NORM_EPS) * g2_ref[...]
    h2 = h2 * (1.0 + mod_ref[4:5, :]) + mod_ref[3:4, :]
    hb = h2.astype(BF16)
    bits = lax.bitcast_convert_type(hb.astype(F32), jnp.uint32)
    half = bits.shape[1] // 2
    packed = (bits[:, :half] >> 16) | (bits[:, half:] & jnp.uint32(0xFFFF0000))
    h2_ref[0] = packed[:, :half // 2]
    h2_ref[1] = packed[:, half // 2:]
    lg_ref[...] = jax.nn.sigmoid(_dot3_nt(wrt_ref[...], h2))
    sh = _silu(_dot(hb, wsg_ref[...])) * _dot(hb, wsu_ref[...])
    shared = _dot(sh.astype(BF16), wsd_ref[...])
    xp_ref[...] = x1 + mod_ref[5:6, :] * shared


def _mid_call(x, attn, yt, ut, mod3, dsk, wglut, bglu, wo1, wo2, g2, wrt, wsg, wsu, wsd, tm=512):
    bsz, seq, dm = x.shape
    aw = attn.shape[2]
    sw = yt.shape[1]
    ne = wrt.shape[0]
    row = lambda b, i: (b, i, 0)
    colm = lambda b, i: (b, 0, i)
    full = lambda a: pl.BlockSpec(a.shape, lambda b, i: (0,) * a.ndim)
    dsk = dsk.reshape(sw, 1)
    bglu = bglu.reshape(sw, 1)
    g2 = g2.reshape(1, dm)
    return pl.pallas_call(
        _mid_kernel,
        grid=(bsz, seq // tm),
        in_specs=[
            pl.BlockSpec((None, tm, dm), row),
            pl.BlockSpec((None, tm, aw), row),
            pl.BlockSpec((None, sw, tm), colm),
            pl.BlockSpec((None, sw, tm), colm),
            pl.BlockSpec((None, N_ADA, dm), lambda b, i: (b, 0, 0)),
            full(dsk), full(wglut), full(bglu), full(wo1), full(wo2), full(g2), full(wrt),
            full(wsg), full(wsu), full(wsd),
        ],
        out_specs=[
            pl.BlockSpec((None, tm, dm), row),
            pl.BlockSpec((2, None, tm, dm // 4), lambda b, i: (0, b, i, 0)),
            pl.BlockSpec((None, ne, tm), colm),
        ],
        out_shape=[
            jax.ShapeDtypeStruct((bsz, seq, dm), F32),
            jax.ShapeDtypeStruct((2, bsz, seq, dm // 4), jnp.uint32),
            jax.ShapeDtypeStruct((bsz, ne, seq), F32),
        ],
        compiler_params=pltpu.CompilerParams(
            dimension_semantics=("parallel", "parallel"), vmem_limit_bytes=VMEM_LIMIT),
        name="mid",
    )(x, attn, yt, ut, mod3, dsk, wglut, bglu, wo1, wo2, g2, wrt, wsg, wsu, wsd)


def _moe_kernel(be_ref, nv_ref, xs_ref, wg_ref, wu_ref, wd_ref, o_ref, wgb, wub, wdb):
    i = pl.program_id(0)
    changed = jnp.logical_or(i == 0, be_ref[i] != be_ref[jnp.maximum(i - 1, 0)])

    @pl.when(changed)
    def _():
        wgb[...] = wg_ref[...].astype(BF16)
        wub[...] = wu_ref[...].astype(BF16)
        wdb[...] = wd_ref[...].astype(BF16)

    q = xs_ref.shape[2]
    valid = lax.broadcasted_iota(jnp.int32, xs_ref.shape[1:], 0) < nv_ref[i]
    gate = up = None
    for h in range(2):
        xu = xs_ref[h]
        lo = jnp.where(valid, lax.bitcast_convert_type(xu << 16, F32), 0.0).astype(BF16)
        hi = jnp.where(valid, lax.bitcast_convert_type(xu & jnp.uint32(0xFFFF0000), F32), 0.0).astype(BF16)
        for xpart, c0 in ((lo, h * q), (hi, (2 + h) * q)):
            g = _dot(xpart, wgb[c0:c0 + q, :])
            u = _dot(xpart, wub[c0:c0 + q, :])
            gate = g if gate is None else gate + g
            up = u if up is None else up + u
    hb = _silu(gate) * up
    o_ref[...] = _dot(hb.astype(BF16), wdb[...]).astype(o_ref.dtype)


def _moe_call(blk_e, n_valid, xs, w_gate, w_up, w_down):
    n_rows = xs.shape[1]
    dm, de = w_gate.shape[1:]
    n_blocks = n_rows // MOE_BLOCK
    grid_spec = pltpu.PrefetchScalarGridSpec(
        num_scalar_prefetch=2,
        grid=(n_blocks,),
        in_specs=[
            pl.BlockSpec((2, MOE_BLOCK, dm // 4), lambda i, be, nv: (0, i, 0)),
            pl.BlockSpec((None, dm, de), lambda i, be, nv: (be[i], 0, 0)),
            pl.BlockSpec((None, dm, de), lambda i, be, nv: (be[i], 0, 0)),
            pl.BlockSpec((None, de, dm), lambda i, be, nv: (be[i], 0, 0)),
        ],
        out_specs=pl.BlockSpec((MOE_BLOCK, dm), lambda i, be, nv: (i, 0)),
        scratch_shapes=[pltpu.VMEM((dm, de), BF16), pltpu.VMEM((dm, de), BF16), pltpu.VMEM((de, dm), BF16)],
    )
    return pl.pallas_call(
        _moe_kernel,
        grid_spec=grid_spec,
        out_shape=jax.ShapeDtypeStruct((n_rows, dm), BF16),
        compiler_params=pltpu.CompilerParams(
            dimension_semantics=("arbitrary",), vmem_limit_bytes=VMEM_LIMIT),
        name="moe",
    )(blk_e, n_valid, xs, w_gate, w_up, w_down)


def _final_kernel(xp_ref, r_ref, w_ref, mod_ref, g_ref, o_ref):
    w = w_ref[...]
    routed = w[:, 0:1] * r_ref[0].astype(F32)
    for k in range(1, r_ref.shape[0]):
        routed = routed + w[:, k:k + 1] * r_ref[k].astype(F32)
    x = xp_ref[...] + mod_ref[5:6, :] * routed
    ms = jnp.mean(x * x, axis=-1, keepdims=True)
    o_ref[...] = x * lax.rsqrt(ms + NORM_EPS) * g_ref[...]


def _final_call(xp, rows, w, mod3, final_g, tm=256):
    bsz, seq, dm = xp.shape
    kk = rows.shape[1]
    row = lambda b, i: (b, i, 0)
    return pl.pallas_call(
        _final_kernel,
        grid=(bsz, seq // tm),
        in_specs=[
            pl.BlockSpec((None, tm, dm), row),
            pl.BlockSpec((None, kk, tm, dm), lambda b, i: (b, 0, i, 0)),
            pl.BlockSpec((None, tm, kk), row),
            pl.BlockSpec((None, N_ADA, dm), lambda b, i: (b, 0, 0)),
            pl.BlockSpec((1, dm), lambda b, i: (0, 0)),
        ],
        out_specs=pl.BlockSpec((None, tm, dm), row),
        out_shape=jax.ShapeDtypeStruct((bsz, seq, dm), F32),
        compiler_params=pltpu.CompilerParams(
            dimension_semantics=("parallel", "parallel"), vmem_limit_bytes=VMEM_LIMIT),
        name="final",
    )(xp, rows, w, mod3, final_g.reshape(1, dm))


def _route_kernel(sc_ref, bias_ref, idx_ref, w_ref, rank_ref, cnt_ref, tri_sc, carry_sc):
    ne, tn = sc_ref.shape
    gsz = ne // N_EXPERT_GROUPS
    neg = -jnp.inf
    first = jnp.logical_and(pl.program_id(0) == 0, pl.program_id(1) == 0)

    @pl.when(first)
    def _():
        r = lax.broadcasted_iota(jnp.int32, (tn, tn), 0)
        c = lax.broadcasted_iota(jnp.int32, (tn, tn), 1)
        tri_sc[...] = jnp.where(r < c, 1.0, 0.0).astype(BF16)
        carry_sc[...] = jnp.zeros(carry_sc.shape, F32)

    s = sc_ref[...]
    sel = s + bias_ref[...]
    gs = []
    for g in range(N_EXPERT_GROUPS):
        blk = sel[g * gsz:(g + 1) * gsz]
        m1 = jnp.max(blk, axis=0, keepdims=True)
        eq = blk == m1
        n_eq = jnp.sum(jnp.where(eq, 1.0, 0.0), axis=0, keepdims=True)
        m2 = jnp.max(jnp.where(eq, neg, blk), axis=0, keepdims=True)
        gs.append(m1 + jnp.where(n_eq >= 2.0, m1, m2))
    gs = jnp.concatenate(gs, axis=0)
    gi = lax.broadcasted_iota(jnp.int32, gs.shape, 0)
    beaten = jnp.zeros(gs.shape, F32)
    for gp in range(N_EXPERT_GROUPS):
        other = gs[gp:gp + 1]
        wins = jnp.where(other > gs, 1.0, jnp.where(other == gs, jnp.where(gi > gp, 1.0, 0.0), 0.0))
        beaten = beaten + wins
    gadd = jnp.where(beaten < float(TOPK_GROUPS), 0.0, neg)
    cur = jnp.concatenate(
        [sel[g * gsz:(g + 1) * gsz] + gadd[g:g + 1] for g in range(N_EXPERT_GROUPS)], axis=0)

    eidx = lax.broadcasted_iota(jnp.int32, (ne, tn), 0).astype(F32)
    chosen = jnp.zeros((ne, tn), F32)
    idx_rows, s_rows = [], []
    for _ in range(TOP_K):
        m = jnp.max(cur, axis=0, keepdims=True)
        ik = jnp.min(jnp.where(cur == m, eidx, float(ne)), axis=0, keepdims=True)
        oh = eidx == ik
        s_rows.append(jnp.sum(jnp.where(oh, s, 0.0), axis=0, keepdims=True))
        cur = jnp.where(oh, neg, cur)
        chosen = chosen + jnp.where(oh, 1.0, 0.0)
        idx_rows.append(ik)
    sk = jnp.concatenate(s_rows, axis=0)
    w_ref[...] = sk / jnp.sum(sk, axis=0, keepdims=True) * ROUTED_SCALE
    idx_ref[...] = jnp.concatenate(idx_rows, axis=0).astype(jnp.int32)

    before = _dot(chosen.astype(BF16), tri_sc[...]) + carry_sc[...]
    rank_rows = [jnp.sum(jnp.where(eidx == ik, before, 0.0), axis=0, keepdims=True) for ik in idx_rows]
    rank_ref[...] = jnp.concatenate(rank_rows, axis=0).astype(jnp.int32)
    carry_sc[...] = carry_sc[...] + jnp.sum(chosen, axis=1, keepdims=True)
    cnt_ref[...] = carry_sc[...]


def _route_call(scores_t, router_bias, tn=512):
    bsz, ne, seq = scores_t.shape
    tok = lambda b, i: (b, 0, i)
    return pl.pallas_call(
        _route_kernel,
        grid=(bsz, seq // tn),
        in_specs=[
            pl.BlockSpec((None, ne, tn), tok),
            pl.BlockSpec((ne, 1), lambda b, i: (0, 0)),
        ],
        out_specs=[
            pl.BlockSpec((None, TOP_K, tn), tok),
            pl.BlockSpec((None, TOP_K, tn), tok),
            pl.BlockSpec((None, TOP_K, tn), tok),
            pl.BlockSpec((ne, 1), lambda b, i: (0, 0)),
        ],
        out_shape=[
            jax.ShapeDtypeStruct((bsz, TOP_K, seq), jnp.int32),
            jax.ShapeDtypeStruct((bsz, TOP_K, seq), F32),
            jax.ShapeDtypeStruct((bsz, TOP_K, seq), jnp.int32),
            jax.ShapeDtypeStruct((ne, 1), F32),
        ],
        scratch_shapes=[pltpu.VMEM((tn, tn), BF16), pltpu.VMEM((ne, 1), F32)],
        compiler_params=pltpu.CompilerParams(
            dimension_semantics=("arbitrary", "arbitrary"), vmem_limit_bytes=VMEM_LIMIT),
        name="route",
    )(scores_t, router_bias.reshape(ne, 1))


def _dispatch_tables(idx, rank, counts):
    ne = counts.shape[0]
    n_tok = idx.shape[0] * idx.shape[2]
    counts = counts.astype(jnp.int32)
    pcounts = (counts + MOE_BLOCK - 1) // MOE_BLOCK * MOE_BLOCK
    pends = jnp.cumsum(pcounts)
    pstarts = pends - pcounts
    n_blocks = -(-(n_tok * TOP_K + ne * (MOE_BLOCK - 1)) // MOE_BLOCK)
    blk_start = jnp.arange(n_blocks, dtype=jnp.int32) * MOE_BLOCK
    blk_e = jnp.minimum(jnp.sum(pends[None, :] <= blk_start[:, None], axis=1), ne - 1).astype(jnp.int32)
    onehot = blk_e[:, None] == jnp.arange(ne, dtype=jnp.int32)[None, :]
    used = jnp.sum(jnp.where(onehot, (counts - (blk_start[:, None] - pstarts[None, :])), 0), axis=1)
    n_valid = jnp.clip(used, 0, MOE_BLOCK).astype(jnp.int32)
    dest = _dest_call(pstarts.astype(jnp.int32), idx, rank)
    return dest, blk_e, n_valid, n_blocks


def _dest_kernel(ps_ref, idx_ref, rank_ref, o_ref):
    idx = idx_ref[...]

    def body(e, acc):
        return jnp.where(idx == e, ps_ref[e], acc)

    o_ref[...] = rank_ref[...] + lax.fori_loop(0, ps_ref.shape[0], body, jnp.zeros(idx.shape, jnp.int32))


def _dest_call(pstarts, idx, rank, tn=2048):
    bsz, kk, seq = idx.shape
    tok = lambda b, i, ps: (b, 0, i)
    grid_spec = pltpu.PrefetchScalarGridSpec(
        num_scalar_prefetch=1,
        grid=(bsz, seq // tn),
        in_specs=[pl.BlockSpec((None, kk, tn), tok), pl.BlockSpec((None, kk, tn), tok)],
        out_specs=pl.BlockSpec((None, kk, tn), tok),
    )
    return pl.pallas_call(
        _dest_kernel,
        grid_spec=grid_spec,
        out_shape=jax.ShapeDtypeStruct(idx.shape, jnp.int32),
        compiler_params=pltpu.CompilerParams(dimension_semantics=("parallel", "parallel")),
        name="dest",
    )(pstarts, idx, rank)


def _scatter_rows(rows, dest, n_out):
    n_tok, width = rows.shape
    k_slots = dest.shape[0]
    win = LANES
    mesh = plsc.VectorSubcoreMesh(core_axis_name="c", subcore_axis_name="s")

    @pl.kernel(out_type=jax.ShapeDtypeStruct((n_out, width), rows.dtype), mesh=mesh, scratch_types=[])
    def scatter(rows_hbm, dest_hbm, out_hbm):
        def body(rows_vmem, idx_vmem):
            pltpu.sync_copy(rows_vmem, out_hbm.at[idx_vmem.at[0]])

        pltpu.emit_pipeline(
            body,
            grid=(n_tok // win, k_slots),
            in_specs=[
                pl.BlockSpec((win, width), lambda i, k: (i, 0)),
                pl.BlockSpec((1, win), lambda i, k: (k, i)),
            ],
            out_specs=[],
            core_axis_name=("c", "s"),
            dimension_semantics=(pltpu.PARALLEL, pltpu.ARBITRARY),
        )(rows_hbm, dest_hbm)

    return scatter(rows, dest)


def kernel(x, c, w_ada, b_ada, norm1_g, w_in, w_out, lambda_q1, lambda_k1, lambda_q2, lambda_k2, subln_g,
           ssm_a_re, ssm_a_im, ssm_log_step, ssm_b_re, ssm_b_im, ssm_c_re, ssm_c_im, ssm_d, w_glu, b_glu,
           norm2_g, w_router, router_bias, w_gate, w_up, w_down, ws_gate, ws_up, ws_down, final_g):
    bsz, seq, dm = x.shape
    n_tok = bsz * seq
    aw = N_ATTN_HEADS * V_HEAD_DIM

    mod3 = _mod_call(c, w_ada[0], b_ada[0]).reshape(bsz, N_ADA, dm)

    wqkv = w_in[0][:, :3 * aw].astype(BF16)
    wut = w_in[0][:, 3 * aw:].T.astype(BF16)
    q, k, v, ut = _inproj_call(x, mod3, norm1_g[0], wqkv, wut)

    lam = (jnp.exp(jnp.sum(lambda_q1[0] * lambda_k1[0])) - jnp.exp(jnp.sum(lambda_q2[0] * lambda_k2[0]))
           + LAM_INIT).reshape(1)
    attn = _attn_call(lam, q, k, v, subln_g[0])

    kmat, wstate, wcarry, dpow = _ssm_operators(ssm_a_re[0], ssm_a_im[0], ssm_log_step[0], ssm_b_re[0],
                                                ssm_b_im[0], ssm_c_re[0], ssm_c_im[0])
    yt = _ssm_call(ut, kmat, wstate, wcarry, dpow)

    xp, h2, scores_t = _mid_call(
        x, attn, yt, ut, mod3, ssm_d[0], w_glu[0].T.astype(BF16), b_glu[0],
        w_out[0][:aw].astype(BF16), w_out[0][aw:].astype(BF16), norm2_g[0], w_router[0].T,
        ws_gate[0].astype(BF16), ws_up[0].astype(BF16), ws_down[0].astype(BF16))

    idx, w, rank, counts = _route_call(scores_t, router_bias[0])
    dest, blk_e, n_valid, n_blocks = _dispatch_tables(idx, rank, counts.reshape(-1))
    n_rows = n_blocks * MOE_BLOCK

    dest_k = jnp.swapaxes(dest, 0, 1).reshape(TOP_K, n_tok)
    dest_half = jnp.concatenate([dest_k, dest_k + n_rows], axis=1)
    xs = _scatter_rows(h2.reshape(2 * n_tok, dm // 4), dest_half, 2 * n_rows).reshape(2, n_rows, dm // 4)

    out = _moe_call(blk_e, n_valid, xs, w_gate[0], w_up[0], w_down[0])
    rows = jnp.take(out, dest, axis=0)
    return _final_call(xp, rows, jnp.swapaxes(w, 1, 2), mod3, final_g)
```

```python
import functools
import math

import jax
import jax.numpy as jnp
from jax import lax
from jax.experimental import pallas as pl
from jax.experimental.pallas import tpu as pltpu
from jax.experimental.pallas import tpu_sc as plsc

F32 = jnp.float32
BF16 = jnp.bfloat16

N_ATTN_HEADS = 4
ATTN_HEAD_DIM = 64
V_HEAD_DIM = 128
SSM_GROUP = 16
N_SSM_GROUPS = 32
SSM_STATE = 64
N_EXPERTS = 256
TOP_K = 8
N_EXPERT_GROUPS = 8
TOPK_GROUPS = 4
ROUTED_SCALE = 2.5
NORM_EPS = 1e-6
SUBLN_EPS = 1e-5
N_ADA = 6
LAM_INIT = 0.8 - 0.6 * math.exp(-0.3 * 0)

LANES = 128
MOE_BLOCK = 256
NEG_BIG = -1e30
LOG2_E = math.log2(math.e)
VMEM_LIMIT = 48 * 1024 * 1024


def _split_bf16(a):
    hi = a.astype(BF16)
    lo = (a - hi.astype(F32)).astype(BF16)
    return hi, lo


def _dot(a, b):
    return jnp.dot(a, b, preferred_element_type=F32)


def _dot_nt(a, b):
    return lax.dot_general(a, b, (((1,), (1,)), ((), ())), preferred_element_type=F32)


def _dot3(a, b):
    ah, al = _split_bf16(a)
    bh, bl = _split_bf16(b)
    return _dot(ah, bh) + _dot(ah, bl) + _dot(al, bh)


def _dot3_nt(a, b):
    ah, al = _split_bf16(a)
    bh, bl = _split_bf16(b)
    return _dot_nt(ah, bh) + _dot_nt(ah, bl) + _dot_nt(al, bh)


def _silu(x):
    return x * jax.nn.sigmoid(x)


def _gelu_tanh(x):
    c = math.sqrt(2.0 / math.pi)
    return 0.5 * x * (1.0 + jnp.tanh(c * (x + 0.044715 * (x * x * x))))


def _mod_kernel(c_ref, w_ref, b_ref, o_ref):
    cond = _silu(c_ref[...])
    o_ref[...] = _dot3(cond, w_ref[...]) + b_ref[...]


def _mod_call(c, w_ada, b_ada):
    bsz, dm = c.shape
    n_out = w_ada.shape[1]
    tn = 1024
    return pl.pallas_call(
        _mod_kernel,
        grid=(n_out // tn,),
        in_specs=[
            pl.BlockSpec((bsz, dm), lambda j: (0, 0)),
            pl.BlockSpec((dm, tn), lambda j: (0, j)),
            pl.BlockSpec((1, tn), lambda j: (0, j)),
        ],
        out_specs=pl.BlockSpec((bsz, tn), lambda j: (0, j)),
        out_shape=jax.ShapeDtypeStruct((bsz, n_out), F32),
        compiler_params=pltpu.CompilerParams(vmem_limit_bytes=VMEM_LIMIT),
        name="mod",
    )(c, w_ada, b_ada.reshape(1, n_out))


def _inproj_kernel(x_ref, mod_ref, g_ref, wqkv_ref, wut_ref, q_ref, k_ref, v_ref, ut_ref, *, aw):
    x = x_ref[...]
    ms = jnp.mean(x * x, axis=-1, keepdims=True)
    h = x * lax.rsqrt(ms + NORM_EPS) * g_ref[...]
    h = h * (1.0 + mod_ref[1:2, :]) + mod_ref[0:1, :]
    hb = h.astype(BF16)
    qkv = _dot(hb, wqkv_ref[...])
    q_ref[...] = (qkv[:, :aw] * (LOG2_E * ATTN_HEAD_DIM ** -0.5)).astype(BF16)
    k_ref[...] = qkv[:, aw:2 * aw].astype(BF16)
    v_ref[...] = qkv[:, 2 * aw:].astype(BF16)
    ut_ref[...] = _dot_nt(wut_ref[...], hb)


def _inproj_call(x, mod3, norm_g, wqkv, wut, tm=512):
    bsz, seq, dm = x.shape
    aw = wqkv.shape[1] // 3
    sw = wut.shape[0]
    row = lambda b, i: (b, i, 0)
    return pl.pallas_call(
        functools.partial(_inproj_kernel, aw=aw),
        grid=(bsz, seq // tm),
        in_specs=[
            pl.BlockSpec((None, tm, dm), row),
            pl.BlockSpec((None, N_ADA, dm), lambda b, i: (b, 0, 0)),
            pl.BlockSpec((1, dm), lambda b, i: (0, 0)),
            pl.BlockSpec(wqkv.shape, lambda b, i: (0, 0)),
            pl.BlockSpec(wut.shape, lambda b, i: (0, 0)),
        ],
        out_specs=[
            pl.BlockSpec((None, tm, aw), row),
            pl.BlockSpec((None, tm, aw), row),
            pl.BlockSpec((None, tm, aw), row),
            pl.BlockSpec((None, sw, tm), lambda b, i: (b, 0, i)),
        ],
        out_shape=[
            jax.ShapeDtypeStruct((bsz, seq, aw), BF16),
            jax.ShapeDtypeStruct((bsz, seq, aw), BF16),
            jax.ShapeDtypeStruct((bsz, seq, aw), BF16),
            jax.ShapeDtypeStruct((bsz, sw, seq), F32),
        ],
        compiler_params=pltpu.CompilerParams(
            dimension_semantics=("parallel", "parallel"), vmem_limit_bytes=VMEM_LIMIT),
        name="inproj",
    )(x, mod3, norm_g.reshape(1, dm), wqkv, wut)


def _attn_kernel(lam_ref, q_ref, k_ref, v_ref, g_ref, o_ref, m_sc, l_sc, acc_sc, *, tq):
    i = pl.program_id(2)
    q = q_ref[...]
    lane = lax.broadcasted_iota(jnp.int32, q.shape, 1)
    zero = jnp.zeros_like(q)
    q2 = jnp.concatenate([jnp.where(lane < ATTN_HEAD_DIM, q, zero),
                          jnp.where(lane >= ATTN_HEAD_DIM, q, zero)], axis=0)
    m_sc[...] = jnp.full(m_sc.shape, NEG_BIG, F32)
    l_sc[...] = jnp.zeros(l_sc.shape, F32)
    acc_sc[...] = jnp.zeros(acc_sc.shape, F32)

    def step(j, masked):
        start = pl.multiple_of(j * tq, tq)
        kt = k_ref[pl.ds(start, tq), :]
        vt = v_ref[pl.ds(start, tq), :]
        s = _dot_nt(q2, kt)
        if masked:
            r = lax.broadcasted_iota(jnp.int32, s.shape, 0)
            c = lax.broadcasted_iota(jnp.int32, s.shape, 1)
            s = jnp.where(c <= (r & (tq - 1)), s, NEG_BIG)
        m_prev = m_sc[...]
        m_new = jnp.maximum(m_prev, jnp.max(s, axis=-1, keepdims=True))
        alpha = jnp.exp2(m_prev - m_new)
        p = jnp.exp2(s - jnp.concatenate([m_new] * (tq // LANES), axis=1))
        psum = p[:, :LANES]
        for c0 in range(LANES, tq, LANES):
            psum = psum + p[:, c0:c0 + LANES]
        l_sc[...] = alpha * l_sc[...] + psum
        acc_sc[...] = alpha * acc_sc[...] + _dot(p.astype(BF16), vt)
        m_sc[...] = m_new

    def body(j, carry):
        step(j, False)
        return carry

    lax.fori_loop(0, i, body, 0)
    step(i, True)

    o_all = acc_sc[...] / jnp.sum(l_sc[...], axis=-1, keepdims=True)
    o = o_all[:tq] - lam_ref[0] * o_all[tq:]
    ms = jnp.mean(o * o, axis=-1, keepdims=True)
    o = o * lax.rsqrt(ms + SUBLN_EPS) * g_ref[...] * (1.0 - LAM_INIT)
    o_ref[...] = o.astype(o_ref.dtype)


def _attn_call(lam, q, k, v, subln_g, tq=1024):
    bsz, seq, aw = q.shape
    nh = aw // V_HEAD_DIM
    qmap = lambda b, h, i: (b, i, h)
    kvmap = lambda b, h, i: (b, 0, h)
    return pl.pallas_call(
        functools.partial(_attn_kernel, tq=tq),
        grid=(bsz, nh, seq // tq),
        in_specs=[
            pl.BlockSpec(memory_space=pltpu.SMEM),
            pl.BlockSpec((None, tq, V_HEAD_DIM), qmap),
            pl.BlockSpec((None, seq, V_HEAD_DIM), kvmap),
            pl.BlockSpec((None, seq, V_HEAD_DIM), kvmap),
            pl.BlockSpec((1, V_HEAD_DIM), lambda b, h, i: (0, 0)),
        ],
        out_specs=pl.BlockSpec((None, tq, V_HEAD_DIM), qmap),
        out_shape=jax.ShapeDtypeStruct((bsz, seq, aw), BF16),
        scratch_shapes=[pltpu.VMEM((2 * tq, V_HEAD_DIM), F32)] * 3,
        compiler_params=pltpu.CompilerParams(
            dimension_semantics=("parallel", "parallel", "parallel"), vmem_limit_bytes=VMEM_LIMIT),
        name="attn",
    )(lam, q, k, v, subln_g.reshape(1, V_HEAD_DIM))


def _ssm_operators(a_re, a_im, log_step, b_re, b_im, c_re, c_im):
    t = LANES
    hi = lax.Precision.HIGHEST
    lam = lax.complex(jnp.minimum(a_re, -1e-4), a_im)
    delta = jnp.exp(log_step)[:, None]
    lam_bar = jnp.exp(lam * delta)
    bbar = ((lam_bar - 1.0) / lam)[:, :, None] * lax.complex(b_re, b_im)
    cmat = lax.complex(c_re, c_im)
    ld = lam * delta
    tau = jnp.arange(t + 1, dtype=F32)
    pw = jnp.exp(ld[:, :, None] * tau)
    cb = cmat[:, None, :, :] * jnp.swapaxes(bbar, 1, 2)[:, :, None, :]
    g = a_re.shape[0]
    cb = cb.reshape(g, SSM_GROUP * SSM_GROUP, SSM_STATE)
    cbcat = jnp.concatenate([cb.real, -cb.imag], axis=-1)
    pcat = jnp.concatenate([pw.real[:, :, :t], pw.imag[:, :, :t]], axis=1)
    kmat = jnp.einsum('gxp,gpt->gxt', cbcat, pcat, precision=hi)
    prev = pw[:, :, t - 1::-1][:, :, :t]
    ws = bbar.transpose(0, 2, 1)[:, :, None, :] * prev.transpose(0, 2, 1)[:, None, :, :]
    wstate = jnp.concatenate([ws.real, ws.imag], axis=-1).reshape(g, SSM_GROUP * t, 2 * SSM_STATE)
    wc = cmat.transpose(0, 2, 1)[:, :, :, None] * pw[:, :, None, 1:]
    wcarry = jnp.concatenate([wc.real, -wc.imag], axis=1).reshape(g, 2 * SSM_STATE, SSM_GROUP * t)
    rows = []
    for i in range(6):
        d = jnp.exp(ld * float(t * (1 << i)))
        rows.append(jnp.concatenate([d.real, d.real], axis=-1))
        rows.append(jnp.concatenate([-d.imag, d.imag], axis=-1))
    rows += [jnp.zeros_like(rows[0])] * 4
    dpow = jnp.stack(rows, axis=1)
    return kmat, wstate.astype(BF16), wcarry.astype(BF16), dpow


def _ssm_kernel(u_ref, k_ref, ws_ref, wc_ref, dp_ref, y_ref, m_sc):
    bsz, nch, n_chunk, t = u_ref.shape
    row = lax.broadcasted_iota(jnp.int32, (t, t), 0)
    col = lax.broadcasted_iota(jnp.int32, (t, t), 1)
    causal = col >= row

    def build(ci, carry):
        r0 = pl.multiple_of(ci * t, t)
        for co in range(nch):
            kr = k_ref[pl.ds(ci * nch + co, 1), :]
            kb = jnp.broadcast_to(kr, (t, t))
            kb = pltpu.roll(kb, 0, 1, stride=1, stride_axis=0)
            m_sc[pl.ds(r0, t), co * t:(co + 1) * t] = jnp.where(causal, kb, 0.0).astype(BF16)
        return carry

    lax.fori_loop(0, nch, build, 0)

    uflat = jnp.concatenate(
        [jnp.concatenate([u_ref[b, ci] for b in range(bsz)], axis=0) for ci in range(nch)],
        axis=1).astype(BF16)
    y = _dot(uflat, m_sc[...])
    z = _dot(uflat, ws_ref[...])
    kidx = lax.broadcasted_iota(jnp.int32, z.shape, 0) & (n_chunk - 1)
    half = z.shape[1] // 2
    shift = 1
    i = 0
    while shift < n_chunk:
        zs = jnp.where(kidx >= shift, pltpu.roll(z, shift, 0), 0.0)
        z = z + zs * dp_ref[2 * i:2 * i + 1, :] + pltpu.roll(zs, half, 1) * dp_ref[2 * i + 1:2 * i + 2, :]
        shift *= 2
        i += 1
    xin = jnp.where(kidx >= 1, pltpu.roll(z, 1, 0), 0.0)
    xh, xl = _split_bf16(xin)
    wc = wc_ref[...]
    y = y + _dot(xh, wc) + _dot(xl, wc)
    for b in range(bsz):
        for co in range(nch):
            y_ref[b, co] = y[b * n_chunk:(b + 1) * n_chunk, co * t:(co + 1) * t]


def _ssm_call(ut, kmat, wstate, wcarry, dpow):
    bsz, sw, seq = ut.shape
    n_groups = sw // SSM_GROUP
    n_chunk = seq // LANES
    assert n_chunk & (n_chunk - 1) == 0 and n_chunk <= 64
    u4 = ut.reshape(bsz, sw, n_chunk, LANES)
    blk = (bsz, SSM_GROUP, n_chunk, LANES)
    gmap = lambda g: (0, g, 0, 0)
    pmap = lambda g: (g, 0, 0)
    y4 = pl.pallas_call(
        _ssm_kernel,
        grid=(n_groups,),
        in_specs=[
            pl.BlockSpec(blk, gmap),
            pl.BlockSpec((None,) + kmat.shape[1:], pmap),
            pl.BlockSpec((None,) + wstate.shape[1:], pmap),
            pl.BlockSpec((None,) + wcarry.shape[1:], pmap),
            pl.BlockSpec((None,) + dpow.shape[1:], pmap),
        ],
        out_specs=pl.BlockSpec(blk, gmap),
        out_shape=jax.ShapeDtypeStruct(u4.shape, F32),
        scratch_shapes=[pltpu.VMEM((SSM_GROUP * LANES, SSM_GROUP * LANES), BF16)],
        compiler_params=pltpu.CompilerParams(
            dimension_semantics=("parallel",), vmem_limit_bytes=VMEM_LIMIT),
        name="ssm",
    )(u4, kmat, wstate, wcarry, dpow)
    return y4.reshape(bsz, sw, seq)


def _mid_kernel(x_ref, attn_ref, yt_ref, ut_ref, mod_ref, dsk_ref, wglut_ref, bglu_ref, wo1_ref, wo2_ref,
                g2_ref, wrt_ref, wsg_ref, wsu_ref, wsd_ref, xp_ref, h2_ref, lg_ref):
    gt = _gelu_tanh(yt_ref[...] + dsk_ref[...] * ut_ref[...])
    zt = _dot(wglut_ref[...], gt.astype(BF16)) + bglu_ref[...]
    st = gt * jax.nn.sigmoid(zt)
    s = st.T.astype(BF16)
    mix = _dot(attn_ref[...], wo1_ref[...]) + _dot(s, wo2_ref[...])
    x1 = x_ref[...] + mod_ref[2:3, :] * mix
    ms = jnp.mean(x1 * x1, axis=-1, keepdims=True)
    h2 = x1 * lax.rsqrt(ms + NORM_EPS) * g2_ref[...]
    h2 = h2 * (1.0 + mod_ref[4:5, :]) + mod_ref[3:4, :]
    hb = h2.astype(BF16)
    bits = lax.bitcast_convert_type(hb.astype(F32), jnp.uint32)
    half = bits.shape[1] // 2
    packed = (bits[:, :half] >> 16) | (bits[:, half:] & jnp.uint32(0xFFFF0000))
    h2_ref[0] = packed[:, :half // 2]
    h2_ref[1] = packed[:, half // 2:]
    lg_ref[...] = jax.nn.sigmoid(_dot3_nt(wrt_ref[...], h2))
    sh = _silu(_dot(hb, wsg_ref[...])) * _dot(hb, wsu_ref[...])
    shared = _dot(sh.astype(BF16), wsd_ref[...])
    xp_ref[...] = x1 + mod_ref[5:6, :] * shared


def _mid_call(x, attn, yt, ut, mod3, dsk, wglut, bglu, wo1, wo2, g2, wrt, wsg, wsu, wsd, tm=512):
    bsz, seq, dm = x.shape
    aw = attn.shape[2]
    sw = yt.shape[1]
    ne = wrt.shape[0]
    row = lambda b, i: (b, i, 0)
    colm = lambda b, i: (b, 0, i)
    full = lambda a: pl.BlockSpec(a.shape, lambda b, i: (0,) * a.ndim)
    dsk = dsk.reshape(sw, 1)
    bglu = bglu.reshape(sw, 1)
    g2 = g2.reshape(1, dm)
    return pl.pallas_call(
        _mid_kernel,
        grid=(bsz, seq // tm),
        in_specs=[
            pl.BlockSpec((None, tm, dm), row),
            pl.BlockSpec((None, tm, aw), row),
            pl.BlockSpec((None, sw, tm), colm),
            pl.BlockSpec((None, sw, tm), colm),
            pl.BlockSpec((None, N_ADA, dm), lambda b, i: (b, 0, 0)),
            full(dsk), full(wglut), full(bglu), full(wo1), full(wo2), full(g2), full(wrt),
            full(wsg), full(wsu), full(wsd),
        ],
        out_specs=[
            pl.BlockSpec((None, tm, dm), row),
            pl.BlockSpec((2, None, tm, dm // 4), lambda b, i: (0, b, i, 0)),
            pl.BlockSpec((None, ne, tm), colm),
        ],
        out_shape=[
            jax.ShapeDtypeStruct((bsz, seq, dm), F32),
            jax.ShapeDtypeStruct((2, bsz, seq, dm // 4), jnp.uint32),
            jax.ShapeDtypeStruct((bsz, ne, seq), F32),
        ],
        compiler_params=pltpu.CompilerParams(
            dimension_semantics=("parallel", "parallel"), vmem_limit_bytes=VMEM_LIMIT),
        name="mid",
    )(x, attn, yt, ut, mod3, dsk, wglut, bglu, wo1, wo2, g2, wrt, wsg, wsu, wsd)


def _moe_kernel(cs_ref, cnt_ref, xs_hbm, wg_ref, wu_ref, wd_ref, out_hbm, xbuf, obuf, wgb, wub, wdb, isem, osem):
    e = pl.program_id(0)
    n_exp = pl.num_programs(0)
    c0, c1, c_end = cs_ref[e], cs_ref[e + 1], cs_ref[n_exp]
    ch = xbuf.shape[2]
    q = xbuf.shape[3]

    def fetch(c, slot):
        return pltpu.make_async_copy(xs_hbm.at[:, pl.ds(c * ch, ch), :], xbuf.at[slot], isem.at[slot])

    def drain(c, slot):
        return pltpu.make_async_copy(obuf.at[slot], out_hbm.at[pl.ds(c * ch, ch), :], osem.at[slot])

    @pl.when(jnp.logical_and(e == 0, c_end > 0))
    def _():
        fetch(0, 0).start()

    @pl.when(c1 > c0)
    def _():
        wgb[...] = wg_ref[...].astype(BF16)
        wub[...] = wu_ref[...].astype(BF16)
        wdb[...] = wd_ref[...].astype(BF16)

        def chunk(c, carry):
            slot = c & 1
            fetch(c, slot).wait()

            @pl.when(c + 1 < c_end)
            def _():
                fetch(c + 1, 1 - slot).start()

            @pl.when(c >= 2)
            def _():
                drain(c - 2, slot).wait()

            n_valid = cnt_ref[e] - (c - c0) * ch
            valid = lax.broadcasted_iota(jnp.int32, (ch, q), 0) < n_valid
            gate = up = None
            for h in range(2):
                xu = xbuf[slot, h]
                lo = jnp.where(valid, lax.bitcast_convert_type(xu << 16, F32), 0.0).astype(BF16)
                hi = jnp.where(valid, lax.bitcast_convert_type(xu & jnp.uint32(0xFFFF0000), F32), 0.0).astype(BF16)
                for xpart, k0 in ((lo, h * q), (hi, (2 + h) * q)):
                    g = _dot(xpart, wgb[k0:k0 + q, :])
                    u = _dot(xpart, wub[k0:k0 + q, :])
                    gate = g if gate is None else gate + g
                    up = u if up is None else up + u
            hb = _silu(gate) * up
            obuf[slot] = _dot(hb.astype(BF16), wdb[...]).astype(obuf.dtype)
            drain(c, slot).start()
            return carry

        lax.fori_loop(c0, c1, chunk, 0)

    @pl.when(e == n_exp - 1)
    def _():
        for back in (2, 1):
            @pl.when(c_end >= back)
            def _():
                drain(c_end - back, (c_end - back) & 1).wait()


def _moe_call(chunk_start, counts, xs, w_gate, w_up, w_down):
    n_rows = xs.shape[1]
    n_exp, dm, de = w_gate.shape
    grid_spec = pltpu.PrefetchScalarGridSpec(
        num_scalar_prefetch=2,
        grid=(n_exp,),
        in_specs=[
            pl.BlockSpec(memory_space=pl.ANY),
            pl.BlockSpec((None, dm, de), lambda e, cs, cnt: (e, 0, 0)),
            pl.BlockSpec((None, dm, de), lambda e, cs, cnt: (e, 0, 0)),
            pl.BlockSpec((None, de, dm), lambda e, cs, cnt: (e, 0, 0)),
        ],
        out_specs=pl.BlockSpec(memory_space=pl.ANY),
        scratch_shapes=[
            pltpu.VMEM((2, 2, MOE_BLOCK, dm // 4), jnp.uint32), pltpu.VMEM((2, MOE_BLOCK, dm), BF16),
            pltpu.VMEM((dm, de), BF16), pltpu.VMEM((dm, de), BF16), pltpu.VMEM((de, dm), BF16),
            pltpu.SemaphoreType.DMA((2,)), pltpu.SemaphoreType.DMA((2,)),
        ],
    )
    return pl.pallas_call(
        _moe_kernel,
        grid_spec=grid_spec,
        out_shape=jax.ShapeDtypeStruct((n_rows, dm), BF16),
        compiler_params=pltpu.CompilerParams(
            dimension_semantics=("arbitrary",), vmem_limit_bytes=VMEM_LIMIT),
        name="moe",
    )(chunk_start, counts, xs, w_gate, w_up, w_down)


def _final_kernel(xp_ref, r_ref, w_ref, mod_ref, g_ref, o_ref):
    w = w_ref[...]
    routed = w[:, 0:1] * r_ref[0].astype(F32)
    for k in range(1, r_ref.shape[0]):
        routed = routed + w[:, k:k + 1] * r_ref[k].astype(F32)
    x = xp_ref[...] + mod_ref[5:6, :] * routed
    ms = jnp.mean(x * x, axis=-1, keepdims=True)
    o_ref[...] = x * lax.rsqrt(ms + NORM_EPS) * g_ref[...]


def _final_call(xp, rows, w, mod3, final_g, tm=256):
    bsz, seq, dm = xp.shape
    kk = rows.shape[1]
    row = lambda b, i: (b, i, 0)
    return pl.pallas_call(
        _final_kernel,
        grid=(bsz, seq // tm),
        in_specs=[
            pl.BlockSpec((None, tm, dm), row),
            pl.BlockSpec((None, kk, tm, dm), lambda b, i: (b, 0, i, 0)),
            pl.BlockSpec((None, tm, kk), row),
            pl.BlockSpec((None, N_ADA, dm), lambda b, i: (b, 0, 0)),
            pl.BlockSpec((1, dm), lambda b, i: (0, 0)),
        ],
        out_specs=pl.BlockSpec((None, tm, dm), row),
        out_shape=jax.ShapeDtypeStruct((bsz, seq, dm), F32),
        compiler_params=pltpu.CompilerParams(
            dimension_semantics=("parallel", "parallel"), vmem_limit_bytes=VMEM_LIMIT),
        name="final",
    )(xp, rows, w, mod3, final_g.reshape(1, dm))


def _route_kernel(sc_ref, bias_ref, idx_ref, w_ref, rank_ref, cnt_ref, tri_sc, carry_sc):
    ne, tn = sc_ref.shape
    gsz = ne // N_EXPERT_GROUPS
    neg = -jnp.inf
    first = jnp.logical_and(pl.program_id(0) == 0, pl.program_id(1) == 0)

    @pl.when(first)
    def _():
        r = lax.broadcasted_iota(jnp.int32, (tn, tn), 0)
        c = lax.broadcasted_iota(jnp.int32, (tn, tn), 1)
        tri_sc[...] = jnp.where(r < c, 1.0, 0.0).astype(BF16)
        carry_sc[...] = jnp.zeros(carry_sc.shape, F32)

    s = sc_ref[...]
    sel = s + bias_ref[...]
    gs = []
    for g in range(N_EXPERT_GROUPS):
        blk = sel[g * gsz:(g + 1) * gsz]
        m1 = jnp.max(blk, axis=0, keepdims=True)
        eq = blk == m1
        n_eq = jnp.sum(jnp.where(eq, 1.0, 0.0), axis=0, keepdims=True)
        m2 = jnp.max(jnp.where(eq, neg, blk), axis=0, keepdims=True)
        gs.append(m1 + jnp.where(n_eq >= 2.0, m1, m2))
    gs = jnp.concatenate(gs, axis=0)
    gi = lax.broadcasted_iota(jnp.int32, gs.shape, 0)
    beaten = jnp.zeros(gs.shape, F32)
    for gp in range(N_EXPERT_GROUPS):
        other = gs[gp:gp + 1]
        wins = jnp.where(other > gs, 1.0, jnp.where(other == gs, jnp.where(gi > gp, 1.0, 0.0), 0.0))
        beaten = beaten + wins
    gadd = jnp.where(beaten < float(TOPK_GROUPS), 0.0, neg)
    cur = jnp.concatenate(
        [sel[g * gsz:(g + 1) * gsz] + gadd[g:g + 1] for g in range(N_EXPERT_GROUPS)], axis=0)

    eidx = lax.broadcasted_iota(jnp.int32, (ne, tn), 0).astype(F32)
    chosen = jnp.zeros((ne, tn), F32)
    idx_rows, s_rows = [], []
    for _ in range(TOP_K):
        m = jnp.max(cur, axis=0, keepdims=True)
        ik = jnp.min(jnp.where(cur == m, eidx, float(ne)), axis=0, keepdims=True)
        oh = eidx == ik
        s_rows.append(jnp.sum(jnp.where(oh, s, 0.0), axis=0, keepdims=True))
        cur = jnp.where(oh, neg, cur)
        chosen = chosen + jnp.where(oh, 1.0, 0.0)
        idx_rows.append(ik)
    sk = jnp.concatenate(s_rows, axis=0)
    w_ref[...] = sk / jnp.sum(sk, axis=0, keepdims=True) * ROUTED_SCALE
    idx_ref[...] = jnp.concatenate(idx_rows, axis=0).astype(jnp.int32)

    before = _dot(chosen.astype(BF16), tri_sc[...]) + carry_sc[...]
    rank_rows = [jnp.sum(jnp.where(eidx == ik, before, 0.0), axis=0, keepdims=True) for ik in idx_rows]
    rank_ref[...] = jnp.concatenate(rank_rows, axis=0).astype(jnp.int32)
    carry_sc[...] = carry_sc[...] + jnp.sum(chosen, axis=1, keepdims=True)
    cnt_ref[...] = carry_sc[...]


def _route_call(scores_t, router_bias, tn=512):
    bsz, ne, seq = scores_t.shape
    tok = lambda b, i: (b, 0, i)
    return pl.pallas_call(
        _route_kernel,
        grid=(bsz, seq // tn),
        in_specs=[
            pl.BlockSpec((None, ne, tn), tok),
            pl.BlockSpec((ne, 1), lambda b, i: (0, 0)),
        ],
        out_specs=[
            pl.BlockSpec((None, TOP_K, tn), tok),
            pl.BlockSpec((None, TOP_K, tn), tok),
            pl.BlockSpec((None, TOP_K, tn), tok),
            pl.BlockSpec((ne, 1), lambda b, i: (0, 0)),
        ],
        out_shape=[
            jax.ShapeDtypeStruct((bsz, TOP_K, seq), jnp.int32),
            jax.ShapeDtypeStruct((bsz, TOP_K, seq), F32),
            jax.ShapeDtypeStruct((bsz, TOP_K, seq), jnp.int32),
            jax.ShapeDtypeStruct((ne, 1), F32),
        ],
        scratch_shapes=[pltpu.VMEM((tn, tn), BF16), pltpu.VMEM((ne, 1), F32)],
        compiler_params=pltpu.CompilerParams(
            dimension_semantics=("arbitrary", "arbitrary"), vmem_limit_bytes=VMEM_LIMIT),
        name="route",
    )(scores_t, router_bias.reshape(ne, 1))


def _dispatch_tables(idx, rank, counts):
    ne = counts.shape[0]
    n_tok = idx.shape[0] * idx.shape[2]
    counts = counts.astype(jnp.int32)
    pcounts = (counts + MOE_BLOCK - 1) // MOE_BLOCK * MOE_BLOCK
    pstarts = jnp.concatenate([jnp.zeros((1,), jnp.int32), jnp.cumsum(pcounts).astype(jnp.int32)])
    n_blocks = -(-(n_tok * TOP_K + ne * (MOE_BLOCK - 1)) // MOE_BLOCK)
    dest = _dest_call(pstarts[:ne], idx, rank)
    return dest, pstarts // MOE_BLOCK, counts, n_blocks


def _dest_kernel(ps_ref, idx_ref, rank_ref, o_ref):
    idx = idx_ref[...]

    def body(e, acc):
        return jnp.where(idx == e, ps_ref[e], acc)

    o_ref[...] = rank_ref[...] + lax.fori_loop(0, ps_ref.shape[0], body, jnp.zeros(idx.shape, jnp.int32))


def _dest_call(pstarts, idx, rank, tn=2048):
    bsz, kk, seq = idx.shape
    tok = lambda b, i, ps: (b, 0, i)
    grid_spec = pltpu.PrefetchScalarGridSpec(
        num_scalar_prefetch=1,
        grid=(bsz, seq // tn),
        in_specs=[pl.BlockSpec((None, kk, tn), tok), pl.BlockSpec((None, kk, tn), tok)],
        out_specs=pl.BlockSpec((None, kk, tn), tok),
    )
    return pl.pallas_call(
        _dest_kernel,
        grid_spec=grid_spec,
        out_shape=jax.ShapeDtypeStruct(idx.shape, jnp.int32),
        compiler_params=pltpu.CompilerParams(dimension_semantics=("parallel", "parallel")),
        name="dest",
    )(pstarts, idx, rank)


def _scatter_rows(rows, dest, n_out):
    n_tok, width = rows.shape
    k_slots = dest.shape[0]
    win = LANES
    mesh = plsc.VectorSubcoreMesh(core_axis_name="c", subcore_axis_name="s")

    @pl.kernel(out_type=jax.ShapeDtypeStruct((n_out, width), rows.dtype), mesh=mesh, scratch_types=[])
    def scatter(rows_hbm, dest_hbm, out_hbm):
        def body(rows_vmem, idx_vmem):
            pltpu.sync_copy(rows_vmem, out_hbm.at[idx_vmem.at[0]])

        pltpu.emit_pipeline(
            body,
            grid=(n_tok // win, k_slots),
            in_specs=[
                pl.BlockSpec((win, width), lambda i, k: (i, 0)),
                pl.BlockSpec((1, win), lambda i, k: (k, i)),
            ],
            out_specs=[],
            core_axis_name=("c", "s"),
            dimension_semantics=(pltpu.PARALLEL, pltpu.ARBITRARY),
        )(rows_hbm, dest_hbm)

    return scatter(rows, dest)


def kernel(x, c, w_ada, b_ada, norm1_g, w_in, w_out, lambda_q1, lambda_k1, lambda_q2, lambda_k2, subln_g,
           ssm_a_re, ssm_a_im, ssm_log_step, ssm_b_re, ssm_b_im, ssm_c_re, ssm_c_im, ssm_d, w_glu, b_glu,
           norm2_g, w_router, router_bias, w_gate, w_up, w_down, ws_gate, ws_up, ws_down, final_g):
    bsz, seq, dm = x.shape
    n_tok = bsz * seq
    aw = N_ATTN_HEADS * V_HEAD_DIM

    mod3 = _mod_call(c, w_ada[0], b_ada[0]).reshape(bsz, N_ADA, dm)

    wqkv = w_in[0][:, :3 * aw].astype(BF16)
    wut = w_in[0][:, 3 * aw:].T.astype(BF16)
    q, k, v, ut = _inproj_call(x, mod3, norm1_g[0], wqkv, wut)

    lam = (jnp.exp(jnp.sum(lambda_q1[0] * lambda_k1[0])) - jnp.exp(jnp.sum(lambda_q2[0] * lambda_k2[0]))
           + LAM_INIT).reshape(1)
    attn = _attn_call(lam, q, k, v, subln_g[0])

    kmat, wstate, wcarry, dpow = _ssm_operators(ssm_a_re[0], ssm_a_im[0], ssm_log_step[0], ssm_b_re[0],
                                                ssm_b_im[0], ssm_c_re[0], ssm_c_im[0])
    yt = _ssm_call(ut, kmat, wstate, wcarry, dpow)

    xp, h2, scores_t = _mid_call(
        x, attn, yt, ut, mod3, ssm_d[0], w_glu[0].T.astype(BF16), b_glu[0],
        w_out[0][:aw].astype(BF16), w_out[0][aw:].astype(BF16), norm2_g[0], w_router[0].T,
        ws_gate[0].astype(BF16), ws_up[0].astype(BF16), ws_down[0].astype(BF16))

    idx, w, rank, counts = _route_call(scores_t, router_bias[0])
    dest, chunk_start, counts, n_blocks = _dispatch_tables(idx, rank, counts.reshape(-1))
    n_rows = n_blocks * MOE_BLOCK

    dest_k = jnp.swapaxes(dest, 0, 1).reshape(TOP_K, n_tok)
    dest_half = jnp.concatenate([dest_k, dest_k + n_rows], axis=1)
    xs = _scatter_rows(h2.reshape(2 * n_tok, dm // 4), dest_half, 2 * n_rows).reshape(2, n_rows, dm // 4)

    out = _moe_call(chunk_start, counts, xs, w_gate[0], w_up[0], w_down[0])
    rows = jnp.take(out, dest, axis=0)
    return _final_call(xp, rows, jnp.swapaxes(w, 1, 2), mod3, final_g)
```

```python
import functools
import math

import jax
import jax.numpy as jnp
from jax import lax
from jax.experimental import pallas as pl
from jax.experimental.pallas import tpu as pltpu
from jax.experimental.pallas import tpu_sc as plsc

F32 = jnp.float32
BF16 = jnp.bfloat16

N_ATTN_HEADS = 4
ATTN_HEAD_DIM = 64
V_HEAD_DIM = 128
SSM_GROUP = 16
N_SSM_GROUPS = 32
SSM_STATE = 64
N_EXPERTS = 256
TOP_K = 8
N_EXPERT_GROUPS = 8
TOPK_GROUPS = 4
ROUTED_SCALE = 2.5
NORM_EPS = 1e-6
SUBLN_EPS = 1e-5
N_ADA = 6
LAM_INIT = 0.8 - 0.6 * math.exp(-0.3 * 0)

LANES = 128
MOE_BLOCK = 256
NEG_BIG = -1e30
LOG2_E = math.log2(math.e)
VMEM_LIMIT = 48 * 1024 * 1024


def _split_bf16(a):
    hi = a.astype(BF16)
    lo = (a - hi.astype(F32)).astype(BF16)
    return hi, lo


def _dot(a, b):
    return jnp.dot(a, b, preferred_element_type=F32)


def _dot_nt(a, b):
    return lax.dot_general(a, b, (((1,), (1,)), ((), ())), preferred_element_type=F32)


def _dot3(a, b):
    ah, al = _split_bf16(a)
    bh, bl = _split_bf16(b)
    return _dot(ah, bh) + _dot(ah, bl) + _dot(al, bh)


def _dot3_nt(a, b):
    ah, al = _split_bf16(a)
    bh, bl = _split_bf16(b)
    return _dot_nt(ah, bh) + _dot_nt(ah, bl) + _dot_nt(al, bh)


def _silu(x):
    return x * jax.nn.sigmoid(x)


def _gelu_tanh(x):
    c = math.sqrt(2.0 / math.pi)
    return 0.5 * x * (1.0 + jnp.tanh(c * (x + 0.044715 * (x * x * x))))


def _mod_kernel(c_ref, w_ref, b_ref, o_ref):
    cond = _silu(c_ref[...])
    o_ref[...] = _dot3(cond, w_ref[...]) + b_ref[...]


def _mod_call(c, w_ada, b_ada):
    bsz, dm = c.shape
    n_out = w_ada.shape[1]
    tn = 1024
    return pl.pallas_call(
        _mod_kernel,
        grid=(n_out // tn,),
        in_specs=[
            pl.BlockSpec((bsz, dm), lambda j: (0, 0)),
            pl.BlockSpec((dm, tn), lambda j: (0, j)),
            pl.BlockSpec((1, tn), lambda j: (0, j)),
        ],
        out_specs=pl.BlockSpec((bsz, tn), lambda j: (0, j)),
        out_shape=jax.ShapeDtypeStruct((bsz, n_out), F32),
        compiler_params=pltpu.CompilerParams(vmem_limit_bytes=VMEM_LIMIT),
        name="mod",
    )(c, w_ada, b_ada.reshape(1, n_out))


def _inproj_kernel(x_ref, mod_ref, g_ref, wqkv_ref, wut_ref, q_ref, k_ref, v_ref, ut_ref, *, aw):
    x = x_ref[...]
    ms = jnp.mean(x * x, axis=-1, keepdims=True)
    h = x * lax.rsqrt(ms + NORM_EPS) * g_ref[...]
    h = h * (1.0 + mod_ref[1:2, :]) + mod_ref[0:1, :]
    hb = h.astype(BF16)
    qkv = _dot(hb, wqkv_ref[...])
    q_ref[...] = (qkv[:, :aw] * (LOG2_E * ATTN_HEAD_DIM ** -0.5)).astype(BF16)
    k_ref[...] = qkv[:, aw:2 * aw].astype(BF16)
    v_ref[...] = qkv[:, 2 * aw:].astype(BF16)
    ut_ref[...] = _dot_nt(wut_ref[...], hb)


def _inproj_call(x, mod3, norm_g, wqkv, wut, tm=512):
    bsz, seq, dm = x.shape
    aw = wqkv.shape[1] // 3
    sw = wut.shape[0]
    row = lambda b, i: (b, i, 0)
    return pl.pallas_call(
        functools.partial(_inproj_kernel, aw=aw),
        grid=(bsz, seq // tm),
        in_specs=[
            pl.BlockSpec((None, tm, dm), row),
            pl.BlockSpec((None, N_ADA, dm), lambda b, i: (b, 0, 0)),
            pl.BlockSpec((1, dm), lambda b, i: (0, 0)),
            pl.BlockSpec(wqkv.shape, lambda b, i: (0, 0)),
            pl.BlockSpec(wut.shape, lambda b, i: (0, 0)),
        ],
        out_specs=[
            pl.BlockSpec((None, tm, aw), row),
            pl.BlockSpec((None, tm, aw), row),
            pl.BlockSpec((None, tm, aw), row),
            pl.BlockSpec((None, sw, tm), lambda b, i: (b, 0, i)),
        ],
        out_shape=[
            jax.ShapeDtypeStruct((bsz, seq, aw), BF16),
            jax.ShapeDtypeStruct((bsz, seq, aw), BF16),
            jax.ShapeDtypeStruct((bsz, seq, aw), BF16),
            jax.ShapeDtypeStruct((bsz, sw, seq), F32),
        ],
        compiler_params=pltpu.CompilerParams(
            dimension_semantics=("parallel", "parallel"), vmem_limit_bytes=VMEM_LIMIT),
        name="inproj",
    )(x, mod3, norm_g.reshape(1, dm), wqkv, wut)


def _attn_kernel(lam_ref, q_ref, k_ref, v_ref, g_ref, o_ref, m_sc, l_sc, acc_sc, *, tq):
    i = pl.program_id(2)
    q = q_ref[...]
    lane = lax.broadcasted_iota(jnp.int32, q.shape, 1)
    zero = jnp.zeros_like(q)
    q2 = jnp.concatenate([jnp.where(lane < ATTN_HEAD_DIM, q, zero),
                          jnp.where(lane >= ATTN_HEAD_DIM, q, zero)], axis=0)
    m_sc[...] = jnp.full(m_sc.shape, NEG_BIG, F32)
    l_sc[...] = jnp.zeros(l_sc.shape, F32)
    acc_sc[...] = jnp.zeros(acc_sc.shape, F32)

    def step(j, masked):
        start = pl.multiple_of(j * tq, tq)
        kt = k_ref[pl.ds(start, tq), :]
        vt = v_ref[pl.ds(start, tq), :]
        s = _dot_nt(q2, kt)
        if masked:
            r = lax.broadcasted_iota(jnp.int32, s.shape, 0)
            c = lax.broadcasted_iota(jnp.int32, s.shape, 1)
            s = jnp.where(c <= (r & (tq - 1)), s, NEG_BIG)
        m_prev = m_sc[...]
        m_new = jnp.maximum(m_prev, jnp.max(s, axis=-1, keepdims=True))
        alpha = jnp.exp2(m_prev - m_new)
        p = jnp.exp2(s - jnp.concatenate([m_new] * (tq // LANES), axis=1))
        psum = p[:, :LANES]
        for c0 in range(LANES, tq, LANES):
            psum = psum + p[:, c0:c0 + LANES]
        l_sc[...] = alpha * l_sc[...] + psum
        acc_sc[...] = alpha * acc_sc[...] + _dot(p.astype(BF16), vt)
        m_sc[...] = m_new

    def body(j, carry):
        step(j, False)
        return carry

    lax.fori_loop(0, i, body, 0)
    step(i, True)

    o_all = acc_sc[...] / jnp.sum(l_sc[...], axis=-1, keepdims=True)
    o = o_all[:tq] - lam_ref[0] * o_all[tq:]
    ms = jnp.mean(o * o, axis=-1, keepdims=True)
    o = o * lax.rsqrt(ms + SUBLN_EPS) * g_ref[...] * (1.0 - LAM_INIT)
    o_ref[...] = o.astype(o_ref.dtype)


def _attn_call(lam, q, k, v, subln_g, tq=1024):
    bsz, seq, aw = q.shape
    nh = aw // V_HEAD_DIM
    qmap = lambda b, h, i: (b, i, h)
    kvmap = lambda b, h, i: (b, 0, h)
    return pl.pallas_call(
        functools.partial(_attn_kernel, tq=tq),
        grid=(bsz, nh, seq // tq),
        in_specs=[
            pl.BlockSpec(memory_space=pltpu.SMEM),
            pl.BlockSpec((None, tq, V_HEAD_DIM), qmap),
            pl.BlockSpec((None, seq, V_HEAD_DIM), kvmap),
            pl.BlockSpec((None, seq, V_HEAD_DIM), kvmap),
            pl.BlockSpec((1, V_HEAD_DIM), lambda b, h, i: (0, 0)),
        ],
        out_specs=pl.BlockSpec((None, tq, V_HEAD_DIM), qmap),
        out_shape=jax.ShapeDtypeStruct((bsz, seq, aw), BF16),
        scratch_shapes=[pltpu.VMEM((2 * tq, V_HEAD_DIM), F32)] * 3,
        compiler_params=pltpu.CompilerParams(
            dimension_semantics=("parallel", "parallel", "parallel"), vmem_limit_bytes=VMEM_LIMIT),
        name="attn",
    )(lam, q, k, v, subln_g.reshape(1, V_HEAD_DIM))


def _ssm_operators(a_re, a_im, log_step, b_re, b_im, c_re, c_im):
    t = LANES
    hi = lax.Precision.HIGHEST
    lam = lax.complex(jnp.minimum(a_re, -1e-4), a_im)
    delta = jnp.exp(log_step)[:, None]
    lam_bar = jnp.exp(lam * delta)
    bbar = ((lam_bar - 1.0) / lam)[:, :, None] * lax.complex(b_re, b_im)
    cmat = lax.complex(c_re, c_im)
    ld = lam * delta
    tau = jnp.arange(t + 1, dtype=F32)
    pw = jnp.exp(ld[:, :, None] * tau)
    cb = cmat[:, None, :, :] * jnp.swapaxes(bbar, 1, 2)[:, :, None, :]
    g = a_re.shape[0]
    cb = cb.reshape(g, SSM_GROUP * SSM_GROUP, SSM_STATE)
    cbcat = jnp.concatenate([cb.real, -cb.imag], axis=-1)
    pcat = jnp.concatenate([pw.real[:, :, :t], pw.imag[:, :, :t]], axis=1)
    kmat = jnp.einsum('gxp,gpt->gxt', cbcat, pcat, precision=hi)
    prev = jnp.swapaxes(pw[:, :, t - 1::-1], 1, 2)
    arev = jnp.concatenate([prev.real, prev.imag], axis=-1)
    bt = jnp.swapaxes(bbar, 1, 2)
    brow = jnp.stack([jnp.concatenate([bt.real, bt.real], axis=-1),
                      jnp.concatenate([-bt.imag, bt.imag], axis=-1)], axis=2).reshape(g, 2 * SSM_GROUP, 2 * SSM_STATE)
    a1 = jnp.concatenate([pw.real[:, :, 1:], pw.imag[:, :, 1:]], axis=1)
    ct = jnp.swapaxes(cmat, 1, 2)
    ccol = jnp.stack([jnp.concatenate([ct.real, -ct.real], axis=1),
                      jnp.concatenate([-ct.imag, -ct.imag], axis=1)], axis=-1).reshape(g, 2 * SSM_STATE, 2 * SSM_GROUP)
    rows = []
    for i in range(6):
        d = jnp.exp(ld * float(t * (1 << i)))
        rows.append(jnp.concatenate([d.real, d.real], axis=-1))
        rows.append(jnp.concatenate([-d.imag, d.imag], axis=-1))
    rows += [jnp.zeros_like(rows[0])] * 4
    dpow = jnp.stack(rows, axis=1)
    return kmat, arev, brow, a1, ccol, dpow


def _ssm_kernel(u_ref, k_ref, arev_ref, brow_ref, a1_ref, ccol_ref, dp_ref, y_ref, m_sc, ws_sc, wc_sc):
    bsz, nch, n_chunk, t = u_ref.shape
    row = lax.broadcasted_iota(jnp.int32, (t, t), 0)
    col = lax.broadcasted_iota(jnp.int32, (t, t), 1)
    causal = col >= row
    half = arev_ref.shape[1] // 2

    arev = arev_ref[...]
    arev_sw = pltpu.roll(arev, half, 1)
    a1 = a1_ref[...]
    a1_sw = pltpu.roll(a1, half, 0)
    for c in range(nch):
        ws_sc[c * t:(c + 1) * t, :] = (arev * brow_ref[2 * c:2 * c + 1, :]
                                       + arev_sw * brow_ref[2 * c + 1:2 * c + 2, :]).astype(BF16)
        wc_sc[:, c * t:(c + 1) * t] = (a1 * ccol_ref[:, 2 * c:2 * c + 1]
                                       + a1_sw * ccol_ref[:, 2 * c + 1:2 * c + 2]).astype(BF16)

    def build(ci, carry):
        r0 = pl.multiple_of(ci * t, t)
        for co in range(nch):
            kr = k_ref[pl.ds(ci * nch + co, 1), :]
            kb = jnp.broadcast_to(kr, (t, t))
            kb = pltpu.roll(kb, 0, 1, stride=1, stride_axis=0)
            m_sc[pl.ds(r0, t), co * t:(co + 1) * t] = jnp.where(causal, kb, 0.0).astype(BF16)
        return carry

    lax.fori_loop(0, nch, build, 0)

    uflat = jnp.concatenate(
        [jnp.concatenate([u_ref[b, ci] for b in range(bsz)], axis=0) for ci in range(nch)],
        axis=1).astype(BF16)
    y = _dot(uflat, m_sc[...])
    z = _dot(uflat, ws_sc[...])
    kidx = lax.broadcasted_iota(jnp.int32, z.shape, 0) & (n_chunk - 1)
    shift = 1
    i = 0
    while shift < n_chunk:
        zs = jnp.where(kidx >= shift, pltpu.roll(z, shift, 0), 0.0)
        z = z + zs * dp_ref[2 * i:2 * i + 1, :] + pltpu.roll(zs, half, 1) * dp_ref[2 * i + 1:2 * i + 2, :]
        shift *= 2
        i += 1
    xin = jnp.where(kidx >= 1, pltpu.roll(z, 1, 0), 0.0)
    xh, xl = _split_bf16(xin)
    wc = wc_sc[...]
    y = y + _dot(xh, wc) + _dot(xl, wc)
    for b in range(bsz):
        for co in range(nch):
            y_ref[b, co] = y[b * n_chunk:(b + 1) * n_chunk, co * t:(co + 1) * t]


def _ssm_call(ut, kmat, arev, brow, a1, ccol, dpow):
    bsz, sw, seq = ut.shape
    n_groups = sw // SSM_GROUP
    n_chunk = seq // LANES
    assert n_chunk & (n_chunk - 1) == 0 and n_chunk <= 64
    u4 = ut.reshape(bsz, sw, n_chunk, LANES)
    blk = (bsz, SSM_GROUP, n_chunk, LANES)
    gmap = lambda g: (0, g, 0, 0)
    pmap = lambda g: (g, 0, 0)
    y4 = pl.pallas_call(
        _ssm_kernel,
        grid=(n_groups,),
        in_specs=[
            pl.BlockSpec(blk, gmap),
            pl.BlockSpec((None,) + kmat.shape[1:], pmap),
            pl.BlockSpec((None,) + arev.shape[1:], pmap),
            pl.BlockSpec((None,) + brow.shape[1:], pmap),
            pl.BlockSpec((None,) + a1.shape[1:], pmap),
            pl.BlockSpec((None,) + ccol.shape[1:], pmap),
            pl.BlockSpec((None,) + dpow.shape[1:], pmap),
        ],
        out_specs=pl.BlockSpec(blk, gmap),
        out_shape=jax.ShapeDtypeStruct(u4.shape, F32),
        scratch_shapes=[pltpu.VMEM((SSM_GROUP * LANES, SSM_GROUP * LANES), BF16),
                        pltpu.VMEM((SSM_GROUP * LANES, 2 * SSM_STATE), BF16),
                        pltpu.VMEM((2 * SSM_STATE, SSM_GROUP * LANES), BF16)],
        compiler_params=pltpu.CompilerParams(
            dimension_semantics=("parallel",), vmem_limit_bytes=VMEM_LIMIT),
        name="ssm",
    )(u4, kmat, arev, brow, a1, ccol, dpow)
    return y4.reshape(bsz, sw, seq)


def _mid_kernel(x_ref, attn_ref, yt_ref, ut_ref, mod_ref, dsk_ref, wglut_ref, bglu_ref, wo1_ref, wo2_ref,
                g2_ref, wrt_ref, wsg_ref, wsu_ref, wsd_ref, xp_ref, h2_ref, lg_ref):
    gt = _gelu_tanh(yt_ref[...] + dsk_ref[...] * ut_ref[...])
    zt = _dot(wglut_ref[...], gt.astype(BF16)) + bglu_ref[...]
    st = gt * jax.nn.sigmoid(zt)
    s = st.T.astype(BF16)
    mix = _dot(attn_ref[...], wo1_ref[...]) + _dot(s, wo2_ref[...])
    x1 = x_ref[...] + mod_ref[2:3, :] * mix
    ms = jnp.mean(x1 * x1, axis=-1, keepdims=True)
    h2 = x1 * lax.rsqrt(ms + NORM_EPS) * g2_ref[...]
    h2 = h2 * (1.0 + mod_ref[4:5, :]) + mod_ref[3:4, :]
    hb = h2.astype(BF16)
    bits = lax.bitcast_convert_type(hb.astype(F32), jnp.uint32)
    half = bits.shape[1] // 2
    packed = (bits[:, :half] >> 16) | (bits[:, half:] & jnp.uint32(0xFFFF0000))
    h2_ref[0] = packed[:, :half // 2]
    h2_ref[1] = packed[:, half // 2:]
    lg_ref[...] = jax.nn.sigmoid(_dot3_nt(wrt_ref[...], h2))
    sh = _silu(_dot(hb, wsg_ref[...])) * _dot(hb, wsu_ref[...])
    shared = _dot(sh.astype(BF16), wsd_ref[...])
    xp_ref[...] = x1 + mod_ref[5:6, :] * shared


def _mid_call(x, attn, yt, ut, mod3, dsk, wglut, bglu, wo1, wo2, g2, wrt, wsg, wsu, wsd, tm=512):
    bsz, seq, dm = x.shape
    aw = attn.shape[2]
    sw = yt.shape[1]
    ne = wrt.shape[0]
    row = lambda b, i: (b, i, 0)
    colm = lambda b, i: (b, 0, i)
    full = lambda a: pl.BlockSpec(a.shape, lambda b, i: (0,) * a.ndim)
    dsk = dsk.reshape(sw, 1)
    bglu = bglu.reshape(sw, 1)
    g2 = g2.reshape(1, dm)
    return pl.pallas_call(
        _mid_kernel,
        grid=(bsz, seq // tm),
        in_specs=[
            pl.BlockSpec((None, tm, dm), row),
            pl.BlockSpec((None, tm, aw), row),
            pl.BlockSpec((None, sw, tm), colm),
            pl.BlockSpec((None, sw, tm), colm),
            pl.BlockSpec((None, N_ADA, dm), lambda b, i: (b, 0, 0)),
            full(dsk), full(wglut), full(bglu), full(wo1), full(wo2), full(g2), full(wrt),
            full(wsg), full(wsu), full(wsd),
        ],
        out_specs=[
            pl.BlockSpec((None, tm, dm), row),
            pl.BlockSpec((2, None, tm, dm // 4), lambda b, i: (0, b, i, 0)),
            pl.BlockSpec((None, ne, tm), colm),
        ],
        out_shape=[
            jax.ShapeDtypeStruct((bsz, seq, dm), F32),
            jax.ShapeDtypeStruct((2, bsz, seq, dm // 4), jnp.uint32),
            jax.ShapeDtypeStruct((bsz, ne, seq), F32),
        ],
        compiler_params=pltpu.CompilerParams(
            dimension_semantics=("parallel", "parallel"), vmem_limit_bytes=VMEM_LIMIT),
        name="mid",
    )(x, attn, yt, ut, mod3, dsk, wglut, bglu, wo1, wo2, g2, wrt, wsg, wsu, wsd)


def _moe_kernel(cs_ref, cnt_ref, xs_hbm, wg_ref, wu_ref, wd_ref, out_hbm, xbuf, obuf, wgb, wub, wdb, isem, osem):
    e = pl.program_id(0)
    n_exp = pl.num_programs(0)
    c0, c1, c_end = cs_ref[e], cs_ref[e + 1], cs_ref[n_exp]
    ch = xbuf.shape[2]
    q = xbuf.shape[3]

    def fetch(c, slot):
        return pltpu.make_async_copy(xs_hbm.at[:, pl.ds(c * ch, ch), :], xbuf.at[slot], isem.at[slot])

    def drain(c, slot):
        return pltpu.make_async_copy(obuf.at[slot], out_hbm.at[pl.ds(c * ch, ch), :], osem.at[slot])

    row_queue = 1

    @pl.when(jnp.logical_and(e == 0, c_end > 0))
    def _():
        fetch(0, 0).start(priority=row_queue)

    @pl.when(c1 > c0)
    def _():
        wgb[...] = wg_ref[...].astype(BF16)
        wub[...] = wu_ref[...].astype(BF16)
        wdb[...] = wd_ref[...].astype(BF16)

        def chunk(c, carry):
            slot = c & 1
            fetch(c, slot).wait()

            @pl.when(c + 1 < c_end)
            def _():
                fetch(c + 1, 1 - slot).start(priority=row_queue)

            @pl.when(c >= 2)
            def _():
                drain(c - 2, slot).wait()

            n_valid = cnt_ref[e] - (c - c0) * ch
            valid = lax.broadcasted_iota(jnp.int32, (ch, q), 0) < n_valid
            gate = up = None
            for h in range(2):
                xu = xbuf[slot, h]
                lo = jnp.where(valid, lax.bitcast_convert_type(xu << 16, F32), 0.0).astype(BF16)
                hi = jnp.where(valid, lax.bitcast_convert_type(xu & jnp.uint32(0xFFFF0000), F32), 0.0).astype(BF16)
                for xpart, k0 in ((lo, h * q), (hi, (2 + h) * q)):
                    g = _dot(xpart, wgb[k0:k0 + q, :])
                    u = _dot(xpart, wub[k0:k0 + q, :])
                    gate = g if gate is None else gate + g
                    up = u if up is None else up + u
            hb = _silu(gate) * up
            obuf[slot] = _dot(hb.astype(BF16), wdb[...]).astype(obuf.dtype)
            drain(c, slot).start(priority=row_queue)
            return carry

        lax.fori_loop(c0, c1, chunk, 0)

    @pl.when(e == n_exp - 1)
    def _():
        for back in (2, 1):
            @pl.when(c_end >= back)
            def _():
                drain(c_end - back, (c_end - back) & 1).wait()


def _moe_call(chunk_start, counts, xs, w_gate, w_up, w_down):
    n_rows = xs.shape[1]
    n_exp, dm, de = w_gate.shape
    grid_spec = pltpu.PrefetchScalarGridSpec(
        num_scalar_prefetch=2,
        grid=(n_exp,),
        in_specs=[
            pl.BlockSpec(memory_space=pl.ANY),
            pl.BlockSpec((None, dm, de), lambda e, cs, cnt: (e, 0, 0)),
            pl.BlockSpec((None, dm, de), lambda e, cs, cnt: (e, 0, 0)),
            pl.BlockSpec((None, de, dm), lambda e, cs, cnt: (e, 0, 0)),
        ],
        out_specs=pl.BlockSpec(memory_space=pl.ANY),
        scratch_shapes=[
            pltpu.VMEM((2, 2, MOE_BLOCK, dm // 4), jnp.uint32), pltpu.VMEM((2, MOE_BLOCK, dm), BF16),
            pltpu.VMEM((dm, de), BF16), pltpu.VMEM((dm, de), BF16), pltpu.VMEM((de, dm), BF16),
            pltpu.SemaphoreType.DMA((2,)), pltpu.SemaphoreType.DMA((2,)),
        ],
    )
    return pl.pallas_call(
        _moe_kernel,
        grid_spec=grid_spec,
        out_shape=jax.ShapeDtypeStruct((n_rows, dm), BF16),
        compiler_params=pltpu.CompilerParams(
            dimension_semantics=("arbitrary",), vmem_limit_bytes=VMEM_LIMIT),
        name="moe",
    )(chunk_start, counts, xs, w_gate, w_up, w_down)


def _final_kernel(xp_ref, r_ref, w_ref, mod_ref, g_ref, o_ref):
    w = w_ref[...]
    routed = w[:, 0:1] * r_ref[0].astype(F32)
    for k in range(1, r_ref.shape[0]):
        routed = routed + w[:, k:k + 1] * r_ref[k].astype(F32)
    x = xp_ref[...] + mod_ref[5:6, :] * routed
    ms = jnp.mean(x * x, axis=-1, keepdims=True)
    o_ref[...] = x * lax.rsqrt(ms + NORM_EPS) * g_ref[...]


def _final_call(xp, rows, w, mod3, final_g, tm=256):
    bsz, seq, dm = xp.shape
    kk = rows.shape[1]
    row = lambda b, i: (b, i, 0)
    return pl.pallas_call(
        _final_kernel,
        grid=(bsz, seq // tm),
        in_specs=[
            pl.BlockSpec((None, tm, dm), row),
            pl.BlockSpec((None, kk, tm, dm), lambda b, i: (b, 0, i, 0)),
            pl.BlockSpec((None, tm, kk), row),
            pl.BlockSpec((None, N_ADA, dm), lambda b, i: (b, 0, 0)),
            pl.BlockSpec((1, dm), lambda b, i: (0, 0)),
        ],
        out_specs=pl.BlockSpec((None, tm, dm), row),
        out_shape=jax.ShapeDtypeStruct((bsz, seq, dm), F32),
        compiler_params=pltpu.CompilerParams(
            dimension_semantics=("parallel", "parallel"), vmem_limit_bytes=VMEM_LIMIT),
        name="final",
    )(xp, rows, w, mod3, final_g.reshape(1, dm))


def _route_kernel(sc_ref, bias_ref, idx_ref, w_ref, rank_ref, cnt_ref, tri_sc, carry_sc):
    ne, tn = sc_ref.shape
    gsz = ne // N_EXPERT_GROUPS
    neg = -jnp.inf
    first = jnp.logical_and(pl.program_id(0) == 0, pl.program_id(1) == 0)

    @pl.when(first)
    def _():
        r = lax.broadcasted_iota(jnp.int32, (tn, tn), 0)
        c = lax.broadcasted_iota(jnp.int32, (tn, tn), 1)
        tri_sc[...] = jnp.where(r < c, 1.0, 0.0).astype(BF16)
        carry_sc[...] = jnp.zeros(carry_sc.shape, F32)

    s = sc_ref[...]
    sel = s + bias_ref[...]
    gs = []
    for g in range(N_EXPERT_GROUPS):
        blk = sel[g * gsz:(g + 1) * gsz]
        m1 = jnp.max(blk, axis=0, keepdims=True)
        eq = blk == m1
        n_eq = jnp.sum(jnp.where(eq, 1.0, 0.0), axis=0, keepdims=True)
        m2 = jnp.max(jnp.where(eq, neg, blk), axis=0, keepdims=True)
        gs.append(m1 + jnp.where(n_eq >= 2.0, m1, m2))
    gs = jnp.concatenate(gs, axis=0)
    gi = lax.broadcasted_iota(jnp.int32, gs.shape, 0)
    beaten = jnp.zeros(gs.shape, F32)
    for gp in range(N_EXPERT_GROUPS):
        other = gs[gp:gp + 1]
        wins = jnp.where(other > gs, 1.0, jnp.where(other == gs, jnp.where(gi > gp, 1.0, 0.0), 0.0))
        beaten = beaten + wins
    gadd = jnp.where(beaten < float(TOPK_GROUPS), 0.0, neg)
    cur = jnp.concatenate(
        [sel[g * gsz:(g + 1) * gsz] + gadd[g:g + 1] for g in range(N_EXPERT_GROUPS)], axis=0)

    eidx = lax.broadcasted_iota(jnp.int32, (ne, tn), 0).astype(F32)
    chosen = jnp.zeros((ne, tn), F32)
    idx_rows, s_rows = [], []
    for _ in range(TOP_K):
        m = jnp.max(cur, axis=0, keepdims=True)
        ik = jnp.min(jnp.where(cur == m, eidx, float(ne)), axis=0, keepdims=True)
        oh = eidx == ik
        s_rows.append(jnp.sum(jnp.where(oh, s, 0.0), axis=0, keepdims=True))
        cur = jnp.where(oh, neg, cur)
        chosen = chosen + jnp.where(oh, 1.0, 0.0)
        idx_rows.append(ik)
    sk = jnp.concatenate(s_rows, axis=0)
    w_ref[...] = sk / jnp.sum(sk, axis=0, keepdims=True) * ROUTED_SCALE
    idx_ref[...] = jnp.concatenate(idx_rows, axis=0).astype(jnp.int32)

    before = _dot(chosen.astype(BF16), tri_sc[...]) + carry_sc[...]
    rank_rows = [jnp.sum(jnp.where(eidx == ik, before, 0.0), axis=0, keepdims=True) for ik in idx_rows]
    rank_ref[...] = jnp.concatenate(rank_rows, axis=0).astype(jnp.int32)
    carry_sc[...] = carry_sc[...] + jnp.sum(chosen, axis=1, keepdims=True)
    cnt_ref[...] = carry_sc[...]


def _route_call(scores_t, router_bias, tn=512):
    bsz, ne, seq = scores_t.shape
    tok = lambda b, i: (b, 0, i)
    return pl.pallas_call(
        _route_kernel,
        grid=(bsz, seq // tn),
        in_specs=[
            pl.BlockSpec((None, ne, tn), tok),
            pl.BlockSpec((ne, 1), lambda b, i: (0, 0)),
        ],
        out_specs=[
            pl.BlockSpec((None, TOP_K, tn), tok),
            pl.BlockSpec((None, TOP_K, tn), tok),
            pl.BlockSpec((None, TOP_K, tn), tok),
            pl.BlockSpec((ne, 1), lambda b, i: (0, 0)),
        ],
        out_shape=[
            jax.ShapeDtypeStruct((bsz, TOP_K, seq), jnp.int32),
            jax.ShapeDtypeStruct((bsz, TOP_K, seq), F32),
            jax.ShapeDtypeStruct((bsz, TOP_K, seq), jnp.int32),
            jax.ShapeDtypeStruct((ne, 1), F32),
        ],
        scratch_shapes=[pltpu.VMEM((tn, tn), BF16), pltpu.VMEM((ne, 1), F32)],
        compiler_params=pltpu.CompilerParams(
            dimension_semantics=("arbitrary", "arbitrary"), vmem_limit_bytes=VMEM_LIMIT),
        name="route",
    )(scores_t, router_bias.reshape(ne, 1))


def _dispatch_tables(idx, rank, counts):
    ne = counts.shape[0]
    n_tok = idx.shape[0] * idx.shape[2]
    counts = counts.astype(jnp.int32)
    pcounts = (counts + MOE_BLOCK - 1) // MOE_BLOCK * MOE_BLOCK
    pstarts = jnp.concatenate([jnp.zeros((1,), jnp.int32), jnp.cumsum(pcounts).astype(jnp.int32)])
    n_blocks = -(-(n_tok * TOP_K + ne * (MOE_BLOCK - 1)) // MOE_BLOCK)
    dest = _dest_call(pstarts[:ne], idx, rank)
    return dest, pstarts // MOE_BLOCK, counts, n_blocks


def _dest_kernel(ps_ref, idx_ref, rank_ref, o_ref):
    idx = idx_ref[...]

    def body(e, acc):
        return jnp.where(idx == e, ps_ref[e], acc)

    o_ref[...] = rank_ref[...] + lax.fori_loop(0, ps_ref.shape[0], body, jnp.zeros(idx.shape, jnp.int32))


def _dest_call(pstarts, idx, rank, tn=2048):
    bsz, kk, seq = idx.shape
    tok = lambda b, i, ps: (b, 0, i)
    grid_spec = pltpu.PrefetchScalarGridSpec(
        num_scalar_prefetch=1,
        grid=(bsz, seq // tn),
        in_specs=[pl.BlockSpec((None, kk, tn), tok), pl.BlockSpec((None, kk, tn), tok)],
        out_specs=pl.BlockSpec((None, kk, tn), tok),
    )
    return pl.pallas_call(
        _dest_kernel,
        grid_spec=grid_spec,
        out_shape=jax.ShapeDtypeStruct(idx.shape, jnp.int32),
        compiler_params=pltpu.CompilerParams(dimension_semantics=("parallel", "parallel")),
        name="dest",
    )(pstarts, idx, rank)


def _scatter_rows(rows, dest, n_out):
    n_tok, width = rows.shape
    k_slots = dest.shape[0]
    win = LANES
    mesh = plsc.VectorSubcoreMesh(core_axis_name="c", subcore_axis_name="s")

    @pl.kernel(out_type=jax.ShapeDtypeStruct((n_out, width), rows.dtype), mesh=mesh, scratch_types=[])
    def scatter(rows_hbm, dest_hbm, out_hbm):
        def body(rows_vmem, idx_vmem):
            pltpu.sync_copy(rows_vmem, out_hbm.at[idx_vmem.at[0]])

        pltpu.emit_pipeline(
            body,
            grid=(n_tok // win, k_slots),
            in_specs=[
                pl.BlockSpec((win, width), lambda i, k: (i, 0)),
                pl.BlockSpec((1, win), lambda i, k: (k, i)),
            ],
            out_specs=[],
            core_axis_name=("c", "s"),
            dimension_semantics=(pltpu.PARALLEL, pltpu.ARBITRARY),
        )(rows_hbm, dest_hbm)

    return scatter(rows, dest)


def kernel(x, c, w_ada, b_ada, norm1_g, w_in, w_out, lambda_q1, lambda_k1, lambda_q2, lambda_k2, subln_g,
           ssm_a_re, ssm_a_im, ssm_log_step, ssm_b_re, ssm_b_im, ssm_c_re, ssm_c_im, ssm_d, w_glu, b_glu,
           norm2_g, w_router, router_bias, w_gate, w_up, w_down, ws_gate, ws_up, ws_down, final_g):
    bsz, seq, dm = x.shape
    n_tok = bsz * seq
    aw = N_ATTN_HEADS * V_HEAD_DIM

    mod3 = _mod_call(c, w_ada[0], b_ada[0]).reshape(bsz, N_ADA, dm)

    wqkv = w_in[0][:, :3 * aw].astype(BF16)
    wut = w_in[0][:, 3 * aw:].T.astype(BF16)
    q, k, v, ut = _inproj_call(x, mod3, norm1_g[0], wqkv, wut)

    lam = (jnp.exp(jnp.sum(lambda_q1[0] * lambda_k1[0])) - jnp.exp(jnp.sum(lambda_q2[0] * lambda_k2[0]))
           + LAM_INIT).reshape(1)
    attn = _attn_call(lam, q, k, v, subln_g[0])

    yt = _ssm_call(ut, *_ssm_operators(ssm_a_re[0], ssm_a_im[0], ssm_log_step[0], ssm_b_re[0], ssm_b_im[0],
                                       ssm_c_re[0], ssm_c_im[0]))

    xp, h2, scores_t = _mid_call(
        x, attn, yt, ut, mod3, ssm_d[0], w_glu[0].T.astype(BF16), b_glu[0],
        w_out[0][:aw].astype(BF16), w_out[0][aw:].astype(BF16), norm2_g[0], w_router[0].T,
        ws_gate[0].astype(BF16), ws_up[0].astype(BF16), ws_down[0].astype(BF16))

    idx, w, rank, counts = _route_call(scores_t, router_bias[0])
    dest, chunk_start, counts, n_blocks = _dispatch_tables(idx, rank, counts.reshape(-1))
    n_rows = n_blocks * MOE_BLOCK

    dest_k = jnp.swapaxes(dest, 0, 1).reshape(TOP_K, n_tok)
    dest_half = jnp.concatenate([dest_k, dest_k + n_rows], axis=1)
    xs = _scatter_rows(h2.reshape(2 * n_tok, dm // 4), dest_half, 2 * n_rows).reshape(2, n_rows, dm // 4)

    out = _moe_call(chunk_start, counts, xs, w_gate[0], w_up[0], w_down[0])
    rows = jnp.take(out, dest, axis=0)
    return _final_call(xp, rows, jnp.swapaxes(w, 1, 2), mod3, final_g)
```

```python
import functools
import math

import jax
import jax.numpy as jnp
from jax import lax
from jax.experimental import pallas as pl
from jax.experimental.pallas import tpu as pltpu
from jax.experimental.pallas import tpu_sc as plsc

F32 = jnp.float32
BF16 = jnp.bfloat16

N_ATTN_HEADS = 4
ATTN_HEAD_DIM = 64
V_HEAD_DIM = 128
SSM_GROUP = 16
N_SSM_GROUPS = 32
SSM_STATE = 64
N_EXPERTS = 256
TOP_K = 8
N_EXPERT_GROUPS = 8
TOPK_GROUPS = 4
ROUTED_SCALE = 2.5
NORM_EPS = 1e-6
SUBLN_EPS = 1e-5
N_ADA = 6
LAM_INIT = 0.8 - 0.6 * math.exp(-0.3 * 0)

LANES = 128
MOE_BLOCK = 256
MOE_SLOTS = 4
NEG_BIG = -1e30
LOG2_E = math.log2(math.e)
VMEM_LIMIT = 48 * 1024 * 1024


def _split_bf16(a):
    hi = a.astype(BF16)
    lo = (a - hi.astype(F32)).astype(BF16)
    return hi, lo


def _dot(a, b):
    return jnp.dot(a, b, preferred_element_type=F32)


def _dot_nt(a, b):
    return lax.dot_general(a, b, (((1,), (1,)), ((), ())), preferred_element_type=F32)


def _dot3(a, b):
    ah, al = _split_bf16(a)
    bh, bl = _split_bf16(b)
    return _dot(ah, bh) + _dot(ah, bl) + _dot(al, bh)


def _dot3_nt(a, b):
    ah, al = _split_bf16(a)
    bh, bl = _split_bf16(b)
    return _dot_nt(ah, bh) + _dot_nt(ah, bl) + _dot_nt(al, bh)


def _silu(x):
    return x * jax.nn.sigmoid(x)


def _gelu_tanh(x):
    c = math.sqrt(2.0 / math.pi)
    return 0.5 * x * (1.0 + jnp.tanh(c * (x + 0.044715 * (x * x * x))))


def _mod_kernel(c_ref, w_ref, b_ref, o_ref):
    cond = _silu(c_ref[...])
    o_ref[...] = _dot3(cond, w_ref[...]) + b_ref[...]


def _mod_call(c, w_ada, b_ada):
    bsz, dm = c.shape
    n_out = w_ada.shape[1]
    tn = 1024
    return pl.pallas_call(
        _mod_kernel,
        grid=(n_out // tn,),
        in_specs=[
            pl.BlockSpec((bsz, dm), lambda j: (0, 0)),
            pl.BlockSpec((dm, tn), lambda j: (0, j)),
            pl.BlockSpec((1, tn), lambda j: (0, j)),
        ],
        out_specs=pl.BlockSpec((bsz, tn), lambda j: (0, j)),
        out_shape=jax.ShapeDtypeStruct((bsz, n_out), F32),
        compiler_params=pltpu.CompilerParams(vmem_limit_bytes=VMEM_LIMIT),
        name="mod",
    )(c, w_ada, b_ada.reshape(1, n_out))


def _inproj_kernel(x_ref, mod_ref, g_ref, wqkv_ref, wut_ref, q_ref, k_ref, v_ref, ut_ref, *, aw):
    x = x_ref[...]
    ms = jnp.mean(x * x, axis=-1, keepdims=True)
    h = x * lax.rsqrt(ms + NORM_EPS) * g_ref[...]
    h = h * (1.0 + mod_ref[1:2, :]) + mod_ref[0:1, :]
    hb = h.astype(BF16)
    qkv = _dot(hb, wqkv_ref[...])
    q_ref[...] = (qkv[:, :aw] * (LOG2_E * ATTN_HEAD_DIM ** -0.5)).astype(BF16)
    k_ref[...] = qkv[:, aw:2 * aw].astype(BF16)
    v_ref[...] = qkv[:, 2 * aw:].astype(BF16)
    ut_ref[...] = _dot_nt(wut_ref[...], hb)


def _inproj_call(x, mod3, norm_g, wqkv, wut, tm=512):
    bsz, seq, dm = x.shape
    aw = wqkv.shape[1] // 3
    sw = wut.shape[0]
    row = lambda b, i: (b, i, 0)
    return pl.pallas_call(
        functools.partial(_inproj_kernel, aw=aw),
        grid=(bsz, seq // tm),
        in_specs=[
            pl.BlockSpec((None, tm, dm), row),
            pl.BlockSpec((None, N_ADA, dm), lambda b, i: (b, 0, 0)),
            pl.BlockSpec((1, dm), lambda b, i: (0, 0)),
            pl.BlockSpec(wqkv.shape, lambda b, i: (0, 0)),
            pl.BlockSpec(wut.shape, lambda b, i: (0, 0)),
        ],
        out_specs=[
            pl.BlockSpec((None, tm, aw), row),
            pl.BlockSpec((None, tm, aw), row),
            pl.BlockSpec((None, tm, aw), row),
            pl.BlockSpec((None, sw, tm), lambda b, i: (b, 0, i)),
        ],
        out_shape=[
            jax.ShapeDtypeStruct((bsz, seq, aw), BF16),
            jax.ShapeDtypeStruct((bsz, seq, aw), BF16),
            jax.ShapeDtypeStruct((bsz, seq, aw), BF16),
            jax.ShapeDtypeStruct((bsz, sw, seq), F32),
        ],
        compiler_params=pltpu.CompilerParams(
            dimension_semantics=("parallel", "parallel"), vmem_limit_bytes=VMEM_LIMIT),
        name="inproj",
    )(x, mod3, norm_g.reshape(1, dm), wqkv, wut)


def _attn_kernel(lam_ref, q_ref, k_ref, v_ref, g_ref, o_ref, m_sc, l_sc, acc_sc, *, tq):
    i = pl.program_id(2)
    q = q_ref[...]
    lane = lax.broadcasted_iota(jnp.int32, q.shape, 1)
    zero = jnp.zeros_like(q)
    q2 = jnp.concatenate([jnp.where(lane < ATTN_HEAD_DIM, q, zero),
                          jnp.where(lane >= ATTN_HEAD_DIM, q, zero)], axis=0)
    m_sc[...] = jnp.full(m_sc.shape, NEG_BIG, F32)
    l_sc[...] = jnp.zeros(l_sc.shape, F32)
    acc_sc[...] = jnp.zeros(acc_sc.shape, F32)

    def step(j, masked):
        start = pl.multiple_of(j * tq, tq)
        kt = k_ref[pl.ds(start, tq), :]
        vt = v_ref[pl.ds(start, tq), :]
        s = _dot_nt(q2, kt)
        if masked:
            r = lax.broadcasted_iota(jnp.int32, s.shape, 0)
            c = lax.broadcasted_iota(jnp.int32, s.shape, 1)
            s = jnp.where(c <= (r & (tq - 1)), s, NEG_BIG)
        m_prev = m_sc[...]
        m_new = jnp.maximum(m_prev, jnp.max(s, axis=-1, keepdims=True))
        alpha = jnp.exp2(m_prev - m_new)
        p = jnp.exp2(s - jnp.concatenate([m_new] * (tq // LANES), axis=1))
        psum = p[:, :LANES]
        for c0 in range(LANES, tq, LANES):
            psum = psum + p[:, c0:c0 + LANES]
        l_sc[...] = alpha * l_sc[...] + psum
        acc_sc[...] = alpha * acc_sc[...] + _dot(p.astype(BF16), vt)
        m_sc[...] = m_new

    def body(j, carry):
        step(j, False)
        return carry

    lax.fori_loop(0, i, body, 0)
    step(i, True)

    o_all = acc_sc[...] / jnp.sum(l_sc[...], axis=-1, keepdims=True)
    o = o_all[:tq] - lam_ref[0] * o_all[tq:]
    ms = jnp.mean(o * o, axis=-1, keepdims=True)
    o = o * lax.rsqrt(ms + SUBLN_EPS) * g_ref[...] * (1.0 - LAM_INIT)
    o_ref[...] = o.astype(o_ref.dtype)


def _attn_call(lam, q, k, v, subln_g, tq=1024):
    bsz, seq, aw = q.shape
    nh = aw // V_HEAD_DIM
    qmap = lambda b, h, i: (b, i, h)
    kvmap = lambda b, h, i: (b, 0, h)
    return pl.pallas_call(
        functools.partial(_attn_kernel, tq=tq),
        grid=(bsz, nh, seq // tq),
        in_specs=[
            pl.BlockSpec(memory_space=pltpu.SMEM),
            pl.BlockSpec((None, tq, V_HEAD_DIM), qmap),
            pl.BlockSpec((None, seq, V_HEAD_DIM), kvmap),
            pl.BlockSpec((None, seq, V_HEAD_DIM), kvmap),
            pl.BlockSpec((1, V_HEAD_DIM), lambda b, h, i: (0, 0)),
        ],
        out_specs=pl.BlockSpec((None, tq, V_HEAD_DIM), qmap),
        out_shape=jax.ShapeDtypeStruct((bsz, seq, aw), BF16),
        scratch_shapes=[pltpu.VMEM((2 * tq, V_HEAD_DIM), F32)] * 3,
        compiler_params=pltpu.CompilerParams(
            dimension_semantics=("parallel", "parallel", "parallel"), vmem_limit_bytes=VMEM_LIMIT),
        name="attn",
    )(lam, q, k, v, subln_g.reshape(1, V_HEAD_DIM))


def _ssm_operators(a_re, a_im, log_step, b_re, b_im, c_re, c_im):
    t = LANES
    hi = lax.Precision.HIGHEST
    lam = lax.complex(jnp.minimum(a_re, -1e-4), a_im)
    delta = jnp.exp(log_step)[:, None]
    lam_bar = jnp.exp(lam * delta)
    bbar = ((lam_bar - 1.0) / lam)[:, :, None] * lax.complex(b_re, b_im)
    cmat = lax.complex(c_re, c_im)
    ld = lam * delta
    tau = jnp.arange(t + 1, dtype=F32)
    pw = jnp.exp(ld[:, :, None] * tau)
    cb = cmat[:, None, :, :] * jnp.swapaxes(bbar, 1, 2)[:, :, None, :]
    g = a_re.shape[0]
    cb = cb.reshape(g, SSM_GROUP * SSM_GROUP, SSM_STATE)
    cbcat = jnp.concatenate([cb.real, -cb.imag], axis=-1)
    pcat = jnp.concatenate([pw.real[:, :, :t], pw.imag[:, :, :t]], axis=1)
    kmat = jnp.einsum('gxp,gpt->gxt', cbcat, pcat, precision=hi)
    prev = jnp.swapaxes(pw[:, :, t - 1::-1], 1, 2)
    arev = jnp.concatenate([prev.real, prev.imag], axis=-1)
    bt = jnp.swapaxes(bbar, 1, 2)
    brow = jnp.stack([jnp.concatenate([bt.real, bt.real], axis=-1),
                      jnp.concatenate([-bt.imag, bt.imag], axis=-1)], axis=2).reshape(g, 2 * SSM_GROUP, 2 * SSM_STATE)
    a1 = jnp.concatenate([pw.real[:, :, 1:], pw.imag[:, :, 1:]], axis=1)
    ct = jnp.swapaxes(cmat, 1, 2)
    ccol = jnp.stack([jnp.concatenate([ct.real, -ct.real], axis=1),
                      jnp.concatenate([-ct.imag, -ct.imag], axis=1)], axis=-1).reshape(g, 2 * SSM_STATE, 2 * SSM_GROUP)
    rows = []
    for i in range(6):
        d = jnp.exp(ld * float(t * (1 << i)))
        rows.append(jnp.concatenate([d.real, d.real], axis=-1))
        rows.append(jnp.concatenate([-d.imag, d.imag], axis=-1))
    rows += [jnp.zeros_like(rows[0])] * 4
    dpow = jnp.stack(rows, axis=1)
    return kmat, arev, brow, a1, ccol, dpow


def _ssm_kernel(u_ref, k_ref, arev_ref, brow_ref, a1_ref, ccol_ref, dp_ref, y_ref, m_sc, ws_sc, wc_sc):
    bsz, nch, n_chunk, t = u_ref.shape
    row = lax.broadcasted_iota(jnp.int32, (t, t), 0)
    col = lax.broadcasted_iota(jnp.int32, (t, t), 1)
    causal = col >= row
    half = arev_ref.shape[1] // 2

    arev = arev_ref[...]
    arev_sw = pltpu.roll(arev, half, 1)
    a1 = a1_ref[...]
    a1_sw = pltpu.roll(a1, half, 0)
    for c in range(nch):
        ws_sc[c * t:(c + 1) * t, :] = (arev * brow_ref[2 * c:2 * c + 1, :]
                                       + arev_sw * brow_ref[2 * c + 1:2 * c + 2, :]).astype(BF16)
        wc_sc[:, c * t:(c + 1) * t] = (a1 * ccol_ref[:, 2 * c:2 * c + 1]
                                       + a1_sw * ccol_ref[:, 2 * c + 1:2 * c + 2]).astype(BF16)

    def build(ci, carry):
        r0 = pl.multiple_of(ci * t, t)
        for co in range(nch):
            kr = k_ref[pl.ds(ci * nch + co, 1), :]
            kb = jnp.broadcast_to(kr, (t, t))
            kb = pltpu.roll(kb, 0, 1, stride=1, stride_axis=0)
            m_sc[pl.ds(r0, t), co * t:(co + 1) * t] = jnp.where(causal, kb, 0.0).astype(BF16)
        return carry

    lax.fori_loop(0, nch, build, 0)

    uflat = jnp.concatenate(
        [jnp.concatenate([u_ref[b, ci] for b in range(bsz)], axis=0) for ci in range(nch)],
        axis=1).astype(BF16)
    y = _dot(uflat, m_sc[...])
    z = _dot(uflat, ws_sc[...])
    kidx = lax.broadcasted_iota(jnp.int32, z.shape, 0) & (n_chunk - 1)
    shift = 1
    i = 0
    while shift < n_chunk:
        zs = jnp.where(kidx >= shift, pltpu.roll(z, shift, 0), 0.0)
        z = z + zs * dp_ref[2 * i:2 * i + 1, :] + pltpu.roll(zs, half, 1) * dp_ref[2 * i + 1:2 * i + 2, :]
        shift *= 2
        i += 1
    xin = jnp.where(kidx >= 1, pltpu.roll(z, 1, 0), 0.0)
    xh, xl = _split_bf16(xin)
    wc = wc_sc[...]
    y = y + _dot(xh, wc) + _dot(xl, wc)
    for b in range(bsz):
        for co in range(nch):
            y_ref[b, co] = y[b * n_chunk:(b + 1) * n_chunk, co * t:(co + 1) * t]


def _ssm_call(ut, kmat, arev, brow, a1, ccol, dpow):
    bsz, sw, seq = ut.shape
    n_groups = sw // SSM_GROUP
    n_chunk = seq // LANES
    assert n_chunk & (n_chunk - 1) == 0 and n_chunk <= 64
    u4 = ut.reshape(bsz, sw, n_chunk, LANES)
    blk = (bsz, SSM_GROUP, n_chunk, LANES)
    gmap = lambda g: (0, g, 0, 0)
    pmap = lambda g: (g, 0, 0)
    y4 = pl.pallas_call(
        _ssm_kernel,
        grid=(n_groups,),
        in_specs=[
            pl.BlockSpec(blk, gmap),
            pl.BlockSpec((None,) + kmat.shape[1:], pmap),
            pl.BlockSpec((None,) + arev.shape[1:], pmap),
            pl.BlockSpec((None,) + brow.shape[1:], pmap),
            pl.BlockSpec((None,) + a1.shape[1:], pmap),
            pl.BlockSpec((None,) + ccol.shape[1:], pmap),
            pl.BlockSpec((None,) + dpow.shape[1:], pmap),
        ],
        out_specs=pl.BlockSpec(blk, gmap),
        out_shape=jax.ShapeDtypeStruct(u4.shape, F32),
        scratch_shapes=[pltpu.VMEM((SSM_GROUP * LANES, SSM_GROUP * LANES), BF16),
                        pltpu.VMEM((SSM_GROUP * LANES, 2 * SSM_STATE), BF16),
                        pltpu.VMEM((2 * SSM_STATE, SSM_GROUP * LANES), BF16)],
        compiler_params=pltpu.CompilerParams(
            dimension_semantics=("parallel",), vmem_limit_bytes=VMEM_LIMIT),
        name="ssm",
    )(u4, kmat, arev, brow, a1, ccol, dpow)
    return y4.reshape(bsz, sw, seq)


def _mid_kernel(x_ref, attn_ref, yt_ref, ut_ref, mod_ref, dsk_ref, wglut_ref, bglu_ref, wo1_ref, wo2_ref,
                g2_ref, wrt_ref, wsg_ref, wsu_ref, wsd_ref, xp_ref, h2_ref, lg_ref):
    gt = _gelu_tanh(yt_ref[...] + dsk_ref[...] * ut_ref[...])
    zt = _dot(wglut_ref[...], gt.astype(BF16)) + bglu_ref[...]
    st = gt * jax.nn.sigmoid(zt)
    s = st.T.astype(BF16)
    mix = _dot(attn_ref[...], wo1_ref[...]) + _dot(s, wo2_ref[...])
    x1 = x_ref[...] + mod_ref[2:3, :] * mix
    ms = jnp.mean(x1 * x1, axis=-1, keepdims=True)
    h2 = x1 * lax.rsqrt(ms + NORM_EPS) * g2_ref[...]
    h2 = h2 * (1.0 + mod_ref[4:5, :]) + mod_ref[3:4, :]
    hb = h2.astype(BF16)
    bits = lax.bitcast_convert_type(hb.astype(F32), jnp.uint32)
    half = bits.shape[1] // 2
    packed = (bits[:, :half] >> 16) | (bits[:, half:] & jnp.uint32(0xFFFF0000))
    h2_ref[0] = packed[:, :half // 2]
    h2_ref[1] = packed[:, half // 2:]
    lg_ref[...] = jax.nn.sigmoid(_dot3_nt(wrt_ref[...], h2))
    sh = _silu(_dot(hb, wsg_ref[...])) * _dot(hb, wsu_ref[...])
    shared = _dot(sh.astype(BF16), wsd_ref[...])
    xp_ref[...] = x1 + mod_ref[5:6, :] * shared


def _mid_call(x, attn, yt, ut, mod3, dsk, wglut, bglu, wo1, wo2, g2, wrt, wsg, wsu, wsd, tm=512):
    bsz, seq, dm = x.shape
    aw = attn.shape[2]
    sw = yt.shape[1]
    ne = wrt.shape[0]
    row = lambda b, i: (b, i, 0)
    colm = lambda b, i: (b, 0, i)
    full = lambda a: pl.BlockSpec(a.shape, lambda b, i: (0,) * a.ndim)
    dsk = dsk.reshape(sw, 1)
    bglu = bglu.reshape(sw, 1)
    g2 = g2.reshape(1, dm)
    return pl.pallas_call(
        _mid_kernel,
        grid=(bsz, seq // tm),
        in_specs=[
            pl.BlockSpec((None, tm, dm), row),
            pl.BlockSpec((None, tm, aw), row),
            pl.BlockSpec((None, sw, tm), colm),
            pl.BlockSpec((None, sw, tm), colm),
            pl.BlockSpec((None, N_ADA, dm), lambda b, i: (b, 0, 0)),
            full(dsk), full(wglut), full(bglu), full(wo1), full(wo2), full(g2), full(wrt),
            full(wsg), full(wsu), full(wsd),
        ],
        out_specs=[
            pl.BlockSpec((None, tm, dm), row),
            pl.BlockSpec((2, None, tm, dm // 4), lambda b, i: (0, b, i, 0)),
            pl.BlockSpec((None, ne, tm), colm),
        ],
        out_shape=[
            jax.ShapeDtypeStruct((bsz, seq, dm), F32),
            jax.ShapeDtypeStruct((2, bsz, seq, dm // 4), jnp.uint32),
            jax.ShapeDtypeStruct((bsz, ne, seq), F32),
        ],
        compiler_params=pltpu.CompilerParams(
            dimension_semantics=("parallel", "parallel"), vmem_limit_bytes=VMEM_LIMIT),
        name="mid",
    )(x, attn, yt, ut, mod3, dsk, wglut, bglu, wo1, wo2, g2, wrt, wsg, wsu, wsd)


def _moe_kernel(cs_ref, cnt_ref, xs_hbm, wg_ref, wu_ref, wd_ref, out_hbm, xbuf, obuf, wgb, wub, wdb, isem, osem):
    e = pl.program_id(0)
    n_exp = pl.num_programs(0)
    c0, c1, c_end = cs_ref[e], cs_ref[e + 1], cs_ref[n_exp]
    n_slots, _, ch, q = xbuf.shape

    def fetch(c, slot):
        return pltpu.make_async_copy(xs_hbm.at[:, pl.ds(c * ch, ch), :], xbuf.at[slot], isem.at[slot])

    def drain(c, slot):
        return pltpu.make_async_copy(obuf.at[slot], out_hbm.at[pl.ds(c * ch, ch), :], osem.at[slot])

    row_queue = 1

    @pl.when(e == 0)
    def _():
        for c in range(n_slots - 1):
            @pl.when(c < c_end)
            def _():
                fetch(c, c).start(priority=row_queue)

    @pl.when(c1 > c0)
    def _():
        wgb[...] = wg_ref[...].astype(BF16)
        wub[...] = wu_ref[...].astype(BF16)
        wdb[...] = wd_ref[...].astype(BF16)

        def chunk(c, carry):
            slot = lax.rem(c, n_slots)
            fetch(c, slot).wait()
            ahead = c + n_slots - 1

            @pl.when(ahead < c_end)
            def _():
                fetch(ahead, lax.rem(ahead, n_slots)).start(priority=row_queue)

            @pl.when(c >= n_slots)
            def _():
                drain(c - n_slots, slot).wait()

            n_valid = cnt_ref[e] - (c - c0) * ch
            valid = lax.broadcasted_iota(jnp.int32, (ch, q), 0) < n_valid
            gate = up = None
            for h in range(2):
                xu = xbuf[slot, h]
                lo = jnp.where(valid, lax.bitcast_convert_type(xu << 16, F32), 0.0).astype(BF16)
                hi = jnp.where(valid, lax.bitcast_convert_type(xu & jnp.uint32(0xFFFF0000), F32), 0.0).astype(BF16)
                for xpart, k0 in ((lo, h * q), (hi, (2 + h) * q)):
                    g = _dot(xpart, wgb[k0:k0 + q, :])
                    u = _dot(xpart, wub[k0:k0 + q, :])
                    gate = g if gate is None else gate + g
                    up = u if up is None else up + u
            hb = _silu(gate) * up
            obuf[slot] = _dot(hb.astype(BF16), wdb[...]).astype(obuf.dtype)
            drain(c, slot).start(priority=row_queue)
            return carry

        lax.fori_loop(c0, c1, chunk, 0)

    @pl.when(e == n_exp - 1)
    def _():
        for back in range(n_slots, 0, -1):
            @pl.when(c_end >= back)
            def _():
                drain(c_end - back, lax.rem(c_end - back, n_slots)).wait()


def _moe_call(chunk_start, counts, xs, w_gate, w_up, w_down):
    n_rows = xs.shape[1]
    n_exp, dm, de = w_gate.shape
    grid_spec = pltpu.PrefetchScalarGridSpec(
        num_scalar_prefetch=2,
        grid=(n_exp,),
        in_specs=[
            pl.BlockSpec(memory_space=pl.ANY),
            pl.BlockSpec((None, dm, de), lambda e, cs, cnt: (e, 0, 0)),
            pl.BlockSpec((None, dm, de), lambda e, cs, cnt: (e, 0, 0)),
            pl.BlockSpec((None, de, dm), lambda e, cs, cnt: (e, 0, 0)),
        ],
        out_specs=pl.BlockSpec(memory_space=pl.ANY),
        scratch_shapes=[
            pltpu.VMEM((MOE_SLOTS, 2, MOE_BLOCK, dm // 4), jnp.uint32), pltpu.VMEM((MOE_SLOTS, MOE_BLOCK, dm), BF16),
            pltpu.VMEM((dm, de), BF16), pltpu.VMEM((dm, de), BF16), pltpu.VMEM((de, dm), BF16),
            pltpu.SemaphoreType.DMA((MOE_SLOTS,)), pltpu.SemaphoreType.DMA((MOE_SLOTS,)),
        ],
    )
    return pl.pallas_call(
        _moe_kernel,
        grid_spec=grid_spec,
        out_shape=jax.ShapeDtypeStruct((n_rows, dm), BF16),
        compiler_params=pltpu.CompilerParams(
            dimension_semantics=("arbitrary",), vmem_limit_bytes=VMEM_LIMIT),
        name="moe",
    )(chunk_start, counts, xs, w_gate, w_up, w_down)


def _final_kernel(xp_ref, r_ref, w_ref, mod_ref, g_ref, o_ref):
    w = w_ref[...]
    routed = w[:, 0:1] * r_ref[0].astype(F32)
    for k in range(1, r_ref.shape[0]):
        routed = routed + w[:, k:k + 1] * r_ref[k].astype(F32)
    x = xp_ref[...] + mod_ref[5:6, :] * routed
    ms = jnp.mean(x * x, axis=-1, keepdims=True)
    o_ref[...] = x * lax.rsqrt(ms + NORM_EPS) * g_ref[...]


def _final_call(xp, rows, w, mod3, final_g, tm=256):
    bsz, seq, dm = xp.shape
    kk = rows.shape[1]
    row = lambda b, i: (b, i, 0)
    return pl.pallas_call(
        _final_kernel,
        grid=(bsz, seq // tm),
        in_specs=[
            pl.BlockSpec((None, tm, dm), row),
            pl.BlockSpec((None, kk, tm, dm), lambda b, i: (b, 0, i, 0)),
            pl.BlockSpec((None, tm, kk), row),
            pl.BlockSpec((None, N_ADA, dm), lambda b, i: (b, 0, 0)),
            pl.BlockSpec((1, dm), lambda b, i: (0, 0)),
        ],
        out_specs=pl.BlockSpec((None, tm, dm), row),
        out_shape=jax.ShapeDtypeStruct((bsz, seq, dm), F32),
        compiler_params=pltpu.CompilerParams(
            dimension_semantics=("parallel", "parallel"), vmem_limit_bytes=VMEM_LIMIT),
        name="final",
    )(xp, rows, w, mod3, final_g.reshape(1, dm))


def _route_kernel(sc_ref, bias_ref, idx_ref, w_ref, rank_ref, cnt_ref, tri_sc, carry_sc):
    ne, tn = sc_ref.shape
    gsz = ne // N_EXPERT_GROUPS
    neg = -jnp.inf
    first = jnp.logical_and(pl.program_id(0) == 0, pl.program_id(1) == 0)

    @pl.when(first)
    def _():
        r = lax.broadcasted_iota(jnp.int32, (tn, tn), 0)
        c = lax.broadcasted_iota(jnp.int32, (tn, tn), 1)
        tri_sc[...] = jnp.where(r < c, 1.0, 0.0).astype(BF16)
        carry_sc[...] = jnp.zeros(carry_sc.shape, F32)

    s = sc_ref[...]
    sel = s + bias_ref[...]
    gs = []
    for g in range(N_EXPERT_GROUPS):
        blk = sel[g * gsz:(g + 1) * gsz]
        m1 = jnp.max(blk, axis=0, keepdims=True)
        eq = blk == m1
        n_eq = jnp.sum(jnp.where(eq, 1.0, 0.0), axis=0, keepdims=True)
        m2 = jnp.max(jnp.where(eq, neg, blk), axis=0, keepdims=True)
        gs.append(m1 + jnp.where(n_eq >= 2.0, m1, m2))
    gs = jnp.concatenate(gs, axis=0)
    gi = lax.broadcasted_iota(jnp.int32, gs.shape, 0)
    beaten = jnp.zeros(gs.shape, F32)
    for gp in range(N_EXPERT_GROUPS):
        other = gs[gp:gp + 1]
        wins = jnp.where(other > gs, 1.0, jnp.where(other == gs, jnp.where(gi > gp, 1.0, 0.0), 0.0))
        beaten = beaten + wins
    gadd = jnp.where(beaten < float(TOPK_GROUPS), 0.0, neg)
    cur = jnp.concatenate(
        [sel[g * gsz:(g + 1) * gsz] + gadd[g:g + 1] for g in range(N_EXPERT_GROUPS)], axis=0)

    eidx = lax.broadcasted_iota(jnp.int32, (ne, tn), 0).astype(F32)
    chosen = jnp.zeros((ne, tn), F32)
    idx_rows, s_rows = [], []
    for _ in range(TOP_K):
        m = jnp.max(cur, axis=0, keepdims=True)
        ik = jnp.min(jnp.where(cur == m, eidx, float(ne)), axis=0, keepdims=True)
        oh = eidx == ik
        s_rows.append(jnp.sum(jnp.where(oh, s, 0.0), axis=0, keepdims=True))
        cur = jnp.where(oh, neg, cur)
        chosen = chosen + jnp.where(oh, 1.0, 0.0)
        idx_rows.append(ik)
    sk = jnp.concatenate(s_rows, axis=0)
    w_ref[...] = sk / jnp.sum(sk, axis=0, keepdims=True) * ROUTED_SCALE
    idx_ref[...] = jnp.concatenate(idx_rows, axis=0).astype(jnp.int32)

    before = _dot(chosen.astype(BF16), tri_sc[...]) + carry_sc[...]
    rank_rows = [jnp.sum(jnp.where(eidx == ik, before, 0.0), axis=0, keepdims=True) for ik in idx_rows]
    rank_ref[...] = jnp.concatenate(rank_rows, axis=0).astype(jnp.int32)
    carry_sc[...] = carry_sc[...] + jnp.sum(chosen, axis=1, keepdims=True)
    cnt_ref[...] = carry_sc[...]


def _route_call(scores_t, router_bias, tn=512):
    bsz, ne, seq = scores_t.shape
    tok = lambda b, i: (b, 0, i)
    return pl.pallas_call(
        _route_kernel,
        grid=(bsz, seq // tn),
        in_specs=[
            pl.BlockSpec((None, ne, tn), tok),
            pl.BlockSpec((ne, 1), lambda b, i: (0, 0)),
        ],
        out_specs=[
            pl.BlockSpec((None, TOP_K, tn), tok),
            pl.BlockSpec((None, TOP_K, tn), tok),
            pl.BlockSpec((None, TOP_K, tn), tok),
            pl.BlockSpec((ne, 1), lambda b, i: (0, 0)),
        ],
        out_shape=[
            jax.ShapeDtypeStruct((bsz, TOP_K, seq), jnp.int32),
            jax.ShapeDtypeStruct((bsz, TOP_K, seq), F32),
            jax.ShapeDtypeStruct((bsz, TOP_K, seq), jnp.int32),
            jax.ShapeDtypeStruct((ne, 1), F32),
        ],
        scratch_shapes=[pltpu.VMEM((tn, tn), BF16), pltpu.VMEM((ne, 1), F32)],
        compiler_params=pltpu.CompilerParams(
            dimension_semantics=("arbitrary", "arbitrary"), vmem_limit_bytes=VMEM_LIMIT),
        name="route",
    )(scores_t, router_bias.reshape(ne, 1))


def _dispatch_tables(idx, rank, counts):
    ne = counts.shape[0]
    n_tok = idx.shape[0] * idx.shape[2]
    counts = counts.astype(jnp.int32)
    pcounts = (counts + MOE_BLOCK - 1) // MOE_BLOCK * MOE_BLOCK
    pstarts = jnp.concatenate([jnp.zeros((1,), jnp.int32), jnp.cumsum(pcounts).astype(jnp.int32)])
    n_blocks = -(-(n_tok * TOP_K + ne * (MOE_BLOCK - 1)) // MOE_BLOCK)
    dest = _dest_call(pstarts[:ne], idx, rank)
    return dest, pstarts // MOE_BLOCK, counts, n_blocks


def _dest_kernel(ps_ref, idx_ref, rank_ref, o_ref):
    idx = idx_ref[...]

    def body(e, acc):
        return jnp.where(idx == e, ps_ref[e], acc)

    o_ref[...] = rank_ref[...] + lax.fori_loop(0, ps_ref.shape[0], body, jnp.zeros(idx.shape, jnp.int32))


def _dest_call(pstarts, idx, rank, tn=2048):
    bsz, kk, seq = idx.shape
    tok = lambda b, i, ps: (b, 0, i)
    grid_spec = pltpu.PrefetchScalarGridSpec(
        num_scalar_prefetch=1,
        grid=(bsz, seq // tn),
        in_specs=[pl.BlockSpec((None, kk, tn), tok), pl.BlockSpec((None, kk, tn), tok)],
        out_specs=pl.BlockSpec((None, kk, tn), tok),
    )
    return pl.pallas_call(
        _dest_kernel,
        grid_spec=grid_spec,
        out_shape=jax.ShapeDtypeStruct(idx.shape, jnp.int32),
        compiler_params=pltpu.CompilerParams(dimension_semantics=("parallel", "parallel")),
        name="dest",
    )(pstarts, idx, rank)


def _scatter_rows(rows, dest, n_out):
    n_tok, width = rows.shape
    k_slots = dest.shape[0]
    win = LANES
    mesh = plsc.VectorSubcoreMesh(core_axis_name="c", subcore_axis_name="s")

    @pl.kernel(out_type=jax.ShapeDtypeStruct((n_out, width), rows.dtype), mesh=mesh, scratch_types=[])
    def scatter(rows_hbm, dest_hbm, out_hbm):
        def body(rows_vmem, idx_vmem):
            pltpu.sync_copy(rows_vmem, out_hbm.at[idx_vmem.at[0]])

        pltpu.emit_pipeline(
            body,
            grid=(n_tok // win, k_slots),
            in_specs=[
                pl.BlockSpec((win, width), lambda i, k: (i, 0)),
                pl.BlockSpec((1, win), lambda i, k: (k, i)),
            ],
            out_specs=[],
            core_axis_name=("c", "s"),
            dimension_semantics=(pltpu.PARALLEL, pltpu.ARBITRARY),
        )(rows_hbm, dest_hbm)

    return scatter(rows, dest)


def kernel(x, c, w_ada, b_ada, norm1_g, w_in, w_out, lambda_q1, lambda_k1, lambda_q2, lambda_k2, subln_g,
           ssm_a_re, ssm_a_im, ssm_log_step, ssm_b_re, ssm_b_im, ssm_c_re, ssm_c_im, ssm_d, w_glu, b_glu,
           norm2_g, w_router, router_bias, w_gate, w_up, w_down, ws_gate, ws_up, ws_down, final_g):
    bsz, seq, dm = x.shape
    n_tok = bsz * seq
    aw = N_ATTN_HEADS * V_HEAD_DIM

    mod3 = _mod_call(c, w_ada[0], b_ada[0]).reshape(bsz, N_ADA, dm)

    wqkv = w_in[0][:, :3 * aw].astype(BF16)
    wut = w_in[0][:, 3 * aw:].T.astype(BF16)
    q, k, v, ut = _inproj_call(x, mod3, norm1_g[0], wqkv, wut)

    lam = (jnp.exp(jnp.sum(lambda_q1[0] * lambda_k1[0])) - jnp.exp(jnp.sum(lambda_q2[0] * lambda_k2[0]))
           + LAM_INIT).reshape(1)
    attn = _attn_call(lam, q, k, v, subln_g[0])

    yt = _ssm_call(ut, *_ssm_operators(ssm_a_re[0], ssm_a_im[0], ssm_log_step[0], ssm_b_re[0], ssm_b_im[0],
                                       ssm_c_re[0], ssm_c_im[0]))

    xp, h2, scores_t = _mid_call(
        x, attn, yt, ut, mod3, ssm_d[0], w_glu[0].T.astype(BF16), b_glu[0],
        w_out[0][:aw].astype(BF16), w_out[0][aw:].astype(BF16), norm2_g[0], w_router[0].T,
        ws_gate[0].astype(BF16), ws_up[0].astype(BF16), ws_down[0].astype(BF16))

    idx, w, rank, counts = _route_call(scores_t, router_bias[0])
    dest, chunk_start, counts, n_blocks = _dispatch_tables(idx, rank, counts.reshape(-1))
    n_rows = n_blocks * MOE_BLOCK

    dest_k = jnp.swapaxes(dest, 0, 1).reshape(TOP_K, n_tok)
    dest_half = jnp.concatenate([dest_k, dest_k + n_rows], axis=1)
    xs = _scatter_rows(h2.reshape(2 * n_tok, dm // 4), dest_half, 2 * n_rows).reshape(2, n_rows, dm // 4)

    out = _moe_call(chunk_start, counts, xs, w_gate[0], w_up[0], w_down[0])
    rows = out.at[dest].get(mode="promise_in_bounds")
    return _final_call(xp, rows, jnp.swapaxes(w, 1, 2), mod3, final_g)
```

```python
import functools
import math

import jax
import jax.numpy as jnp
from jax import lax
from jax.experimental import pallas as pl
from jax.experimental.pallas import tpu as pltpu
from jax.experimental.pallas import tpu_sc as plsc

F32 = jnp.float32
BF16 = jnp.bfloat16

N_ATTN_HEADS = 4
ATTN_HEAD_DIM = 64
V_HEAD_DIM = 128
SSM_GROUP = 16
N_SSM_GROUPS = 32
SSM_STATE = 64
N_EXPERTS = 256
TOP_K = 8
N_EXPERT_GROUPS = 8
TOPK_GROUPS = 4
ROUTED_SCALE = 2.5
NORM_EPS = 1e-6
SUBLN_EPS = 1e-5
N_ADA = 6
LAM_INIT = 0.8 - 0.6 * math.exp(-0.3 * 0)

LANES = 128
MOE_BLOCK = 256
MOE_SLOTS = 4
NEG_BIG = -1e30
LOG2_E = math.log2(math.e)
VMEM_LIMIT = 48 * 1024 * 1024


def _split_bf16(a):
    hi = a.astype(BF16)
    lo = (a - hi.astype(F32)).astype(BF16)
    return hi, lo


def _dot(a, b):
    return jnp.dot(a, b, preferred_element_type=F32)


def _dot_nt(a, b):
    return lax.dot_general(a, b, (((1,), (1,)), ((), ())), preferred_element_type=F32)


def _dot3(a, b):
    ah, al = _split_bf16(a)
    bh, bl = _split_bf16(b)
    return _dot(ah, bh) + _dot(ah, bl) + _dot(al, bh)


def _dot3_nt(a, b):
    ah, al = _split_bf16(a)
    bh, bl = _split_bf16(b)
    return _dot_nt(ah, bh) + _dot_nt(ah, bl) + _dot_nt(al, bh)


def _silu(x):
    return x * jax.nn.sigmoid(x)


def _gelu_tanh(x):
    c = math.sqrt(2.0 / math.pi)
    return 0.5 * x * (1.0 + jnp.tanh(c * (x + 0.044715 * (x * x * x))))


def _pack_rows(x):
    bits = lax.bitcast_convert_type(x.astype(BF16).astype(F32), jnp.uint32)
    half = bits.shape[1] // 2
    packed = (bits[:, :half] >> 16) | (bits[:, half:] & jnp.uint32(0xFFFF0000))
    return packed[:, :half // 2], packed[:, half // 2:]


def _unpack_plane(xu):
    return (lax.bitcast_convert_type(xu << 16, F32),
            lax.bitcast_convert_type(xu & jnp.uint32(0xFFFF0000), F32))


def _mod_kernel(c_ref, w_ref, b_ref, o_ref):
    cond = _silu(c_ref[...])
    o_ref[...] = _dot3(cond, w_ref[...]) + b_ref[...]


def _mod_call(c, w_ada, b_ada):
    bsz, dm = c.shape
    n_out = w_ada.shape[1]
    tn = 1024
    return pl.pallas_call(
        _mod_kernel,
        grid=(n_out // tn,),
        in_specs=[
            pl.BlockSpec((bsz, dm), lambda j: (0, 0)),
            pl.BlockSpec((dm, tn), lambda j: (0, j)),
            pl.BlockSpec((1, tn), lambda j: (0, j)),
        ],
        out_specs=pl.BlockSpec((bsz, tn), lambda j: (0, j)),
        out_shape=jax.ShapeDtypeStruct((bsz, n_out), F32),
        compiler_params=pltpu.CompilerParams(vmem_limit_bytes=VMEM_LIMIT),
        name="mod",
    )(c, w_ada, b_ada.reshape(1, n_out))


def _inproj_kernel(x_ref, mod_ref, g_ref, wqkv_ref, wut_ref, q_ref, k_ref, v_ref, ut_ref, *, aw):
    x = x_ref[...]
    ms = jnp.mean(x * x, axis=-1, keepdims=True)
    h = x * lax.rsqrt(ms + NORM_EPS) * g_ref[...]
    h = h * (1.0 + mod_ref[1:2, :]) + mod_ref[0:1, :]
    hb = h.astype(BF16)
    qkv = _dot(hb, wqkv_ref[...])
    q_ref[...] = (qkv[:, :aw] * (LOG2_E * ATTN_HEAD_DIM ** -0.5)).astype(BF16)
    k_ref[...] = qkv[:, aw:2 * aw].astype(BF16)
    v_ref[...] = qkv[:, 2 * aw:].astype(BF16)
    ut_ref[...] = _dot_nt(wut_ref[...], hb)


def _inproj_call(x, mod3, norm_g, wqkv, wut, tm=512):
    bsz, seq, dm = x.shape
    aw = wqkv.shape[1] // 3
    sw = wut.shape[0]
    row = lambda b, i: (b, i, 0)
    return pl.pallas_call(
        functools.partial(_inproj_kernel, aw=aw),
        grid=(bsz, seq // tm),
        in_specs=[
            pl.BlockSpec((None, tm, dm), row),
            pl.BlockSpec((None, N_ADA, dm), lambda b, i: (b, 0, 0)),
            pl.BlockSpec((1, dm), lambda b, i: (0, 0)),
            pl.BlockSpec(wqkv.shape, lambda b, i: (0, 0)),
            pl.BlockSpec(wut.shape, lambda b, i: (0, 0)),
        ],
        out_specs=[
            pl.BlockSpec((None, tm, aw), row),
            pl.BlockSpec((None, tm, aw), row),
            pl.BlockSpec((None, tm, aw), row),
            pl.BlockSpec((None, sw, tm), lambda b, i: (b, 0, i)),
        ],
        out_shape=[
            jax.ShapeDtypeStruct((bsz, seq, aw), BF16),
            jax.ShapeDtypeStruct((bsz, seq, aw), BF16),
            jax.ShapeDtypeStruct((bsz, seq, aw), BF16),
            jax.ShapeDtypeStruct((bsz, sw, seq), F32),
        ],
        compiler_params=pltpu.CompilerParams(
            dimension_semantics=("parallel", "parallel"), vmem_limit_bytes=VMEM_LIMIT),
        name="inproj",
    )(x, mod3, norm_g.reshape(1, dm), wqkv, wut)


def _attn_kernel(lam_ref, q_ref, k_ref, v_ref, g_ref, o_ref, m_sc, l_sc, acc_sc, *, tq):
    i = pl.program_id(2)
    q = q_ref[...]
    lane = lax.broadcasted_iota(jnp.int32, q.shape, 1)
    zero = jnp.zeros_like(q)
    q2 = jnp.concatenate([jnp.where(lane < ATTN_HEAD_DIM, q, zero),
                          jnp.where(lane >= ATTN_HEAD_DIM, q, zero)], axis=0)
    m_sc[...] = jnp.full(m_sc.shape, NEG_BIG, F32)
    l_sc[...] = jnp.zeros(l_sc.shape, F32)
    acc_sc[...] = jnp.zeros(acc_sc.shape, F32)

    def step(j, masked):
        start = pl.multiple_of(j * tq, tq)
        kt = k_ref[pl.ds(start, tq), :]
        vt = v_ref[pl.ds(start, tq), :]
        s = _dot_nt(q2, kt)
        if masked:
            r = lax.broadcasted_iota(jnp.int32, s.shape, 0)
            c = lax.broadcasted_iota(jnp.int32, s.shape, 1)
            s = jnp.where(c <= (r & (tq - 1)), s, NEG_BIG)
        m_prev = m_sc[...]
        m_new = jnp.maximum(m_prev, jnp.max(s, axis=-1, keepdims=True))
        alpha = jnp.exp2(m_prev - m_new)
        p = jnp.exp2(s - jnp.concatenate([m_new] * (tq // LANES), axis=1))
        psum = p[:, :LANES]
        for c0 in range(LANES, tq, LANES):
            psum = psum + p[:, c0:c0 + LANES]
        l_sc[...] = alpha * l_sc[...] + psum
        acc_sc[...] = alpha * acc_sc[...] + _dot(p.astype(BF16), vt)
        m_sc[...] = m_new

    def body(j, carry):
        step(j, False)
        return carry

    lax.fori_loop(0, i, body, 0)
    step(i, True)

    o_all = acc_sc[...] / jnp.sum(l_sc[...], axis=-1, keepdims=True)
    o = o_all[:tq] - lam_ref[0] * o_all[tq:]
    ms = jnp.mean(o * o, axis=-1, keepdims=True)
    o = o * lax.rsqrt(ms + SUBLN_EPS) * g_ref[...] * (1.0 - LAM_INIT)
    o_ref[...] = o.astype(o_ref.dtype)


def _attn_call(lam, q, k, v, subln_g, tq=1024):
    bsz, seq, aw = q.shape
    nh = aw // V_HEAD_DIM
    qmap = lambda b, h, i: (b, i, h)
    kvmap = lambda b, h, i: (b, 0, h)
    return pl.pallas_call(
        functools.partial(_attn_kernel, tq=tq),
        grid=(bsz, nh, seq // tq),
        in_specs=[
            pl.BlockSpec(memory_space=pltpu.SMEM),
            pl.BlockSpec((None, tq, V_HEAD_DIM), qmap),
            pl.BlockSpec((None, seq, V_HEAD_DIM), kvmap),
            pl.BlockSpec((None, seq, V_HEAD_DIM), kvmap),
            pl.BlockSpec((1, V_HEAD_DIM), lambda b, h, i: (0, 0)),
        ],
        out_specs=pl.BlockSpec((None, tq, V_HEAD_DIM), qmap),
        out_shape=jax.ShapeDtypeStruct((bsz, seq, aw), BF16),
        scratch_shapes=[pltpu.VMEM((2 * tq, V_HEAD_DIM), F32)] * 3,
        compiler_params=pltpu.CompilerParams(
            dimension_semantics=("parallel", "parallel", "parallel"), vmem_limit_bytes=VMEM_LIMIT),
        name="attn",
    )(lam, q, k, v, subln_g.reshape(1, V_HEAD_DIM))


def _ssm_operators(a_re, a_im, log_step, b_re, b_im, c_re, c_im):
    t = LANES
    hi = lax.Precision.HIGHEST
    lam = lax.complex(jnp.minimum(a_re, -1e-4), a_im)
    delta = jnp.exp(log_step)[:, None]
    lam_bar = jnp.exp(lam * delta)
    bbar = ((lam_bar - 1.0) / lam)[:, :, None] * lax.complex(b_re, b_im)
    cmat = lax.complex(c_re, c_im)
    ld = lam * delta
    tau = jnp.arange(t + 1, dtype=F32)
    pw = jnp.exp(ld[:, :, None] * tau)
    cb = cmat[:, None, :, :] * jnp.swapaxes(bbar, 1, 2)[:, :, None, :]
    g = a_re.shape[0]
    cb = cb.reshape(g, SSM_GROUP * SSM_GROUP, SSM_STATE)
    cbcat = jnp.concatenate([cb.real, -cb.imag], axis=-1)
    pcat = jnp.concatenate([pw.real[:, :, :t], pw.imag[:, :, :t]], axis=1)
    kmat = jnp.einsum('gxp,gpt->gxt', cbcat, pcat, precision=hi)
    prev = jnp.swapaxes(pw[:, :, t - 1::-1], 1, 2)
    arev = jnp.concatenate([prev.real, prev.imag], axis=-1)
    bt = jnp.swapaxes(bbar, 1, 2)
    brow = jnp.stack([jnp.concatenate([bt.real, bt.real], axis=-1),
                      jnp.concatenate([-bt.imag, bt.imag], axis=-1)], axis=2).reshape(g, 2 * SSM_GROUP, 2 * SSM_STATE)
    a1 = jnp.concatenate([pw.real[:, :, 1:], pw.imag[:, :, 1:]], axis=1)
    ct = jnp.swapaxes(cmat, 1, 2)
    ccol = jnp.stack([jnp.concatenate([ct.real, -ct.real], axis=1),
                      jnp.concatenate([-ct.imag, -ct.imag], axis=1)], axis=-1).reshape(g, 2 * SSM_STATE, 2 * SSM_GROUP)
    rows = []
    for i in range(6):
        d = jnp.exp(ld * float(t * (1 << i)))
        rows.append(jnp.concatenate([d.real, d.real], axis=-1))
        rows.append(jnp.concatenate([-d.imag, d.imag], axis=-1))
    rows += [jnp.zeros_like(rows[0])] * 4
    dpow = jnp.stack(rows, axis=1)
    return kmat, arev, brow, a1, ccol, dpow


def _ssm_kernel(u_ref, k_ref, arev_ref, brow_ref, a1_ref, ccol_ref, dp_ref, y_ref, m_sc, ws_sc, wc_sc):
    bsz, nch, n_chunk, t = u_ref.shape
    row = lax.broadcasted_iota(jnp.int32, (t, t), 0)
    col = lax.broadcasted_iota(jnp.int32, (t, t), 1)
    causal = col >= row
    half = arev_ref.shape[1] // 2

    arev = arev_ref[...]
    arev_sw = pltpu.roll(arev, half, 1)
    a1 = a1_ref[...]
    a1_sw = pltpu.roll(a1, half, 0)
    for c in range(nch):
        ws_sc[c * t:(c + 1) * t, :] = (arev * brow_ref[2 * c:2 * c + 1, :]
                                       + arev_sw * brow_ref[2 * c + 1:2 * c + 2, :]).astype(BF16)
        wc_sc[:, c * t:(c + 1) * t] = (a1 * ccol_ref[:, 2 * c:2 * c + 1]
                                       + a1_sw * ccol_ref[:, 2 * c + 1:2 * c + 2]).astype(BF16)

    def build(ci, carry):
        r0 = pl.multiple_of(ci * t, t)
        for co in range(nch):
            kr = k_ref[pl.ds(ci * nch + co, 1), :]
            kb = jnp.broadcast_to(kr, (t, t))
            kb = pltpu.roll(kb, 0, 1, stride=1, stride_axis=0)
            m_sc[pl.ds(r0, t), co * t:(co + 1) * t] = jnp.where(causal, kb, 0.0).astype(BF16)
        return carry

    lax.fori_loop(0, nch, build, 0)

    uflat = jnp.concatenate(
        [jnp.concatenate([u_ref[b, ci] for b in range(bsz)], axis=0) for ci in range(nch)],
        axis=1).astype(BF16)
    y = _dot(uflat, m_sc[...])
    z = _dot(uflat, ws_sc[...])
    kidx = lax.broadcasted_iota(jnp.int32, z.shape, 0) & (n_chunk - 1)
    shift = 1
    i = 0
    while shift < n_chunk:
        zs = jnp.where(kidx >= shift, pltpu.roll(z, shift, 0), 0.0)
        z = z + zs * dp_ref[2 * i:2 * i + 1, :] + pltpu.roll(zs, half, 1) * dp_ref[2 * i + 1:2 * i + 2, :]
        shift *= 2
        i += 1
    xin = jnp.where(kidx >= 1, pltpu.roll(z, 1, 0), 0.0)
    xh, xl = _split_bf16(xin)
    wc = wc_sc[...]
    y = y + _dot(xh, wc) + _dot(xl, wc)
    for b in range(bsz):
        for co in range(nch):
            y_ref[b, co] = y[b * n_chunk:(b + 1) * n_chunk, co * t:(co + 1) * t]


def _ssm_call(ut, kmat, arev, brow, a1, ccol, dpow):
    bsz, sw, seq = ut.shape
    n_groups = sw // SSM_GROUP
    n_chunk = seq // LANES
    assert n_chunk & (n_chunk - 1) == 0 and n_chunk <= 64
    u4 = ut.reshape(bsz, sw, n_chunk, LANES)
    blk = (bsz, SSM_GROUP, n_chunk, LANES)
    gmap = lambda g: (0, g, 0, 0)
    pmap = lambda g: (g, 0, 0)
    y4 = pl.pallas_call(
        _ssm_kernel,
        grid=(n_groups,),
        in_specs=[
            pl.BlockSpec(blk, gmap),
            pl.BlockSpec((None,) + kmat.shape[1:], pmap),
            pl.BlockSpec((None,) + arev.shape[1:], pmap),
            pl.BlockSpec((None,) + brow.shape[1:], pmap),
            pl.BlockSpec((None,) + a1.shape[1:], pmap),
            pl.BlockSpec((None,) + ccol.shape[1:], pmap),
            pl.BlockSpec((None,) + dpow.shape[1:], pmap),
        ],
        out_specs=pl.BlockSpec(blk, gmap),
        out_shape=jax.ShapeDtypeStruct(u4.shape, F32),
        scratch_shapes=[pltpu.VMEM((SSM_GROUP * LANES, SSM_GROUP * LANES), BF16),
                        pltpu.VMEM((SSM_GROUP * LANES, 2 * SSM_STATE), BF16),
                        pltpu.VMEM((2 * SSM_STATE, SSM_GROUP * LANES), BF16)],
        compiler_params=pltpu.CompilerParams(
            dimension_semantics=("parallel",), vmem_limit_bytes=VMEM_LIMIT),
        name="ssm",
    )(u4, kmat, arev, brow, a1, ccol, dpow)
    return y4.reshape(bsz, sw, seq)


def _mid_kernel(x_ref, attn_ref, yt_ref, ut_ref, mod_ref, dsk_ref, wglut_ref, bglu_ref, wo1_ref, wo2_ref,
                g2_ref, wrt_ref, wsg_ref, wsu_ref, wsd_ref, xp_ref, h2_ref, lg_ref):
    gt = _gelu_tanh(yt_ref[...] + dsk_ref[...] * ut_ref[...])
    zt = _dot(wglut_ref[...], gt.astype(BF16)) + bglu_ref[...]
    st = gt * jax.nn.sigmoid(zt)
    s = st.T.astype(BF16)
    mix = _dot(attn_ref[...], wo1_ref[...]) + _dot(s, wo2_ref[...])
    x1 = x_ref[...] + mod_ref[2:3, :] * mix
    ms = jnp.mean(x1 * x1, axis=-1, keepdims=True)
    h2 = x1 * lax.rsqrt(ms + NORM_EPS) * g2_ref[...]
    h2 = h2 * (1.0 + mod_ref[4:5, :]) + mod_ref[3:4, :]
    hb = h2.astype(BF16)
    h2_ref[0], h2_ref[1] = _pack_rows(h2)
    lg_ref[...] = jax.nn.sigmoid(_dot3_nt(wrt_ref[...], h2))
    sh = _silu(_dot(hb, wsg_ref[...])) * _dot(hb, wsu_ref[...])
    shared = _dot(sh.astype(BF16), wsd_ref[...])
    xp_ref[...] = x1 + mod_ref[5:6, :] * shared


def _mid_call(x, attn, yt, ut, mod3, dsk, wglut, bglu, wo1, wo2, g2, wrt, wsg, wsu, wsd, tm=512):
    bsz, seq, dm = x.shape
    aw = attn.shape[2]
    sw = yt.shape[1]
    ne = wrt.shape[0]
    row = lambda b, i: (b, i, 0)
    colm = lambda b, i: (b, 0, i)
    full = lambda a: pl.BlockSpec(a.shape, lambda b, i: (0,) * a.ndim)
    dsk = dsk.reshape(sw, 1)
    bglu = bglu.reshape(sw, 1)
    g2 = g2.reshape(1, dm)
    return pl.pallas_call(
        _mid_kernel,
        grid=(bsz, seq // tm),
        in_specs=[
            pl.BlockSpec((None, tm, dm), row),
            pl.BlockSpec((None, tm, aw), row),
            pl.BlockSpec((None, sw, tm), colm),
            pl.BlockSpec((None, sw, tm), colm),
            pl.BlockSpec((None, N_ADA, dm), lambda b, i: (b, 0, 0)),
            full(dsk), full(wglut), full(bglu), full(wo1), full(wo2), full(g2), full(wrt),
            full(wsg), full(wsu), full(wsd),
        ],
        out_specs=[
            pl.BlockSpec((None, tm, dm), row),
            pl.BlockSpec((2, None, tm, dm // 4), lambda b, i: (0, b, i, 0)),
            pl.BlockSpec((None, ne, tm), colm),
        ],
        out_shape=[
            jax.ShapeDtypeStruct((bsz, seq, dm), F32),
            jax.ShapeDtypeStruct((2, bsz, seq, dm // 4), jnp.uint32),
            jax.ShapeDtypeStruct((bsz, ne, seq), F32),
        ],
        compiler_params=pltpu.CompilerParams(
            dimension_semantics=("parallel", "parallel"), vmem_limit_bytes=VMEM_LIMIT),
        name="mid",
    )(x, attn, yt, ut, mod3, dsk, wglut, bglu, wo1, wo2, g2, wrt, wsg, wsu, wsd)


def _moe_kernel(cs_ref, cnt_ref, xs_hbm, wg_ref, wu_ref, wd_ref, out_hbm, xbuf, obuf, wgb, wub, wdb, isem, osem):
    e = pl.program_id(0)
    n_exp = pl.num_programs(0)
    c0, c1, c_end = cs_ref[e], cs_ref[e + 1], cs_ref[n_exp]
    n_slots, _, ch, q = xbuf.shape

    def fetch(c, slot):
        return pltpu.make_async_copy(xs_hbm.at[:, pl.ds(c * ch, ch), :], xbuf.at[slot], isem.at[slot])

    def drain(c, slot):
        return pltpu.make_async_copy(obuf.at[slot], out_hbm.at[:, pl.ds(c * ch, ch), :], osem.at[slot])

    row_queue = 1

    @pl.when(e == 0)
    def _():
        for c in range(n_slots - 1):
            @pl.when(c < c_end)
            def _():
                fetch(c, c).start(priority=row_queue)

    @pl.when(c1 > c0)
    def _():
        wgb[...] = wg_ref[...].astype(BF16)
        wub[...] = wu_ref[...].astype(BF16)
        wdb[...] = wd_ref[...].astype(BF16)

        def chunk(c, carry):
            slot = lax.rem(c, n_slots)
            fetch(c, slot).wait()
            ahead = c + n_slots - 1

            @pl.when(ahead < c_end)
            def _():
                fetch(ahead, lax.rem(ahead, n_slots)).start(priority=row_queue)

            @pl.when(c >= n_slots)
            def _():
                drain(c - n_slots, slot).wait()

            n_valid = cnt_ref[e] - (c - c0) * ch
            valid = lax.broadcasted_iota(jnp.int32, (ch, q), 0) < n_valid
            gate = up = None
            for h in range(2):
                lo, hi = _unpack_plane(xbuf[slot, h])
                lo = jnp.where(valid, lo, 0.0).astype(BF16)
                hi = jnp.where(valid, hi, 0.0).astype(BF16)
                for xpart, k0 in ((lo, h * q), (hi, (2 + h) * q)):
                    g = _dot(xpart, wgb[k0:k0 + q, :])
                    u = _dot(xpart, wub[k0:k0 + q, :])
                    gate = g if gate is None else gate + g
                    up = u if up is None else up + u
            hb = _silu(gate) * up
            obuf[slot, 0], obuf[slot, 1] = _pack_rows(_dot(hb.astype(BF16), wdb[...]))
            drain(c, slot).start(priority=row_queue)
            return carry

        lax.fori_loop(c0, c1, chunk, 0)

    @pl.when(e == n_exp - 1)
    def _():
        for back in range(n_slots, 0, -1):
            @pl.when(c_end >= back)
            def _():
                drain(c_end - back, lax.rem(c_end - back, n_slots)).wait()


def _moe_call(chunk_start, counts, xs, w_gate, w_up, w_down):
    n_rows = xs.shape[1]
    n_exp, dm, de = w_gate.shape
    grid_spec = pltpu.PrefetchScalarGridSpec(
        num_scalar_prefetch=2,
        grid=(n_exp,),
        in_specs=[
            pl.BlockSpec(memory_space=pl.ANY),
            pl.BlockSpec((None, dm, de), lambda e, cs, cnt: (e, 0, 0)),
            pl.BlockSpec((None, dm, de), lambda e, cs, cnt: (e, 0, 0)),
            pl.BlockSpec((None, de, dm), lambda e, cs, cnt: (e, 0, 0)),
        ],
        out_specs=pl.BlockSpec(memory_space=pl.ANY),
        scratch_shapes=[
            pltpu.VMEM((MOE_SLOTS, 2, MOE_BLOCK, dm // 4), jnp.uint32),
            pltpu.VMEM((MOE_SLOTS, 2, MOE_BLOCK, dm // 4), jnp.uint32),
            pltpu.VMEM((dm, de), BF16), pltpu.VMEM((dm, de), BF16), pltpu.VMEM((de, dm), BF16),
            pltpu.SemaphoreType.DMA((MOE_SLOTS,)), pltpu.SemaphoreType.DMA((MOE_SLOTS,)),
        ],
    )
    return pl.pallas_call(
        _moe_kernel,
        grid_spec=grid_spec,
        out_shape=jax.ShapeDtypeStruct(xs.shape, jnp.uint32),
        compiler_params=pltpu.CompilerParams(
            dimension_semantics=("arbitrary",), vmem_limit_bytes=VMEM_LIMIT),
        name="moe",
    )(chunk_start, counts, xs, w_gate, w_up, w_down)


def _final_kernel(xp_ref, r_ref, w_ref, mod_ref, g_ref, o_ref):
    w = w_ref[...]
    groups = [None] * 4
    for h in range(2):
        for k in range(r_ref.shape[1]):
            lo, hi = _unpack_plane(r_ref[h, k])
            wk = w[:, k:k + 1]
            groups[h] = wk * lo if groups[h] is None else groups[h] + wk * lo
            groups[2 + h] = wk * hi if groups[2 + h] is None else groups[2 + h] + wk * hi
    routed = jnp.concatenate(groups, axis=1)
    x = xp_ref[...] + mod_ref[5:6, :] * routed
    ms = jnp.mean(x * x, axis=-1, keepdims=True)
    o_ref[...] = x * lax.rsqrt(ms + NORM_EPS) * g_ref[...]


def _final_call(xp, rows, w, mod3, final_g, tm=256):
    bsz, seq, dm = xp.shape
    kk, q = rows.shape[1], rows.shape[3]
    nt = seq // tm
    row = lambda b, i: (b, i, 0)
    return pl.pallas_call(
        _final_kernel,
        grid=(bsz, nt),
        in_specs=[
            pl.BlockSpec((None, tm, dm), row),
            pl.BlockSpec((2, kk, tm, q), lambda b, i: (0, 0, b * nt + i, 0)),
            pl.BlockSpec((None, tm, kk), row),
            pl.BlockSpec((None, N_ADA, dm), lambda b, i: (b, 0, 0)),
            pl.BlockSpec((1, dm), lambda b, i: (0, 0)),
        ],
        out_specs=pl.BlockSpec((None, tm, dm), row),
        out_shape=jax.ShapeDtypeStruct((bsz, seq, dm), F32),
        compiler_params=pltpu.CompilerParams(
            dimension_semantics=("parallel", "parallel"), vmem_limit_bytes=VMEM_LIMIT),
        name="final",
    )(xp, rows, w, mod3, final_g.reshape(1, dm))


def _route_kernel(sc_ref, bias_ref, idx_ref, w_ref, rank_ref, cnt_ref, tri_sc, carry_sc):
    ne, tn = sc_ref.shape
    gsz = ne // N_EXPERT_GROUPS
    neg = -jnp.inf
    first = jnp.logical_and(pl.program_id(0) == 0, pl.program_id(1) == 0)

    @pl.when(first)
    def _():
        r = lax.broadcasted_iota(jnp.int32, (tn, tn), 0)
        c = lax.broadcasted_iota(jnp.int32, (tn, tn), 1)
        tri_sc[...] = jnp.where(r < c, 1.0, 0.0).astype(BF16)
        carry_sc[...] = jnp.zeros(carry_sc.shape, F32)

    s = sc_ref[...]
    sel = s + bias_ref[...]
    gs = []
    for g in range(N_EXPERT_GROUPS):
        blk = sel[g * gsz:(g + 1) * gsz]
        m1 = jnp.max(blk, axis=0, keepdims=True)
        eq = blk == m1
        n_eq = jnp.sum(jnp.where(eq, 1.0, 0.0), axis=0, keepdims=True)
        m2 = jnp.max(jnp.where(eq, neg, blk), axis=0, keepdims=True)
        gs.append(m1 + jnp.where(n_eq >= 2.0, m1, m2))
    gs = jnp.concatenate(gs, axis=0)
    gi = lax.broadcasted_iota(jnp.int32, gs.shape, 0)
    beaten = jnp.zeros(gs.shape, F32)
    for gp in range(N_EXPERT_GROUPS):
        other = gs[gp:gp + 1]
        wins = jnp.where(other > gs, 1.0, jnp.where(other == gs, jnp.where(gi > gp, 1.0, 0.0), 0.0))
        beaten = beaten + wins
    gadd = jnp.where(beaten < float(TOPK_GROUPS), 0.0, neg)
    cur = jnp.concatenate(
        [sel[g * gsz:(g + 1) * gsz] + gadd[g:g + 1] for g in range(N_EXPERT_GROUPS)], axis=0)

    eidx = lax.broadcasted_iota(jnp.int32, (ne, tn), 0).astype(F32)
    chosen = jnp.zeros((ne, tn), F32)
    idx_rows, s_rows = [], []
    for _ in range(TOP_K):
        m = jnp.max(cur, axis=0, keepdims=True)
        ik = jnp.min(jnp.where(cur == m, eidx, float(ne)), axis=0, keepdims=True)
        oh = eidx == ik
        s_rows.append(jnp.sum(jnp.where(oh, s, 0.0), axis=0, keepdims=True))
        cur = jnp.where(oh, neg, cur)
        chosen = chosen + jnp.where(oh, 1.0, 0.0)
        idx_rows.append(ik)
    sk = jnp.concatenate(s_rows, axis=0)
    w_ref[...] = sk / jnp.sum(sk, axis=0, keepdims=True) * ROUTED_SCALE
    idx_ref[...] = jnp.concatenate(idx_rows, axis=0).astype(jnp.int32)

    before = _dot(chosen.astype(BF16), tri_sc[...]) + carry_sc[...]
    rank_rows = [jnp.sum(jnp.where(eidx == ik, before, 0.0), axis=0, keepdims=True) for ik in idx_rows]
    rank_ref[...] = jnp.concatenate(rank_rows, axis=0).astype(jnp.int32)
    carry_sc[...] = carry_sc[...] + jnp.sum(chosen, axis=1, keepdims=True)
    cnt_ref[...] = carry_sc[...]


def _route_call(scores_t, router_bias, tn=512):
    bsz, ne, seq = scores_t.shape
    tok = lambda b, i: (b, 0, i)
    return pl.pallas_call(
        _route_kernel,
        grid=(bsz, seq // tn),
        in_specs=[
            pl.BlockSpec((None, ne, tn), tok),
            pl.BlockSpec((ne, 1), lambda b, i: (0, 0)),
        ],
        out_specs=[
            pl.BlockSpec((None, TOP_K, tn), tok),
            pl.BlockSpec((None, TOP_K, tn), tok),
            pl.BlockSpec((None, TOP_K, tn), tok),
            pl.BlockSpec((ne, 1), lambda b, i: (0, 0)),
        ],
        out_shape=[
            jax.ShapeDtypeStruct((bsz, TOP_K, seq), jnp.int32),
            jax.ShapeDtypeStruct((bsz, TOP_K, seq), F32),
            jax.ShapeDtypeStruct((bsz, TOP_K, seq), jnp.int32),
            jax.ShapeDtypeStruct((ne, 1), F32),
        ],
        scratch_shapes=[pltpu.VMEM((tn, tn), BF16), pltpu.VMEM((ne, 1), F32)],
        compiler_params=pltpu.CompilerParams(
            dimension_semantics=("arbitrary", "arbitrary"), vmem_limit_bytes=VMEM_LIMIT),
        name="route",
    )(scores_t, router_bias.reshape(ne, 1))


def _dispatch_tables(idx, rank, counts):
    ne = counts.shape[0]
    n_tok = idx.shape[0] * idx.shape[2]
    counts = counts.astype(jnp.int32)
    pcounts = (counts + MOE_BLOCK - 1) // MOE_BLOCK * MOE_BLOCK
    pstarts = jnp.concatenate([jnp.zeros((1,), jnp.int32), jnp.cumsum(pcounts).astype(jnp.int32)])
    n_blocks = -(-(n_tok * TOP_K + ne * (MOE_BLOCK - 1)) // MOE_BLOCK)
    dest = _dest_call(pstarts[:ne], idx, rank)
    return dest, pstarts // MOE_BLOCK, counts, n_blocks


def _dest_kernel(ps_ref, idx_ref, rank_ref, o_ref):
    idx = idx_ref[...]

    def body(e, acc):
        return jnp.where(idx == e, ps_ref[e], acc)

    o_ref[...] = rank_ref[...] + lax.fori_loop(0, ps_ref.shape[0], body, jnp.zeros(idx.shape, jnp.int32))


def _dest_call(pstarts, idx, rank, tn=2048):
    bsz, kk, seq = idx.shape
    tok = lambda b, i, ps: (b, 0, i)
    grid_spec = pltpu.PrefetchScalarGridSpec(
        num_scalar_prefetch=1,
        grid=(bsz, seq // tn),
        in_specs=[pl.BlockSpec((None, kk, tn), tok), pl.BlockSpec((None, kk, tn), tok)],
        out_specs=pl.BlockSpec((None, kk, tn), tok),
    )
    return pl.pallas_call(
        _dest_kernel,
        grid_spec=grid_spec,
        out_shape=jax.ShapeDtypeStruct(idx.shape, jnp.int32),
        compiler_params=pltpu.CompilerParams(dimension_semantics=("parallel", "parallel")),
        name="dest",
    )(pstarts, idx, rank)


def _scatter_rows(rows, dest, n_out):
    n_tok, width = rows.shape
    k_slots = dest.shape[0]
    win = LANES
    mesh = plsc.VectorSubcoreMesh(core_axis_name="c", subcore_axis_name="s")

    @pl.kernel(out_type=jax.ShapeDtypeStruct((n_out, width), rows.dtype), mesh=mesh, scratch_types=[])
    def scatter(rows_hbm, dest_hbm, out_hbm):
        def body(rows_vmem, idx_vmem):
            pltpu.sync_copy(rows_vmem, out_hbm.at[idx_vmem.at[0]])

        pltpu.emit_pipeline(
            body,
            grid=(n_tok // win, k_slots),
            in_specs=[
                pl.BlockSpec((win, width), lambda i, k: (i, 0)),
                pl.BlockSpec((1, win), lambda i, k: (k, i)),
            ],
            out_specs=[],
            core_axis_name=("c", "s"),
            dimension_semantics=(pltpu.PARALLEL, pltpu.ARBITRARY),
        )(rows_hbm, dest_hbm)

    return scatter(rows, dest)


def _gather_rows(rows, idx):
    n = idx.shape[0]
    width = rows.shape[1]
    win = LANES
    mesh = plsc.VectorSubcoreMesh(core_axis_name="c", subcore_axis_name="s")

    @pl.kernel(out_type=jax.ShapeDtypeStruct((n, width), rows.dtype), mesh=mesh, scratch_types=[])
    def gather(rows_hbm, idx_hbm, out_hbm):
        def body(idx_vmem, out_vmem):
            pltpu.sync_copy(rows_hbm.at[idx_vmem.at[0]], out_vmem)

        pltpu.emit_pipeline(
            body,
            grid=(n // win,),
            in_specs=[pl.BlockSpec((1, win), lambda i: (0, i))],
            out_specs=[pl.BlockSpec((win, width), lambda i: (i, 0))],
            core_axis_name=("c", "s"),
            dimension_semantics=(pltpu.PARALLEL,),
        )(idx_hbm, out_hbm)

    return gather(rows, idx.reshape(1, n))


def kernel(x, c, w_ada, b_ada, norm1_g, w_in, w_out, lambda_q1, lambda_k1, lambda_q2, lambda_k2, subln_g,
           ssm_a_re, ssm_a_im, ssm_log_step, ssm_b_re, ssm_b_im, ssm_c_re, ssm_c_im, ssm_d, w_glu, b_glu,
           norm2_g, w_router, router_bias, w_gate, w_up, w_down, ws_gate, ws_up, ws_down, final_g):
    bsz, seq, dm = x.shape
    n_tok = bsz * seq
    aw = N_ATTN_HEADS * V_HEAD_DIM

    mod3 = _mod_call(c, w_ada[0], b_ada[0]).reshape(bsz, N_ADA, dm)

    wqkv = w_in[0][:, :3 * aw].astype(BF16)
    wut = w_in[0][:, 3 * aw:].T.astype(BF16)
    q, k, v, ut = _inproj_call(x, mod3, norm1_g[0], wqkv, wut)

    lam = (jnp.exp(jnp.sum(lambda_q1[0] * lambda_k1[0])) - jnp.exp(jnp.sum(lambda_q2[0] * lambda_k2[0]))
           + LAM_INIT).reshape(1)
    attn = _attn_call(lam, q, k, v, subln_g[0])

    yt = _ssm_call(ut, *_ssm_operators(ssm_a_re[0], ssm_a_im[0], ssm_log_step[0], ssm_b_re[0], ssm_b_im[0],
                                       ssm_c_re[0], ssm_c_im[0]))

    xp, h2, scores_t = _mid_call(
        x, attn, yt, ut, mod3, ssm_d[0], w_glu[0].T.astype(BF16), b_glu[0],
        w_out[0][:aw].astype(BF16), w_out[0][aw:].astype(BF16), norm2_g[0], w_router[0].T,
        ws_gate[0].astype(BF16), ws_up[0].astype(BF16), ws_down[0].astype(BF16))

    idx, w, rank, counts = _route_call(scores_t, router_bias[0])
    dest, chunk_start, counts, n_blocks = _dispatch_tables(idx, rank, counts.reshape(-1))
    n_rows = n_blocks * MOE_BLOCK

    dest_k = jnp.swapaxes(dest, 0, 1).reshape(TOP_K, n_tok)
    dest_half = jnp.concatenate([dest_k, dest_k + n_rows], axis=1)
    xs = _scatter_rows(h2.reshape(2 * n_tok, dm // 4), dest_half, 2 * n_rows).reshape(2, n_rows, dm // 4)

    out = _moe_call(chunk_start, counts, xs, w_gate[0], w_up[0], w_down[0])
    src = jnp.concatenate([dest_k.reshape(-1), (dest_k + n_rows).reshape(-1)])
    rows = _gather_rows(out.reshape(2 * n_rows, dm // 4), src)
    return _final_call(xp, rows.reshape(2, TOP_K, n_tok, dm // 4), jnp.swapaxes(w, 1, 2), mod3, final_g)
```

```python
import functools
import math

import jax
import jax.numpy as jnp
from jax import lax
from jax.experimental import pallas as pl
from jax.experimental.pallas import tpu as pltpu
from jax.experimental.pallas import tpu_sc as plsc

F32 = jnp.float32
BF16 = jnp.bfloat16

N_ATTN_HEADS = 4
ATTN_HEAD_DIM = 64
V_HEAD_DIM = 128
SSM_GROUP = 16
N_SSM_GROUPS = 32
SSM_STATE = 64
N_EXPERTS = 256
TOP_K = 8
N_EXPERT_GROUPS = 8
TOPK_GROUPS = 4
ROUTED_SCALE = 2.5
NORM_EPS = 1e-6
SUBLN_EPS = 1e-5
N_ADA = 6
LAM_INIT = 0.8 - 0.6 * math.exp(-0.3 * 0)

LANES = 128
MOE_BLOCK = 256
MOE_SLOTS = 6
NEG_BIG = -1e30
LOG2_E = math.log2(math.e)
VMEM_LIMIT = 48 * 1024 * 1024


def _split_bf16(a):
    hi = a.astype(BF16)
    lo = (a - hi.astype(F32)).astype(BF16)
    return hi, lo


def _dot(a, b):
    return jnp.dot(a, b, preferred_element_type=F32)


def _dot_nt(a, b):
    return lax.dot_general(a, b, (((1,), (1,)), ((), ())), preferred_element_type=F32)


def _dot3(a, b):
    ah, al = _split_bf16(a)
    bh, bl = _split_bf16(b)
    return _dot(ah, bh) + _dot(ah, bl) + _dot(al, bh)


def _dot3_nt(a, b):
    ah, al = _split_bf16(a)
    bh, bl = _split_bf16(b)
    return _dot_nt(ah, bh) + _dot_nt(ah, bl) + _dot_nt(al, bh)


def _silu(x):
    return x * jax.nn.sigmoid(x)


def _gelu_tanh(x):
    c = math.sqrt(2.0 / math.pi)
    return 0.5 * x * (1.0 + jnp.tanh(c * (x + 0.044715 * (x * x * x))))


def _pack_rows(x):
    bits = lax.bitcast_convert_type(x.astype(BF16).astype(F32), jnp.uint32)
    half = bits.shape[1] // 2
    packed = (bits[:, :half] >> 16) | (bits[:, half:] & jnp.uint32(0xFFFF0000))
    return packed[:, :half // 2], packed[:, half // 2:]


def _unpack_plane(xu):
    return (lax.bitcast_convert_type(xu << 16, F32),
            lax.bitcast_convert_type(xu & jnp.uint32(0xFFFF0000), F32))


def _mod_kernel(c_ref, w_ref, b_ref, o_ref):
    cond = _silu(c_ref[...])
    o_ref[...] = _dot3(cond, w_ref[...]) + b_ref[...]


def _mod_call(c, w_ada, b_ada):
    bsz, dm = c.shape
    n_out = w_ada.shape[1]
    tn = 1024
    return pl.pallas_call(
        _mod_kernel,
        grid=(n_out // tn,),
        in_specs=[
            pl.BlockSpec((bsz, dm), lambda j: (0, 0)),
            pl.BlockSpec((dm, tn), lambda j: (0, j)),
            pl.BlockSpec((1, tn), lambda j: (0, j)),
        ],
        out_specs=pl.BlockSpec((bsz, tn), lambda j: (0, j)),
        out_shape=jax.ShapeDtypeStruct((bsz, n_out), F32),
        compiler_params=pltpu.CompilerParams(vmem_limit_bytes=VMEM_LIMIT),
        name="mod",
    )(c, w_ada, b_ada.reshape(1, n_out))


def _inproj_kernel(x_ref, mod_ref, g_ref, wqkv_ref, wut_ref, q_ref, k_ref, v_ref, ut_ref, *, aw):
    x = x_ref[...]
    ms = jnp.mean(x * x, axis=-1, keepdims=True)
    h = x * lax.rsqrt(ms + NORM_EPS) * g_ref[...]
    h = h * (1.0 + mod_ref[1:2, :]) + mod_ref[0:1, :]
    hb = h.astype(BF16)
    qkv = _dot(hb, wqkv_ref[...])
    q_ref[...] = (qkv[:, :aw] * (LOG2_E * ATTN_HEAD_DIM ** -0.5)).astype(BF16)
    k_ref[...] = qkv[:, aw:2 * aw].astype(BF16)
    v_ref[...] = qkv[:, 2 * aw:].astype(BF16)
    ut_ref[...] = _dot_nt(wut_ref[...], hb)


def _inproj_call(x, mod3, norm_g, wqkv, wut, tm=512):
    bsz, seq, dm = x.shape
    aw = wqkv.shape[1] // 3
    sw = wut.shape[0]
    row = lambda b, i: (b, i, 0)
    return pl.pallas_call(
        functools.partial(_inproj_kernel, aw=aw),
        grid=(bsz, seq // tm),
        in_specs=[
            pl.BlockSpec((None, tm, dm), row),
            pl.BlockSpec((None, N_ADA, dm), lambda b, i: (b, 0, 0)),
            pl.BlockSpec((1, dm), lambda b, i: (0, 0)),
            pl.BlockSpec(wqkv.shape, lambda b, i: (0, 0)),
            pl.BlockSpec(wut.shape, lambda b, i: (0, 0)),
        ],
        out_specs=[
            pl.BlockSpec((None, tm, aw), row),
            pl.BlockSpec((None, tm, aw), row),
            pl.BlockSpec((None, tm, aw), row),
            pl.BlockSpec((None, sw, tm), lambda b, i: (b, 0, i)),
        ],
        out_shape=[
            jax.ShapeDtypeStruct((bsz, seq, aw), BF16),
            jax.ShapeDtypeStruct((bsz, seq, aw), BF16),
            jax.ShapeDtypeStruct((bsz, seq, aw), BF16),
            jax.ShapeDtypeStruct((bsz, sw, seq), F32),
        ],
        compiler_params=pltpu.CompilerParams(
            dimension_semantics=("parallel", "parallel"), vmem_limit_bytes=VMEM_LIMIT),
        name="inproj",
    )(x, mod3, norm_g.reshape(1, dm), wqkv, wut)


def _attn_kernel(lam_ref, q_ref, k_ref, v_ref, g_ref, o_ref, m_sc, l_sc, acc_sc, *, tq):
    i = pl.program_id(2)
    q = q_ref[...]
    lane = lax.broadcasted_iota(jnp.int32, q.shape, 1)
    zero = jnp.zeros_like(q)
    q2 = jnp.concatenate([jnp.where(lane < ATTN_HEAD_DIM, q, zero),
                          jnp.where(lane >= ATTN_HEAD_DIM, q, zero)], axis=0)
    m_sc[...] = jnp.full(m_sc.shape, NEG_BIG, F32)
    l_sc[...] = jnp.zeros(l_sc.shape, F32)
    acc_sc[...] = jnp.zeros(acc_sc.shape, F32)

    def step(j, masked):
        start = pl.multiple_of(j * tq, tq)
        kt = k_ref[pl.ds(start, tq), :]
        vt = v_ref[pl.ds(start, tq), :]
        s = _dot_nt(q2, kt)
        if masked:
            r = lax.broadcasted_iota(jnp.int32, s.shape, 0)
            c = lax.broadcasted_iota(jnp.int32, s.shape, 1)
            s = jnp.where(c <= (r & (tq - 1)), s, NEG_BIG)
        m_prev = m_sc[...]
        m_new = jnp.maximum(m_prev, jnp.max(s, axis=-1, keepdims=True))
        alpha = jnp.exp2(m_prev - m_new)
        p = jnp.exp2(s - jnp.concatenate([m_new] * (tq // LANES), axis=1))
        psum = p[:, :LANES]
        for c0 in range(LANES, tq, LANES):
            psum = psum + p[:, c0:c0 + LANES]
        l_sc[...] = alpha * l_sc[...] + psum
        acc_sc[...] = alpha * acc_sc[...] + _dot(p.astype(BF16), vt)
        m_sc[...] = m_new

    def body(j, carry):
        step(j, False)
        return carry

    lax.fori_loop(0, i, body, 0)
    step(i, True)

    o_all = acc_sc[...] / jnp.sum(l_sc[...], axis=-1, keepdims=True)
    o = o_all[:tq] - lam_ref[0] * o_all[tq:]
    ms = jnp.mean(o * o, axis=-1, keepdims=True)
    o = o * lax.rsqrt(ms + SUBLN_EPS) * g_ref[...] * (1.0 - LAM_INIT)
    o_ref[...] = o.astype(o_ref.dtype)


def _attn_call(lam, q, k, v, subln_g, tq=1024):
    bsz, seq, aw = q.shape
    nh = aw // V_HEAD_DIM
    qmap = lambda b, h, i: (b, i, h)
    kvmap = lambda b, h, i: (b, 0, h)
    return pl.pallas_call(
        functools.partial(_attn_kernel, tq=tq),
        grid=(bsz, nh, seq // tq),
        in_specs=[
            pl.BlockSpec(memory_space=pltpu.SMEM),
            pl.BlockSpec((None, tq, V_HEAD_DIM), qmap),
            pl.BlockSpec((None, seq, V_HEAD_DIM), kvmap),
            pl.BlockSpec((None, seq, V_HEAD_DIM), kvmap),
            pl.BlockSpec((1, V_HEAD_DIM), lambda b, h, i: (0, 0)),
        ],
        out_specs=pl.BlockSpec((None, tq, V_HEAD_DIM), qmap),
        out_shape=jax.ShapeDtypeStruct((bsz, seq, aw), BF16),
        scratch_shapes=[pltpu.VMEM((2 * tq, V_HEAD_DIM), F32)] * 3,
        compiler_params=pltpu.CompilerParams(
            dimension_semantics=("parallel", "parallel", "parallel"), vmem_limit_bytes=VMEM_LIMIT),
        name="attn",
    )(lam, q, k, v, subln_g.reshape(1, V_HEAD_DIM))


def _ssm_operators(a_re, a_im, log_step, b_re, b_im, c_re, c_im):
    t = LANES
    hi = lax.Precision.HIGHEST
    lam = lax.complex(jnp.minimum(a_re, -1e-4), a_im)
    delta = jnp.exp(log_step)[:, None]
    lam_bar = jnp.exp(lam * delta)
    bbar = ((lam_bar - 1.0) / lam)[:, :, None] * lax.complex(b_re, b_im)
    cmat = lax.complex(c_re, c_im)
    ld = lam * delta
    tau = jnp.arange(t + 1, dtype=F32)
    pw = jnp.exp(ld[:, :, None] * tau)
    cb = cmat[:, None, :, :] * jnp.swapaxes(bbar, 1, 2)[:, :, None, :]
    g = a_re.shape[0]
    cb = cb.reshape(g, SSM_GROUP * SSM_GROUP, SSM_STATE)
    cbcat = jnp.concatenate([cb.real, -cb.imag], axis=-1)
    pcat = jnp.concatenate([pw.real[:, :, :t], pw.imag[:, :, :t]], axis=1)
    kmat = jnp.einsum('gxp,gpt->gxt', cbcat, pcat, precision=hi)
    prev = jnp.swapaxes(pw[:, :, t - 1::-1], 1, 2)
    arev = jnp.concatenate([prev.real, prev.imag], axis=-1)
    bt = jnp.swapaxes(bbar, 1, 2)
    brow = jnp.stack([jnp.concatenate([bt.real, bt.real], axis=-1),
                      jnp.concatenate([-bt.imag, bt.imag], axis=-1)], axis=2).reshape(g, 2 * SSM_GROUP, 2 * SSM_STATE)
    a1 = jnp.concatenate([pw.real[:, :, 1:], pw.imag[:, :, 1:]], axis=1)
    ct = jnp.swapaxes(cmat, 1, 2)
    ccol = jnp.stack([jnp.concatenate([ct.real, -ct.real], axis=1),
                      jnp.concatenate([-ct.imag, -ct.imag], axis=1)], axis=-1).reshape(g, 2 * SSM_STATE, 2 * SSM_GROUP)
    rows = []
    for i in range(6):
        d = jnp.exp(ld * float(t * (1 << i)))
        rows.append(jnp.concatenate([d.real, d.real], axis=-1))
        rows.append(jnp.concatenate([-d.imag, d.imag], axis=-1))
    rows += [jnp.zeros_like(rows[0])] * 4
    dpow = jnp.stack(rows, axis=1)
    return kmat, arev, brow, a1, ccol, dpow


def _ssm_kernel(u_ref, k_ref, arev_ref, brow_ref, a1_ref, ccol_ref, dp_ref, y_ref, m_sc, ws_sc, wc_sc):
    bsz, nch, n_chunk, t = u_ref.shape
    row = lax.broadcasted_iota(jnp.int32, (t, t), 0)
    col = lax.broadcasted_iota(jnp.int32, (t, t), 1)
    causal = col >= row
    half = arev_ref.shape[1] // 2

    arev = arev_ref[...]
    arev_sw = pltpu.roll(arev, half, 1)
    a1 = a1_ref[...]
    a1_sw = pltpu.roll(a1, half, 0)
    for c in range(nch):
        ws_sc[c * t:(c + 1) * t, :] = (arev * brow_ref[2 * c:2 * c + 1, :]
                                       + arev_sw * brow_ref[2 * c + 1:2 * c + 2, :]).astype(BF16)
        wc_sc[:, c * t:(c + 1) * t] = (a1 * ccol_ref[:, 2 * c:2 * c + 1]
                                       + a1_sw * ccol_ref[:, 2 * c + 1:2 * c + 2]).astype(BF16)

    def build(ci, carry):
        r0 = pl.multiple_of(ci * t, t)
        for co in range(nch):
            kr = k_ref[pl.ds(ci * nch + co, 1), :]
            kb = jnp.broadcast_to(kr, (t, t))
            kb = pltpu.roll(kb, 0, 1, stride=1, stride_axis=0)
            m_sc[pl.ds(r0, t), co * t:(co + 1) * t] = jnp.where(causal, kb, 0.0).astype(BF16)
        return carry

    lax.fori_loop(0, nch, build, 0)

    uflat = jnp.concatenate(
        [jnp.concatenate([u_ref[b, ci] for b in range(bsz)], axis=0) for ci in range(nch)],
        axis=1).astype(BF16)
    y = _dot(uflat, m_sc[...])
    z = _dot(uflat, ws_sc[...])
    kidx = lax.broadcasted_iota(jnp.int32, z.shape, 0) & (n_chunk - 1)
    shift = 1
    i = 0
    while shift < n_chunk:
        zs = jnp.where(kidx >= shift, pltpu.roll(z, shift, 0), 0.0)
        z = z + zs * dp_ref[2 * i:2 * i + 1, :] + pltpu.roll(zs, half, 1) * dp_ref[2 * i + 1:2 * i + 2, :]
        shift *= 2
        i += 1
    xin = jnp.where(kidx >= 1, pltpu.roll(z, 1, 0), 0.0)
    xh, xl = _split_bf16(xin)
    wc = wc_sc[...]
    y = y + _dot(xh, wc) + _dot(xl, wc)
    for b in range(bsz):
        for co in range(nch):
            y_ref[b, co] = y[b * n_chunk:(b + 1) * n_chunk, co * t:(co + 1) * t]


def _ssm_call(ut, kmat, arev, brow, a1, ccol, dpow):
    bsz, sw, seq = ut.shape
    n_groups = sw // SSM_GROUP
    n_chunk = seq // LANES
    assert n_chunk & (n_chunk - 1) == 0 and n_chunk <= 64
    u4 = ut.reshape(bsz, sw, n_chunk, LANES)
    blk = (bsz, SSM_GROUP, n_chunk, LANES)
    gmap = lambda g: (0, g, 0, 0)
    pmap = lambda g: (g, 0, 0)
    y4 = pl.pallas_call(
        _ssm_kernel,
        grid=(n_groups,),
        in_specs=[
            pl.BlockSpec(blk, gmap),
            pl.BlockSpec((None,) + kmat.shape[1:], pmap),
            pl.BlockSpec((None,) + arev.shape[1:], pmap),
            pl.BlockSpec((None,) + brow.shape[1:], pmap),
            pl.BlockSpec((None,) + a1.shape[1:], pmap),
            pl.BlockSpec((None,) + ccol.shape[1:], pmap),
            pl.BlockSpec((None,) + dpow.shape[1:], pmap),
        ],
        out_specs=pl.BlockSpec(blk, gmap),
        out_shape=jax.ShapeDtypeStruct(u4.shape, F32),
        scratch_shapes=[pltpu.VMEM((SSM_GROUP * LANES, SSM_GROUP * LANES), BF16),
                        pltpu.VMEM((SSM_GROUP * LANES, 2 * SSM_STATE), BF16),
                        pltpu.VMEM((2 * SSM_STATE, SSM_GROUP * LANES), BF16)],
        compiler_params=pltpu.CompilerParams(
            dimension_semantics=("parallel",), vmem_limit_bytes=VMEM_LIMIT),
        name="ssm",
    )(u4, kmat, arev, brow, a1, ccol, dpow)
    return y4.reshape(bsz, sw, seq)


def _mid_kernel(x_ref, attn_ref, yt_ref, ut_ref, mod_ref, dsk_ref, wglut_ref, bglu_ref, wo1_ref, wo2_ref,
                g2_ref, wrt_ref, wsg_ref, wsu_ref, wsd_ref, xp_ref, h2_ref, lg_ref):
    gt = _gelu_tanh(yt_ref[...] + dsk_ref[...] * ut_ref[...])
    zt = _dot(wglut_ref[...], gt.astype(BF16)) + bglu_ref[...]
    st = gt * jax.nn.sigmoid(zt)
    s = st.T.astype(BF16)
    mix = _dot(attn_ref[...], wo1_ref[...]) + _dot(s, wo2_ref[...])
    x1 = x_ref[...] + mod_ref[2:3, :] * mix
    ms = jnp.mean(x1 * x1, axis=-1, keepdims=True)
    h2 = x1 * lax.rsqrt(ms + NORM_EPS) * g2_ref[...]
    h2 = h2 * (1.0 + mod_ref[4:5, :]) + mod_ref[3:4, :]
    hb = h2.astype(BF16)
    h2_ref[0], h2_ref[1] = _pack_rows(h2)
    lg_ref[...] = jax.nn.sigmoid(_dot3_nt(wrt_ref[...], h2))
    sh = _silu(_dot(hb, wsg_ref[...])) * _dot(hb, wsu_ref[...])
    shared = _dot(sh.astype(BF16), wsd_ref[...])
    xp_ref[...] = x1 + mod_ref[5:6, :] * shared


def _mid_call(x, attn, yt, ut, mod3, dsk, wglut, bglu, wo1, wo2, g2, wrt, wsg, wsu, wsd, tm=512):
    bsz, seq, dm = x.shape
    aw = attn.shape[2]
    sw = yt.shape[1]
    ne = wrt.shape[0]
    row = lambda b, i: (b, i, 0)
    colm = lambda b, i: (b, 0, i)
    full = lambda a: pl.BlockSpec(a.shape, lambda b, i: (0,) * a.ndim)
    dsk = dsk.reshape(sw, 1)
    bglu = bglu.reshape(sw, 1)
    g2 = g2.reshape(1, dm)
    return pl.pallas_call(
        _mid_kernel,
        grid=(bsz, seq // tm),
        in_specs=[
            pl.BlockSpec((None, tm, dm), row),
            pl.BlockSpec((None, tm, aw), row),
            pl.BlockSpec((None, sw, tm), colm),
            pl.BlockSpec((None, sw, tm), colm),
            pl.BlockSpec((None, N_ADA, dm), lambda b, i: (b, 0, 0)),
            full(dsk), full(wglut), full(bglu), full(wo1), full(wo2), full(g2), full(wrt),
            full(wsg), full(wsu), full(wsd),
        ],
        out_specs=[
            pl.BlockSpec((None, tm, dm), row),
            pl.BlockSpec((2, None, tm, dm // 4), lambda b, i: (0, b, i, 0)),
            pl.BlockSpec((None, ne, tm), colm),
        ],
        out_shape=[
            jax.ShapeDtypeStruct((bsz, seq, dm), F32),
            jax.ShapeDtypeStruct((2, bsz, seq, dm // 4), jnp.uint32),
            jax.ShapeDtypeStruct((bsz, ne, seq), F32),
        ],
        compiler_params=pltpu.CompilerParams(
            dimension_semantics=("parallel", "parallel"), vmem_limit_bytes=VMEM_LIMIT),
        name="mid",
    )(x, attn, yt, ut, mod3, dsk, wglut, bglu, wo1, wo2, g2, wrt, wsg, wsu, wsd)


def _moe_kernel(cs_ref, cnt_ref, xs_hbm, wg_ref, wu_ref, wd_ref, out_hbm, xbuf, obuf, wgb, wub, wdb, isem, osem):
    e = pl.program_id(0)
    n_exp = pl.num_programs(0)
    c0, c1, c_end = cs_ref[e], cs_ref[e + 1], cs_ref[n_exp]
    n_slots, _, ch, q = xbuf.shape

    def fetch(c, slot):
        return pltpu.make_async_copy(xs_hbm.at[:, pl.ds(c * ch, ch), :], xbuf.at[slot], isem.at[slot])

    def drain(c, slot):
        return pltpu.make_async_copy(obuf.at[slot], out_hbm.at[:, pl.ds(c * ch, ch), :], osem.at[slot])

    row_queue = 1

    look = n_slots - 2

    @pl.when(e == 0)
    def _():
        for c in range(look):
            @pl.when(c < c_end)
            def _():
                fetch(c, c).start(priority=row_queue)

    def arrive(c, n):
        for i in range(n):
            fetch(c + i, lax.rem(c + i, n_slots)).wait()
        for i in range(n):
            ahead = c + look + i

            @pl.when(ahead < c_end)
            def _():
                fetch(ahead, lax.rem(ahead, n_slots)).start(priority=row_queue)
        for i in range(n):
            @pl.when(c + i >= n_slots)
            def _():
                drain(c + i - n_slots, lax.rem(c + i, n_slots)).wait()

    def compute(c, n):
        slots = [lax.rem(c + i, n_slots) for i in range(n)]
        n_valid = cnt_ref[e] - (c - c0) * ch
        valid = lax.broadcasted_iota(jnp.int32, (n * ch, q), 0) < n_valid
        gate = up = None
        for h in range(2):
            lo, hi = _unpack_plane(jnp.concatenate([xbuf[s, h] for s in slots], axis=0))
            lo = jnp.where(valid, lo, 0.0).astype(BF16)
            hi = jnp.where(valid, hi, 0.0).astype(BF16)
            for xpart, k0 in ((lo, h * q), (hi, (2 + h) * q)):
                g = _dot(xpart, wgb[k0:k0 + q, :])
                u = _dot(xpart, wub[k0:k0 + q, :])
                gate = g if gate is None else gate + g
                up = u if up is None else up + u
        hb = _silu(gate) * up
        p0, p1 = _pack_rows(_dot(hb.astype(BF16), wdb[...]))
        for i, s in enumerate(slots):
            obuf[s, 0] = p0[i * ch:(i + 1) * ch]
            obuf[s, 1] = p1[i * ch:(i + 1) * ch]

    def depart(c, n):
        for i in range(n):
            drain(c + i, lax.rem(c + i, n_slots)).start(priority=row_queue)

    @pl.when(c1 > c0)
    def _():
        wgb[...] = wg_ref[...].astype(BF16)
        wub[...] = wu_ref[...].astype(BF16)
        wdb[...] = wd_ref[...].astype(BF16)
        n_pairs = (c1 - c0) // 2

        def pair(i, carry):
            c = c0 + 2 * i
            arrive(c, 2)
            compute(c, 2)
            depart(c, 2)
            return carry

        lax.fori_loop(0, n_pairs, pair, 0)

        @pl.when(c0 + 2 * n_pairs < c1)
        def _():
            arrive(c1 - 1, 1)
            compute(c1 - 1, 1)
            depart(c1 - 1, 1)

    @pl.when(e == n_exp - 1)
    def _():
        for back in range(n_slots, 0, -1):
            @pl.when(c_end >= back)
            def _():
                drain(c_end - back, lax.rem(c_end - back, n_slots)).wait()


def _moe_call(chunk_start, counts, xs, w_gate, w_up, w_down):
    n_rows = xs.shape[1]
    n_exp, dm, de = w_gate.shape
    grid_spec = pltpu.PrefetchScalarGridSpec(
        num_scalar_prefetch=2,
        grid=(n_exp,),
        in_specs=[
            pl.BlockSpec(memory_space=pl.ANY),
            pl.BlockSpec((None, dm, de), lambda e, cs, cnt: (e, 0, 0)),
            pl.BlockSpec((None, dm, de), lambda e, cs, cnt: (e, 0, 0)),
            pl.BlockSpec((None, de, dm), lambda e, cs, cnt: (e, 0, 0)),
        ],
        out_specs=pl.BlockSpec(memory_space=pl.ANY),
        scratch_shapes=[
            pltpu.VMEM((MOE_SLOTS, 2, MOE_BLOCK, dm // 4), jnp.uint32),
            pltpu.VMEM((MOE_SLOTS, 2, MOE_BLOCK, dm // 4), jnp.uint32),
            pltpu.VMEM((dm, de), BF16), pltpu.VMEM((dm, de), BF16), pltpu.VMEM((de, dm), BF16),
            pltpu.SemaphoreType.DMA((MOE_SLOTS,)), pltpu.SemaphoreType.DMA((MOE_SLOTS,)),
        ],
    )
    return pl.pallas_call(
        _moe_kernel,
        grid_spec=grid_spec,
        out_shape=jax.ShapeDtypeStruct(xs.shape, jnp.uint32),
        compiler_params=pltpu.CompilerParams(
            dimension_semantics=("arbitrary",), vmem_limit_bytes=VMEM_LIMIT),
        name="moe",
    )(chunk_start, counts, xs, w_gate, w_up, w_down)


def _final_kernel(xp_ref, r_ref, w_ref, mod_ref, g_ref, o_ref):
    w = w_ref[...]
    groups = [None] * 4
    for h in range(2):
        for k in range(r_ref.shape[1]):
            lo, hi = _unpack_plane(r_ref[h, k])
            wk = w[:, k:k + 1]
            groups[h] = wk * lo if groups[h] is None else groups[h] + wk * lo
            groups[2 + h] = wk * hi if groups[2 + h] is None else groups[2 + h] + wk * hi
    routed = jnp.concatenate(groups, axis=1)
    x = xp_ref[...] + mod_ref[5:6, :] * routed
    ms = jnp.mean(x * x, axis=-1, keepdims=True)
    o_ref[...] = x * lax.rsqrt(ms + NORM_EPS) * g_ref[...]


def _final_call(xp, rows, w, mod3, final_g, tm=256):
    bsz, seq, dm = xp.shape
    kk, q = rows.shape[1], rows.shape[3]
    nt = seq // tm
    row = lambda b, i: (b, i, 0)
    return pl.pallas_call(
        _final_kernel,
        grid=(bsz, nt),
        in_specs=[
            pl.BlockSpec((None, tm, dm), row),
            pl.BlockSpec((2, kk, tm, q), lambda b, i: (0, 0, b * nt + i, 0)),
            pl.BlockSpec((None, tm, kk), row),
            pl.BlockSpec((None, N_ADA, dm), lambda b, i: (b, 0, 0)),
            pl.BlockSpec((1, dm), lambda b, i: (0, 0)),
        ],
        out_specs=pl.BlockSpec((None, tm, dm), row),
        out_shape=jax.ShapeDtypeStruct((bsz, seq, dm), F32),
        compiler_params=pltpu.CompilerParams(
            dimension_semantics=("parallel", "parallel"), vmem_limit_bytes=VMEM_LIMIT),
        name="final",
    )(xp, rows, w, mod3, final_g.reshape(1, dm))


def _route_kernel(sc_ref, bias_ref, idx_ref, w_ref, rank_ref, cnt_ref, tri_sc, carry_sc):
    ne, tn = sc_ref.shape
    gsz = ne // N_EXPERT_GROUPS
    neg = -jnp.inf
    first = jnp.logical_and(pl.program_id(0) == 0, pl.program_id(1) == 0)

    @pl.when(first)
    def _():
        r = lax.broadcasted_iota(jnp.int32, (tn, tn), 0)
        c = lax.broadcasted_iota(jnp.int32, (tn, tn), 1)
        tri_sc[...] = jnp.where(r < c, 1.0, 0.0).astype(BF16)
        carry_sc[...] = jnp.zeros(carry_sc.shape, F32)

    s = sc_ref[...]
    sel = s + bias_ref[...]
    gs = []
    for g in range(N_EXPERT_GROUPS):
        blk = sel[g * gsz:(g + 1) * gsz]
        m1 = jnp.max(blk, axis=0, keepdims=True)
        eq = blk == m1
        n_eq = jnp.sum(jnp.where(eq, 1.0, 0.0), axis=0, keepdims=True)
        m2 = jnp.max(jnp.where(eq, neg, blk), axis=0, keepdims=True)
        gs.append(m1 + jnp.where(n_eq >= 2.0, m1, m2))
    gs = jnp.concatenate(gs, axis=0)
    gi = lax.broadcasted_iota(jnp.int32, gs.shape, 0)
    beaten = jnp.zeros(gs.shape, F32)
    for gp in range(N_EXPERT_GROUPS):
        other = gs[gp:gp + 1]
        wins = jnp.where(other > gs, 1.0, jnp.where(other == gs, jnp.where(gi > gp, 1.0, 0.0), 0.0))
        beaten = beaten + wins
    gadd = jnp.where(beaten < float(TOPK_GROUPS), 0.0, neg)
    cur = jnp.concatenate(
        [sel[g * gsz:(g + 1) * gsz] + gadd[g:g + 1] for g in range(N_EXPERT_GROUPS)], axis=0)

    eidx = lax.broadcasted_iota(jnp.int32, (ne, tn), 0).astype(F32)
    chosen = jnp.zeros((ne, tn), F32)
    idx_rows, s_rows = [], []
    for _ in range(TOP_K):
        m = jnp.max(cur, axis=0, keepdims=True)
        ik = jnp.min(jnp.where(cur == m, eidx, float(ne)), axis=0, keepdims=True)
        oh = eidx == ik
        s_rows.append(jnp.sum(jnp.where(oh, s, 0.0), axis=0, keepdims=True))
        cur = jnp.where(oh, neg, cur)
        chosen = chosen + jnp.where(oh, 1.0, 0.0)
        idx_rows.append(ik)
    sk = jnp.concatenate(s_rows, axis=0)
    w_ref[...] = sk / jnp.sum(sk, axis=0, keepdims=True) * ROUTED_SCALE
    idx_ref[...] = jnp.concatenate(idx_rows, axis=0).astype(jnp.int32)

    before = _dot(chosen.astype(BF16), tri_sc[...]) + carry_sc[...]
    rank_rows = [jnp.sum(jnp.where(eidx == ik, before, 0.0), axis=0, keepdims=True) for ik in idx_rows]
    rank_ref[...] = jnp.concatenate(rank_rows, axis=0).astype(jnp.int32)
    carry_sc[...] = carry_sc[...] + jnp.sum(chosen, axis=1, keepdims=True)
    cnt_ref[...] = carry_sc[...]


def _route_call(scores_t, router_bias, tn=512):
    bsz, ne, seq = scores_t.shape
    tok = lambda b, i: (b, 0, i)
    return pl.pallas_call(
        _route_kernel,
        grid=(bsz, seq // tn),
        in_specs=[
            pl.BlockSpec((None, ne, tn), tok),
            pl.BlockSpec((ne, 1), lambda b, i: (0, 0)),
        ],
        out_specs=[
            pl.BlockSpec((None, TOP_K, tn), tok),
            pl.BlockSpec((None, TOP_K, tn), tok),
            pl.BlockSpec((None, TOP_K, tn), tok),
            pl.BlockSpec((ne, 1), lambda b, i: (0, 0)),
        ],
        out_shape=[
            jax.ShapeDtypeStruct((bsz, TOP_K, seq), jnp.int32),
            jax.ShapeDtypeStruct((bsz, TOP_K, seq), F32),
            jax.ShapeDtypeStruct((bsz, TOP_K, seq), jnp.int32),
            jax.ShapeDtypeStruct((ne, 1), F32),
        ],
        scratch_shapes=[pltpu.VMEM((tn, tn), BF16), pltpu.VMEM((ne, 1), F32)],
        compiler_params=pltpu.CompilerParams(
            dimension_semantics=("arbitrary", "arbitrary"), vmem_limit_bytes=VMEM_LIMIT),
        name="route",
    )(scores_t, router_bias.reshape(ne, 1))


def _dispatch_tables(idx, rank, counts):
    ne = counts.shape[0]
    n_tok = idx.shape[0] * idx.shape[2]
    counts = counts.astype(jnp.int32)
    pcounts = (counts + MOE_BLOCK - 1) // MOE_BLOCK * MOE_BLOCK
    pstarts = jnp.concatenate([jnp.zeros((1,), jnp.int32), jnp.cumsum(pcounts).astype(jnp.int32)])
    n_blocks = -(-(n_tok * TOP_K + ne * (MOE_BLOCK - 1)) // MOE_BLOCK)
    dest = _dest_call(pstarts[:ne], idx, rank)
    return dest, pstarts // MOE_BLOCK, counts, n_blocks


def _dest_kernel(ps_ref, idx_ref, rank_ref, o_ref):
    idx = idx_ref[...]

    def body(e, acc):
        return jnp.where(idx == e, ps_ref[e], acc)

    o_ref[...] = rank_ref[...] + lax.fori_loop(0, ps_ref.shape[0], body, jnp.zeros(idx.shape, jnp.int32))


def _dest_call(pstarts, idx, rank, tn=2048):
    bsz, kk, seq = idx.shape
    tok = lambda b, i, ps: (b, 0, i)
    grid_spec = pltpu.PrefetchScalarGridSpec(
        num_scalar_prefetch=1,
        grid=(bsz, seq // tn),
        in_specs=[pl.BlockSpec((None, kk, tn), tok), pl.BlockSpec((None, kk, tn), tok)],
        out_specs=pl.BlockSpec((None, kk, tn), tok),
    )
    return pl.pallas_call(
        _dest_kernel,
        grid_spec=grid_spec,
        out_shape=jax.ShapeDtypeStruct(idx.shape, jnp.int32),
        compiler_params=pltpu.CompilerParams(dimension_semantics=("parallel", "parallel")),
        name="dest",
    )(pstarts, idx, rank)


def _scatter_rows(rows, dest, n_out):
    n_tok, width = rows.shape
    k_slots = dest.shape[0]
    win = LANES
    mesh = plsc.VectorSubcoreMesh(core_axis_name="c", subcore_axis_name="s")

    @pl.kernel(out_type=jax.ShapeDtypeStruct((n_out, width), rows.dtype), mesh=mesh, scratch_types=[])
    def scatter(rows_hbm, dest_hbm, out_hbm):
        def body(rows_vmem, idx_vmem):
            pltpu.sync_copy(rows_vmem, out_hbm.at[idx_vmem.at[0]])

        pltpu.emit_pipeline(
            body,
            grid=(n_tok // win, k_slots),
            in_specs=[
                pl.BlockSpec((win, width), lambda i, k: (i, 0)),
                pl.BlockSpec((1, win), lambda i, k: (k, i)),
            ],
            out_specs=[],
            core_axis_name=("c", "s"),
            dimension_semantics=(pltpu.PARALLEL, pltpu.ARBITRARY),
        )(rows_hbm, dest_hbm)

    return scatter(rows, dest)


def _gather_rows(rows, idx):
    n = idx.shape[0]
    width = rows.shape[1]
    win = LANES
    mesh = plsc.VectorSubcoreMesh(core_axis_name="c", subcore_axis_name="s")

    @pl.kernel(out_type=jax.ShapeDtypeStruct((n, width), rows.dtype), mesh=mesh, scratch_types=[])
    def gather(rows_hbm, idx_hbm, out_hbm):
        def body(idx_vmem, out_vmem):
            pltpu.sync_copy(rows_hbm.at[idx_vmem.at[0]], out_vmem)

        pltpu.emit_pipeline(
            body,
            grid=(n // win,),
            in_specs=[pl.BlockSpec((1, win), lambda i: (0, i))],
            out_specs=[pl.BlockSpec((win, width), lambda i: (i, 0))],
            core_axis_name=("c", "s"),
            dimension_semantics=(pltpu.PARALLEL,),
        )(idx_hbm, out_hbm)

    return gather(rows, idx.reshape(1, n))


def kernel(x, c, w_ada, b_ada, norm1_g, w_in, w_out, lambda_q1, lambda_k1, lambda_q2, lambda_k2, subln_g,
           ssm_a_re, ssm_a_im, ssm_log_step, ssm_b_re, ssm_b_im, ssm_c_re, ssm_c_im, ssm_d, w_glu, b_glu,
           norm2_g, w_router, router_bias, w_gate, w_up, w_down, ws_gate, ws_up, ws_down, final_g):
    bsz, seq, dm = x.shape
    n_tok = bsz * seq
    aw = N_ATTN_HEADS * V_HEAD_DIM

    mod3 = _mod_call(c, w_ada[0], b_ada[0]).reshape(bsz, N_ADA, dm)

    wqkv = w_in[0][:, :3 * aw].astype(BF16)
    wut = w_in[0][:, 3 * aw:].T.astype(BF16)
    q, k, v, ut = _inproj_call(x, mod3, norm1_g[0], wqkv, wut)

    lam = (jnp.exp(jnp.sum(lambda_q1[0] * lambda_k1[0])) - jnp.exp(jnp.sum(lambda_q2[0] * lambda_k2[0]))
           + LAM_INIT).reshape(1)
    attn = _attn_call(lam, q, k, v, subln_g[0])

    yt = _ssm_call(ut, *_ssm_operators(ssm_a_re[0], ssm_a_im[0], ssm_log_step[0], ssm_b_re[0], ssm_b_im[0],
                                       ssm_c_re[0], ssm_c_im[0]))

    xp, h2, scores_t = _mid_call(
        x, attn, yt, ut, mod3, ssm_d[0], w_glu[0].T.astype(BF16), b_glu[0],
        w_out[0][:aw].astype(BF16), w_out[0][aw:].astype(BF16), norm2_g[0], w_router[0].T,
        ws_gate[0].astype(BF16), ws_up[0].astype(BF16), ws_down[0].astype(BF16))

    idx, w, rank, counts = _route_call(scores_t, router_bias[0])
    dest, chunk_start, counts, n_blocks = _dispatch_tables(idx, rank, counts.reshape(-1))
    n_rows = n_blocks * MOE_BLOCK

    dest_k = jnp.swapaxes(dest, 0, 1).reshape(TOP_K, n_tok)
    dest_half = jnp.concatenate([dest_k, dest_k + n_rows], axis=1)
    xs = _scatter_rows(h2.reshape(2 * n_tok, dm // 4), dest_half, 2 * n_rows).reshape(2, n_rows, dm // 4)

    out = _moe_call(chunk_start, counts, xs, w_gate[0], w_up[0], w_down[0])
    src = jnp.concatenate([dest_k.reshape(-1), (dest_k + n_rows).reshape(-1)])
    rows = _gather_rows(out.reshape(2 * n_rows, dm // 4), src)
    return _final_call(xp, rows.reshape(2, TOP_K, n_tok, dm // 4), jnp.swapaxes(w, 1, 2), mod3, final_g)
```

```python
import functools
import math

import jax
import jax.numpy as jnp
from jax import lax
from jax.experimental import pallas as pl
from jax.experimental.pallas import tpu as pltpu
from jax.experimental.pallas import tpu_sc as plsc

F32 = jnp.float32
BF16 = jnp.bfloat16

N_ATTN_HEADS = 4
ATTN_HEAD_DIM = 64
V_HEAD_DIM = 128
SSM_GROUP = 16
N_SSM_GROUPS = 32
SSM_STATE = 64
N_EXPERTS = 256
TOP_K = 8
N_EXPERT_GROUPS = 8
TOPK_GROUPS = 4
ROUTED_SCALE = 2.5
NORM_EPS = 1e-6
SUBLN_EPS = 1e-5
N_ADA = 6
LAM_INIT = 0.8 - 0.6 * math.exp(-0.3 * 0)

LANES = 128
MOE_BLOCK = 256
MOE_SLOTS = 6
MOE_EXPERTS_PER_STEP = 2
NEG_BIG = -1e30
LOG2_E = math.log2(math.e)
VMEM_LIMIT = 48 * 1024 * 1024


def _split_bf16(a):
    hi = a.astype(BF16)
    lo = (a - hi.astype(F32)).astype(BF16)
    return hi, lo


def _dot(a, b):
    return jnp.dot(a, b, preferred_element_type=F32)


def _dot_nt(a, b):
    return lax.dot_general(a, b, (((1,), (1,)), ((), ())), preferred_element_type=F32)


def _dot3(a, b):
    ah, al = _split_bf16(a)
    bh, bl = _split_bf16(b)
    return _dot(ah, bh) + _dot(ah, bl) + _dot(al, bh)


def _dot3_nt(a, b):
    ah, al = _split_bf16(a)
    bh, bl = _split_bf16(b)
    return _dot_nt(ah, bh) + _dot_nt(ah, bl) + _dot_nt(al, bh)


def _silu(x):
    return x * jax.nn.sigmoid(x)


def _gelu_tanh(x):
    c = math.sqrt(2.0 / math.pi)
    return 0.5 * x * (1.0 + jnp.tanh(c * (x + 0.044715 * (x * x * x))))


def _pack_rows(x):
    bits = lax.bitcast_convert_type(x.astype(BF16).astype(F32), jnp.uint32)
    half = bits.shape[1] // 2
    packed = (bits[:, :half] >> 16) | (bits[:, half:] & jnp.uint32(0xFFFF0000))
    return packed[:, :half // 2], packed[:, half // 2:]


def _unpack_plane(xu):
    return (lax.bitcast_convert_type(xu << 16, F32),
            lax.bitcast_convert_type(xu & jnp.uint32(0xFFFF0000), F32))


def _mod_kernel(c_ref, w_ref, b_ref, o_ref):
    cond = _silu(c_ref[...])
    o_ref[...] = _dot3(cond, w_ref[...]) + b_ref[...]


def _mod_call(c, w_ada, b_ada):
    bsz, dm = c.shape
    n_out = w_ada.shape[1]
    tn = 1024
    return pl.pallas_call(
        _mod_kernel,
        grid=(n_out // tn,),
        in_specs=[
            pl.BlockSpec((bsz, dm), lambda j: (0, 0)),
            pl.BlockSpec((dm, tn), lambda j: (0, j)),
            pl.BlockSpec((1, tn), lambda j: (0, j)),
        ],
        out_specs=pl.BlockSpec((bsz, tn), lambda j: (0, j)),
        out_shape=jax.ShapeDtypeStruct((bsz, n_out), F32),
        compiler_params=pltpu.CompilerParams(vmem_limit_bytes=VMEM_LIMIT),
        name="mod",
    )(c, w_ada, b_ada.reshape(1, n_out))


def _inproj_kernel(x_ref, mod_ref, g_ref, wqkv_ref, wut_ref, q_ref, k_ref, v_ref, ut_ref, *, aw):
    x = x_ref[...]
    ms = jnp.mean(x * x, axis=-1, keepdims=True)
    h = x * lax.rsqrt(ms + NORM_EPS) * g_ref[...]
    h = h * (1.0 + mod_ref[1:2, :]) + mod_ref[0:1, :]
    hb = h.astype(BF16)
    qkv = _dot(hb, wqkv_ref[...])
    q_ref[...] = (qkv[:, :aw] * (LOG2_E * ATTN_HEAD_DIM ** -0.5)).astype(BF16)
    k_ref[...] = qkv[:, aw:2 * aw].astype(BF16)
    v_ref[...] = qkv[:, 2 * aw:].astype(BF16)
    ut_ref[...] = _dot_nt(wut_ref[...], hb)


def _inproj_call(x, mod3, norm_g, wqkv, wut, tm=512):
    bsz, seq, dm = x.shape
    aw = wqkv.shape[1] // 3
    sw = wut.shape[0]
    row = lambda b, i: (b, i, 0)
    return pl.pallas_call(
        functools.partial(_inproj_kernel, aw=aw),
        grid=(bsz, seq // tm),
        in_specs=[
            pl.BlockSpec((None, tm, dm), row),
            pl.BlockSpec((None, N_ADA, dm), lambda b, i: (b, 0, 0)),
            pl.BlockSpec((1, dm), lambda b, i: (0, 0)),
            pl.BlockSpec(wqkv.shape, lambda b, i: (0, 0)),
            pl.BlockSpec(wut.shape, lambda b, i: (0, 0)),
        ],
        out_specs=[
            pl.BlockSpec((None, tm, aw), row),
            pl.BlockSpec((None, tm, aw), row),
            pl.BlockSpec((None, tm, aw), row),
            pl.BlockSpec((None, sw, tm), lambda b, i: (b, 0, i)),
        ],
        out_shape=[
            jax.ShapeDtypeStruct((bsz, seq, aw), BF16),
            jax.ShapeDtypeStruct((bsz, seq, aw), BF16),
            jax.ShapeDtypeStruct((bsz, seq, aw), BF16),
            jax.ShapeDtypeStruct((bsz, sw, seq), F32),
        ],
        compiler_params=pltpu.CompilerParams(
            dimension_semantics=("parallel", "parallel"), vmem_limit_bytes=VMEM_LIMIT),
        name="inproj",
    )(x, mod3, norm_g.reshape(1, dm), wqkv, wut)


def _attn_kernel(lam_ref, q_ref, k_ref, v_ref, g_ref, o_ref, m_sc, l_sc, acc_sc, *, tq):
    i = pl.program_id(2)
    th = tq // 2
    q = q_ref[...]
    lane = lax.broadcasted_iota(jnp.int32, q.shape, 1)
    zero = jnp.zeros_like(q)
    qa = jnp.where(lane < ATTN_HEAD_DIM, q, zero)
    qb = jnp.where(lane >= ATTN_HEAD_DIM, q, zero)
    q2 = jnp.concatenate([qa[:th], qb[:th], qa[th:], qb[th:]], axis=0)
    m_sc[...] = jnp.full(m_sc.shape, NEG_BIG, F32)
    l_sc[...] = jnp.zeros(l_sc.shape, F32)
    acc_sc[...] = jnp.zeros(acc_sc.shape, F32)

    def step(j, r0, nr, nc, qpos0):
        start = pl.multiple_of(j * tq, tq)
        kt = k_ref[pl.ds(start, nc), :]
        vt = v_ref[pl.ds(start, nc), :]
        s = _dot_nt(q2[r0:r0 + nr], kt)
        if qpos0 is not None:
            r = lax.broadcasted_iota(jnp.int32, s.shape, 0)
            c = lax.broadcasted_iota(jnp.int32, s.shape, 1)
            s = jnp.where(c <= qpos0 + (r & (th - 1)), s, NEG_BIG)
        m_prev = m_sc[r0:r0 + nr, :]
        m_new = jnp.maximum(m_prev, jnp.max(s, axis=-1, keepdims=True))
        alpha = jnp.exp2(m_prev - m_new)
        p = jnp.exp2(s - jnp.concatenate([m_new] * (nc // LANES), axis=1))
        psum = p[:, :LANES]
        for c0 in range(LANES, nc, LANES):
            psum = psum + p[:, c0:c0 + LANES]
        l_sc[r0:r0 + nr, :] = alpha * l_sc[r0:r0 + nr, :] + psum
        acc_sc[r0:r0 + nr, :] = alpha * acc_sc[r0:r0 + nr, :] + _dot(p.astype(BF16), vt)
        m_sc[r0:r0 + nr, :] = m_new

    def body(jj, carry):
        step(2 * jj, 0, 2 * tq, tq, None)
        step(2 * jj + 1, 0, 2 * tq, tq, None)
        return carry

    lax.fori_loop(0, i // 2, body, 0)

    @pl.when(i % 2 == 1)
    def _():
        step(i - 1, 0, 2 * tq, tq, None)

    step(i, 0, tq, th, 0)
    step(i, tq, tq, tq, th)

    o_all = acc_sc[...] / jnp.sum(l_sc[...], axis=-1, keepdims=True)
    o0 = jnp.concatenate([o_all[:th], o_all[tq:tq + th]], axis=0)
    o1 = jnp.concatenate([o_all[th:tq], o_all[tq + th:]], axis=0)
    o = o0 - lam_ref[0] * o1
    ms = jnp.mean(o * o, axis=-1, keepdims=True)
    o = o * lax.rsqrt(ms + SUBLN_EPS) * g_ref[...] * (1.0 - LAM_INIT)
    o_ref[...] = o.astype(o_ref.dtype)


def _attn_call(lam, q, k, v, subln_g, tq=1024):
    bsz, seq, aw = q.shape
    nh = aw // V_HEAD_DIM
    qmap = lambda b, h, i: (b, i, h)
    kvmap = lambda b, h, i: (b, 0, h)
    return pl.pallas_call(
        functools.partial(_attn_kernel, tq=tq),
        grid=(bsz, nh, seq // tq),
        in_specs=[
            pl.BlockSpec(memory_space=pltpu.SMEM),
            pl.BlockSpec((None, tq, V_HEAD_DIM), qmap),
            pl.BlockSpec((None, seq, V_HEAD_DIM), kvmap),
            pl.BlockSpec((None, seq, V_HEAD_DIM), kvmap),
            pl.BlockSpec((1, V_HEAD_DIM), lambda b, h, i: (0, 0)),
        ],
        out_specs=pl.BlockSpec((None, tq, V_HEAD_DIM), qmap),
        out_shape=jax.ShapeDtypeStruct((bsz, seq, aw), BF16),
        scratch_shapes=[pltpu.VMEM((2 * tq, V_HEAD_DIM), F32)] * 3,
        compiler_params=pltpu.CompilerParams(
            dimension_semantics=("parallel", "parallel", "parallel"), vmem_limit_bytes=VMEM_LIMIT),
        name="attn",
    )(lam, q, k, v, subln_g.reshape(1, V_HEAD_DIM))


def _ssm_operators(a_re, a_im, log_step, b_re, b_im, c_re, c_im):
    t = LANES
    hi = lax.Precision.HIGHEST
    lam = lax.complex(jnp.minimum(a_re, -1e-4), a_im)
    delta = jnp.exp(log_step)[:, None]
    lam_bar = jnp.exp(lam * delta)
    bbar = ((lam_bar - 1.0) / lam)[:, :, None] * lax.complex(b_re, b_im)
    cmat = lax.complex(c_re, c_im)
    ld = lam * delta
    tau = jnp.arange(t + 1, dtype=F32)
    pw = jnp.exp(ld[:, :, None] * tau)
    cb = cmat[:, None, :, :] * jnp.swapaxes(bbar, 1, 2)[:, :, None, :]
    g = a_re.shape[0]
    cb = cb.reshape(g, SSM_GROUP * SSM_GROUP, SSM_STATE)
    cbcat = jnp.concatenate([cb.real, -cb.imag], axis=-1)
    pcat = jnp.concatenate([pw.real[:, :, :t], pw.imag[:, :, :t]], axis=1)
    kmat = jnp.einsum('gxp,gpt->gxt', cbcat, pcat, precision=hi)
    prev = jnp.swapaxes(pw[:, :, t - 1::-1], 1, 2)
    arev = jnp.concatenate([prev.real, prev.imag], axis=-1)
    bt = jnp.swapaxes(bbar, 1, 2)
    brow = jnp.stack([jnp.concatenate([bt.real, bt.real], axis=-1),
                      jnp.concatenate([-bt.imag, bt.imag], axis=-1)], axis=2).reshape(g, 2 * SSM_GROUP, 2 * SSM_STATE)
    a1 = jnp.concatenate([pw.real[:, :, 1:], pw.imag[:, :, 1:]], axis=1)
    ct = jnp.swapaxes(cmat, 1, 2)
    ccol = jnp.stack([jnp.concatenate([ct.real, -ct.real], axis=1),
                      jnp.concatenate([-ct.imag, -ct.imag], axis=1)], axis=-1).reshape(g, 2 * SSM_STATE, 2 * SSM_GROUP)
    rows = []
    for i in range(6):
        d = jnp.exp(ld * float(t * (1 << i)))
        rows.append(jnp.concatenate([d.real, d.real], axis=-1))
        rows.append(jnp.concatenate([-d.imag, d.imag], axis=-1))
    rows += [jnp.zeros_like(rows[0])] * 4
    dpow = jnp.stack(rows, axis=1)
    return kmat, arev, brow, a1, ccol, dpow


def _ssm_kernel(u_ref, k_ref, arev_ref, brow_ref, a1_ref, ccol_ref, dp_ref, y_ref, m_sc, ws_sc, wc_sc):
    bsz, nch, n_chunk, t = u_ref.shape
    row = lax.broadcasted_iota(jnp.int32, (t, t), 0)
    col = lax.broadcasted_iota(jnp.int32, (t, t), 1)
    causal = col >= row
    half = arev_ref.shape[1] // 2

    arev = arev_ref[...]
    arev_sw = pltpu.roll(arev, half, 1)
    a1 = a1_ref[...]
    a1_sw = pltpu.roll(a1, half, 0)
    for c in range(nch):
        ws_sc[c * t:(c + 1) * t, :] = (arev * brow_ref[2 * c:2 * c + 1, :]
                                       + arev_sw * brow_ref[2 * c + 1:2 * c + 2, :]).astype(BF16)
        wc_sc[:, c * t:(c + 1) * t] = (a1 * ccol_ref[:, 2 * c:2 * c + 1]
                                       + a1_sw * ccol_ref[:, 2 * c + 1:2 * c + 2]).astype(BF16)

    def build(ci, carry):
        r0 = pl.multiple_of(ci * t, t)
        for co in range(nch):
            kr = k_ref[pl.ds(ci * nch + co, 1), :]
            kb = jnp.broadcast_to(kr, (t, t))
            kb = pltpu.roll(kb, 0, 1, stride=1, stride_axis=0)
            m_sc[pl.ds(r0, t), co * t:(co + 1) * t] = jnp.where(causal, kb, 0.0).astype(BF16)
        return carry

    lax.fori_loop(0, nch, build, 0)

    uflat = jnp.concatenate(
        [jnp.concatenate([u_ref[b, ci] for b in range(bsz)], axis=0) for ci in range(nch)],
        axis=1).astype(BF16)
    y = _dot(uflat, m_sc[...])
    z = _dot(uflat, ws_sc[...])
    kidx = lax.broadcasted_iota(jnp.int32, z.shape, 0) & (n_chunk - 1)
    shift = 1
    i = 0
    while shift < n_chunk:
        zs = jnp.where(kidx >= shift, pltpu.roll(z, shift, 0), 0.0)
        z = z + zs * dp_ref[2 * i:2 * i + 1, :] + pltpu.roll(zs, half, 1) * dp_ref[2 * i + 1:2 * i + 2, :]
        shift *= 2
        i += 1
    xin = jnp.where(kidx >= 1, pltpu.roll(z, 1, 0), 0.0)
    xh, xl = _split_bf16(xin)
    wc = wc_sc[...]
    y = y + _dot(xh, wc) + _dot(xl, wc)
    for b in range(bsz):
        for co in range(nch):
            y_ref[b, co] = y[b * n_chunk:(b + 1) * n_chunk, co * t:(co + 1) * t]


def _ssm_call(ut, kmat, arev, brow, a1, ccol, dpow):
    bsz, sw, seq = ut.shape
    n_groups = sw // SSM_GROUP
    n_chunk = seq // LANES
    assert n_chunk & (n_chunk - 1) == 0 and n_chunk <= 64
    u4 = ut.reshape(bsz, sw, n_chunk, LANES)
    blk = (bsz, SSM_GROUP, n_chunk, LANES)
    gmap = lambda g: (0, g, 0, 0)
    pmap = lambda g: (g, 0, 0)
    y4 = pl.pallas_call(
        _ssm_kernel,
        grid=(n_groups,),
        in_specs=[
            pl.BlockSpec(blk, gmap),
            pl.BlockSpec((None,) + kmat.shape[1:], pmap),
            pl.BlockSpec((None,) + arev.shape[1:], pmap),
            pl.BlockSpec((None,) + brow.shape[1:], pmap),
            pl.BlockSpec((None,) + a1.shape[1:], pmap),
            pl.BlockSpec((None,) + ccol.shape[1:], pmap),
            pl.BlockSpec((None,) + dpow.shape[1:], pmap),
        ],
        out_specs=pl.BlockSpec(blk, gmap),
        out_shape=jax.ShapeDtypeStruct(u4.shape, F32),
        scratch_shapes=[pltpu.VMEM((SSM_GROUP * LANES, SSM_GROUP * LANES), BF16),
                        pltpu.VMEM((SSM_GROUP * LANES, 2 * SSM_STATE), BF16),
                        pltpu.VMEM((2 * SSM_STATE, SSM_GROUP * LANES), BF16)],
        compiler_params=pltpu.CompilerParams(
            dimension_semantics=("parallel",), vmem_limit_bytes=VMEM_LIMIT),
        name="ssm",
    )(u4, kmat, arev, brow, a1, ccol, dpow)
    return y4.reshape(bsz, sw, seq)


def _mid_kernel(x_ref, attn_ref, yt_ref, ut_ref, mod_ref, dsk_ref, wglut_ref, bglu_ref, wo1_ref, wo2_ref,
                g2_ref, wrt_ref, wsg_ref, wsu_ref, wsd_ref, xp_ref, h2_ref, lg_ref):
    gt = _gelu_tanh(yt_ref[...] + dsk_ref[...] * ut_ref[...])
    zt = _dot(wglut_ref[...], gt.astype(BF16)) + bglu_ref[...]
    st = gt * jax.nn.sigmoid(zt)
    s = st.T.astype(BF16)
    mix = _dot(attn_ref[...], wo1_ref[...]) + _dot(s, wo2_ref[...])
    x1 = x_ref[...] + mod_ref[2:3, :] * mix
    ms = jnp.mean(x1 * x1, axis=-1, keepdims=True)
    h2 = x1 * lax.rsqrt(ms + NORM_EPS) * g2_ref[...]
    h2 = h2 * (1.0 + mod_ref[4:5, :]) + mod_ref[3:4, :]
    hb = h2.astype(BF16)
    h2_ref[0], h2_ref[1] = _pack_rows(h2)
    lg_ref[...] = jax.nn.sigmoid(_dot3_nt(wrt_ref[...], h2))
    sh = _silu(_dot(hb, wsg_ref[...])) * _dot(hb, wsu_ref[...])
    shared = _dot(sh.astype(BF16), wsd_ref[...])
    xp_ref[...] = x1 + mod_ref[5:6, :] * shared


def _mid_call(x, attn, yt, ut, mod3, dsk, wglut, bglu, wo1, wo2, g2, wrt, wsg, wsu, wsd, tm=512):
    bsz, seq, dm = x.shape
    aw = attn.shape[2]
    sw = yt.shape[1]
    ne = wrt.shape[0]
    row = lambda b, i: (b, i, 0)
    colm = lambda b, i: (b, 0, i)
    full = lambda a: pl.BlockSpec(a.shape, lambda b, i: (0,) * a.ndim)
    dsk = dsk.reshape(sw, 1)
    bglu = bglu.reshape(sw, 1)
    g2 = g2.reshape(1, dm)
    return pl.pallas_call(
        _mid_kernel,
        grid=(bsz, seq // tm),
        in_specs=[
            pl.BlockSpec((None, tm, dm), row),
            pl.BlockSpec((None, tm, aw), row),
            pl.BlockSpec((None, sw, tm), colm),
            pl.BlockSpec((None, sw, tm), colm),
            pl.BlockSpec((None, N_ADA, dm), lambda b, i: (b, 0, 0)),
            full(dsk), full(wglut), full(bglu), full(wo1), full(wo2), full(g2), full(wrt),
            full(wsg), full(wsu), full(wsd),
        ],
        out_specs=[
            pl.BlockSpec((None, tm, dm), row),
            pl.BlockSpec((2, None, tm, dm // 4), lambda b, i: (0, b, i, 0)),
            pl.BlockSpec((None, ne, tm), colm),
        ],
        out_shape=[
            jax.ShapeDtypeStruct((bsz, seq, dm), F32),
            jax.ShapeDtypeStruct((2, bsz, seq, dm // 4), jnp.uint32),
            jax.ShapeDtypeStruct((bsz, ne, seq), F32),
        ],
        compiler_params=pltpu.CompilerParams(
            dimension_semantics=("parallel", "parallel"), vmem_limit_bytes=VMEM_LIMIT),
        name="mid",
    )(x, attn, yt, ut, mod3, dsk, wglut, bglu, wo1, wo2, g2, wrt, wsg, wsu, wsd)


def _moe_kernel(cs_ref, cnt_ref, xs_hbm, wg_ref, wu_ref, wd_ref, out_hbm, xbuf, obuf, wgb, wub, wdb, isem, osem):
    step = pl.program_id(0)
    n_steps = pl.num_programs(0)
    per_step = wg_ref.shape[0]
    c_end = cs_ref[n_steps * per_step]
    n_slots, _, ch, q = xbuf.shape

    def fetch(c, slot):
        return pltpu.make_async_copy(xs_hbm.at[:, pl.ds(c * ch, ch), :], xbuf.at[slot], isem.at[slot])

    def drain(c, slot):
        return pltpu.make_async_copy(obuf.at[slot], out_hbm.at[:, pl.ds(c * ch, ch), :], osem.at[slot])

    row_queue = 1

    look = n_slots - 2

    @pl.when(step == 0)
    def _():
        for c in range(look):
            @pl.when(c < c_end)
            def _():
                fetch(c, c).start(priority=row_queue)

    def arrive(c, n):
        for i in range(n):
            fetch(c + i, lax.rem(c + i, n_slots)).wait()
        for i in range(n):
            ahead = c + look + i

            @pl.when(ahead < c_end)
            def _():
                fetch(ahead, lax.rem(ahead, n_slots)).start(priority=row_queue)
        for i in range(n):
            @pl.when(c + i >= n_slots)
            def _():
                drain(c + i - n_slots, lax.rem(c + i, n_slots)).wait()

    def compute(c, n, n_valid):
        slots = [lax.rem(c + i, n_slots) for i in range(n)]
        valid = lax.broadcasted_iota(jnp.int32, (n * ch, q), 0) < n_valid
        gate = up = None
        for h in range(2):
            lo, hi = _unpack_plane(jnp.concatenate([xbuf[s, h] for s in slots], axis=0))
            lo = jnp.where(valid, lo, 0.0).astype(BF16)
            hi = jnp.where(valid, hi, 0.0).astype(BF16)
            for xpart, k0 in ((lo, h * q), (hi, (2 + h) * q)):
                g = _dot(xpart, wgb[k0:k0 + q, :])
                u = _dot(xpart, wub[k0:k0 + q, :])
                gate = g if gate is None else gate + g
                up = u if up is None else up + u
        hb = _silu(gate) * up
        p0, p1 = _pack_rows(_dot(hb.astype(BF16), wdb[...]))
        for i, s in enumerate(slots):
            obuf[s, 0] = p0[i * ch:(i + 1) * ch]
            obuf[s, 1] = p1[i * ch:(i + 1) * ch]

    def depart(c, n):
        for i in range(n):
            drain(c + i, lax.rem(c + i, n_slots)).start(priority=row_queue)

    def expert(sub):
        e = step * per_step + sub
        c0, c1 = cs_ref[e], cs_ref[e + 1]

        @pl.when(c1 > c0)
        def _():
            wgb[...] = wg_ref[sub].astype(BF16)
            wub[...] = wu_ref[sub].astype(BF16)
            wdb[...] = wd_ref[sub].astype(BF16)
            n_pairs = (c1 - c0) // 2

            def pair(i, carry):
                c = c0 + 2 * i
                arrive(c, 2)
                compute(c, 2, cnt_ref[e] - (c - c0) * ch)
                depart(c, 2)
                return carry

            lax.fori_loop(0, n_pairs, pair, 0)

            @pl.when(c0 + 2 * n_pairs < c1)
            def _():
                arrive(c1 - 1, 1)
                compute(c1 - 1, 1, cnt_ref[e] - (c1 - 1 - c0) * ch)
                depart(c1 - 1, 1)

    for sub in range(per_step):
        expert(sub)

    @pl.when(step == n_steps - 1)
    def _():
        for back in range(n_slots, 0, -1):
            @pl.when(c_end >= back)
            def _():
                drain(c_end - back, lax.rem(c_end - back, n_slots)).wait()


def _moe_call(chunk_start, counts, xs, w_gate, w_up, w_down):
    n_rows = xs.shape[1]
    n_exp, dm, de = w_gate.shape
    grid_spec = pltpu.PrefetchScalarGridSpec(
        num_scalar_prefetch=2,
        grid=(n_exp // MOE_EXPERTS_PER_STEP,),
        in_specs=[
            pl.BlockSpec(memory_space=pl.ANY),
            pl.BlockSpec((MOE_EXPERTS_PER_STEP, dm, de), lambda s, cs, cnt: (s, 0, 0)),
            pl.BlockSpec((MOE_EXPERTS_PER_STEP, dm, de), lambda s, cs, cnt: (s, 0, 0)),
            pl.BlockSpec((MOE_EXPERTS_PER_STEP, de, dm), lambda s, cs, cnt: (s, 0, 0)),
        ],
        out_specs=pl.BlockSpec(memory_space=pl.ANY),
        scratch_shapes=[
            pltpu.VMEM((MOE_SLOTS, 2, MOE_BLOCK, dm // 4), jnp.uint32),
            pltpu.VMEM((MOE_SLOTS, 2, MOE_BLOCK, dm // 4), jnp.uint32),
            pltpu.VMEM((dm, de), BF16), pltpu.VMEM((dm, de), BF16), pltpu.VMEM((de, dm), BF16),
            pltpu.SemaphoreType.DMA((MOE_SLOTS,)), pltpu.SemaphoreType.DMA((MOE_SLOTS,)),
        ],
    )
    return pl.pallas_call(
        _moe_kernel,
        grid_spec=grid_spec,
        out_shape=jax.ShapeDtypeStruct(xs.shape, jnp.uint32),
        compiler_params=pltpu.CompilerParams(
            dimension_semantics=("arbitrary",), vmem_limit_bytes=VMEM_LIMIT),
        name="moe",
    )(chunk_start, counts, xs, w_gate, w_up, w_down)


def _final_kernel(xp_ref, r_ref, w_ref, mod_ref, g_ref, o_ref):
    w = w_ref[...]
    groups = [None] * 4
    for h in range(2):
        for k in range(r_ref.shape[1]):
            lo, hi = _unpack_plane(r_ref[h, k])
            wk = w[:, k:k + 1]
            groups[h] = wk * lo if groups[h] is None else groups[h] + wk * lo
            groups[2 + h] = wk * hi if groups[2 + h] is None else groups[2 + h] + wk * hi
    routed = jnp.concatenate(groups, axis=1)
    x = xp_ref[...] + mod_ref[5:6, :] * routed
    ms = jnp.mean(x * x, axis=-1, keepdims=True)
    o_ref[...] = x * lax.rsqrt(ms + NORM_EPS) * g_ref[...]


def _final_call(xp, rows, w, mod3, final_g, tm=256):
    bsz, seq, dm = xp.shape
    kk, q = rows.shape[1], rows.shape[3]
    nt = seq // tm
    row = lambda b, i: (b, i, 0)
    return pl.pallas_call(
        _final_kernel,
        grid=(bsz, nt),
        in_specs=[
            pl.BlockSpec((None, tm, dm), row),
            pl.BlockSpec((2, kk, tm, q), lambda b, i: (0, 0, b * nt + i, 0)),
            pl.BlockSpec((None, tm, kk), row),
            pl.BlockSpec((None, N_ADA, dm), lambda b, i: (b, 0, 0)),
            pl.BlockSpec((1, dm), lambda b, i: (0, 0)),
        ],
        out_specs=pl.BlockSpec((None, tm, dm), row),
        out_shape=jax.ShapeDtypeStruct((bsz, seq, dm), F32),
        compiler_params=pltpu.CompilerParams(
            dimension_semantics=("parallel", "parallel"), vmem_limit_bytes=VMEM_LIMIT),
        name="final",
    )(xp, rows, w, mod3, final_g.reshape(1, dm))


def _route_kernel(sc_ref, bias_ref, idx_ref, w_ref, rank_ref, cnt_ref, tri_sc, carry_sc):
    ne, tn = sc_ref.shape
    gsz = ne // N_EXPERT_GROUPS
    neg = -jnp.inf
    first = jnp.logical_and(pl.program_id(0) == 0, pl.program_id(1) == 0)

    @pl.when(first)
    def _():
        r = lax.broadcasted_iota(jnp.int32, (tn, tn), 0)
        c = lax.broadcasted_iota(jnp.int32, (tn, tn), 1)
        tri_sc[...] = jnp.where(r < c, 1.0, 0.0).astype(BF16)
        carry_sc[...] = jnp.zeros(carry_sc.shape, F32)

    s = sc_ref[...]
    sel = s + bias_ref[...]
    gs = []
    for g in range(N_EXPERT_GROUPS):
        blk = sel[g * gsz:(g + 1) * gsz]
        m1 = jnp.max(blk, axis=0, keepdims=True)
        eq = blk == m1
        n_eq = jnp.sum(jnp.where(eq, 1.0, 0.0), axis=0, keepdims=True)
        m2 = jnp.max(jnp.where(eq, neg, blk), axis=0, keepdims=True)
        gs.append(m1 + jnp.where(n_eq >= 2.0, m1, m2))
    gs = jnp.concatenate(gs, axis=0)
    gi = lax.broadcasted_iota(jnp.int32, gs.shape, 0)
    beaten = jnp.zeros(gs.shape, F32)
    for gp in range(N_EXPERT_GROUPS):
        other = gs[gp:gp + 1]
        wins = jnp.where(other > gs, 1.0, jnp.where(other == gs, jnp.where(gi > gp, 1.0, 0.0), 0.0))
        beaten = beaten + wins
    gadd = jnp.where(beaten < float(TOPK_GROUPS), 0.0, neg)
    cur = jnp.concatenate(
        [sel[g * gsz:(g + 1) * gsz] + gadd[g:g + 1] for g in range(N_EXPERT_GROUPS)], axis=0)

    eidx = lax.broadcasted_iota(jnp.int32, (ne, tn), 0).astype(F32)
    chosen = jnp.zeros((ne, tn), F32)
    idx_rows, s_rows = [], []
    for _ in range(TOP_K):
        m = jnp.max(cur, axis=0, keepdims=True)
        ik = jnp.min(jnp.where(cur == m, eidx, float(ne)), axis=0, keepdims=True)
        oh = eidx == ik
        s_rows.append(jnp.sum(jnp.where(oh, s, 0.0), axis=0, keepdims=True))
        cur = jnp.where(oh, neg, cur)
        chosen = chosen + jnp.where(oh, 1.0, 0.0)
        idx_rows.append(ik)
    sk = jnp.concatenate(s_rows, axis=0)
    w_ref[...] = sk / jnp.sum(sk, axis=0, keepdims=True) * ROUTED_SCALE
    idx_ref[...] = jnp.concatenate(idx_rows, axis=0).astype(jnp.int32)

    before = _dot(chosen.astype(BF16), tri_sc[...]) + carry_sc[...]
    rank_rows = [jnp.sum(jnp.where(eidx == ik, before, 0.0), axis=0, keepdims=True) for ik in idx_rows]
    rank_ref[...] = jnp.concatenate(rank_rows, axis=0).astype(jnp.int32)
    carry_sc[...] = carry_sc[...] + jnp.sum(chosen, axis=1, keepdims=True)
    cnt_ref[...] = carry_sc[...]


def _route_call(scores_t, router_bias, tn=512):
    bsz, ne, seq = scores_t.shape
    tok = lambda b, i: (b, 0, i)
    return pl.pallas_call(
        _route_kernel,
        grid=(bsz, seq // tn),
        in_specs=[
            pl.BlockSpec((None, ne, tn), tok),
            pl.BlockSpec((ne, 1), lambda b, i: (0, 0)),
        ],
        out_specs=[
            pl.BlockSpec((None, TOP_K, tn), tok),
            pl.BlockSpec((None, TOP_K, tn), tok),
            pl.BlockSpec((None, TOP_K, tn), tok),
            pl.BlockSpec((ne, 1), lambda b, i: (0, 0)),
        ],
        out_shape=[
            jax.ShapeDtypeStruct((bsz, TOP_K, seq), jnp.int32),
            jax.ShapeDtypeStruct((bsz, TOP_K, seq), F32),
            jax.ShapeDtypeStruct((bsz, TOP_K, seq), jnp.int32),
            jax.ShapeDtypeStruct((ne, 1), F32),
        ],
        scratch_shapes=[pltpu.VMEM((tn, tn), BF16), pltpu.VMEM((ne, 1), F32)],
        compiler_params=pltpu.CompilerParams(
            dimension_semantics=("arbitrary", "arbitrary"), vmem_limit_bytes=VMEM_LIMIT),
        name="route",
    )(scores_t, router_bias.reshape(ne, 1))


def _dispatch_tables(idx, rank, counts):
    ne = counts.shape[0]
    n_tok = idx.shape[0] * idx.shape[2]
    counts = counts.astype(jnp.int32)
    pcounts = (counts + MOE_BLOCK - 1) // MOE_BLOCK * MOE_BLOCK
    pstarts = jnp.concatenate([jnp.zeros((1,), jnp.int32), jnp.cumsum(pcounts).astype(jnp.int32)])
    n_blocks = -(-(n_tok * TOP_K + ne * (MOE_BLOCK - 1)) // MOE_BLOCK)
    dest = _dest_call(pstarts[:ne], idx, rank)
    return dest, pstarts // MOE_BLOCK, counts, n_blocks


def _dest_kernel(ps_ref, idx_ref, rank_ref, o_ref):
    idx = idx_ref[...]

    def body(e, acc):
        return jnp.where(idx == e, ps_ref[e], acc)

    o_ref[...] = rank_ref[...] + lax.fori_loop(0, ps_ref.shape[0], body, jnp.zeros(idx.shape, jnp.int32))


def _dest_call(pstarts, idx, rank, tn=2048):
    bsz, kk, seq = idx.shape
    tok = lambda b, i, ps: (b, 0, i)
    grid_spec = pltpu.PrefetchScalarGridSpec(
        num_scalar_prefetch=1,
        grid=(bsz, seq // tn),
        in_specs=[pl.BlockSpec((None, kk, tn), tok), pl.BlockSpec((None, kk, tn), tok)],
        out_specs=pl.BlockSpec((None, kk, tn), tok),
    )
    return pl.pallas_call(
        _dest_kernel,
        grid_spec=grid_spec,
        out_shape=jax.ShapeDtypeStruct(idx.shape, jnp.int32),
        compiler_params=pltpu.CompilerParams(dimension_semantics=("parallel", "parallel")),
        name="dest",
    )(pstarts, idx, rank)


def _scatter_rows(rows, dest, n_out):
    n_tok, width = rows.shape
    k_slots = dest.shape[0]
    win = LANES
    mesh = plsc.VectorSubcoreMesh(core_axis_name="c", subcore_axis_name="s")

    @pl.kernel(out_type=jax.ShapeDtypeStruct((n_out, width), rows.dtype), mesh=mesh, scratch_types=[])
    def scatter(rows_hbm, dest_hbm, out_hbm):
        def body(rows_vmem, idx_vmem):
            pltpu.sync_copy(rows_vmem, out_hbm.at[idx_vmem.at[0]])

        pltpu.emit_pipeline(
            body,
            grid=(n_tok // win, k_slots),
            in_specs=[
                pl.BlockSpec((win, width), lambda i, k: (i, 0)),
                pl.BlockSpec((1, win), lambda i, k: (k, i)),
            ],
            out_specs=[],
            core_axis_name=("c", "s"),
            dimension_semantics=(pltpu.PARALLEL, pltpu.ARBITRARY),
        )(rows_hbm, dest_hbm)

    return scatter(rows, dest)


def _gather_rows(rows, idx):
    n = idx.shape[0]
    width = rows.shape[1]
    win = LANES
    mesh = plsc.VectorSubcoreMesh(core_axis_name="c", subcore_axis_name="s")

    @pl.kernel(out_type=jax.ShapeDtypeStruct((n, width), rows.dtype), mesh=mesh, scratch_types=[])
    def gather(rows_hbm, idx_hbm, out_hbm):
        def body(idx_vmem, out_vmem):
            pltpu.sync_copy(rows_hbm.at[idx_vmem.at[0]], out_vmem)

        pltpu.emit_pipeline(
            body,
            grid=(n // win,),
            in_specs=[pl.BlockSpec((1, win), lambda i: (0, i))],
            out_specs=[pl.BlockSpec((win, width), lambda i: (i, 0))],
            core_axis_name=("c", "s"),
            dimension_semantics=(pltpu.PARALLEL,),
        )(idx_hbm, out_hbm)

    return gather(rows, idx.reshape(1, n))


def kernel(x, c, w_ada, b_ada, norm1_g, w_in, w_out, lambda_q1, lambda_k1, lambda_q2, lambda_k2, subln_g,
           ssm_a_re, ssm_a_im, ssm_log_step, ssm_b_re, ssm_b_im, ssm_c_re, ssm_c_im, ssm_d, w_glu, b_glu,
           norm2_g, w_router, router_bias, w_gate, w_up, w_down, ws_gate, ws_up, ws_down, final_g):
    bsz, seq, dm = x.shape
    n_tok = bsz * seq
    aw = N_ATTN_HEADS * V_HEAD_DIM

    mod3 = _mod_call(c, w_ada[0], b_ada[0]).reshape(bsz, N_ADA, dm)

    wqkv = w_in[0][:, :3 * aw].astype(BF16)
    wut = w_in[0][:, 3 * aw:].T.astype(BF16)
    q, k, v, ut = _inproj_call(x, mod3, norm1_g[0], wqkv, wut)

    lam = (jnp.exp(jnp.sum(lambda_q1[0] * lambda_k1[0])) - jnp.exp(jnp.sum(lambda_q2[0] * lambda_k2[0]))
           + LAM_INIT).reshape(1)
    attn = _attn_call(lam, q, k, v, subln_g[0])

    yt = _ssm_call(ut, *_ssm_operators(ssm_a_re[0], ssm_a_im[0], ssm_log_step[0], ssm_b_re[0], ssm_b_im[0],
                                       ssm_c_re[0], ssm_c_im[0]))

    xp, h2, scores_t = _mid_call(
        x, attn, yt, ut, mod3, ssm_d[0], w_glu[0].T.astype(BF16), b_glu[0],
        w_out[0][:aw].astype(BF16), w_out[0][aw:].astype(BF16), norm2_g[0], w_router[0].T,
        ws_gate[0].astype(BF16), ws_up[0].astype(BF16), ws_down[0].astype(BF16))

    idx, w, rank, counts = _route_call(scores_t, router_bias[0])
    dest, chunk_start, counts, n_blocks = _dispatch_tables(idx, rank, counts.reshape(-1))
    n_rows = n_blocks * MOE_BLOCK

    dest_k = jnp.swapaxes(dest, 0, 1).reshape(TOP_K, n_tok)
    dest_half = jnp.concatenate([dest_k, dest_k + n_rows], axis=1)
    xs = _scatter_rows(h2.reshape(2 * n_tok, dm // 4), dest_half, 2 * n_rows).reshape(2, n_rows, dm // 4)

    out = _moe_call(chunk_start, counts, xs, w_gate[0], w_up[0], w_down[0])
    src = jnp.concatenate([dest_k.reshape(-1), (dest_k + n_rows).reshape(-1)])
    rows = _gather_rows(out.reshape(2 * n_rows, dm // 4), src)
    return _final_call(xp, rows.reshape(2, TOP_K, n_tok, dm // 4), jnp.swapaxes(w, 1, 2), mod3, final_g)
```

```python
import functools
import math

import jax
import jax.numpy as jnp
from jax import lax
from jax.experimental import pallas as pl
from jax.experimental.pallas import tpu as pltpu
from jax.experimental.pallas import tpu_sc as plsc

F32 = jnp.float32
BF16 = jnp.bfloat16

N_ATTN_HEADS = 4
ATTN_HEAD_DIM = 64
V_HEAD_DIM = 128
SSM_GROUP = 16
N_SSM_GROUPS = 32
SSM_STATE = 64
N_EXPERTS = 256
TOP_K = 8
N_EXPERT_GROUPS = 8
TOPK_GROUPS = 4
ROUTED_SCALE = 2.5
NORM_EPS = 1e-6
SUBLN_EPS = 1e-5
N_ADA = 6
LAM_INIT = 0.8 - 0.6 * math.exp(-0.3 * 0)

LANES = 128
MOE_BLOCK = 256
MOE_SLOTS = 6
MOE_EXPERTS_PER_STEP = 2
NEG_BIG = -1e30
LOG2_E = math.log2(math.e)
VMEM_LIMIT = 48 * 1024 * 1024


def _split_bf16(a):
    hi = a.astype(BF16)
    lo = (a - hi.astype(F32)).astype(BF16)
    return hi, lo


def _dot(a, b):
    return jnp.dot(a, b, preferred_element_type=F32)


def _dot_nt(a, b):
    return lax.dot_general(a, b, (((1,), (1,)), ((), ())), preferred_element_type=F32)


def _dot3(a, b):
    ah, al = _split_bf16(a)
    bh, bl = _split_bf16(b)
    return _dot(ah, bh) + _dot(ah, bl) + _dot(al, bh)


def _dot3_nt(a, b):
    ah, al = _split_bf16(a)
    bh, bl = _split_bf16(b)
    return _dot_nt(ah, bh) + _dot_nt(ah, bl) + _dot_nt(al, bh)


def _silu(x):
    return x * jax.nn.sigmoid(x)


def _gelu_tanh(x):
    c = math.sqrt(2.0 / math.pi)
    return 0.5 * x * (1.0 + jnp.tanh(c * (x + 0.044715 * (x * x * x))))


def _pack_rows(x):
    bits = lax.bitcast_convert_type(x.astype(BF16).astype(F32), jnp.uint32)
    half = bits.shape[1] // 2
    packed = (bits[:, :half] >> 16) | (bits[:, half:] & jnp.uint32(0xFFFF0000))
    return packed[:, :half // 2], packed[:, half // 2:]


def _unpack_plane(xu):
    return (lax.bitcast_convert_type(xu << 16, F32),
            lax.bitcast_convert_type(xu & jnp.uint32(0xFFFF0000), F32))


def _mod_kernel(c_ref, w_ref, b_ref, o_ref):
    cond = _silu(c_ref[...])
    o_ref[...] = _dot3(cond, w_ref[...]) + b_ref[...]


def _mod_call(c, w_ada, b_ada):
    bsz, dm = c.shape
    n_out = w_ada.shape[1]
    tn = 1024
    return pl.pallas_call(
        _mod_kernel,
        grid=(n_out // tn,),
        in_specs=[
            pl.BlockSpec((bsz, dm), lambda j: (0, 0)),
            pl.BlockSpec((dm, tn), lambda j: (0, j)),
            pl.BlockSpec((1, tn), lambda j: (0, j)),
        ],
        out_specs=pl.BlockSpec((bsz, tn), lambda j: (0, j)),
        out_shape=jax.ShapeDtypeStruct((bsz, n_out), F32),
        compiler_params=pltpu.CompilerParams(vmem_limit_bytes=VMEM_LIMIT),
        name="mod",
    )(c, w_ada, b_ada.reshape(1, n_out))


def _inproj_kernel(x_ref, mod_ref, g_ref, wqkv_ref, wut_ref, q_ref, k_ref, v_ref, ut_ref, *, aw):
    x = x_ref[...]
    ms = jnp.mean(x * x, axis=-1, keepdims=True)
    h = x * lax.rsqrt(ms + NORM_EPS) * g_ref[...]
    h = h * (1.0 + mod_ref[1:2, :]) + mod_ref[0:1, :]
    hb = h.astype(BF16)
    qkv = _dot(hb, wqkv_ref[...])
    q_ref[...] = (qkv[:, :aw] * (LOG2_E * ATTN_HEAD_DIM ** -0.5)).astype(BF16)
    k_ref[...] = qkv[:, aw:2 * aw].astype(BF16)
    v_ref[...] = qkv[:, 2 * aw:].astype(BF16)
    ut_ref[...] = _dot_nt(wut_ref[...], hb)


def _inproj_call(x, mod3, norm_g, wqkv, wut, tm=512):
    bsz, seq, dm = x.shape
    aw = wqkv.shape[1] // 3
    sw = wut.shape[0]
    row = lambda b, i: (b, i, 0)
    return pl.pallas_call(
        functools.partial(_inproj_kernel, aw=aw),
        grid=(bsz, seq // tm),
        in_specs=[
            pl.BlockSpec((None, tm, dm), row),
            pl.BlockSpec((None, N_ADA, dm), lambda b, i: (b, 0, 0)),
            pl.BlockSpec((1, dm), lambda b, i: (0, 0)),
            pl.BlockSpec(wqkv.shape, lambda b, i: (0, 0)),
            pl.BlockSpec(wut.shape, lambda b, i: (0, 0)),
        ],
        out_specs=[
            pl.BlockSpec((None, tm, aw), row),
            pl.BlockSpec((None, tm, aw), row),
            pl.BlockSpec((None, tm, aw), row),
            pl.BlockSpec((None, sw, tm), lambda b, i: (b, 0, i)),
        ],
        out_shape=[
            jax.ShapeDtypeStruct((bsz, seq, aw), BF16),
            jax.ShapeDtypeStruct((bsz, seq, aw), BF16),
            jax.ShapeDtypeStruct((bsz, seq, aw), BF16),
            jax.ShapeDtypeStruct((bsz, sw, seq), F32),
        ],
        compiler_params=pltpu.CompilerParams(
            dimension_semantics=("parallel", "parallel"), vmem_limit_bytes=VMEM_LIMIT),
        name="inproj",
    )(x, mod3, norm_g.reshape(1, dm), wqkv, wut)


def _attn_kernel(lam_ref, q_ref, k_ref, v_ref, g_ref, o_ref, m_sc, l_sc, acc_sc, *, tq):
    i = pl.program_id(2)
    th = tq // 2
    q = q_ref[...]
    lane = lax.broadcasted_iota(jnp.int32, q.shape, 1)
    zero = jnp.zeros_like(q)
    qa = jnp.where(lane < ATTN_HEAD_DIM, q, zero)
    qb = jnp.where(lane >= ATTN_HEAD_DIM, q, zero)
    q2 = jnp.concatenate([qa[:th], qb[:th], qa[th:], qb[th:]], axis=0)
    m_sc[...] = jnp.full(m_sc.shape, NEG_BIG, F32)
    l_sc[...] = jnp.zeros(l_sc.shape, F32)
    acc_sc[...] = jnp.zeros(acc_sc.shape, F32)

    def step(j, r0, nr, nc, qpos0):
        start = pl.multiple_of(j * tq, tq)
        kt = k_ref[pl.ds(start, nc), :]
        vt = v_ref[pl.ds(start, nc), :]
        s = _dot_nt(q2[r0:r0 + nr], kt)
        if qpos0 is not None:
            r = lax.broadcasted_iota(jnp.int32, s.shape, 0)
            c = lax.broadcasted_iota(jnp.int32, s.shape, 1)
            s = jnp.where(c <= qpos0 + (r & (th - 1)), s, NEG_BIG)
        m_prev = m_sc[r0:r0 + nr, :]
        m_new = jnp.maximum(m_prev, jnp.max(s, axis=-1, keepdims=True))
        alpha = jnp.exp2(m_prev - m_new)
        p = jnp.exp2(s - jnp.concatenate([m_new] * (nc // LANES), axis=1))
        psum = p[:, :LANES]
        for c0 in range(LANES, nc, LANES):
            psum = psum + p[:, c0:c0 + LANES]
        l_sc[r0:r0 + nr, :] = alpha * l_sc[r0:r0 + nr, :] + psum
        acc_sc[r0:r0 + nr, :] = alpha * acc_sc[r0:r0 + nr, :] + _dot(p.astype(BF16), vt)
        m_sc[r0:r0 + nr, :] = m_new

    def body(jj, carry):
        step(2 * jj, 0, 2 * tq, tq, None)
        step(2 * jj + 1, 0, 2 * tq, tq, None)
        return carry

    lax.fori_loop(0, i // 2, body, 0)

    @pl.when(i % 2 == 1)
    def _():
        step(i - 1, 0, 2 * tq, tq, None)

    step(i, 0, tq, th, 0)
    step(i, tq, tq, tq, th)

    o_all = acc_sc[...] / jnp.sum(l_sc[...], axis=-1, keepdims=True)
    o0 = jnp.concatenate([o_all[:th], o_all[tq:tq + th]], axis=0)
    o1 = jnp.concatenate([o_all[th:tq], o_all[tq + th:]], axis=0)
    o = o0 - lam_ref[0] * o1
    ms = jnp.mean(o * o, axis=-1, keepdims=True)
    o = o * lax.rsqrt(ms + SUBLN_EPS) * g_ref[...] * (1.0 - LAM_INIT)
    o_ref[...] = o.astype(o_ref.dtype)


def _attn_call(lam, q, k, v, subln_g, tq=1024):
    bsz, seq, aw = q.shape
    nh = aw // V_HEAD_DIM
    qmap = lambda b, h, i: (b, i, h)
    kvmap = lambda b, h, i: (b, 0, h)
    return pl.pallas_call(
        functools.partial(_attn_kernel, tq=tq),
        grid=(bsz, nh, seq // tq),
        in_specs=[
            pl.BlockSpec(memory_space=pltpu.SMEM),
            pl.BlockSpec((None, tq, V_HEAD_DIM), qmap),
            pl.BlockSpec((None, seq, V_HEAD_DIM), kvmap),
            pl.BlockSpec((None, seq, V_HEAD_DIM), kvmap),
            pl.BlockSpec((1, V_HEAD_DIM), lambda b, h, i: (0, 0)),
        ],
        out_specs=pl.BlockSpec((None, tq, V_HEAD_DIM), qmap),
        out_shape=jax.ShapeDtypeStruct((bsz, seq, aw), BF16),
        scratch_shapes=[pltpu.VMEM((2 * tq, V_HEAD_DIM), F32)] * 3,
        compiler_params=pltpu.CompilerParams(
            dimension_semantics=("parallel", "parallel", "parallel"), vmem_limit_bytes=VMEM_LIMIT),
        name="attn",
    )(lam, q, k, v, subln_g.reshape(1, V_HEAD_DIM))


def _ssm_operators(a_re, a_im, log_step, b_re, b_im, c_re, c_im):
    t = LANES
    hi = lax.Precision.HIGHEST
    lam = lax.complex(jnp.minimum(a_re, -1e-4), a_im)
    delta = jnp.exp(log_step)[:, None]
    lam_bar = jnp.exp(lam * delta)
    bbar = ((lam_bar - 1.0) / lam)[:, :, None] * lax.complex(b_re, b_im)
    cmat = lax.complex(c_re, c_im)
    ld = lam * delta
    tau = jnp.arange(t + 1, dtype=F32)
    pw = jnp.exp(ld[:, :, None] * tau)
    cb = cmat[:, None, :, :] * jnp.swapaxes(bbar, 1, 2)[:, :, None, :]
    g = a_re.shape[0]
    cb = cb.reshape(g, SSM_GROUP * SSM_GROUP, SSM_STATE)
    cbcat = jnp.concatenate([cb.real, -cb.imag], axis=-1)
    pcat = jnp.concatenate([pw.real[:, :, :t], pw.imag[:, :, :t]], axis=1)
    kmat = jnp.einsum('gxp,gpt->gxt', cbcat, pcat, precision=hi)
    prev = jnp.swapaxes(pw[:, :, t - 1::-1], 1, 2)
    arev = jnp.concatenate([prev.real, prev.imag], axis=-1)
    bt = jnp.swapaxes(bbar, 1, 2)
    brow = jnp.stack([jnp.concatenate([bt.real, bt.real], axis=-1),
                      jnp.concatenate([-bt.imag, bt.imag], axis=-1)], axis=2).reshape(g, 2 * SSM_GROUP, 2 * SSM_STATE)
    a1 = jnp.concatenate([pw.real[:, :, 1:], pw.imag[:, :, 1:]], axis=1)
    ct = jnp.swapaxes(cmat, 1, 2)
    ccol = jnp.stack([jnp.concatenate([ct.real, -ct.real], axis=1),
                      jnp.concatenate([-ct.imag, -ct.imag], axis=1)], axis=-1).reshape(g, 2 * SSM_STATE, 2 * SSM_GROUP)
    rows = []
    for i in range(6):
        d = jnp.exp(ld * float(t * (1 << i)))
        rows.append(jnp.concatenate([d.real, d.real], axis=-1))
        rows.append(jnp.concatenate([-d.imag, d.imag], axis=-1))
    rows += [jnp.zeros_like(rows[0])] * 4
    dpow = jnp.stack(rows, axis=1)
    return kmat, arev, brow, a1, ccol, dpow


def _ssm_kernel(u_ref, k_ref, arev_ref, brow_ref, a1_ref, ccol_ref, dp_ref, y_ref, m_sc, ws_sc, wc_sc):
    bsz, nch, n_chunk, t = u_ref.shape
    row = lax.broadcasted_iota(jnp.int32, (t, t), 0)
    col = lax.broadcasted_iota(jnp.int32, (t, t), 1)
    causal = col >= row
    half = arev_ref.shape[1] // 2

    arev = arev_ref[...]
    arev_sw = pltpu.roll(arev, half, 1)
    a1 = a1_ref[...]
    a1_sw = pltpu.roll(a1, half, 0)
    for c in range(nch):
        ws_sc[c * t:(c + 1) * t, :] = (arev * brow_ref[2 * c:2 * c + 1, :]
                                       + arev_sw * brow_ref[2 * c + 1:2 * c + 2, :]).astype(BF16)
        wc_sc[:, c * t:(c + 1) * t] = (a1 * ccol_ref[:, 2 * c:2 * c + 1]
                                       + a1_sw * ccol_ref[:, 2 * c + 1:2 * c + 2]).astype(BF16)

    uflat = jnp.concatenate(
        [jnp.concatenate([u_ref[b, ci] for b in range(bsz)], axis=0) for ci in range(nch)],
        axis=1).astype(BF16)

    y = None
    for c0 in range(0, nch, 2):
        for ci in (c0, c0 + 1):
            for co in range(nch):
                kb = jnp.broadcast_to(k_ref[ci * nch + co:ci * nch + co + 1, :], (t, t))
                kb = pltpu.roll(kb, 0, 1, stride=1, stride_axis=0)
                m_sc[ci * t:(ci + 1) * t, co * t:(co + 1) * t] = jnp.where(causal, kb, 0.0).astype(BF16)
        part = _dot(uflat[:, c0 * t:(c0 + 2) * t], m_sc[c0 * t:(c0 + 2) * t, :])
        y = part if y is None else y + part
    z = _dot(uflat, ws_sc[...])
    kidx = lax.broadcasted_iota(jnp.int32, z.shape, 0) & (n_chunk - 1)
    shift = 1
    i = 0
    while shift < n_chunk:
        zs = jnp.where(kidx >= shift, pltpu.roll(z, shift, 0), 0.0)
        z = z + zs * dp_ref[2 * i:2 * i + 1, :] + pltpu.roll(zs, half, 1) * dp_ref[2 * i + 1:2 * i + 2, :]
        shift *= 2
        i += 1
    xin = jnp.where(kidx >= 1, pltpu.roll(z, 1, 0), 0.0)
    xh, xl = _split_bf16(xin)
    wc = wc_sc[...]
    y = y + _dot(xh, wc) + _dot(xl, wc)
    for b in range(bsz):
        for co in range(nch):
            y_ref[b, co] = y[b * n_chunk:(b + 1) * n_chunk, co * t:(co + 1) * t]


def _ssm_call(ut, kmat, arev, brow, a1, ccol, dpow):
    bsz, sw, seq = ut.shape
    n_groups = sw // SSM_GROUP
    n_chunk = seq // LANES
    assert n_chunk & (n_chunk - 1) == 0 and n_chunk <= 64
    u4 = ut.reshape(bsz, sw, n_chunk, LANES)
    blk = (bsz, SSM_GROUP, n_chunk, LANES)
    gmap = lambda g: (0, g, 0, 0)
    pmap = lambda g: (g, 0, 0)
    y4 = pl.pallas_call(
        _ssm_kernel,
        grid=(n_groups,),
        in_specs=[
            pl.BlockSpec(blk, gmap),
            pl.BlockSpec((None,) + kmat.shape[1:], pmap),
            pl.BlockSpec((None,) + arev.shape[1:], pmap),
            pl.BlockSpec((None,) + brow.shape[1:], pmap),
            pl.BlockSpec((None,) + a1.shape[1:], pmap),
            pl.BlockSpec((None,) + ccol.shape[1:], pmap),
            pl.BlockSpec((None,) + dpow.shape[1:], pmap),
        ],
        out_specs=pl.BlockSpec(blk, gmap),
        out_shape=jax.ShapeDtypeStruct(u4.shape, F32),
        scratch_shapes=[pltpu.VMEM((SSM_GROUP * LANES, SSM_GROUP * LANES), BF16),
                        pltpu.VMEM((SSM_GROUP * LANES, 2 * SSM_STATE), BF16),
                        pltpu.VMEM((2 * SSM_STATE, SSM_GROUP * LANES), BF16)],
        compiler_params=pltpu.CompilerParams(
            dimension_semantics=("parallel",), vmem_limit_bytes=VMEM_LIMIT),
        name="ssm",
    )(u4, kmat, arev, brow, a1, ccol, dpow)
    return y4.reshape(bsz, sw, seq)


def _mid_kernel(x_ref, attn_ref, yt_ref, ut_ref, mod_ref, dsk_ref, wglut_ref, bglu_ref, wo1_ref, wo2_ref,
                g2_ref, wrt_ref, wsg_ref, wsu_ref, wsd_ref, xp_ref, h2_ref, lg_ref):
    gt = _gelu_tanh(yt_ref[...] + dsk_ref[...] * ut_ref[...])
    zt = _dot(wglut_ref[...], gt.astype(BF16)) + bglu_ref[...]
    st = gt * jax.nn.sigmoid(zt)
    s = st.T.astype(BF16)
    mix = _dot(attn_ref[...], wo1_ref[...]) + _dot(s, wo2_ref[...])
    x1 = x_ref[...] + mod_ref[2:3, :] * mix
    ms = jnp.mean(x1 * x1, axis=-1, keepdims=True)
    h2 = x1 * lax.rsqrt(ms + NORM_EPS) * g2_ref[...]
    h2 = h2 * (1.0 + mod_ref[4:5, :]) + mod_ref[3:4, :]
    hb = h2.astype(BF16)
    h2_ref[0], h2_ref[1] = _pack_rows(h2)
    lg_ref[...] = jax.nn.sigmoid(_dot3_nt(wrt_ref[...], h2))
    sh = _silu(_dot(hb, wsg_ref[...])) * _dot(hb, wsu_ref[...])
    shared = _dot(sh.astype(BF16), wsd_ref[...])
    xp_ref[...] = x1 + mod_ref[5:6, :] * shared


def _mid_call(x, attn, yt, ut, mod3, dsk, wglut, bglu, wo1, wo2, g2, wrt, wsg, wsu, wsd, tm=512):
    bsz, seq, dm = x.shape
    aw = attn.shape[2]
    sw = yt.shape[1]
    ne = wrt.shape[0]
    row = lambda b, i: (b, i, 0)
    colm = lambda b, i: (b, 0, i)
    full = lambda a: pl.BlockSpec(a.shape, lambda b, i: (0,) * a.ndim)
    dsk = dsk.reshape(sw, 1)
    bglu = bglu.reshape(sw, 1)
    g2 = g2.reshape(1, dm)
    return pl.pallas_call(
        _mid_kernel,
        grid=(bsz, seq // tm),
        in_specs=[
            pl.BlockSpec((None, tm, dm), row),
            pl.BlockSpec((None, tm, aw), row),
            pl.BlockSpec((None, sw, tm), colm),
            pl.BlockSpec((None, sw, tm), colm),
            pl.BlockSpec((None, N_ADA, dm), lambda b, i: (b, 0, 0)),
            full(dsk), full(wglut), full(bglu), full(wo1), full(wo2), full(g2), full(wrt),
            full(wsg), full(wsu), full(wsd),
        ],
        out_specs=[
            pl.BlockSpec((None, tm, dm), row),
            pl.BlockSpec((2, None, tm, dm // 4), lambda b, i: (0, b, i, 0)),
            pl.BlockSpec((None, ne, tm), colm),
        ],
        out_shape=[
            jax.ShapeDtypeStruct((bsz, seq, dm), F32),
            jax.ShapeDtypeStruct((2, bsz, seq, dm // 4), jnp.uint32),
            jax.ShapeDtypeStruct((bsz, ne, seq), F32),
        ],
        compiler_params=pltpu.CompilerParams(
            dimension_semantics=("parallel", "parallel"), vmem_limit_bytes=VMEM_LIMIT),
        name="mid",
    )(x, attn, yt, ut, mod3, dsk, wglut, bglu, wo1, wo2, g2, wrt, wsg, wsu, wsd)


def _moe_kernel(cs_ref, cnt_ref, xs_hbm, wg_ref, wu_ref, wd_ref, out_hbm, xbuf, obuf, wgb, wub, wdb, isem, osem):
    step = pl.program_id(0)
    n_steps = pl.num_programs(0)
    per_step = wg_ref.shape[0]
    c_end = cs_ref[n_steps * per_step]
    n_slots, _, ch, q = xbuf.shape

    def fetch(c, slot):
        return pltpu.make_async_copy(xs_hbm.at[:, pl.ds(c * ch, ch), :], xbuf.at[slot], isem.at[slot])

    def drain(c, slot):
        return pltpu.make_async_copy(obuf.at[slot], out_hbm.at[:, pl.ds(c * ch, ch), :], osem.at[slot])

    row_queue = 1

    look = n_slots - 2

    @pl.when(step == 0)
    def _():
        for c in range(look):
            @pl.when(c < c_end)
            def _():
                fetch(c, c).start(priority=row_queue)

    def arrive(c, n):
        for i in range(n):
            fetch(c + i, lax.rem(c + i, n_slots)).wait()
        for i in range(n):
            ahead = c + look + i

            @pl.when(ahead < c_end)
            def _():
                fetch(ahead, lax.rem(ahead, n_slots)).start(priority=row_queue)
        for i in range(n):
            @pl.when(c + i >= n_slots)
            def _():
                drain(c + i - n_slots, lax.rem(c + i, n_slots)).wait()

    def compute(c, n, n_valid):
        slots = [lax.rem(c + i, n_slots) for i in range(n)]
        valid = lax.broadcasted_iota(jnp.int32, (n * ch, q), 0) < n_valid
        gate = up = None
        for h in range(2):
            lo, hi = _unpack_plane(jnp.concatenate([xbuf[s, h] for s in slots], axis=0))
            lo = jnp.where(valid, lo, 0.0).astype(BF16)
            hi = jnp.where(valid, hi, 0.0).astype(BF16)
            for xpart, k0 in ((lo, h * q), (hi, (2 + h) * q)):
                g = _dot(xpart, wgb[k0:k0 + q, :])
                u = _dot(xpart, wub[k0:k0 + q, :])
                gate = g if gate is None else gate + g
                up = u if up is None else up + u
        hb = _silu(gate) * up
        p0, p1 = _pack_rows(_dot(hb.astype(BF16), wdb[...]))
        for i, s in enumerate(slots):
            obuf[s, 0] = p0[i * ch:(i + 1) * ch]
            obuf[s, 1] = p1[i * ch:(i + 1) * ch]

    def depart(c, n):
        for i in range(n):
            drain(c + i, lax.rem(c + i, n_slots)).start(priority=row_queue)

    def expert(sub):
        e = step * per_step + sub
        c0, c1 = cs_ref[e], cs_ref[e + 1]

        @pl.when(c1 > c0)
        def _():
            wgb[...] = wg_ref[sub].astype(BF16)
            wub[...] = wu_ref[sub].astype(BF16)
            wdb[...] = wd_ref[sub].astype(BF16)
            n_pairs = (c1 - c0) // 2

            def pair(i, carry):
                c = c0 + 2 * i
                arrive(c, 2)
                compute(c, 2, cnt_ref[e] - (c - c0) * ch)
                depart(c, 2)
                return carry

            lax.fori_loop(0, n_pairs, pair, 0)

            @pl.when(c0 + 2 * n_pairs < c1)
            def _():
                arrive(c1 - 1, 1)
                compute(c1 - 1, 1, cnt_ref[e] - (c1 - 1 - c0) * ch)
                depart(c1 - 1, 1)

    for sub in range(per_step):
        expert(sub)

    @pl.when(step == n_steps - 1)
    def _():
        for back in range(n_slots, 0, -1):
            @pl.when(c_end >= back)
            def _():
                drain(c_end - back, lax.rem(c_end - back, n_slots)).wait()


def _moe_call(chunk_start, counts, xs, w_gate, w_up, w_down):
    n_rows = xs.shape[1]
    n_exp, dm, de = w_gate.shape
    grid_spec = pltpu.PrefetchScalarGridSpec(
        num_scalar_prefetch=2,
        grid=(n_exp // MOE_EXPERTS_PER_STEP,),
        in_specs=[
            pl.BlockSpec(memory_space=pl.ANY),
            pl.BlockSpec((MOE_EXPERTS_PER_STEP, dm, de), lambda s, cs, cnt: (s, 0, 0)),
            pl.BlockSpec((MOE_EXPERTS_PER_STEP, dm, de), lambda s, cs, cnt: (s, 0, 0)),
            pl.BlockSpec((MOE_EXPERTS_PER_STEP, de, dm), lambda s, cs, cnt: (s, 0, 0)),
        ],
        out_specs=pl.BlockSpec(memory_space=pl.ANY),
        scratch_shapes=[
            pltpu.VMEM((MOE_SLOTS, 2, MOE_BLOCK, dm // 4), jnp.uint32),
            pltpu.VMEM((MOE_SLOTS, 2, MOE_BLOCK, dm // 4), jnp.uint32),
            pltpu.VMEM((dm, de), BF16), pltpu.VMEM((dm, de), BF16), pltpu.VMEM((de, dm), BF16),
            pltpu.SemaphoreType.DMA((MOE_SLOTS,)), pltpu.SemaphoreType.DMA((MOE_SLOTS,)),
        ],
    )
    return pl.pallas_call(
        _moe_kernel,
        grid_spec=grid_spec,
        out_shape=jax.ShapeDtypeStruct(xs.shape, jnp.uint32),
        compiler_params=pltpu.CompilerParams(
            dimension_semantics=("arbitrary",), vmem_limit_bytes=VMEM_LIMIT),
        name="moe",
    )(chunk_start, counts, xs, w_gate, w_up, w_down)


def _final_kernel(xp_ref, r_ref, w_ref, mod_ref, g_ref, o_ref):
    w = w_ref[...]
    groups = [None] * 4
    for h in range(2):
        for k in range(r_ref.shape[1]):
            lo, hi = _unpack_plane(r_ref[h, k])
            wk = w[:, k:k + 1]
            groups[h] = wk * lo if groups[h] is None else groups[h] + wk * lo
            groups[2 + h] = wk * hi if groups[2 + h] is None else groups[2 + h] + wk * hi
    routed = jnp.concatenate(groups, axis=1)
    x = xp_ref[...] + mod_ref[5:6, :] * routed
    ms = jnp.mean(x * x, axis=-1, keepdims=True)
    o_ref[...] = x * lax.rsqrt(ms + NORM_EPS) * g_ref[...]


def _final_call(xp, rows, w, mod3, final_g, tm=256):
    bsz, seq, dm = xp.shape
    kk, q = rows.shape[1], rows.shape[3]
    nt = seq // tm
    row = lambda b, i: (b, i, 0)
    return pl.pallas_call(
        _final_kernel,
        grid=(bsz, nt),
        in_specs=[
            pl.BlockSpec((None, tm, dm), row),
            pl.BlockSpec((2, kk, tm, q), lambda b, i: (0, 0, b * nt + i, 0)),
            pl.BlockSpec((None, tm, kk), row),
            pl.BlockSpec((None, N_ADA, dm), lambda b, i: (b, 0, 0)),
            pl.BlockSpec((1, dm), lambda b, i: (0, 0)),
        ],
        out_specs=pl.BlockSpec((None, tm, dm), row),
        out_shape=jax.ShapeDtypeStruct((bsz, seq, dm), F32),
        compiler_params=pltpu.CompilerParams(
            dimension_semantics=("parallel", "parallel"), vmem_limit_bytes=VMEM_LIMIT),
        name="final",
    )(xp, rows, w, mod3, final_g.reshape(1, dm))


def _route_kernel(sc_ref, bias_ref, idx_ref, w_ref, rank_ref, cnt_ref, tri_sc, carry_sc):
    ne, tn = sc_ref.shape
    gsz = ne // N_EXPERT_GROUPS
    neg = -jnp.inf
    first = jnp.logical_and(pl.program_id(0) == 0, pl.program_id(1) == 0)

    @pl.when(first)
    def _():
        r = lax.broadcasted_iota(jnp.int32, (tn, tn), 0)
        c = lax.broadcasted_iota(jnp.int32, (tn, tn), 1)
        tri_sc[...] = jnp.where(r < c, 1.0, 0.0).astype(BF16)
        carry_sc[...] = jnp.zeros(carry_sc.shape, F32)

    s = sc_ref[...]
    sel = s + bias_ref[...]
    gs = []
    for g in range(N_EXPERT_GROUPS):
        blk = sel[g * gsz:(g + 1) * gsz]
        m1 = jnp.max(blk, axis=0, keepdims=True)
        eq = blk == m1
        n_eq = jnp.sum(jnp.where(eq, 1.0, 0.0), axis=0, keepdims=True)
        m2 = jnp.max(jnp.where(eq, neg, blk), axis=0, keepdims=True)
        gs.append(m1 + jnp.where(n_eq >= 2.0, m1, m2))
    gs = jnp.concatenate(gs, axis=0)
    gi = lax.broadcasted_iota(jnp.int32, gs.shape, 0)
    beaten = jnp.zeros(gs.shape, F32)
    for gp in range(N_EXPERT_GROUPS):
        other = gs[gp:gp + 1]
        wins = jnp.where(other > gs, 1.0, jnp.where(other == gs, jnp.where(gi > gp, 1.0, 0.0), 0.0))
        beaten = beaten + wins
    gadd = jnp.where(beaten < float(TOPK_GROUPS), 0.0, neg)
    cur = jnp.concatenate(
        [sel[g * gsz:(g + 1) * gsz] + gadd[g:g + 1] for g in range(N_EXPERT_GROUPS)], axis=0)

    eidx = lax.broadcasted_iota(jnp.int32, (ne, tn), 0).astype(F32)
    chosen = jnp.zeros((ne, tn), F32)
    idx_rows, s_rows = [], []
    for _ in range(TOP_K):
        m = jnp.max(cur, axis=0, keepdims=True)
        ik = jnp.min(jnp.where(cur == m, eidx, float(ne)), axis=0, keepdims=True)
        oh = eidx == ik
        s_rows.append(jnp.sum(jnp.where(oh, s, 0.0), axis=0, keepdims=True))
        cur = jnp.where(oh, neg, cur)
        chosen = chosen + jnp.where(oh, 1.0, 0.0)
        idx_rows.append(ik)
    sk = jnp.concatenate(s_rows, axis=0)
    w_ref[...] = sk / jnp.sum(sk, axis=0, keepdims=True) * ROUTED_SCALE
    idx_ref[...] = jnp.concatenate(idx_rows, axis=0).astype(jnp.int32)

    before = _dot(chosen.astype(BF16), tri_sc[...]) + carry_sc[...]
    rank_rows = [jnp.sum(jnp.where(eidx == ik, before, 0.0), axis=0, keepdims=True) for ik in idx_rows]
    rank_ref[...] = jnp.concatenate(rank_rows, axis=0).astype(jnp.int32)
    carry_sc[...] = carry_sc[...] + jnp.sum(chosen, axis=1, keepdims=True)
    cnt_ref[...] = carry_sc[...]


def _route_call(scores_t, router_bias, tn=512):
    bsz, ne, seq = scores_t.shape
    tok = lambda b, i: (b, 0, i)
    return pl.pallas_call(
        _route_kernel,
        grid=(bsz, seq // tn),
        in_specs=[
            pl.BlockSpec((None, ne, tn), tok),
            pl.BlockSpec((ne, 1), lambda b, i: (0, 0)),
        ],
        out_specs=[
            pl.BlockSpec((None, TOP_K, tn), tok),
            pl.BlockSpec((None, TOP_K, tn), tok),
            pl.BlockSpec((None, TOP_K, tn), tok),
            pl.BlockSpec((ne, 1), lambda b, i: (0, 0)),
        ],
        out_shape=[
            jax.ShapeDtypeStruct((bsz, TOP_K, seq), jnp.int32),
            jax.ShapeDtypeStruct((bsz, TOP_K, seq), F32),
            jax.ShapeDtypeStruct((bsz, TOP_K, seq), jnp.int32),
            jax.ShapeDtypeStruct((ne, 1), F32),
        ],
        scratch_shapes=[pltpu.VMEM((tn, tn), BF16), pltpu.VMEM((ne, 1), F32)],
        compiler_params=pltpu.CompilerParams(
            dimension_semantics=("arbitrary", "arbitrary"), vmem_limit_bytes=VMEM_LIMIT),
        name="route",
    )(scores_t, router_bias.reshape(ne, 1))


def _dispatch_tables(idx, rank, counts):
    ne = counts.shape[0]
    n_tok = idx.shape[0] * idx.shape[2]
    counts = counts.astype(jnp.int32)
    pcounts = (counts + MOE_BLOCK - 1) // MOE_BLOCK * MOE_BLOCK
    pstarts = jnp.concatenate([jnp.zeros((1,), jnp.int32), jnp.cumsum(pcounts).astype(jnp.int32)])
    n_blocks = -(-(n_tok * TOP_K + ne * (MOE_BLOCK - 1)) // MOE_BLOCK)
    dest = _dest_call(pstarts[:ne], idx, rank)
    return dest, pstarts // MOE_BLOCK, counts, n_blocks


def _dest_kernel(ps_ref, idx_ref, rank_ref, o_ref):
    idx = idx_ref[...]

    def body(e, acc):
        return jnp.where(idx == e, ps_ref[e], acc)

    o_ref[...] = rank_ref[...] + lax.fori_loop(0, ps_ref.shape[0], body, jnp.zeros(idx.shape, jnp.int32))


def _dest_call(pstarts, idx, rank, tn=2048):
    bsz, kk, seq = idx.shape
    tok = lambda b, i, ps: (b, 0, i)
    grid_spec = pltpu.PrefetchScalarGridSpec(
        num_scalar_prefetch=1,
        grid=(bsz, seq // tn),
        in_specs=[pl.BlockSpec((None, kk, tn), tok), pl.BlockSpec((None, kk, tn), tok)],
        out_specs=pl.BlockSpec((None, kk, tn), tok),
    )
    return pl.pallas_call(
        _dest_kernel,
        grid_spec=grid_spec,
        out_shape=jax.ShapeDtypeStruct(idx.shape, jnp.int32),
        compiler_params=pltpu.CompilerParams(dimension_semantics=("parallel", "parallel")),
        name="dest",
    )(pstarts, idx, rank)


def _scatter_rows(rows, dest, n_out):
    n_tok, width = rows.shape
    k_slots = dest.shape[0]
    win = LANES
    mesh = plsc.VectorSubcoreMesh(core_axis_name="c", subcore_axis_name="s")

    @pl.kernel(out_type=jax.ShapeDtypeStruct((n_out, width), rows.dtype), mesh=mesh, scratch_types=[])
    def scatter(rows_hbm, dest_hbm, out_hbm):
        def body(rows_vmem, idx_vmem):
            pltpu.sync_copy(rows_vmem, out_hbm.at[idx_vmem.at[0]])

        pltpu.emit_pipeline(
            body,
            grid=(n_tok // win, k_slots),
            in_specs=[
                pl.BlockSpec((win, width), lambda i, k: (i, 0)),
                pl.BlockSpec((1, win), lambda i, k: (k, i)),
            ],
            out_specs=[],
            core_axis_name=("c", "s"),
            dimension_semantics=(pltpu.PARALLEL, pltpu.ARBITRARY),
        )(rows_hbm, dest_hbm)

    return scatter(rows, dest)


def _gather_rows(rows, idx):
    n = idx.shape[0]
    width = rows.shape[1]
    win = LANES
    mesh = plsc.VectorSubcoreMesh(core_axis_name="c", subcore_axis_name="s")

    @pl.kernel(out_type=jax.ShapeDtypeStruct((n, width), rows.dtype), mesh=mesh, scratch_types=[])
    def gather(rows_hbm, idx_hbm, out_hbm):
        def body(idx_vmem, out_vmem):
            pltpu.sync_copy(rows_hbm.at[idx_vmem.at[0]], out_vmem)

        pltpu.emit_pipeline(
            body,
            grid=(n // win,),
            in_specs=[pl.BlockSpec((1, win), lambda i: (0, i))],
            out_specs=[pl.BlockSpec((win, width), lambda i: (i, 0))],
            core_axis_name=("c", "s"),
            dimension_semantics=(pltpu.PARALLEL,),
        )(idx_hbm, out_hbm)

    return gather(rows, idx.reshape(1, n))


def kernel(x, c, w_ada, b_ada, norm1_g, w_in, w_out, lambda_q1, lambda_k1, lambda_q2, lambda_k2, subln_g,
           ssm_a_re, ssm_a_im, ssm_log_step, ssm_b_re, ssm_b_im, ssm_c_re, ssm_c_im, ssm_d, w_glu, b_glu,
           norm2_g, w_router, router_bias, w_gate, w_up, w_down, ws_gate, ws_up, ws_down, final_g):
    bsz, seq, dm = x.shape
    n_tok = bsz * seq
    aw = N_ATTN_HEADS * V_HEAD_DIM

    mod3 = _mod_call(c, w_ada[0], b_ada[0]).reshape(bsz, N_ADA, dm)

    wqkv = w_in[0][:, :3 * aw].astype(BF16)
    wut = w_in[0][:, 3 * aw:].T.astype(BF16)
    q, k, v, ut = _inproj_call(x, mod3, norm1_g[0], wqkv, wut)

    lam = (jnp.exp(jnp.sum(lambda_q1[0] * lambda_k1[0])) - jnp.exp(jnp.sum(lambda_q2[0] * lambda_k2[0]))
           + LAM_INIT).reshape(1)
    attn = _attn_call(lam, q, k, v, subln_g[0])

    yt = _ssm_call(ut, *_ssm_operators(ssm_a_re[0], ssm_a_im[0], ssm_log_step[0], ssm_b_re[0], ssm_b_im[0],
                                       ssm_c_re[0], ssm_c_im[0]))

    xp, h2, scores_t = _mid_call(
        x, attn, yt, ut, mod3, ssm_d[0], w_glu[0].T.astype(BF16), b_glu[0],
        w_out[0][:aw].astype(BF16), w_out[0][aw:].astype(BF16), norm2_g[0], w_router[0].T,
        ws_gate[0].astype(BF16), ws_up[0].astype(BF16), ws_down[0].astype(BF16))

    idx, w, rank, counts = _route_call(scores_t, router_bias[0])
    dest, chunk_start, counts, n_blocks = _dispatch_tables(idx, rank, counts.reshape(-1))
    n_rows = n_blocks * MOE_BLOCK

    dest_k = jnp.swapaxes(dest, 0, 1).reshape(TOP_K, n_tok)
    dest_half = jnp.concatenate([dest_k, dest_k + n_rows], axis=1)
    xs = _scatter_rows(h2.reshape(2 * n_tok, dm // 4), dest_half, 2 * n_rows).reshape(2, n_rows, dm // 4)

    out = _moe_call(chunk_start, counts, xs, w_gate[0], w_up[0], w_down[0])
    src = jnp.concatenate([dest_k.reshape(-1), (dest_k + n_rows).reshape(-1)])
    rows = _gather_rows(out.reshape(2 * n_rows, dm // 4), src)
    return _final_call(xp, rows.reshape(2, TOP_K, n_tok, dm // 4), jnp.swapaxes(w, 1, 2), mod3, final_g)
```

```python
import functools
import math

import jax
import jax.numpy as jnp
from jax import lax
from jax.experimental import pallas as pl
from jax.experimental.pallas import tpu as pltpu
from jax.experimental.pallas import tpu_sc as plsc

F32 = jnp.float32
BF16 = jnp.bfloat16

N_ATTN_HEADS = 4
ATTN_HEAD_DIM = 64
V_HEAD_DIM = 128
SSM_GROUP = 16
N_SSM_GROUPS = 32
SSM_STATE = 64
N_EXPERTS = 256
TOP_K = 8
N_EXPERT_GROUPS = 8
TOPK_GROUPS = 4
ROUTED_SCALE = 2.5
NORM_EPS = 1e-6
SUBLN_EPS = 1e-5
N_ADA = 6
LAM_INIT = 0.8 - 0.6 * math.exp(-0.3 * 0)

LANES = 128
MOE_BLOCK = 256
MOE_SLOTS = 6
MOE_EXPERTS_PER_STEP = 2
NEG_BIG = -1e30
LOG2_E = math.log2(math.e)
VMEM_LIMIT = 48 * 1024 * 1024


def _split_bf16(a):
    hi = a.astype(BF16)
    lo = (a - hi.astype(F32)).astype(BF16)
    return hi, lo


def _dot(a, b):
    return jnp.dot(a, b, preferred_element_type=F32)


def _dot_nt(a, b):
    return lax.dot_general(a, b, (((1,), (1,)), ((), ())), preferred_element_type=F32)


def _dot3(a, b):
    ah, al = _split_bf16(a)
    bh, bl = _split_bf16(b)
    return _dot(ah, bh) + _dot(ah, bl) + _dot(al, bh)


def _dot3_nt(a, b):
    ah, al = _split_bf16(a)
    bh, bl = _split_bf16(b)
    return _dot_nt(ah, bh) + _dot_nt(ah, bl) + _dot_nt(al, bh)


def _silu(x):
    return x * jax.nn.sigmoid(x)


def _gelu_tanh(x):
    c = math.sqrt(2.0 / math.pi)
    return 0.5 * x * (1.0 + jnp.tanh(c * (x + 0.044715 * (x * x * x))))


def _pack_rows(x):
    bits = lax.bitcast_convert_type(x.astype(BF16).astype(F32), jnp.uint32)
    half = bits.shape[1] // 2
    packed = (bits[:, :half] >> 16) | (bits[:, half:] & jnp.uint32(0xFFFF0000))
    return packed[:, :half // 2], packed[:, half // 2:]


def _unpack_plane(xu):
    return (lax.bitcast_convert_type(xu << 16, F32),
            lax.bitcast_convert_type(xu & jnp.uint32(0xFFFF0000), F32))


def _mod_kernel(c_ref, w_ref, b_ref, o_ref):
    cond = _silu(c_ref[...])
    o_ref[...] = _dot3(cond, w_ref[...]) + b_ref[...]


def _mod_call(c, w_ada, b_ada):
    bsz, dm = c.shape
    n_out = w_ada.shape[1]
    tn = 1024
    return pl.pallas_call(
        _mod_kernel,
        grid=(n_out // tn,),
        in_specs=[
            pl.BlockSpec((bsz, dm), lambda j: (0, 0)),
            pl.BlockSpec((dm, tn), lambda j: (0, j)),
            pl.BlockSpec((1, tn), lambda j: (0, j)),
        ],
        out_specs=pl.BlockSpec((bsz, tn), lambda j: (0, j)),
        out_shape=jax.ShapeDtypeStruct((bsz, n_out), F32),
        compiler_params=pltpu.CompilerParams(vmem_limit_bytes=VMEM_LIMIT),
        name="mod",
    )(c, w_ada, b_ada.reshape(1, n_out))


def _inproj_kernel(x_ref, mod_ref, g_ref, wqkv_ref, wut_ref, q_ref, k_ref, v_ref, ut_ref, *, aw):
    x = x_ref[...]
    ms = jnp.mean(x * x, axis=-1, keepdims=True)
    h = x * lax.rsqrt(ms + NORM_EPS) * g_ref[...]
    h = h * (1.0 + mod_ref[1:2, :]) + mod_ref[0:1, :]
    hb = h.astype(BF16)
    qkv = _dot(hb, wqkv_ref[...])
    q_ref[...] = (qkv[:, :aw] * (LOG2_E * ATTN_HEAD_DIM ** -0.5)).astype(BF16)
    k_ref[...] = qkv[:, aw:2 * aw].astype(BF16)
    v_ref[...] = qkv[:, 2 * aw:].astype(BF16)
    ut_ref[...] = _dot_nt(wut_ref[...], hb)


def _inproj_call(x, mod3, norm_g, wqkv, wut, tm=512):
    bsz, seq, dm = x.shape
    aw = wqkv.shape[1] // 3
    sw = wut.shape[0]
    row = lambda b, i: (b, i, 0)
    return pl.pallas_call(
        functools.partial(_inproj_kernel, aw=aw),
        grid=(bsz, seq // tm),
        in_specs=[
            pl.BlockSpec((None, tm, dm), row),
            pl.BlockSpec((None, N_ADA, dm), lambda b, i: (b, 0, 0)),
            pl.BlockSpec((1, dm), lambda b, i: (0, 0)),
            pl.BlockSpec(wqkv.shape, lambda b, i: (0, 0)),
            pl.BlockSpec(wut.shape, lambda b, i: (0, 0)),
        ],
        out_specs=[
            pl.BlockSpec((None, tm, aw), row),
            pl.BlockSpec((None, tm, aw), row),
            pl.BlockSpec((None, tm, aw), row),
            pl.BlockSpec((None, sw, tm), lambda b, i: (b, 0, i)),
        ],
        out_shape=[
            jax.ShapeDtypeStruct((bsz, seq, aw), BF16),
            jax.ShapeDtypeStruct((bsz, seq, aw), BF16),
            jax.ShapeDtypeStruct((bsz, seq, aw), BF16),
            jax.ShapeDtypeStruct((bsz, sw, seq), F32),
        ],
        compiler_params=pltpu.CompilerParams(
            dimension_semantics=("parallel", "parallel"), vmem_limit_bytes=VMEM_LIMIT),
        name="inproj",
    )(x, mod3, norm_g.reshape(1, dm), wqkv, wut)


def _attn_kernel(lam_ref, q_ref, k_ref, v_ref, g_ref, o_ref, m_sc, l_sc, acc_sc, *, tq):
    i = pl.program_id(2)
    th = tq // 2
    q = q_ref[...]
    lane = lax.broadcasted_iota(jnp.int32, q.shape, 1)
    zero = jnp.zeros_like(q)
    qa = jnp.where(lane < ATTN_HEAD_DIM, q, zero)
    qb = jnp.where(lane >= ATTN_HEAD_DIM, q, zero)
    q2 = jnp.concatenate([qa[:th], qb[:th], qa[th:], qb[th:]], axis=0)
    m_sc[...] = jnp.full(m_sc.shape, NEG_BIG, F32)
    l_sc[...] = jnp.zeros(l_sc.shape, F32)
    acc_sc[...] = jnp.zeros(acc_sc.shape, F32)

    def step(j, r0, nr, nc, qpos0):
        start = pl.multiple_of(j * tq, tq)
        kt = k_ref[pl.ds(start, nc), :]
        vt = v_ref[pl.ds(start, nc), :]
        s = _dot_nt(q2[r0:r0 + nr], kt)
        if qpos0 is not None:
            r = lax.broadcasted_iota(jnp.int32, s.shape, 0)
            c = lax.broadcasted_iota(jnp.int32, s.shape, 1)
            s = jnp.where(c <= qpos0 + (r & (th - 1)), s, NEG_BIG)
        m_prev = m_sc[r0:r0 + nr, :]
        m_new = jnp.maximum(m_prev, jnp.max(s, axis=-1, keepdims=True))
        alpha = jnp.exp2(m_prev - m_new)
        p = jnp.exp2(s - jnp.concatenate([m_new] * (nc // LANES), axis=1))
        psum = p[:, :LANES]
        for c0 in range(LANES, nc, LANES):
            psum = psum + p[:, c0:c0 + LANES]
        l_sc[r0:r0 + nr, :] = alpha * l_sc[r0:r0 + nr, :] + psum
        acc_sc[r0:r0 + nr, :] = alpha * acc_sc[r0:r0 + nr, :] + _dot(p.astype(BF16), vt)
        m_sc[r0:r0 + nr, :] = m_new

    def body(jj, carry):
        step(2 * jj, 0, 2 * tq, tq, None)
        step(2 * jj + 1, 0, 2 * tq, tq, None)
        return carry

    lax.fori_loop(0, i // 2, body, 0)

    @pl.when(i % 2 == 1)
    def _():
        step(i - 1, 0, 2 * tq, tq, None)

    step(i, 0, tq, th, 0)
    step(i, tq, tq, tq, th)

    o_all = acc_sc[...] / jnp.sum(l_sc[...], axis=-1, keepdims=True)
    o0 = jnp.concatenate([o_all[:th], o_all[tq:tq + th]], axis=0)
    o1 = jnp.concatenate([o_all[th:tq], o_all[tq + th:]], axis=0)
    o = o0 - lam_ref[0] * o1
    ms = jnp.mean(o * o, axis=-1, keepdims=True)
    o = o * lax.rsqrt(ms + SUBLN_EPS) * g_ref[...] * (1.0 - LAM_INIT)
    o_ref[...] = o.astype(o_ref.dtype)


def _attn_call(lam, q, k, v, subln_g, tq=1024):
    bsz, seq, aw = q.shape
    nh = aw // V_HEAD_DIM
    qmap = lambda b, h, i: (b, i, h)
    kvmap = lambda b, h, i: (b, 0, h)
    return pl.pallas_call(
        functools.partial(_attn_kernel, tq=tq),
        grid=(bsz, nh, seq // tq),
        in_specs=[
            pl.BlockSpec(memory_space=pltpu.SMEM),
            pl.BlockSpec((None, tq, V_HEAD_DIM), qmap),
            pl.BlockSpec((None, seq, V_HEAD_DIM), kvmap),
            pl.BlockSpec((None, seq, V_HEAD_DIM), kvmap),
            pl.BlockSpec((1, V_HEAD_DIM), lambda b, h, i: (0, 0)),
        ],
        out_specs=pl.BlockSpec((None, tq, V_HEAD_DIM), qmap),
        out_shape=jax.ShapeDtypeStruct((bsz, seq, aw), BF16),
        scratch_shapes=[pltpu.VMEM((2 * tq, V_HEAD_DIM), F32)] * 3,
        compiler_params=pltpu.CompilerParams(
            dimension_semantics=("parallel", "parallel", "parallel"), vmem_limit_bytes=VMEM_LIMIT),
        name="attn",
    )(lam, q, k, v, subln_g.reshape(1, V_HEAD_DIM))


def _ssm_operators(a_re, a_im, log_step, b_re, b_im, c_re, c_im):
    t = LANES
    hi = lax.Precision.HIGHEST
    lam = lax.complex(jnp.minimum(a_re, -1e-4), a_im)
    delta = jnp.exp(log_step)[:, None]
    lam_bar = jnp.exp(lam * delta)
    bbar = ((lam_bar - 1.0) / lam)[:, :, None] * lax.complex(b_re, b_im)
    cmat = lax.complex(c_re, c_im)
    ld = lam * delta
    tau = jnp.arange(t + 1, dtype=F32)
    pw = jnp.exp(ld[:, :, None] * tau)
    cb = cmat[:, None, :, :] * jnp.swapaxes(bbar, 1, 2)[:, :, None, :]
    g = a_re.shape[0]
    cb = cb.reshape(g, SSM_GROUP * SSM_GROUP, SSM_STATE)
    cbcat = jnp.concatenate([cb.real, -cb.imag], axis=-1)
    pcat = jnp.concatenate([pw.real[:, :, :t], pw.imag[:, :, :t]], axis=1)
    kmat = jnp.einsum('gxp,gpt->gxt', cbcat, pcat, precision=hi)
    prev = jnp.swapaxes(pw[:, :, t - 1::-1], 1, 2)
    arev = jnp.concatenate([prev.real, prev.imag], axis=-1)
    bt = jnp.swapaxes(bbar, 1, 2)
    brow = jnp.stack([jnp.concatenate([bt.real, bt.real], axis=-1),
                      jnp.concatenate([-bt.imag, bt.imag], axis=-1)], axis=2).reshape(g, 2 * SSM_GROUP, 2 * SSM_STATE)
    a1 = jnp.concatenate([pw.real[:, :, 1:], pw.imag[:, :, 1:]], axis=1)
    ct = jnp.swapaxes(cmat, 1, 2)
    ccol = jnp.stack([jnp.concatenate([ct.real, -ct.real], axis=1),
                      jnp.concatenate([-ct.imag, -ct.imag], axis=1)], axis=-1).reshape(g, 2 * SSM_STATE, 2 * SSM_GROUP)
    rows = []
    for i in range(6):
        d = jnp.exp(ld * float(t * (1 << i)))
        rows.append(jnp.concatenate([d.real, d.real], axis=-1))
        rows.append(jnp.concatenate([-d.imag, d.imag], axis=-1))
    rows += [jnp.zeros_like(rows[0])] * 4
    dpow = jnp.stack(rows, axis=1)
    return kmat, arev, brow, a1, ccol, dpow


def _ssm_kernel(u_ref, k_ref, arev_ref, brow_ref, a1_ref, ccol_ref, dp_ref, y_ref, m_sc, ws_sc, wc_sc):
    bsz, nch, n_chunk, t = u_ref.shape
    row = lax.broadcasted_iota(jnp.int32, (t, t), 0)
    col = lax.broadcasted_iota(jnp.int32, (t, t), 1)
    causal = col >= row
    half = arev_ref.shape[1] // 2

    arev = arev_ref[...]
    arev_sw = pltpu.roll(arev, half, 1)
    a1 = a1_ref[...]
    a1_sw = pltpu.roll(a1, half, 0)
    for c in range(nch):
        ws_sc[c * t:(c + 1) * t, :] = (arev * brow_ref[2 * c:2 * c + 1, :]
                                       + arev_sw * brow_ref[2 * c + 1:2 * c + 2, :]).astype(BF16)
        wc_sc[:, c * t:(c + 1) * t] = (a1 * ccol_ref[:, 2 * c:2 * c + 1]
                                       + a1_sw * ccol_ref[:, 2 * c + 1:2 * c + 2]).astype(BF16)

    uflat = jnp.concatenate(
        [jnp.concatenate([u_ref[b, ci] for b in range(bsz)], axis=0) for ci in range(nch)],
        axis=1).astype(BF16)

    y = None
    for c0 in range(0, nch, 2):
        for ci in (c0, c0 + 1):
            for co in range(nch):
                kb = jnp.broadcast_to(k_ref[ci * nch + co:ci * nch + co + 1, :], (t, t))
                kb = pltpu.roll(kb, 0, 1, stride=1, stride_axis=0)
                m_sc[ci * t:(ci + 1) * t, co * t:(co + 1) * t] = jnp.where(causal, kb, 0.0).astype(BF16)
        part = _dot(uflat[:, c0 * t:(c0 + 2) * t], m_sc[c0 * t:(c0 + 2) * t, :])
        y = part if y is None else y + part
    z = _dot(uflat, ws_sc[...])
    kidx = lax.broadcasted_iota(jnp.int32, z.shape, 0) & (n_chunk - 1)
    shift = 1
    i = 0
    while shift < n_chunk:
        zs = jnp.where(kidx >= shift, pltpu.roll(z, shift, 0), 0.0)
        z = z + zs * dp_ref[2 * i:2 * i + 1, :] + pltpu.roll(zs, half, 1) * dp_ref[2 * i + 1:2 * i + 2, :]
        shift *= 2
        i += 1
    xin = jnp.where(kidx >= 1, pltpu.roll(z, 1, 0), 0.0)
    xh, xl = _split_bf16(xin)
    wc = wc_sc[...]
    y = y + _dot(xh, wc) + _dot(xl, wc)
    for b in range(bsz):
        for co in range(nch):
            y_ref[b, co] = y[b * n_chunk:(b + 1) * n_chunk, co * t:(co + 1) * t]


def _ssm_call(ut, kmat, arev, brow, a1, ccol, dpow):
    bsz, sw, seq = ut.shape
    n_groups = sw // SSM_GROUP
    n_chunk = seq // LANES
    assert n_chunk & (n_chunk - 1) == 0 and n_chunk <= 64
    u4 = ut.reshape(bsz, sw, n_chunk, LANES)
    blk = (bsz, SSM_GROUP, n_chunk, LANES)
    gmap = lambda g: (0, g, 0, 0)
    pmap = lambda g: (g, 0, 0)
    y4 = pl.pallas_call(
        _ssm_kernel,
        grid=(n_groups,),
        in_specs=[
            pl.BlockSpec(blk, gmap),
            pl.BlockSpec((None,) + kmat.shape[1:], pmap),
            pl.BlockSpec((None,) + arev.shape[1:], pmap),
            pl.BlockSpec((None,) + brow.shape[1:], pmap),
            pl.BlockSpec((None,) + a1.shape[1:], pmap),
            pl.BlockSpec((None,) + ccol.shape[1:], pmap),
            pl.BlockSpec((None,) + dpow.shape[1:], pmap),
        ],
        out_specs=pl.BlockSpec(blk, gmap),
        out_shape=jax.ShapeDtypeStruct(u4.shape, F32),
        scratch_shapes=[pltpu.VMEM((SSM_GROUP * LANES, SSM_GROUP * LANES), BF16),
                        pltpu.VMEM((SSM_GROUP * LANES, 2 * SSM_STATE), BF16),
                        pltpu.VMEM((2 * SSM_STATE, SSM_GROUP * LANES), BF16)],
        compiler_params=pltpu.CompilerParams(
            dimension_semantics=("parallel",), vmem_limit_bytes=VMEM_LIMIT),
        name="ssm",
    )(u4, kmat, arev, brow, a1, ccol, dpow)
    return y4.reshape(bsz, sw, seq)


def _mid_kernel(x_ref, attn_ref, yt_ref, ut_ref, mod_ref, dsk_ref, wglut_ref, bglu_ref, wo1_ref, wo2_ref,
                g2_ref, wrt_ref, x1_ref, h2_ref, lg_ref):
    gt = _gelu_tanh(yt_ref[...] + dsk_ref[...] * ut_ref[...])
    zt = _dot(wglut_ref[...], gt.astype(BF16)) + bglu_ref[...]
    st = gt * jax.nn.sigmoid(zt)
    s = st.T.astype(BF16)
    mix = _dot(attn_ref[...], wo1_ref[...]) + _dot(s, wo2_ref[...])
    x1 = x_ref[...] + mod_ref[2:3, :] * mix
    ms = jnp.mean(x1 * x1, axis=-1, keepdims=True)
    h2 = x1 * lax.rsqrt(ms + NORM_EPS) * g2_ref[...]
    h2 = h2 * (1.0 + mod_ref[4:5, :]) + mod_ref[3:4, :]
    x1_ref[...] = x1
    h2_ref[0], h2_ref[1] = _pack_rows(h2)
    lg_ref[...] = jax.nn.sigmoid(_dot3_nt(wrt_ref[...], h2))


def _mid_call(x, attn, yt, ut, mod3, dsk, wglut, bglu, wo1, wo2, g2, wrt, tm=512):
    bsz, seq, dm = x.shape
    aw = attn.shape[2]
    sw = yt.shape[1]
    ne = wrt.shape[0]
    row = lambda b, i: (b, i, 0)
    colm = lambda b, i: (b, 0, i)
    full = lambda a: pl.BlockSpec(a.shape, lambda b, i: (0,) * a.ndim)
    dsk = dsk.reshape(sw, 1)
    bglu = bglu.reshape(sw, 1)
    g2 = g2.reshape(1, dm)
    return pl.pallas_call(
        _mid_kernel,
        grid=(bsz, seq // tm),
        in_specs=[
            pl.BlockSpec((None, tm, dm), row),
            pl.BlockSpec((None, tm, aw), row),
            pl.BlockSpec((None, sw, tm), colm),
            pl.BlockSpec((None, sw, tm), colm),
            pl.BlockSpec((None, N_ADA, dm), lambda b, i: (b, 0, 0)),
            full(dsk), full(wglut), full(bglu), full(wo1), full(wo2), full(g2), full(wrt),
        ],
        out_specs=[
            pl.BlockSpec((None, tm, dm), row),
            pl.BlockSpec((2, None, tm, dm // 4), lambda b, i: (0, b, i, 0)),
            pl.BlockSpec((None, ne, tm), colm),
        ],
        out_shape=[
            jax.ShapeDtypeStruct((bsz, seq, dm), F32),
            jax.ShapeDtypeStruct((2, bsz, seq, dm // 4), jnp.uint32),
            jax.ShapeDtypeStruct((bsz, ne, seq), F32),
        ],
        compiler_params=pltpu.CompilerParams(
            dimension_semantics=("parallel", "parallel"), vmem_limit_bytes=VMEM_LIMIT),
        name="mid",
    )(x, attn, yt, ut, mod3, dsk, wglut, bglu, wo1, wo2, g2, wrt)


def _moe_kernel(cs_ref, cnt_ref, xs_hbm, wg_ref, wu_ref, wd_ref, out_hbm, xbuf, obuf, wgb, wub, wdb, isem, osem):
    step = pl.program_id(0)
    n_steps = pl.num_programs(0)
    per_step = wg_ref.shape[0]
    c_end = cs_ref[n_steps * per_step]
    n_slots, _, ch, q = xbuf.shape

    def fetch(c, slot):
        return pltpu.make_async_copy(xs_hbm.at[:, pl.ds(c * ch, ch), :], xbuf.at[slot], isem.at[slot])

    def drain(c, slot):
        return pltpu.make_async_copy(obuf.at[slot], out_hbm.at[:, pl.ds(c * ch, ch), :], osem.at[slot])

    row_queue = 1

    look = n_slots - 2

    @pl.when(step == 0)
    def _():
        for c in range(look):
            @pl.when(c < c_end)
            def _():
                fetch(c, c).start(priority=row_queue)

    def arrive(c, n):
        for i in range(n):
            fetch(c + i, lax.rem(c + i, n_slots)).wait()
        for i in range(n):
            ahead = c + look + i

            @pl.when(ahead < c_end)
            def _():
                fetch(ahead, lax.rem(ahead, n_slots)).start(priority=row_queue)
        for i in range(n):
            @pl.when(c + i >= n_slots)
            def _():
                drain(c + i - n_slots, lax.rem(c + i, n_slots)).wait()

    def compute(c, n, n_valid):
        slots = [lax.rem(c + i, n_slots) for i in range(n)]
        valid = lax.broadcasted_iota(jnp.int32, (n * ch, q), 0) < n_valid
        gate = up = None
        for h in range(2):
            lo, hi = _unpack_plane(jnp.concatenate([xbuf[s, h] for s in slots], axis=0))
            lo = jnp.where(valid, lo, 0.0).astype(BF16)
            hi = jnp.where(valid, hi, 0.0).astype(BF16)
            for xpart, k0 in ((lo, h * q), (hi, (2 + h) * q)):
                g = _dot(xpart, wgb[k0:k0 + q, :])
                u = _dot(xpart, wub[k0:k0 + q, :])
                gate = g if gate is None else gate + g
                up = u if up is None else up + u
        hb = _silu(gate) * up
        p0, p1 = _pack_rows(_dot(hb.astype(BF16), wdb[...]))
        for i, s in enumerate(slots):
            obuf[s, 0] = p0[i * ch:(i + 1) * ch]
            obuf[s, 1] = p1[i * ch:(i + 1) * ch]

    def depart(c, n):
        for i in range(n):
            drain(c + i, lax.rem(c + i, n_slots)).start(priority=row_queue)

    def expert(sub):
        e = step * per_step + sub
        c0, c1 = cs_ref[e], cs_ref[e + 1]

        @pl.when(c1 > c0)
        def _():
            wgb[...] = wg_ref[sub].astype(BF16)
            wub[...] = wu_ref[sub].astype(BF16)
            wdb[...] = wd_ref[sub].astype(BF16)
            n_pairs = (c1 - c0) // 2

            def pair(i, carry):
                c = c0 + 2 * i
                arrive(c, 2)
                compute(c, 2, cnt_ref[e] - (c - c0) * ch)
                depart(c, 2)
                return carry

            lax.fori_loop(0, n_pairs, pair, 0)

            @pl.when(c0 + 2 * n_pairs < c1)
            def _():
                arrive(c1 - 1, 1)
                compute(c1 - 1, 1, cnt_ref[e] - (c1 - 1 - c0) * ch)
                depart(c1 - 1, 1)

    for sub in range(per_step):
        expert(sub)

    @pl.when(step == n_steps - 1)
    def _():
        for back in range(n_slots, 0, -1):
            @pl.when(c_end >= back)
            def _():
                drain(c_end - back, lax.rem(c_end - back, n_slots)).wait()


def _moe_call(chunk_start, counts, xs, w_gate, w_up, w_down):
    n_rows = xs.shape[1]
    n_exp, dm, de = w_gate.shape
    grid_spec = pltpu.PrefetchScalarGridSpec(
        num_scalar_prefetch=2,
        grid=(n_exp // MOE_EXPERTS_PER_STEP,),
        in_specs=[
            pl.BlockSpec(memory_space=pl.ANY),
            pl.BlockSpec((MOE_EXPERTS_PER_STEP, dm, de), lambda s, cs, cnt: (s, 0, 0)),
            pl.BlockSpec((MOE_EXPERTS_PER_STEP, dm, de), lambda s, cs, cnt: (s, 0, 0)),
            pl.BlockSpec((MOE_EXPERTS_PER_STEP, de, dm), lambda s, cs, cnt: (s, 0, 0)),
        ],
        out_specs=pl.BlockSpec(memory_space=pl.ANY),
        scratch_shapes=[
            pltpu.VMEM((MOE_SLOTS, 2, MOE_BLOCK, dm // 4), jnp.uint32),
            pltpu.VMEM((MOE_SLOTS, 2, MOE_BLOCK, dm // 4), jnp.uint32),
            pltpu.VMEM((dm, de), BF16), pltpu.VMEM((dm, de), BF16), pltpu.VMEM((de, dm), BF16),
            pltpu.SemaphoreType.DMA((MOE_SLOTS,)), pltpu.SemaphoreType.DMA((MOE_SLOTS,)),
        ],
    )
    return pl.pallas_call(
        _moe_kernel,
        grid_spec=grid_spec,
        out_shape=jax.ShapeDtypeStruct(xs.shape, jnp.uint32),
        compiler_params=pltpu.CompilerParams(
            dimension_semantics=("arbitrary",), vmem_limit_bytes=VMEM_LIMIT),
        name="moe",
    )(chunk_start, counts, xs, w_gate, w_up, w_down)


def _final_kernel(x1_ref, h2_ref, r_ref, w_ref, mod_ref, g_ref, wsg_ref, wsu_ref, wsd_ref, o_ref):
    w = w_ref[...]
    groups = [None] * 4
    hparts = [None] * 4
    for h in range(2):
        hparts[h], hparts[2 + h] = _unpack_plane(h2_ref[h])
        for k in range(r_ref.shape[1]):
            lo, hi = _unpack_plane(r_ref[h, k])
            wk = w[:, k:k + 1]
            groups[h] = wk * lo if groups[h] is None else groups[h] + wk * lo
            groups[2 + h] = wk * hi if groups[2 + h] is None else groups[2 + h] + wk * hi
    routed = jnp.concatenate(groups, axis=1)
    hb = jnp.concatenate(hparts, axis=1).astype(BF16)
    sh = _silu(_dot(hb, wsg_ref[...])) * _dot(hb, wsu_ref[...])
    shared = _dot(sh.astype(BF16), wsd_ref[...])
    x = x1_ref[...] + mod_ref[5:6, :] * (routed + shared)
    ms = jnp.mean(x * x, axis=-1, keepdims=True)
    o_ref[...] = x * lax.rsqrt(ms + NORM_EPS) * g_ref[...]


def _final_call(x1, h2, rows, w, mod3, final_g, wsg, wsu, wsd, tm=256):
    bsz, seq, dm = x1.shape
    kk, q = rows.shape[1], rows.shape[3]
    nt = seq // tm
    row = lambda b, i: (b, i, 0)
    full = lambda a: pl.BlockSpec(a.shape, lambda b, i: (0,) * a.ndim)
    return pl.pallas_call(
        _final_kernel,
        grid=(bsz, nt),
        in_specs=[
            pl.BlockSpec((None, tm, dm), row),
            pl.BlockSpec((2, None, tm, q), lambda b, i: (0, b, i, 0)),
            pl.BlockSpec((2, kk, tm, q), lambda b, i: (0, 0, b * nt + i, 0)),
            pl.BlockSpec((None, tm, kk), row),
            pl.BlockSpec((None, N_ADA, dm), lambda b, i: (b, 0, 0)),
            pl.BlockSpec((1, dm), lambda b, i: (0, 0)),
            full(wsg), full(wsu), full(wsd),
        ],
        out_specs=pl.BlockSpec((None, tm, dm), row),
        out_shape=jax.ShapeDtypeStruct((bsz, seq, dm), F32),
        compiler_params=pltpu.CompilerParams(
            dimension_semantics=("parallel", "parallel"), vmem_limit_bytes=VMEM_LIMIT),
        name="final",
    )(x1, h2, rows, w, mod3, final_g.reshape(1, dm), wsg, wsu, wsd)


def _route_kernel(sc_ref, bias_ref, idx_ref, w_ref, rank_ref, cnt_ref, tri_sc, carry_sc):
    ne, tn = sc_ref.shape
    gsz = ne // N_EXPERT_GROUPS
    neg = -jnp.inf
    first = jnp.logical_and(pl.program_id(0) == 0, pl.program_id(1) == 0)

    @pl.when(first)
    def _():
        r = lax.broadcasted_iota(jnp.int32, (tn, tn), 0)
        c = lax.broadcasted_iota(jnp.int32, (tn, tn), 1)
        tri_sc[...] = jnp.where(r < c, 1.0, 0.0).astype(BF16)
        carry_sc[...] = jnp.zeros(carry_sc.shape, F32)

    s = sc_ref[...]
    sel = s + bias_ref[...]
    gs = []
    for g in range(N_EXPERT_GROUPS):
        blk = sel[g * gsz:(g + 1) * gsz]
        m1 = jnp.max(blk, axis=0, keepdims=True)
        eq = blk == m1
        n_eq = jnp.sum(jnp.where(eq, 1.0, 0.0), axis=0, keepdims=True)
        m2 = jnp.max(jnp.where(eq, neg, blk), axis=0, keepdims=True)
        gs.append(m1 + jnp.where(n_eq >= 2.0, m1, m2))
    gs = jnp.concatenate(gs, axis=0)
    gi = lax.broadcasted_iota(jnp.int32, gs.shape, 0)
    beaten = jnp.zeros(gs.shape, F32)
    for gp in range(N_EXPERT_GROUPS):
        other = gs[gp:gp + 1]
        wins = jnp.where(other > gs, 1.0, jnp.where(other == gs, jnp.where(gi > gp, 1.0, 0.0), 0.0))
        beaten = beaten + wins
    gadd = jnp.where(beaten < float(TOPK_GROUPS), 0.0, neg)
    cur = jnp.concatenate(
        [sel[g * gsz:(g + 1) * gsz] + gadd[g:g + 1] for g in range(N_EXPERT_GROUPS)], axis=0)

    eidx = lax.broadcasted_iota(jnp.int32, (ne, tn), 0).astype(F32)
    cur0 = cur
    idx_rows, s_rows = [], []
    for _ in range(TOP_K):
        m = jnp.max(cur, axis=0, keepdims=True)
        ik = jnp.min(jnp.where(cur == m, eidx, float(ne)), axis=0, keepdims=True)
        oh = eidx == ik
        s_rows.append(jnp.sum(jnp.where(oh, s, 0.0), axis=0, keepdims=True))
        cur = jnp.where(oh, neg, cur)
        idx_rows.append(ik)
    chosen = jnp.where(cur == neg, jnp.where(cur0 > neg, 1.0, 0.0), 0.0)
    sk = jnp.concatenate(s_rows, axis=0)
    w_ref[...] = sk / jnp.sum(sk, axis=0, keepdims=True) * ROUTED_SCALE
    idx_ref[...] = jnp.concatenate(idx_rows, axis=0).astype(jnp.int32)

    before = _dot(chosen.astype(BF16), tri_sc[...]) + carry_sc[...]
    rank_rows = [jnp.sum(jnp.where(eidx == ik, before, 0.0), axis=0, keepdims=True) for ik in idx_rows]
    rank_ref[...] = jnp.concatenate(rank_rows, axis=0).astype(jnp.int32)
    carry_sc[...] = carry_sc[...] + jnp.sum(chosen, axis=1, keepdims=True)
    cnt_ref[...] = carry_sc[...]


def _route_call(scores_t, router_bias, tn=512):
    bsz, ne, seq = scores_t.shape
    tok = lambda b, i: (b, 0, i)
    return pl.pallas_call(
        _route_kernel,
        grid=(bsz, seq // tn),
        in_specs=[
            pl.BlockSpec((None, ne, tn), tok),
            pl.BlockSpec((ne, 1), lambda b, i: (0, 0)),
        ],
        out_specs=[
            pl.BlockSpec((None, TOP_K, tn), tok),
            pl.BlockSpec((None, TOP_K, tn), tok),
            pl.BlockSpec((None, TOP_K, tn), tok),
            pl.BlockSpec((ne, 1), lambda b, i: (0, 0)),
        ],
        out_shape=[
            jax.ShapeDtypeStruct((bsz, TOP_K, seq), jnp.int32),
            jax.ShapeDtypeStruct((bsz, TOP_K, seq), F32),
            jax.ShapeDtypeStruct((bsz, TOP_K, seq), jnp.int32),
            jax.ShapeDtypeStruct((ne, 1), F32),
        ],
        scratch_shapes=[pltpu.VMEM((tn, tn), BF16), pltpu.VMEM((ne, 1), F32)],
        compiler_params=pltpu.CompilerParams(
            dimension_semantics=("arbitrary", "arbitrary"), vmem_limit_bytes=VMEM_LIMIT),
        name="route",
    )(scores_t, router_bias.reshape(ne, 1))


def _dispatch_tables(idx, rank, counts):
    ne = counts.shape[0]
    n_tok = idx.shape[0] * idx.shape[2]
    counts = counts.astype(jnp.int32)
    pcounts = (counts + MOE_BLOCK - 1) // MOE_BLOCK * MOE_BLOCK
    pstarts = jnp.concatenate([jnp.zeros((1,), jnp.int32), jnp.cumsum(pcounts).astype(jnp.int32)])
    n_blocks = -(-(n_tok * TOP_K + ne * (MOE_BLOCK - 1)) // MOE_BLOCK)
    dest = _dest_call(pstarts[:ne], idx, rank)
    return dest, pstarts // MOE_BLOCK, counts, n_blocks


def _dest_kernel(ps_ref, idx_ref, rank_ref, o_ref):
    idx = idx_ref[...]

    def body(e, acc):
        return jnp.where(idx == e, ps_ref[e], acc)

    o_ref[...] = rank_ref[...] + lax.fori_loop(0, ps_ref.shape[0], body, jnp.zeros(idx.shape, jnp.int32))


def _dest_call(pstarts, idx, rank, tn=2048):
    bsz, kk, seq = idx.shape
    tok = lambda b, i, ps: (b, 0, i)
    grid_spec = pltpu.PrefetchScalarGridSpec(
        num_scalar_prefetch=1,
        grid=(bsz, seq // tn),
        in_specs=[pl.BlockSpec((None, kk, tn), tok), pl.BlockSpec((None, kk, tn), tok)],
        out_specs=pl.BlockSpec((None, kk, tn), tok),
    )
    return pl.pallas_call(
        _dest_kernel,
        grid_spec=grid_spec,
        out_shape=jax.ShapeDtypeStruct(idx.shape, jnp.int32),
        compiler_params=pltpu.CompilerParams(dimension_semantics=("parallel", "parallel")),
        name="dest",
    )(pstarts, idx, rank)


def _scatter_rows(rows, dest, n_out):
    n_tok, width = rows.shape
    k_slots = dest.shape[0]
    win = LANES
    mesh = plsc.VectorSubcoreMesh(core_axis_name="c", subcore_axis_name="s")

    @pl.kernel(out_type=jax.ShapeDtypeStruct((n_out, width), rows.dtype), mesh=mesh, scratch_types=[])
    def scatter(rows_hbm, dest_hbm, out_hbm):
        def body(rows_vmem, idx_vmem):
            pltpu.sync_copy(rows_vmem, out_hbm.at[idx_vmem.at[0]])

        pltpu.emit_pipeline(
            body,
            grid=(n_tok // win, k_slots),
            in_specs=[
                pl.BlockSpec((win, width), lambda i, k: (i, 0)),
                pl.BlockSpec((1, win), lambda i, k: (k, i)),
            ],
            out_specs=[],
            core_axis_name=("c", "s"),
            dimension_semantics=(pltpu.PARALLEL, pltpu.ARBITRARY),
        )(rows_hbm, dest_hbm)

    return scatter(rows, dest)


def _gather_rows(rows, idx):
    n = idx.shape[0]
    width = rows.shape[1]
    win = LANES
    mesh = plsc.VectorSubcoreMesh(core_axis_name="c", subcore_axis_name="s")

    @pl.kernel(out_type=jax.ShapeDtypeStruct((n, width), rows.dtype), mesh=mesh, scratch_types=[])
    def gather(rows_hbm, idx_hbm, out_hbm):
        def body(idx_vmem, out_vmem):
            pltpu.sync_copy(rows_hbm.at[idx_vmem.at[0]], out_vmem)

        pltpu.emit_pipeline(
            body,
            grid=(n // win,),
            in_specs=[pl.BlockSpec((1, win), lambda i: (0, i))],
            out_specs=[pl.BlockSpec((win, width), lambda i: (i, 0))],
            core_axis_name=("c", "s"),
            dimension_semantics=(pltpu.PARALLEL,),
        )(idx_hbm, out_hbm)

    return gather(rows, idx.reshape(1, n))


def kernel(x, c, w_ada, b_ada, norm1_g, w_in, w_out, lambda_q1, lambda_k1, lambda_q2, lambda_k2, subln_g,
           ssm_a_re, ssm_a_im, ssm_log_step, ssm_b_re, ssm_b_im, ssm_c_re, ssm_c_im, ssm_d, w_glu, b_glu,
           norm2_g, w_router, router_bias, w_gate, w_up, w_down, ws_gate, ws_up, ws_down, final_g):
    bsz, seq, dm = x.shape
    n_tok = bsz * seq
    aw = N_ATTN_HEADS * V_HEAD_DIM

    mod3 = _mod_call(c, w_ada[0], b_ada[0]).reshape(bsz, N_ADA, dm)

    wqkv = w_in[0][:, :3 * aw].astype(BF16)
    wut = w_in[0][:, 3 * aw:].T.astype(BF16)
    q, k, v, ut = _inproj_call(x, mod3, norm1_g[0], wqkv, wut)

    lam = (jnp.exp(jnp.sum(lambda_q1[0] * lambda_k1[0])) - jnp.exp(jnp.sum(lambda_q2[0] * lambda_k2[0]))
           + LAM_INIT).reshape(1)
    attn = _attn_call(lam, q, k, v, subln_g[0])

    yt = _ssm_call(ut, *_ssm_operators(ssm_a_re[0], ssm_a_im[0], ssm_log_step[0], ssm_b_re[0], ssm_b_im[0],
                                       ssm_c_re[0], ssm_c_im[0]))

    x1, h2, scores_t = _mid_call(
        x, attn, yt, ut, mod3, ssm_d[0], w_glu[0].T.astype(BF16), b_glu[0],
        w_out[0][:aw].astype(BF16), w_out[0][aw:].astype(BF16), norm2_g[0], w_router[0].T)

    idx, w, rank, counts = _route_call(scores_t, router_bias[0])
    dest, chunk_start, counts, n_blocks = _dispatch_tables(idx, rank, counts.reshape(-1))
    n_rows = n_blocks * MOE_BLOCK

    dest_k = jnp.swapaxes(dest, 0, 1).reshape(TOP_K, n_tok)
    dest_half = jnp.concatenate([dest_k, dest_k + n_rows], axis=1)
    xs = _scatter_rows(h2.reshape(2 * n_tok, dm // 4), dest_half, 2 * n_rows).reshape(2, n_rows, dm // 4)

    out = _moe_call(chunk_start, counts, xs, w_gate[0], w_up[0], w_down[0])
    src = jnp.concatenate([dest_k.reshape(-1), (dest_k + n_rows).reshape(-1)])
    rows = _gather_rows(out.reshape(2 * n_rows, dm // 4), src)
    return _final_call(x1, h2, rows.reshape(2, TOP_K, n_tok, dm // 4), jnp.swapaxes(w, 1, 2), mod3, final_g,
                       ws_gate[0].astype(BF16), ws_up[0].astype(BF16), ws_down[0].astype(BF16))
```

```python
import functools
import math

import jax
import jax.numpy as jnp
from jax import lax
from jax.experimental import pallas as pl
from jax.experimental.pallas import tpu as pltpu
from jax.experimental.pallas import tpu_sc as plsc

F32 = jnp.float32
BF16 = jnp.bfloat16

N_ATTN_HEADS = 4
ATTN_HEAD_DIM = 64
V_HEAD_DIM = 128
SSM_GROUP = 16
N_SSM_GROUPS = 32
SSM_STATE = 64
N_EXPERTS = 256
TOP_K = 8
N_EXPERT_GROUPS = 8
TOPK_GROUPS = 4
ROUTED_SCALE = 2.5
NORM_EPS = 1e-6
SUBLN_EPS = 1e-5
N_ADA = 6
LAM_INIT = 0.8 - 0.6 * math.exp(-0.3 * 0)

LANES = 128
MOE_BLOCK = 256
MOE_SLOTS = 6
MOE_EXPERTS_PER_STEP = 2
MOE_COPY_PARTS = 4
NEG_BIG = -1e30
LOG2_E = math.log2(math.e)
VMEM_LIMIT = 48 * 1024 * 1024


def _split_bf16(a):
    hi = a.astype(BF16)
    lo = (a - hi.astype(F32)).astype(BF16)
    return hi, lo


def _dot(a, b):
    return jnp.dot(a, b, preferred_element_type=F32)


def _dot_nt(a, b):
    return lax.dot_general(a, b, (((1,), (1,)), ((), ())), preferred_element_type=F32)


def _dot3(a, b):
    ah, al = _split_bf16(a)
    bh, bl = _split_bf16(b)
    return _dot(ah, bh) + _dot(ah, bl) + _dot(al, bh)


def _dot3_nt(a, b):
    ah, al = _split_bf16(a)
    bh, bl = _split_bf16(b)
    return _dot_nt(ah, bh) + _dot_nt(ah, bl) + _dot_nt(al, bh)


def _silu(x):
    return x * jax.nn.sigmoid(x)


def _gelu_tanh(x):
    c = math.sqrt(2.0 / math.pi)
    return 0.5 * x * (1.0 + jnp.tanh(c * (x + 0.044715 * (x * x * x))))


def _pack_rows(x):
    bits = lax.bitcast_convert_type(x.astype(BF16).astype(F32), jnp.uint32)
    half = bits.shape[1] // 2
    packed = (bits[:, :half] >> 16) | (bits[:, half:] & jnp.uint32(0xFFFF0000))
    return packed[:, :half // 2], packed[:, half // 2:]


def _unpack_plane(xu):
    return (lax.bitcast_convert_type(xu << 16, F32),
            lax.bitcast_convert_type(xu & jnp.uint32(0xFFFF0000), F32))


def _mod_kernel(c_ref, w_ref, b_ref, o_ref):
    cond = _silu(c_ref[...])
    o_ref[...] = _dot3(cond, w_ref[...]) + b_ref[...]


def _mod_call(c, w_ada, b_ada):
    bsz, dm = c.shape
    n_out = w_ada.shape[1]
    tn = 1024
    return pl.pallas_call(
        _mod_kernel,
        grid=(n_out // tn,),
        in_specs=[
            pl.BlockSpec((bsz, dm), lambda j: (0, 0)),
            pl.BlockSpec((dm, tn), lambda j: (0, j)),
            pl.BlockSpec((1, tn), lambda j: (0, j)),
        ],
        out_specs=pl.BlockSpec((bsz, tn), lambda j: (0, j)),
        out_shape=jax.ShapeDtypeStruct((bsz, n_out), F32),
        compiler_params=pltpu.CompilerParams(vmem_limit_bytes=VMEM_LIMIT),
        name="mod",
    )(c, w_ada, b_ada.reshape(1, n_out))


def _inproj_kernel(x_ref, mod_ref, g_ref, wqkv_ref, wut_ref, q_ref, k_ref, v_ref, ut_ref, *, aw):
    x = x_ref[...]
    ms = jnp.mean(x * x, axis=-1, keepdims=True)
    h = x * lax.rsqrt(ms + NORM_EPS) * g_ref[...]
    h = h * (1.0 + mod_ref[1:2, :]) + mod_ref[0:1, :]
    hb = h.astype(BF16)
    qkv = _dot(hb, wqkv_ref[...])
    q_ref[...] = (qkv[:, :aw] * (LOG2_E * ATTN_HEAD_DIM ** -0.5)).astype(BF16)
    k_ref[...] = qkv[:, aw:2 * aw].astype(BF16)
    v_ref[...] = qkv[:, 2 * aw:].astype(BF16)
    ut_ref[...] = _dot_nt(wut_ref[...], hb)


def _inproj_call(x, mod3, norm_g, wqkv, wut, tm=512):
    bsz, seq, dm = x.shape
    aw = wqkv.shape[1] // 3
    sw = wut.shape[0]
    row = lambda b, i: (b, i, 0)
    return pl.pallas_call(
        functools.partial(_inproj_kernel, aw=aw),
        grid=(bsz, seq // tm),
        in_specs=[
            pl.BlockSpec((None, tm, dm), row),
            pl.BlockSpec((None, N_ADA, dm), lambda b, i: (b, 0, 0)),
            pl.BlockSpec((1, dm), lambda b, i: (0, 0)),
            pl.BlockSpec(wqkv.shape, lambda b, i: (0, 0)),
            pl.BlockSpec(wut.shape, lambda b, i: (0, 0)),
        ],
        out_specs=[
            pl.BlockSpec((None, tm, aw), row),
            pl.BlockSpec((None, tm, aw), row),
            pl.BlockSpec((None, tm, aw), row),
            pl.BlockSpec((None, sw, tm), lambda b, i: (b, 0, i)),
        ],
        out_shape=[
            jax.ShapeDtypeStruct((bsz, seq, aw), BF16),
            jax.ShapeDtypeStruct((bsz, seq, aw), BF16),
            jax.ShapeDtypeStruct((bsz, seq, aw), BF16),
            jax.ShapeDtypeStruct((bsz, sw, seq), F32),
        ],
        compiler_params=pltpu.CompilerParams(
            dimension_semantics=("parallel", "parallel"), vmem_limit_bytes=VMEM_LIMIT),
        name="inproj",
    )(x, mod3, norm_g.reshape(1, dm), wqkv, wut)


def _attn_kernel(lam_ref, q_ref, k_ref, v_ref, g_ref, o_ref, m_sc, l_sc, acc_sc, *, tq):
    i = pl.program_id(2)
    th = tq // 2
    q = q_ref[...]
    lane = lax.broadcasted_iota(jnp.int32, q.shape, 1)
    zero = jnp.zeros_like(q)
    qa = jnp.where(lane < ATTN_HEAD_DIM, q, zero)
    qb = jnp.where(lane >= ATTN_HEAD_DIM, q, zero)
    q2 = jnp.concatenate([qa[:th], qb[:th], qa[th:], qb[th:]], axis=0)
    m_sc[...] = jnp.full(m_sc.shape, NEG_BIG, F32)
    l_sc[...] = jnp.zeros(l_sc.shape, F32)
    acc_sc[...] = jnp.zeros(acc_sc.shape, F32)

    def step(j, r0, nr, nc, qpos0):
        start = pl.multiple_of(j * tq, tq)
        kt = k_ref[pl.ds(start, nc), :]
        vt = v_ref[pl.ds(start, nc), :]
        s = _dot_nt(q2[r0:r0 + nr], kt)
        if qpos0 is not None:
            r = lax.broadcasted_iota(jnp.int32, s.shape, 0)
            c = lax.broadcasted_iota(jnp.int32, s.shape, 1)
            s = jnp.where(c <= qpos0 + (r & (th - 1)), s, NEG_BIG)
        m_prev = m_sc[r0:r0 + nr, :]
        m_new = jnp.maximum(m_prev, jnp.max(s, axis=-1, keepdims=True))
        alpha = jnp.exp2(m_prev - m_new)
        p = jnp.exp2(s - jnp.concatenate([m_new] * (nc // LANES), axis=1))
        psum = p[:, :LANES]
        for c0 in range(LANES, nc, LANES):
            psum = psum + p[:, c0:c0 + LANES]
        l_sc[r0:r0 + nr, :] = alpha * l_sc[r0:r0 + nr, :] + psum
        acc_sc[r0:r0 + nr, :] = alpha * acc_sc[r0:r0 + nr, :] + _dot(p.astype(BF16), vt)
        m_sc[r0:r0 + nr, :] = m_new

    def body(jj, carry):
        step(2 * jj, 0, 2 * tq, tq, None)
        step(2 * jj + 1, 0, 2 * tq, tq, None)
        return carry

    lax.fori_loop(0, i // 2, body, 0)

    @pl.when(i % 2 == 1)
    def _():
        step(i - 1, 0, 2 * tq, tq, None)

    step(i, 0, tq, th, 0)
    step(i, tq, tq, tq, th)

    o_all = acc_sc[...] / jnp.sum(l_sc[...], axis=-1, keepdims=True)
    o0 = jnp.concatenate([o_all[:th], o_all[tq:tq + th]], axis=0)
    o1 = jnp.concatenate([o_all[th:tq], o_all[tq + th:]], axis=0)
    o = o0 - lam_ref[0] * o1
    ms = jnp.mean(o * o, axis=-1, keepdims=True)
    o = o * lax.rsqrt(ms + SUBLN_EPS) * g_ref[...] * (1.0 - LAM_INIT)
    o_ref[...] = o.astype(o_ref.dtype)


def _attn_call(lam, q, k, v, subln_g, tq=1024):
    bsz, seq, aw = q.shape
    nh = aw // V_HEAD_DIM
    qmap = lambda b, h, i: (b, i, h)
    kvmap = lambda b, h, i: (b, 0, h)
    return pl.pallas_call(
        functools.partial(_attn_kernel, tq=tq),
        grid=(bsz, nh, seq // tq),
        in_specs=[
            pl.BlockSpec(memory_space=pltpu.SMEM),
            pl.BlockSpec((None, tq, V_HEAD_DIM), qmap),
            pl.BlockSpec((None, seq, V_HEAD_DIM), kvmap),
            pl.BlockSpec((None, seq, V_HEAD_DIM), kvmap),
            pl.BlockSpec((1, V_HEAD_DIM), lambda b, h, i: (0, 0)),
        ],
        out_specs=pl.BlockSpec((None, tq, V_HEAD_DIM), qmap),
        out_shape=jax.ShapeDtypeStruct((bsz, seq, aw), BF16),
        scratch_shapes=[pltpu.VMEM((2 * tq, V_HEAD_DIM), F32)] * 3,
        compiler_params=pltpu.CompilerParams(
            dimension_semantics=("parallel", "parallel", "parallel"), vmem_limit_bytes=VMEM_LIMIT),
        name="attn",
    )(lam, q, k, v, subln_g.reshape(1, V_HEAD_DIM))


def _ssm_operators(a_re, a_im, log_step, b_re, b_im, c_re, c_im):
    t = LANES
    hi = lax.Precision.HIGHEST
    lam = lax.complex(jnp.minimum(a_re, -1e-4), a_im)
    delta = jnp.exp(log_step)[:, None]
    lam_bar = jnp.exp(lam * delta)
    bbar = ((lam_bar - 1.0) / lam)[:, :, None] * lax.complex(b_re, b_im)
    cmat = lax.complex(c_re, c_im)
    ld = lam * delta
    tau = jnp.arange(t + 1, dtype=F32)
    pw = jnp.exp(ld[:, :, None] * tau)
    cb = cmat[:, None, :, :] * jnp.swapaxes(bbar, 1, 2)[:, :, None, :]
    g = a_re.shape[0]
    cb = cb.reshape(g, SSM_GROUP * SSM_GROUP, SSM_STATE)
    cbcat = jnp.concatenate([cb.real, -cb.imag], axis=-1)
    pcat = jnp.concatenate([pw.real[:, :, :t], pw.imag[:, :, :t]], axis=1)
    kmat = jnp.einsum('gxp,gpt->gxt', cbcat, pcat, precision=hi)
    prev = jnp.swapaxes(pw[:, :, t - 1::-1], 1, 2)
    arev = jnp.concatenate([prev.real, prev.imag], axis=-1)
    bt = jnp.swapaxes(bbar, 1, 2)
    brow = jnp.stack([jnp.concatenate([bt.real, bt.real], axis=-1),
                      jnp.concatenate([-bt.imag, bt.imag], axis=-1)], axis=2).reshape(g, 2 * SSM_GROUP, 2 * SSM_STATE)
    a1 = jnp.concatenate([pw.real[:, :, 1:], pw.imag[:, :, 1:]], axis=1)
    ct = jnp.swapaxes(cmat, 1, 2)
    ccol = jnp.stack([jnp.concatenate([ct.real, -ct.real], axis=1),
                      jnp.concatenate([-ct.imag, -ct.imag], axis=1)], axis=-1).reshape(g, 2 * SSM_STATE, 2 * SSM_GROUP)
    rows = []
    for i in range(6):
        d = jnp.exp(ld * float(t * (1 << i)))
        rows.append(jnp.concatenate([d.real, d.real], axis=-1))
        rows.append(jnp.concatenate([-d.imag, d.imag], axis=-1))
    rows += [jnp.zeros_like(rows[0])] * 4
    dpow = jnp.stack(rows, axis=1)
    return kmat, arev, brow, a1, ccol, dpow


def _ssm_kernel(u_ref, k_ref, arev_ref, brow_ref, a1_ref, ccol_ref, dp_ref, y_ref, m_sc, ws_sc, wc_sc):
    bsz, nch, n_chunk, t = u_ref.shape
    row = lax.broadcasted_iota(jnp.int32, (t, t), 0)
    col = lax.broadcasted_iota(jnp.int32, (t, t), 1)
    causal = col >= row
    half = arev_ref.shape[1] // 2

    arev = arev_ref[...]
    arev_sw = pltpu.roll(arev, half, 1)
    a1 = a1_ref[...]
    a1_sw = pltpu.roll(a1, half, 0)
    for c in range(nch):
        ws_sc[c * t:(c + 1) * t, :] = (arev * brow_ref[2 * c:2 * c + 1, :]
                                       + arev_sw * brow_ref[2 * c + 1:2 * c + 2, :]).astype(BF16)
        wc_sc[:, c * t:(c + 1) * t] = (a1 * ccol_ref[:, 2 * c:2 * c + 1]
                                       + a1_sw * ccol_ref[:, 2 * c + 1:2 * c + 2]).astype(BF16)

    uflat = jnp.concatenate(
        [jnp.concatenate([u_ref[b, ci] for b in range(bsz)], axis=0) for ci in range(nch)],
        axis=1).astype(BF16)

    y = None
    for c0 in range(0, nch, 2):
        for ci in (c0, c0 + 1):
            for co in range(nch):
                kb = jnp.broadcast_to(k_ref[ci * nch + co:ci * nch + co + 1, :], (t, t))
                kb = pltpu.roll(kb, 0, 1, stride=1, stride_axis=0)
                m_sc[ci * t:(ci + 1) * t, co * t:(co + 1) * t] = jnp.where(causal, kb, 0.0).astype(BF16)
        part = _dot(uflat[:, c0 * t:(c0 + 2) * t], m_sc[c0 * t:(c0 + 2) * t, :])
        y = part if y is None else y + part
    z = _dot(uflat, ws_sc[...])
    kidx = lax.broadcasted_iota(jnp.int32, z.shape, 0) & (n_chunk - 1)
    shift = 1
    i = 0
    while shift < n_chunk:
        zs = jnp.where(kidx >= shift, pltpu.roll(z, shift, 0), 0.0)
        z = z + zs * dp_ref[2 * i:2 * i + 1, :] + pltpu.roll(zs, half, 1) * dp_ref[2 * i + 1:2 * i + 2, :]
        shift *= 2
        i += 1
    xin = jnp.where(kidx >= 1, pltpu.roll(z, 1, 0), 0.0)
    xh, xl = _split_bf16(xin)
    wc = wc_sc[...]
    y = y + _dot(xh, wc) + _dot(xl, wc)
    for b in range(bsz):
        for co in range(nch):
            y_ref[b, co] = y[b * n_chunk:(b + 1) * n_chunk, co * t:(co + 1) * t]


def _ssm_call(ut, kmat, arev, brow, a1, ccol, dpow):
    bsz, sw, seq = ut.shape
    n_groups = sw // SSM_GROUP
    n_chunk = seq // LANES
    assert n_chunk & (n_chunk - 1) == 0 and n_chunk <= 64
    u4 = ut.reshape(bsz, sw, n_chunk, LANES)
    blk = (bsz, SSM_GROUP, n_chunk, LANES)
    gmap = lambda g: (0, g, 0, 0)
    pmap = lambda g: (g, 0, 0)
    y4 = pl.pallas_call(
        _ssm_kernel,
        grid=(n_groups,),
        in_specs=[
            pl.BlockSpec(blk, gmap),
            pl.BlockSpec((None,) + kmat.shape[1:], pmap),
            pl.BlockSpec((None,) + arev.shape[1:], pmap),
            pl.BlockSpec((None,) + brow.shape[1:], pmap),
            pl.BlockSpec((None,) + a1.shape[1:], pmap),
            pl.BlockSpec((None,) + ccol.shape[1:], pmap),
            pl.BlockSpec((None,) + dpow.shape[1:], pmap),
        ],
        out_specs=pl.BlockSpec(blk, gmap),
        out_shape=jax.ShapeDtypeStruct(u4.shape, F32),
        scratch_shapes=[pltpu.VMEM((SSM_GROUP * LANES, SSM_GROUP * LANES), BF16),
                        pltpu.VMEM((SSM_GROUP * LANES, 2 * SSM_STATE), BF16),
                        pltpu.VMEM((2 * SSM_STATE, SSM_GROUP * LANES), BF16)],
        compiler_params=pltpu.CompilerParams(
            dimension_semantics=("parallel",), vmem_limit_bytes=VMEM_LIMIT),
        name="ssm",
    )(u4, kmat, arev, brow, a1, ccol, dpow)
    return y4.reshape(bsz, sw, seq)


def _mid_kernel(x_ref, attn_ref, yt_ref, ut_ref, mod_ref, dsk_ref, wglut_ref, bglu_ref, wo1_ref, wo2_ref,
                g2_ref, wrt_ref, x1_ref, h2_ref, lg_ref):
    gt = _gelu_tanh(yt_ref[...] + dsk_ref[...] * ut_ref[...])
    zt = _dot(wglut_ref[...], gt.astype(BF16)) + bglu_ref[...]
    st = gt * jax.nn.sigmoid(zt)
    s = st.T.astype(BF16)
    mix = _dot(attn_ref[...], wo1_ref[...]) + _dot(s, wo2_ref[...])
    x1 = x_ref[...] + mod_ref[2:3, :] * mix
    ms = jnp.mean(x1 * x1, axis=-1, keepdims=True)
    h2 = x1 * lax.rsqrt(ms + NORM_EPS) * g2_ref[...]
    h2 = h2 * (1.0 + mod_ref[4:5, :]) + mod_ref[3:4, :]
    x1_ref[...] = x1
    h2_ref[0], h2_ref[1] = _pack_rows(h2)
    lg_ref[...] = jax.nn.sigmoid(_dot3_nt(wrt_ref[...], h2))


def _mid_call(x, attn, yt, ut, mod3, dsk, wglut, bglu, wo1, wo2, g2, wrt, tm=512):
    bsz, seq, dm = x.shape
    aw = attn.shape[2]
    sw = yt.shape[1]
    ne = wrt.shape[0]
    row = lambda b, i: (b, i, 0)
    colm = lambda b, i: (b, 0, i)
    full = lambda a: pl.BlockSpec(a.shape, lambda b, i: (0,) * a.ndim)
    dsk = dsk.reshape(sw, 1)
    bglu = bglu.reshape(sw, 1)
    g2 = g2.reshape(1, dm)
    return pl.pallas_call(
        _mid_kernel,
        grid=(bsz, seq // tm),
        in_specs=[
            pl.BlockSpec((None, tm, dm), row),
            pl.BlockSpec((None, tm, aw), row),
            pl.BlockSpec((None, sw, tm), colm),
            pl.BlockSpec((None, sw, tm), colm),
            pl.BlockSpec((None, N_ADA, dm), lambda b, i: (b, 0, 0)),
            full(dsk), full(wglut), full(bglu), full(wo1), full(wo2), full(g2), full(wrt),
        ],
        out_specs=[
            pl.BlockSpec((None, tm, dm), row),
            pl.BlockSpec((2, None, tm, dm // 4), lambda b, i: (0, b, i, 0)),
            pl.BlockSpec((None, ne, tm), colm),
        ],
        out_shape=[
            jax.ShapeDtypeStruct((bsz, seq, dm), F32),
            jax.ShapeDtypeStruct((2, bsz, seq, dm // 4), jnp.uint32),
            jax.ShapeDtypeStruct((bsz, ne, seq), F32),
        ],
        compiler_params=pltpu.CompilerParams(
            dimension_semantics=("parallel", "parallel"), vmem_limit_bytes=VMEM_LIMIT),
        name="mid",
    )(x, attn, yt, ut, mod3, dsk, wglut, bglu, wo1, wo2, g2, wrt)


def _moe_kernel(cs_ref, cnt_ref, nv_ref, xs_hbm, wg_ref, wu_ref, wd_ref, out_hbm, xbuf, obuf, wgb, wub, wdb,
                isem, osem):
    step = pl.program_id(0)
    n_steps = pl.num_programs(0)
    per_step = wg_ref.shape[0]
    c_end = cs_ref[n_steps * per_step]
    n_slots, _, ch, q = xbuf.shape

    n_parts = MOE_COPY_PARTS
    pr = ch // n_parts
    row_queue = 1

    def pieces(c, slot, j, kind):
        rows = pl.ds(j * pr, pr)
        hbm_rows = pl.ds(c * ch + j * pr, pr)
        if kind == "fetch":
            return pltpu.make_async_copy(xs_hbm.at[:, hbm_rows, :], xbuf.at[slot, :, rows, :], isem.at[slot])
        return pltpu.make_async_copy(obuf.at[slot, :, rows, :], out_hbm.at[:, hbm_rows, :], osem.at[slot])

    def copy_op(c, slot, kind, op):
        for j in range(n_parts):
            @pl.when(nv_ref[c] > j * pr)
            def _(j=j):
                cp = pieces(c, slot, j, kind)
                cp.start(priority=row_queue) if op == "start" else cp.wait()

    look = n_slots - 2

    @pl.when(step == 0)
    def _():
        for c in range(look):
            @pl.when(c < c_end)
            def _():
                copy_op(c, c, "fetch", "start")

    def arrive(c, n):
        for i in range(n):
            copy_op(c + i, lax.rem(c + i, n_slots), "fetch", "wait")
        for i in range(n):
            ahead = c + look + i

            @pl.when(ahead < c_end)
            def _():
                copy_op(ahead, lax.rem(ahead, n_slots), "fetch", "start")
        for i in range(n):
            @pl.when(c + i >= n_slots)
            def _():
                copy_op(c + i - n_slots, lax.rem(c + i, n_slots), "drain", "wait")

    def compute(c, n, n_valid):
        slots = [lax.rem(c + i, n_slots) for i in range(n)]
        valid = lax.broadcasted_iota(jnp.int32, (n * ch, q), 0) < n_valid
        gate = up = None
        for h in range(2):
            lo, hi = _unpack_plane(jnp.concatenate([xbuf[s, h] for s in slots], axis=0))
            lo = jnp.where(valid, lo, 0.0).astype(BF16)
            hi = jnp.where(valid, hi, 0.0).astype(BF16)
            for xpart, k0 in ((lo, h * q), (hi, (2 + h) * q)):
                g = _dot(xpart, wgb[k0:k0 + q, :])
                u = _dot(xpart, wub[k0:k0 + q, :])
                gate = g if gate is None else gate + g
                up = u if up is None else up + u
        hb = _silu(gate) * up
        p0, p1 = _pack_rows(_dot(hb.astype(BF16), wdb[...]))
        for i, s in enumerate(slots):
            obuf[s, 0] = p0[i * ch:(i + 1) * ch]
            obuf[s, 1] = p1[i * ch:(i + 1) * ch]

    def depart(c, n):
        for i in range(n):
            copy_op(c + i, lax.rem(c + i, n_slots), "drain", "start")

    def expert(sub):
        e = step * per_step + sub
        c0, c1 = cs_ref[e], cs_ref[e + 1]

        @pl.when(c1 > c0)
        def _():
            wgb[...] = wg_ref[sub].astype(BF16)
            wub[...] = wu_ref[sub].astype(BF16)
            wdb[...] = wd_ref[sub].astype(BF16)
            n_pairs = (c1 - c0) // 2

            def pair(i, carry):
                c = c0 + 2 * i
                arrive(c, 2)
                compute(c, 2, cnt_ref[e] - (c - c0) * ch)
                depart(c, 2)
                return carry

            lax.fori_loop(0, n_pairs, pair, 0)

            @pl.when(c0 + 2 * n_pairs < c1)
            def _():
                arrive(c1 - 1, 1)
                compute(c1 - 1, 1, cnt_ref[e] - (c1 - 1 - c0) * ch)
                depart(c1 - 1, 1)

    for sub in range(per_step):
        expert(sub)

    @pl.when(step == n_steps - 1)
    def _():
        for back in range(n_slots, 0, -1):
            @pl.when(c_end >= back)
            def _():
                copy_op(c_end - back, lax.rem(c_end - back, n_slots), "drain", "wait")


def _moe_call(chunk_start, counts, chunk_valid, xs, w_gate, w_up, w_down):
    n_rows = xs.shape[1]
    n_exp, dm, de = w_gate.shape
    grid_spec = pltpu.PrefetchScalarGridSpec(
        num_scalar_prefetch=3,
        grid=(n_exp // MOE_EXPERTS_PER_STEP,),
        in_specs=[
            pl.BlockSpec(memory_space=pl.ANY),
            pl.BlockSpec((MOE_EXPERTS_PER_STEP, dm, de), lambda s, cs, cnt, nv: (s, 0, 0)),
            pl.BlockSpec((MOE_EXPERTS_PER_STEP, dm, de), lambda s, cs, cnt, nv: (s, 0, 0)),
            pl.BlockSpec((MOE_EXPERTS_PER_STEP, de, dm), lambda s, cs, cnt, nv: (s, 0, 0)),
        ],
        out_specs=pl.BlockSpec(memory_space=pl.ANY),
        scratch_shapes=[
            pltpu.VMEM((MOE_SLOTS, 2, MOE_BLOCK, dm // 4), jnp.uint32),
            pltpu.VMEM((MOE_SLOTS, 2, MOE_BLOCK, dm // 4), jnp.uint32),
            pltpu.VMEM((dm, de), BF16), pltpu.VMEM((dm, de), BF16), pltpu.VMEM((de, dm), BF16),
            pltpu.SemaphoreType.DMA((MOE_SLOTS,)), pltpu.SemaphoreType.DMA((MOE_SLOTS,)),
        ],
    )
    return pl.pallas_call(
        _moe_kernel,
        grid_spec=grid_spec,
        out_shape=jax.ShapeDtypeStruct(xs.shape, jnp.uint32),
        compiler_params=pltpu.CompilerParams(
            dimension_semantics=("arbitrary",), vmem_limit_bytes=VMEM_LIMIT),
        name="moe",
    )(chunk_start, counts, chunk_valid, xs, w_gate, w_up, w_down)


def _final_kernel(x1_ref, h2_ref, r_ref, w_ref, mod_ref, g_ref, wsg_ref, wsu_ref, wsd_ref, o_ref):
    w = w_ref[...]
    groups = [None] * 4
    hparts = [None] * 4
    for h in range(2):
        hparts[h], hparts[2 + h] = _unpack_plane(h2_ref[h])
        for k in range(r_ref.shape[1]):
            lo, hi = _unpack_plane(r_ref[h, k])
            wk = w[:, k:k + 1]
            groups[h] = wk * lo if groups[h] is None else groups[h] + wk * lo
            groups[2 + h] = wk * hi if groups[2 + h] is None else groups[2 + h] + wk * hi
    routed = jnp.concatenate(groups, axis=1)
    hb = jnp.concatenate(hparts, axis=1).astype(BF16)
    sh = _silu(_dot(hb, wsg_ref[...])) * _dot(hb, wsu_ref[...])
    shared = _dot(sh.astype(BF16), wsd_ref[...])
    x = x1_ref[...] + mod_ref[5:6, :] * (routed + shared)
    ms = jnp.mean(x * x, axis=-1, keepdims=True)
    o_ref[...] = x * lax.rsqrt(ms + NORM_EPS) * g_ref[...]


def _final_call(x1, h2, rows, w, mod3, final_g, wsg, wsu, wsd, tm=256):
    bsz, seq, dm = x1.shape
    kk, q = rows.shape[1], rows.shape[3]
    nt = seq // tm
    row = lambda b, i: (b, i, 0)
    full = lambda a: pl.BlockSpec(a.shape, lambda b, i: (0,) * a.ndim)
    return pl.pallas_call(
        _final_kernel,
        grid=(bsz, nt),
        in_specs=[
            pl.BlockSpec((None, tm, dm), row),
            pl.BlockSpec((2, None, tm, q), lambda b, i: (0, b, i, 0)),
            pl.BlockSpec((2, kk, tm, q), lambda b, i: (0, 0, b * nt + i, 0)),
            pl.BlockSpec((None, tm, kk), row),
            pl.BlockSpec((None, N_ADA, dm), lambda b, i: (b, 0, 0)),
            pl.BlockSpec((1, dm), lambda b, i: (0, 0)),
            full(wsg), full(wsu), full(wsd),
        ],
        out_specs=pl.BlockSpec((None, tm, dm), row),
        out_shape=jax.ShapeDtypeStruct((bsz, seq, dm), F32),
        compiler_params=pltpu.CompilerParams(
            dimension_semantics=("parallel", "parallel"), vmem_limit_bytes=VMEM_LIMIT),
        name="final",
    )(x1, h2, rows, w, mod3, final_g.reshape(1, dm), wsg, wsu, wsd)


def _route_kernel(sc_ref, bias_ref, idx_ref, w_ref, rank_ref, cnt_ref, tri_sc, carry_sc):
    ne, tn = sc_ref.shape
    gsz = ne // N_EXPERT_GROUPS
    neg = -jnp.inf
    first = jnp.logical_and(pl.program_id(0) == 0, pl.program_id(1) == 0)

    @pl.when(first)
    def _():
        r = lax.broadcasted_iota(jnp.int32, (tn, tn), 0)
        c = lax.broadcasted_iota(jnp.int32, (tn, tn), 1)
        tri_sc[...] = jnp.where(r < c, 1.0, 0.0).astype(BF16)
        carry_sc[...] = jnp.zeros(carry_sc.shape, F32)

    s = sc_ref[...]
    sel = s + bias_ref[...]
    gs = []
    for g in range(N_EXPERT_GROUPS):
        blk = sel[g * gsz:(g + 1) * gsz]
        m1 = jnp.max(blk, axis=0, keepdims=True)
        eq = blk == m1
        n_eq = jnp.sum(jnp.where(eq, 1.0, 0.0), axis=0, keepdims=True)
        m2 = jnp.max(jnp.where(eq, neg, blk), axis=0, keepdims=True)
        gs.append(m1 + jnp.where(n_eq >= 2.0, m1, m2))
    gs = jnp.concatenate(gs, axis=0)
    gi = lax.broadcasted_iota(jnp.int32, gs.shape, 0)
    beaten = jnp.zeros(gs.shape, F32)
    for gp in range(N_EXPERT_GROUPS):
        other = gs[gp:gp + 1]
        wins = jnp.where(other > gs, 1.0, jnp.where(other == gs, jnp.where(gi > gp, 1.0, 0.0), 0.0))
        beaten = beaten + wins
    gadd = jnp.where(beaten < float(TOPK_GROUPS), 0.0, neg)
    cur = jnp.concatenate(
        [sel[g * gsz:(g + 1) * gsz] + gadd[g:g + 1] for g in range(N_EXPERT_GROUPS)], axis=0)

    eidx = lax.broadcasted_iota(jnp.int32, (ne, tn), 0).astype(F32)
    cur0 = cur
    idx_rows, s_rows = [], []
    for _ in range(TOP_K):
        m = jnp.max(cur, axis=0, keepdims=True)
        ik = jnp.min(jnp.where(cur == m, eidx, float(ne)), axis=0, keepdims=True)
        oh = eidx == ik
        s_rows.append(jnp.sum(jnp.where(oh, s, 0.0), axis=0, keepdims=True))
        cur = jnp.where(oh, neg, cur)
        idx_rows.append(ik)
    chosen = jnp.where(cur == neg, jnp.where(cur0 > neg, 1.0, 0.0), 0.0)
    sk = jnp.concatenate(s_rows, axis=0)
    w_ref[...] = sk / jnp.sum(sk, axis=0, keepdims=True) * ROUTED_SCALE
    idx_ref[...] = jnp.concatenate(idx_rows, axis=0).astype(jnp.int32)

    before = _dot(chosen.astype(BF16), tri_sc[...]) + carry_sc[...]
    rank_rows = [jnp.sum(jnp.where(eidx == ik, before, 0.0), axis=0, keepdims=True) for ik in idx_rows]
    rank_ref[...] = jnp.concatenate(rank_rows, axis=0).astype(jnp.int32)
    carry_sc[...] = carry_sc[...] + jnp.sum(chosen, axis=1, keepdims=True)
    cnt_ref[...] = carry_sc[...]


def _route_call(scores_t, router_bias, tn=512):
    bsz, ne, seq = scores_t.shape
    tok = lambda b, i: (b, 0, i)
    return pl.pallas_call(
        _route_kernel,
        grid=(bsz, seq // tn),
        in_specs=[
            pl.BlockSpec((None, ne, tn), tok),
            pl.BlockSpec((ne, 1), lambda b, i: (0, 0)),
        ],
        out_specs=[
            pl.BlockSpec((None, TOP_K, tn), tok),
            pl.BlockSpec((None, TOP_K, tn), tok),
            pl.BlockSpec((None, TOP_K, tn), tok),
            pl.BlockSpec((ne, 1), lambda b, i: (0, 0)),
        ],
        out_shape=[
            jax.ShapeDtypeStruct((bsz, TOP_K, seq), jnp.int32),
            jax.ShapeDtypeStruct((bsz, TOP_K, seq), F32),
            jax.ShapeDtypeStruct((bsz, TOP_K, seq), jnp.int32),
            jax.ShapeDtypeStruct((ne, 1), F32),
        ],
        scratch_shapes=[pltpu.VMEM((tn, tn), BF16), pltpu.VMEM((ne, 1), F32)],
        compiler_params=pltpu.CompilerParams(
            dimension_semantics=("arbitrary", "arbitrary"), vmem_limit_bytes=VMEM_LIMIT),
        name="route",
    )(scores_t, router_bias.reshape(ne, 1))


def _dispatch_tables(idx, rank, counts):
    ne = counts.shape[0]
    n_tok = idx.shape[0] * idx.shape[2]
    counts = counts.astype(jnp.int32)
    pcounts = (counts + MOE_BLOCK - 1) // MOE_BLOCK * MOE_BLOCK
    pstarts = jnp.concatenate([jnp.zeros((1,), jnp.int32), jnp.cumsum(pcounts).astype(jnp.int32)])
    n_blocks = -(-(n_tok * TOP_K + ne * (MOE_BLOCK - 1)) // MOE_BLOCK)
    dest = _dest_call(pstarts[:ne], idx, rank)
    chunk_row = jnp.arange(n_blocks, dtype=jnp.int32)[:, None] * MOE_BLOCK
    owner = jnp.logical_and(pstarts[None, :ne] <= chunk_row, chunk_row < pstarts[None, 1:])
    left = jnp.sum(jnp.where(owner, counts[None, :] - (chunk_row - pstarts[None, :ne]), 0), axis=1)
    chunk_valid = jnp.clip(left, 0, MOE_BLOCK).astype(jnp.int32)
    return dest, pstarts // MOE_BLOCK, counts, chunk_valid, n_blocks


def _dest_kernel(ps_ref, idx_ref, rank_ref, o_ref):
    idx = idx_ref[...]

    def body(e, acc):
        return jnp.where(idx == e, ps_ref[e], acc)

    o_ref[...] = rank_ref[...] + lax.fori_loop(0, ps_ref.shape[0], body, jnp.zeros(idx.shape, jnp.int32))


def _dest_call(pstarts, idx, rank, tn=2048):
    bsz, kk, seq = idx.shape
    tok = lambda b, i, ps: (b, 0, i)
    grid_spec = pltpu.PrefetchScalarGridSpec(
        num_scalar_prefetch=1,
        grid=(bsz, seq // tn),
        in_specs=[pl.BlockSpec((None, kk, tn), tok), pl.BlockSpec((None, kk, tn), tok)],
        out_specs=pl.BlockSpec((None, kk, tn), tok),
    )
    return pl.pallas_call(
        _dest_kernel,
        grid_spec=grid_spec,
        out_shape=jax.ShapeDtypeStruct(idx.shape, jnp.int32),
        compiler_params=pltpu.CompilerParams(dimension_semantics=("parallel", "parallel")),
        name="dest",
    )(pstarts, idx, rank)


def _scatter_rows(rows, dest, n_out):
    n_tok, width = rows.shape
    k_slots = dest.shape[0]
    win = LANES
    mesh = plsc.VectorSubcoreMesh(core_axis_name="c", subcore_axis_name="s")

    @pl.kernel(out_type=jax.ShapeDtypeStruct((n_out, width), rows.dtype), mesh=mesh, scratch_types=[])
    def scatter(rows_hbm, dest_hbm, out_hbm):
        def body(rows_vmem, idx_vmem):
            pltpu.sync_copy(rows_vmem, out_hbm.at[idx_vmem.at[0]])

        pltpu.emit_pipeline(
            body,
            grid=(n_tok // win, k_slots),
            in_specs=[
                pl.BlockSpec((win, width), lambda i, k: (i, 0)),
                pl.BlockSpec((1, win), lambda i, k: (k, i)),
            ],
            out_specs=[],
            core_axis_name=("c", "s"),
            dimension_semantics=(pltpu.PARALLEL, pltpu.ARBITRARY),
        )(rows_hbm, dest_hbm)

    return scatter(rows, dest)


def _gather_rows(rows, idx):
    n = idx.shape[0]
    width = rows.shape[1]
    win = LANES
    mesh = plsc.VectorSubcoreMesh(core_axis_name="c", subcore_axis_name="s")

    @pl.kernel(out_type=jax.ShapeDtypeStruct((n, width), rows.dtype), mesh=mesh, scratch_types=[])
    def gather(rows_hbm, idx_hbm, out_hbm):
        def body(idx_vmem, out_vmem):
            pltpu.sync_copy(rows_hbm.at[idx_vmem.at[0]], out_vmem)

        pltpu.emit_pipeline(
            body,
            grid=(n // win,),
            in_specs=[pl.BlockSpec((1, win), lambda i: (0, i))],
            out_specs=[pl.BlockSpec((win, width), lambda i: (i, 0))],
            core_axis_name=("c", "s"),
            dimension_semantics=(pltpu.PARALLEL,),
        )(idx_hbm, out_hbm)

    return gather(rows, idx.reshape(1, n))


def kernel(x, c, w_ada, b_ada, norm1_g, w_in, w_out, lambda_q1, lambda_k1, lambda_q2, lambda_k2, subln_g,
           ssm_a_re, ssm_a_im, ssm_log_step, ssm_b_re, ssm_b_im, ssm_c_re, ssm_c_im, ssm_d, w_glu, b_glu,
           norm2_g, w_router, router_bias, w_gate, w_up, w_down, ws_gate, ws_up, ws_down, final_g):
    bsz, seq, dm = x.shape
    n_tok = bsz * seq
    aw = N_ATTN_HEADS * V_HEAD_DIM

    mod3 = _mod_call(c, w_ada[0], b_ada[0]).reshape(bsz, N_ADA, dm)

    wqkv = w_in[0][:, :3 * aw].astype(BF16)
    wut = w_in[0][:, 3 * aw:].T.astype(BF16)
    q, k, v, ut = _inproj_call(x, mod3, norm1_g[0], wqkv, wut)

    lam = (jnp.exp(jnp.sum(lambda_q1[0] * lambda_k1[0])) - jnp.exp(jnp.sum(lambda_q2[0] * lambda_k2[0]))
           + LAM_INIT).reshape(1)
    attn = _attn_call(lam, q, k, v, subln_g[0])

    yt = _ssm_call(ut, *_ssm_operators(ssm_a_re[0], ssm_a_im[0], ssm_log_step[0], ssm_b_re[0], ssm_b_im[0],
                                       ssm_c_re[0], ssm_c_im[0]))

    x1, h2, scores_t = _mid_call(
        x, attn, yt, ut, mod3, ssm_d[0], w_glu[0].T.astype(BF16), b_glu[0],
        w_out[0][:aw].astype(BF16), w_out[0][aw:].astype(BF16), norm2_g[0], w_router[0].T)

    idx, w, rank, counts = _route_call(scores_t, router_bias[0])
    dest, chunk_start, counts, chunk_valid, n_blocks = _dispatch_tables(idx, rank, counts.reshape(-1))
    n_rows = n_blocks * MOE_BLOCK

    dest_k = jnp.swapaxes(dest, 0, 1).reshape(TOP_K, n_tok)
    dest_half = jnp.concatenate([dest_k, dest_k + n_rows], axis=1)
    xs = _scatter_rows(h2.reshape(2 * n_tok, dm // 4), dest_half, 2 * n_rows).reshape(2, n_rows, dm // 4)

    out = _moe_call(chunk_start, counts, chunk_valid, xs, w_gate[0], w_up[0], w_down[0])
    src = jnp.concatenate([dest_k.reshape(-1), (dest_k + n_rows).reshape(-1)])
    rows = _gather_rows(out.reshape(2 * n_rows, dm // 4), src)
    return _final_call(x1, h2, rows.reshape(2, TOP_K, n_tok, dm // 4), jnp.swapaxes(w, 1, 2), mod3, final_g,
                       ws_gate[0].astype(BF16), ws_up[0].astype(BF16), ws_down[0].astype(BF16))
```

```python
import functools
import math

import jax
import jax.numpy as jnp
from jax import lax
from jax.experimental import pallas as pl
from jax.experimental.pallas import tpu as pltpu
from jax.experimental.pallas import tpu_sc as plsc

F32 = jnp.float32
BF16 = jnp.bfloat16

N_ATTN_HEADS = 4
ATTN_HEAD_DIM = 64
V_HEAD_DIM = 128
SSM_GROUP = 16
N_SSM_GROUPS = 32
SSM_STATE = 64
N_EXPERTS = 256
TOP_K = 8
N_EXPERT_GROUPS = 8
TOPK_GROUPS = 4
ROUTED_SCALE = 2.5
NORM_EPS = 1e-6
SUBLN_EPS = 1e-5
N_ADA = 6
LAM_INIT = 0.8 - 0.6 * math.exp(-0.3 * 0)

LANES = 128
MOE_BLOCK = 256
MOE_SLOTS = 6
MOE_EXPERTS_PER_STEP = 2
MOE_COPY_PARTS = 4
NEG_BIG = -1e30
LOG2_E = math.log2(math.e)
VMEM_LIMIT = 48 * 1024 * 1024


def _split_bf16(a):
    hi = a.astype(BF16)
    lo = (a - hi.astype(F32)).astype(BF16)
    return hi, lo


def _dot(a, b):
    return jnp.dot(a, b, preferred_element_type=F32)


def _dot_nt(a, b):
    return lax.dot_general(a, b, (((1,), (1,)), ((), ())), preferred_element_type=F32)


def _dot3(a, b):
    ah, al = _split_bf16(a)
    bh, bl = _split_bf16(b)
    return _dot(ah, bh) + _dot(ah, bl) + _dot(al, bh)


def _dot3_nt(a, b):
    ah, al = _split_bf16(a)
    bh, bl = _split_bf16(b)
    return _dot_nt(ah, bh) + _dot_nt(ah, bl) + _dot_nt(al, bh)


def _silu(x):
    return x * jax.nn.sigmoid(x)


def _gelu_tanh(x):
    c = math.sqrt(2.0 / math.pi)
    return 0.5 * x * (1.0 + jnp.tanh(c * (x + 0.044715 * (x * x * x))))


def _pack_rows(x):
    bits = lax.bitcast_convert_type(x.astype(BF16).astype(F32), jnp.uint32)
    half = bits.shape[1] // 2
    packed = (bits[:, :half] >> 16) | (bits[:, half:] & jnp.uint32(0xFFFF0000))
    return packed[:, :half // 2], packed[:, half // 2:]


def _unpack_plane(xu):
    return (lax.bitcast_convert_type(xu << 16, F32),
            lax.bitcast_convert_type(xu & jnp.uint32(0xFFFF0000), F32))


def _mod_kernel(c_ref, w_ref, b_ref, o_ref):
    cond = _silu(c_ref[...])
    o_ref[...] = _dot3(cond, w_ref[...]) + b_ref[...]


def _mod_call(c, w_ada, b_ada):
    bsz, dm = c.shape
    n_out = w_ada.shape[1]
    tn = 1024
    return pl.pallas_call(
        _mod_kernel,
        grid=(n_out // tn,),
        in_specs=[
            pl.BlockSpec((bsz, dm), lambda j: (0, 0)),
            pl.BlockSpec((dm, tn), lambda j: (0, j)),
            pl.BlockSpec((1, tn), lambda j: (0, j)),
        ],
        out_specs=pl.BlockSpec((bsz, tn), lambda j: (0, j)),
        out_shape=jax.ShapeDtypeStruct((bsz, n_out), F32),
        compiler_params=pltpu.CompilerParams(vmem_limit_bytes=VMEM_LIMIT),
        name="mod",
    )(c, w_ada, b_ada.reshape(1, n_out))


def _inproj_kernel(x_ref, mod_ref, g_ref, wqkv_ref, wut_ref, q_ref, k_ref, v_ref, ut_ref, *, aw):
    x = x_ref[...]
    ms = jnp.mean(x * x, axis=-1, keepdims=True)
    h = x * lax.rsqrt(ms + NORM_EPS) * g_ref[...]
    h = h * (1.0 + mod_ref[1:2, :]) + mod_ref[0:1, :]
    hb = h.astype(BF16)
    qkv = _dot(hb, wqkv_ref[...])
    q_ref[...] = (qkv[:, :aw] * (LOG2_E * ATTN_HEAD_DIM ** -0.5)).astype(BF16)
    k_ref[...] = qkv[:, aw:2 * aw].astype(BF16)
    v_ref[...] = qkv[:, 2 * aw:].astype(BF16)
    ut_ref[...] = _dot_nt(wut_ref[...], hb)


def _inproj_call(x, mod3, norm_g, wqkv, wut, tm=512):
    bsz, seq, dm = x.shape
    aw = wqkv.shape[1] // 3
    sw = wut.shape[0]
    row = lambda b, i: (b, i, 0)
    return pl.pallas_call(
        functools.partial(_inproj_kernel, aw=aw),
        grid=(bsz, seq // tm),
        in_specs=[
            pl.BlockSpec((None, tm, dm), row),
            pl.BlockSpec((None, N_ADA, dm), lambda b, i: (b, 0, 0)),
            pl.BlockSpec((1, dm), lambda b, i: (0, 0)),
            pl.BlockSpec(wqkv.shape, lambda b, i: (0, 0)),
            pl.BlockSpec(wut.shape, lambda b, i: (0, 0)),
        ],
        out_specs=[
            pl.BlockSpec((None, tm, aw), row),
            pl.BlockSpec((None, tm, aw), row),
            pl.BlockSpec((None, tm, aw), row),
            pl.BlockSpec((None, sw, tm), lambda b, i: (b, 0, i)),
        ],
        out_shape=[
            jax.ShapeDtypeStruct((bsz, seq, aw), BF16),
            jax.ShapeDtypeStruct((bsz, seq, aw), BF16),
            jax.ShapeDtypeStruct((bsz, seq, aw), BF16),
            jax.ShapeDtypeStruct((bsz, sw, seq), F32),
        ],
        compiler_params=pltpu.CompilerParams(
            dimension_semantics=("parallel", "parallel"), vmem_limit_bytes=VMEM_LIMIT),
        name="inproj",
    )(x, mod3, norm_g.reshape(1, dm), wqkv, wut)


def _attn_kernel(lam_ref, q_ref, k_ref, v_ref, g_ref, o_ref, m_sc, l_sc, acc_sc, *, tq):
    i = pl.program_id(2)
    th = tq // 2
    q = q_ref[...]
    lane = lax.broadcasted_iota(jnp.int32, q.shape, 1)
    zero = jnp.zeros_like(q)
    qa = jnp.where(lane < ATTN_HEAD_DIM, q, zero)
    qb = jnp.where(lane >= ATTN_HEAD_DIM, q, zero)
    q2 = jnp.concatenate([qa[:th], qb[:th], qa[th:], qb[th:]], axis=0)
    m_sc[...] = jnp.full(m_sc.shape, NEG_BIG, F32)
    l_sc[...] = jnp.zeros(l_sc.shape, F32)
    acc_sc[...] = jnp.zeros(acc_sc.shape, F32)

    def step(j, r0, nr, nc, qpos0):
        start = pl.multiple_of(j * tq, tq)
        kt = k_ref[pl.ds(start, nc), :]
        vt = v_ref[pl.ds(start, nc), :]
        s = _dot_nt(q2[r0:r0 + nr], kt)
        if qpos0 is not None:
            r = lax.broadcasted_iota(jnp.int32, s.shape, 0)
            c = lax.broadcasted_iota(jnp.int32, s.shape, 1)
            s = jnp.where(c <= qpos0 + (r & (th - 1)), s, NEG_BIG)
        m_prev = m_sc[r0:r0 + nr, :]
        m_new = jnp.maximum(m_prev, jnp.max(s, axis=-1, keepdims=True))
        alpha = jnp.exp2(m_prev - m_new)
        p = jnp.exp2(s - jnp.concatenate([m_new] * (nc // LANES), axis=1))
        psum = p[:, :LANES]
        for c0 in range(LANES, nc, LANES):
            psum = psum + p[:, c0:c0 + LANES]
        l_sc[r0:r0 + nr, :] = alpha * l_sc[r0:r0 + nr, :] + psum
        acc_sc[r0:r0 + nr, :] = alpha * acc_sc[r0:r0 + nr, :] + _dot(p.astype(BF16), vt)
        m_sc[r0:r0 + nr, :] = m_new

    def body(jj, carry):
        step(2 * jj, 0, 2 * tq, tq, None)
        step(2 * jj + 1, 0, 2 * tq, tq, None)
        return carry

    lax.fori_loop(0, i // 2, body, 0)

    @pl.when(i % 2 == 1)
    def _():
        step(i - 1, 0, 2 * tq, tq, None)

    step(i, 0, tq, th, 0)
    step(i, tq, tq, tq, th)

    o_all = acc_sc[...] / jnp.sum(l_sc[...], axis=-1, keepdims=True)
    o0 = jnp.concatenate([o_all[:th], o_all[tq:tq + th]], axis=0)
    o1 = jnp.concatenate([o_all[th:tq], o_all[tq + th:]], axis=0)
    o = o0 - lam_ref[0] * o1
    ms = jnp.mean(o * o, axis=-1, keepdims=True)
    o = o * lax.rsqrt(ms + SUBLN_EPS) * g_ref[...] * (1.0 - LAM_INIT)
    o_ref[...] = o.astype(o_ref.dtype)


def _attn_call(lam, q, k, v, subln_g, tq=1024):
    bsz, seq, aw = q.shape
    nh = aw // V_HEAD_DIM
    qmap = lambda b, h, i: (b, i, h)
    kvmap = lambda b, h, i: (b, 0, h)
    return pl.pallas_call(
        functools.partial(_attn_kernel, tq=tq),
        grid=(bsz, nh, seq // tq),
        in_specs=[
            pl.BlockSpec(memory_space=pltpu.SMEM),
            pl.BlockSpec((None, tq, V_HEAD_DIM), qmap),
            pl.BlockSpec((None, seq, V_HEAD_DIM), kvmap),
            pl.BlockSpec((None, seq, V_HEAD_DIM), kvmap),
            pl.BlockSpec((1, V_HEAD_DIM), lambda b, h, i: (0, 0)),
        ],
        out_specs=pl.BlockSpec((None, tq, V_HEAD_DIM), qmap),
        out_shape=jax.ShapeDtypeStruct((bsz, seq, aw), BF16),
        scratch_shapes=[pltpu.VMEM((2 * tq, V_HEAD_DIM), F32)] * 3,
        compiler_params=pltpu.CompilerParams(
            dimension_semantics=("parallel", "parallel", "parallel"), vmem_limit_bytes=VMEM_LIMIT),
        name="attn",
    )(lam, q, k, v, subln_g.reshape(1, V_HEAD_DIM))


def _ssm_operators(a_re, a_im, log_step, b_re, b_im, c_re, c_im):
    t = LANES
    hi = lax.Precision.HIGHEST
    lam = lax.complex(jnp.minimum(a_re, -1e-4), a_im)
    delta = jnp.exp(log_step)[:, None]
    lam_bar = jnp.exp(lam * delta)
    bbar = ((lam_bar - 1.0) / lam)[:, :, None] * lax.complex(b_re, b_im)
    cmat = lax.complex(c_re, c_im)
    ld = lam * delta
    tau = jnp.arange(t + 1, dtype=F32)
    pw = jnp.exp(ld[:, :, None] * tau)
    cb = cmat[:, None, :, :] * jnp.swapaxes(bbar, 1, 2)[:, :, None, :]
    g = a_re.shape[0]
    cb = cb.reshape(g, SSM_GROUP * SSM_GROUP, SSM_STATE)
    cbcat = jnp.concatenate([cb.real, -cb.imag], axis=-1)
    pcat = jnp.concatenate([pw.real[:, :, :t], pw.imag[:, :, :t]], axis=1)
    kmat = jnp.einsum('gxp,gpt->gxt', cbcat, pcat, precision=hi)
    prev = jnp.swapaxes(pw[:, :, t - 1::-1], 1, 2)
    arev = jnp.concatenate([prev.real, prev.imag], axis=-1)
    bt = jnp.swapaxes(bbar, 1, 2)
    brow = jnp.stack([jnp.concatenate([bt.real, bt.real], axis=-1),
                      jnp.concatenate([-bt.imag, bt.imag], axis=-1)], axis=2).reshape(g, 2 * SSM_GROUP, 2 * SSM_STATE)
    a1 = jnp.concatenate([pw.real[:, :, 1:], pw.imag[:, :, 1:]], axis=1)
    ct = jnp.swapaxes(cmat, 1, 2)
    ccol = jnp.stack([jnp.concatenate([ct.real, -ct.real], axis=1),
                      jnp.concatenate([-ct.imag, -ct.imag], axis=1)], axis=-1).reshape(g, 2 * SSM_STATE, 2 * SSM_GROUP)
    rows = []
    for i in range(6):
        d = jnp.exp(ld * float(t * (1 << i)))
        rows.append(jnp.concatenate([d.real, d.real], axis=-1))
        rows.append(jnp.concatenate([-d.imag, d.imag], axis=-1))
    rows += [jnp.zeros_like(rows[0])] * 4
    dpow = jnp.stack(rows, axis=1)
    return kmat, arev, brow, a1, ccol, dpow


def _ssm_kernel(u_ref, k_ref, arev_ref, brow_ref, a1_ref, ccol_ref, dp_ref, y_ref, m_sc, ws_sc, wc_sc):
    bsz, nch, n_chunk, t = u_ref.shape
    row = lax.broadcasted_iota(jnp.int32, (t, t), 0)
    col = lax.broadcasted_iota(jnp.int32, (t, t), 1)
    causal = col >= row
    half = arev_ref.shape[1] // 2

    arev = arev_ref[...]
    arev_sw = pltpu.roll(arev, half, 1)
    a1 = a1_ref[...]
    a1_sw = pltpu.roll(a1, half, 0)
    for c in range(nch):
        ws_sc[c * t:(c + 1) * t, :] = (arev * brow_ref[2 * c:2 * c + 1, :]
                                       + arev_sw * brow_ref[2 * c + 1:2 * c + 2, :]).astype(BF16)
        wc_sc[:, c * t:(c + 1) * t] = (a1 * ccol_ref[:, 2 * c:2 * c + 1]
                                       + a1_sw * ccol_ref[:, 2 * c + 1:2 * c + 2]).astype(BF16)

    uflat = jnp.concatenate(
        [jnp.concatenate([u_ref[b, ci] for b in range(bsz)], axis=0) for ci in range(nch)],
        axis=1).astype(BF16)

    y = None
    for c0 in range(0, nch, 2):
        for ci in (c0, c0 + 1):
            for co in range(nch):
                kb = jnp.broadcast_to(k_ref[ci * nch + co:ci * nch + co + 1, :], (t, t))
                kb = pltpu.roll(kb, 0, 1, stride=1, stride_axis=0)
                m_sc[ci * t:(ci + 1) * t, co * t:(co + 1) * t] = jnp.where(causal, kb, 0.0).astype(BF16)
        part = _dot(uflat[:, c0 * t:(c0 + 2) * t], m_sc[c0 * t:(c0 + 2) * t, :])
        y = part if y is None else y + part
    z = _dot(uflat, ws_sc[...])
    kidx = lax.broadcasted_iota(jnp.int32, z.shape, 0) & (n_chunk - 1)
    shift = 1
    i = 0
    while shift < n_chunk:
        zs = jnp.where(kidx >= shift, pltpu.roll(z, shift, 0), 0.0)
        z = z + zs * dp_ref[2 * i:2 * i + 1, :] + pltpu.roll(zs, half, 1) * dp_ref[2 * i + 1:2 * i + 2, :]
        shift *= 2
        i += 1
    xin = jnp.where(kidx >= 1, pltpu.roll(z, 1, 0), 0.0)
    xh, xl = _split_bf16(xin)
    wc = wc_sc[...]
    y = y + _dot(xh, wc) + _dot(xl, wc)
    for b in range(bsz):
        for co in range(nch):
            y_ref[b, co] = y[b * n_chunk:(b + 1) * n_chunk, co * t:(co + 1) * t]


def _ssm_call(ut, kmat, arev, brow, a1, ccol, dpow):
    bsz, sw, seq = ut.shape
    n_groups = sw // SSM_GROUP
    n_chunk = seq // LANES
    assert n_chunk & (n_chunk - 1) == 0 and n_chunk <= 64
    u4 = ut.reshape(bsz, sw, n_chunk, LANES)
    blk = (bsz, SSM_GROUP, n_chunk, LANES)
    gmap = lambda g: (0, g, 0, 0)
    pmap = lambda g: (g, 0, 0)
    y4 = pl.pallas_call(
        _ssm_kernel,
        grid=(n_groups,),
        in_specs=[
            pl.BlockSpec(blk, gmap),
            pl.BlockSpec((None,) + kmat.shape[1:], pmap),
            pl.BlockSpec((None,) + arev.shape[1:], pmap),
            pl.BlockSpec((None,) + brow.shape[1:], pmap),
            pl.BlockSpec((None,) + a1.shape[1:], pmap),
            pl.BlockSpec((None,) + ccol.shape[1:], pmap),
            pl.BlockSpec((None,) + dpow.shape[1:], pmap),
        ],
        out_specs=pl.BlockSpec(blk, gmap),
        out_shape=jax.ShapeDtypeStruct(u4.shape, F32),
        scratch_shapes=[pltpu.VMEM((SSM_GROUP * LANES, SSM_GROUP * LANES), BF16),
                        pltpu.VMEM((SSM_GROUP * LANES, 2 * SSM_STATE), BF16),
                        pltpu.VMEM((2 * SSM_STATE, SSM_GROUP * LANES), BF16)],
        compiler_params=pltpu.CompilerParams(
            dimension_semantics=("parallel",), vmem_limit_bytes=VMEM_LIMIT),
        name="ssm",
    )(u4, kmat, arev, brow, a1, ccol, dpow)
    return y4


def _mid_kernel(x_ref, attn_ref, yt_ref, ut_ref, mod_ref, dsk_ref, wglut_ref, bglu_ref, wo1_ref, wo2_ref,
                g2_ref, wrt_ref, x1_ref, h2_ref, lg_ref):
    yt = jnp.concatenate([yt_ref[:, j, :] for j in range(yt_ref.shape[1])], axis=1)
    gt = _gelu_tanh(yt + dsk_ref[...] * ut_ref[...])
    zt = _dot(wglut_ref[...], gt.astype(BF16)) + bglu_ref[...]
    st = gt * jax.nn.sigmoid(zt)
    s = st.T.astype(BF16)
    mix = _dot(attn_ref[...], wo1_ref[...]) + _dot(s, wo2_ref[...])
    x1 = x_ref[...] + mod_ref[2:3, :] * mix
    ms = jnp.mean(x1 * x1, axis=-1, keepdims=True)
    h2 = x1 * lax.rsqrt(ms + NORM_EPS) * g2_ref[...]
    h2 = h2 * (1.0 + mod_ref[4:5, :]) + mod_ref[3:4, :]
    x1_ref[...] = x1
    h2_ref[0], h2_ref[1] = _pack_rows(h2)
    lg_ref[...] = jax.nn.sigmoid(_dot3_nt(wrt_ref[...], h2))


def _mid_call(x, attn, y4, ut, mod3, dsk, wglut, bglu, wo1, wo2, g2, wrt, tm=1024):
    bsz, seq, dm = x.shape
    aw = attn.shape[2]
    sw = y4.shape[1]
    ne = wrt.shape[0]
    row = lambda b, i: (b, i, 0)
    colm = lambda b, i: (b, 0, i)
    full = lambda a: pl.BlockSpec(a.shape, lambda b, i: (0,) * a.ndim)
    dsk = dsk.reshape(sw, 1)
    bglu = bglu.reshape(sw, 1)
    g2 = g2.reshape(1, dm)
    return pl.pallas_call(
        _mid_kernel,
        grid=(bsz, seq // tm),
        in_specs=[
            pl.BlockSpec((None, tm, dm), row),
            pl.BlockSpec((None, tm, aw), row),
            pl.BlockSpec((None, sw, tm // LANES, LANES), lambda b, i: (b, 0, i, 0)),
            pl.BlockSpec((None, sw, tm), colm),
            pl.BlockSpec((None, N_ADA, dm), lambda b, i: (b, 0, 0)),
            full(dsk), full(wglut), full(bglu), full(wo1), full(wo2), full(g2), full(wrt),
        ],
        out_specs=[
            pl.BlockSpec((None, tm, dm), row),
            pl.BlockSpec((2, None, tm, dm // 4), lambda b, i: (0, b, i, 0)),
            pl.BlockSpec((None, ne, tm), colm),
        ],
        out_shape=[
            jax.ShapeDtypeStruct((bsz, seq, dm), F32),
            jax.ShapeDtypeStruct((2, bsz, seq, dm // 4), jnp.uint32),
            jax.ShapeDtypeStruct((bsz, ne, seq), F32),
        ],
        compiler_params=pltpu.CompilerParams(
            dimension_semantics=("parallel", "parallel"), vmem_limit_bytes=VMEM_LIMIT),
        name="mid",
    )(x, attn, y4, ut, mod3, dsk, wglut, bglu, wo1, wo2, g2, wrt)


def _moe_kernel(cs_ref, cnt_ref, nv_ref, xs_hbm, wg_ref, wu_ref, wd_ref, out_hbm, xbuf, obuf, isem, osem):
    step = pl.program_id(0)
    n_steps = pl.num_programs(0)
    per_step = wg_ref.shape[0]
    c_end = cs_ref[n_steps * per_step]
    n_slots, _, ch, q = xbuf.shape

    n_parts = MOE_COPY_PARTS
    pr = ch // n_parts
    row_queue = 1

    def pieces(c, slot, j, kind):
        rows = pl.ds(j * pr, pr)
        hbm_rows = pl.ds(c * ch + j * pr, pr)
        if kind == "fetch":
            return pltpu.make_async_copy(xs_hbm.at[:, hbm_rows, :], xbuf.at[slot, :, rows, :], isem.at[slot])
        return pltpu.make_async_copy(obuf.at[slot, :, rows, :], out_hbm.at[:, hbm_rows, :], osem.at[slot])

    def copy_op(c, slot, kind, op):
        for j in range(n_parts):
            @pl.when(nv_ref[c] > j * pr)
            def _(j=j):
                cp = pieces(c, slot, j, kind)
                cp.start(priority=row_queue) if op == "start" else cp.wait()

    look = n_slots - 2

    @pl.when(step == 0)
    def _():
        for c in range(look):
            @pl.when(c < c_end)
            def _():
                copy_op(c, c, "fetch", "start")

    def arrive(c, n):
        for i in range(n):
            copy_op(c + i, lax.rem(c + i, n_slots), "fetch", "wait")
        for i in range(n):
            ahead = c + look + i

            @pl.when(ahead < c_end)
            def _():
                copy_op(ahead, lax.rem(ahead, n_slots), "fetch", "start")
        for i in range(n):
            @pl.when(c + i >= n_slots)
            def _():
                copy_op(c + i - n_slots, lax.rem(c + i, n_slots), "drain", "wait")

    def compute(c, n, n_valid, sub):
        slots = [lax.rem(c + i, n_slots) for i in range(n)]
        valid = lax.broadcasted_iota(jnp.int32, (n * ch, q), 0) < n_valid
        gate = up = None
        for h in range(2):
            lo, hi = _unpack_plane(jnp.concatenate([xbuf[s, h] for s in slots], axis=0))
            lo = jnp.where(valid, lo, 0.0).astype(BF16)
            hi = jnp.where(valid, hi, 0.0).astype(BF16)
            for xpart, k0 in ((lo, h * q), (hi, (2 + h) * q)):
                g = _dot(xpart, wg_ref[sub, k0:k0 + q, :].astype(BF16))
                u = _dot(xpart, wu_ref[sub, k0:k0 + q, :].astype(BF16))
                gate = g if gate is None else gate + g
                up = u if up is None else up + u
        hb = _silu(gate) * up
        p0, p1 = _pack_rows(_dot(hb.astype(BF16), wd_ref[sub].astype(BF16)))
        for i, s in enumerate(slots):
            obuf[s, 0] = p0[i * ch:(i + 1) * ch]
            obuf[s, 1] = p1[i * ch:(i + 1) * ch]

    def depart(c, n):
        for i in range(n):
            copy_op(c + i, lax.rem(c + i, n_slots), "drain", "start")

    def expert(sub):
        e = step * per_step + sub
        c0, c1 = cs_ref[e], cs_ref[e + 1]

        @pl.when(c1 > c0)
        def _():
            n_pairs = (c1 - c0) // 2

            def pair(i, carry):
                c = c0 + 2 * i
                arrive(c, 2)
                compute(c, 2, cnt_ref[e] - (c - c0) * ch, sub)
                depart(c, 2)
                return carry

            lax.fori_loop(0, n_pairs, pair, 0)

            @pl.when(c0 + 2 * n_pairs < c1)
            def _():
                arrive(c1 - 1, 1)
                compute(c1 - 1, 1, cnt_ref[e] - (c1 - 1 - c0) * ch, sub)
                depart(c1 - 1, 1)

    for sub in range(per_step):
        expert(sub)

    @pl.when(step == n_steps - 1)
    def _():
        for back in range(n_slots, 0, -1):
            @pl.when(c_end >= back)
            def _():
                copy_op(c_end - back, lax.rem(c_end - back, n_slots), "drain", "wait")


def _moe_call(chunk_start, counts, chunk_valid, xs, w_gate, w_up, w_down):
    n_rows = xs.shape[1]
    n_exp, dm, de = w_gate.shape
    grid_spec = pltpu.PrefetchScalarGridSpec(
        num_scalar_prefetch=3,
        grid=(n_exp // MOE_EXPERTS_PER_STEP,),
        in_specs=[
            pl.BlockSpec(memory_space=pl.ANY),
            pl.BlockSpec((MOE_EXPERTS_PER_STEP, dm, de), lambda s, cs, cnt, nv: (s, 0, 0)),
            pl.BlockSpec((MOE_EXPERTS_PER_STEP, dm, de), lambda s, cs, cnt, nv: (s, 0, 0)),
            pl.BlockSpec((MOE_EXPERTS_PER_STEP, de, dm), lambda s, cs, cnt, nv: (s, 0, 0)),
        ],
        out_specs=pl.BlockSpec(memory_space=pl.ANY),
        scratch_shapes=[
            pltpu.VMEM((MOE_SLOTS, 2, MOE_BLOCK, dm // 4), jnp.uint32),
            pltpu.VMEM((MOE_SLOTS, 2, MOE_BLOCK, dm // 4), jnp.uint32),
            pltpu.SemaphoreType.DMA((MOE_SLOTS,)), pltpu.SemaphoreType.DMA((MOE_SLOTS,)),
        ],
    )
    return pl.pallas_call(
        _moe_kernel,
        grid_spec=grid_spec,
        out_shape=jax.ShapeDtypeStruct(xs.shape, jnp.uint32),
        compiler_params=pltpu.CompilerParams(
            dimension_semantics=("arbitrary",), vmem_limit_bytes=VMEM_LIMIT),
        name="moe",
    )(chunk_start, counts, chunk_valid, xs, w_gate, w_up, w_down)


def _final_kernel(x1_ref, h2_ref, r_ref, w_ref, mod_ref, g_ref, wsg_ref, wsu_ref, wsd_ref, o_ref):
    w = w_ref[...].T
    groups = [None] * 4
    hparts = [None] * 4
    for h in range(2):
        hparts[h], hparts[2 + h] = _unpack_plane(h2_ref[h])
        for k in range(r_ref.shape[1]):
            lo, hi = _unpack_plane(r_ref[h, k])
            wk = w[:, k:k + 1]
            groups[h] = wk * lo if groups[h] is None else groups[h] + wk * lo
            groups[2 + h] = wk * hi if groups[2 + h] is None else groups[2 + h] + wk * hi
    routed = jnp.concatenate(groups, axis=1)
    hb = jnp.concatenate(hparts, axis=1).astype(BF16)
    sh = _silu(_dot(hb, wsg_ref[...])) * _dot(hb, wsu_ref[...])
    shared = _dot(sh.astype(BF16), wsd_ref[...])
    x = x1_ref[...] + mod_ref[5:6, :] * (routed + shared)
    ms = jnp.mean(x * x, axis=-1, keepdims=True)
    o_ref[...] = x * lax.rsqrt(ms + NORM_EPS) * g_ref[...]


def _final_call(x1, h2, rows, w, mod3, final_g, wsg, wsu, wsd, tm=256):
    bsz, seq, dm = x1.shape
    kk, q = rows.shape[1], rows.shape[3]
    nt = seq // tm
    row = lambda b, i: (b, i, 0)
    full = lambda a: pl.BlockSpec(a.shape, lambda b, i: (0,) * a.ndim)
    return pl.pallas_call(
        _final_kernel,
        grid=(bsz, nt),
        in_specs=[
            pl.BlockSpec((None, tm, dm), row),
            pl.BlockSpec((2, None, tm, q), lambda b, i: (0, b, i, 0)),
            pl.BlockSpec((2, kk, tm, q), lambda b, i: (0, 0, b * nt + i, 0)),
            pl.BlockSpec((None, kk, tm), lambda b, i: (b, 0, i)),
            pl.BlockSpec((None, N_ADA, dm), lambda b, i: (b, 0, 0)),
            pl.BlockSpec((1, dm), lambda b, i: (0, 0)),
            full(wsg), full(wsu), full(wsd),
        ],
        out_specs=pl.BlockSpec((None, tm, dm), row),
        out_shape=jax.ShapeDtypeStruct((bsz, seq, dm), F32),
        compiler_params=pltpu.CompilerParams(
            dimension_semantics=("parallel", "parallel"), vmem_limit_bytes=VMEM_LIMIT),
        name="final",
    )(x1, h2, rows, w, mod3, final_g.reshape(1, dm), wsg, wsu, wsd)


def _route_kernel(sc_ref, bias_ref, idx_ref, w_ref, rank_ref, cnt_ref, tri_sc, carry_sc):
    ne, tn = sc_ref.shape
    gsz = ne // N_EXPERT_GROUPS
    neg = -jnp.inf
    first = jnp.logical_and(pl.program_id(0) == 0, pl.program_id(1) == 0)

    @pl.when(first)
    def _():
        r = lax.broadcasted_iota(jnp.int32, (tn, tn), 0)
        c = lax.broadcasted_iota(jnp.int32, (tn, tn), 1)
        tri_sc[...] = jnp.where(r < c, 1.0, 0.0).astype(BF16)
        carry_sc[...] = jnp.zeros(carry_sc.shape, F32)

    s = sc_ref[...]
    sel = s + bias_ref[...]
    gs = []
    for g in range(N_EXPERT_GROUPS):
        blk = sel[g * gsz:(g + 1) * gsz]
        m1 = jnp.max(blk, axis=0, keepdims=True)
        eq = blk == m1
        n_eq = jnp.sum(jnp.where(eq, 1.0, 0.0), axis=0, keepdims=True)
        m2 = jnp.max(jnp.where(eq, neg, blk), axis=0, keepdims=True)
        gs.append(m1 + jnp.where(n_eq >= 2.0, m1, m2))
    gs = jnp.concatenate(gs, axis=0)
    gi = lax.broadcasted_iota(jnp.int32, gs.shape, 0)
    beaten = jnp.zeros(gs.shape, F32)
    for gp in range(N_EXPERT_GROUPS):
        other = gs[gp:gp + 1]
        wins = jnp.where(other > gs, 1.0, jnp.where(other == gs, jnp.where(gi > gp, 1.0, 0.0), 0.0))
        beaten = beaten + wins
    gadd = jnp.where(beaten < float(TOPK_GROUPS), 0.0, neg)
    cur = jnp.concatenate(
        [sel[g * gsz:(g + 1) * gsz] + gadd[g:g + 1] for g in range(N_EXPERT_GROUPS)], axis=0)

    eidx = lax.broadcasted_iota(jnp.int32, (ne, tn), 0).astype(F32)
    cur0 = cur
    idx_rows, s_rows = [], []
    for _ in range(TOP_K):
        m = jnp.max(cur, axis=0, keepdims=True)
        ik = jnp.min(jnp.where(cur == m, eidx, float(ne)), axis=0, keepdims=True)
        oh = eidx == ik
        s_rows.append(jnp.sum(jnp.where(oh, s, 0.0), axis=0, keepdims=True))
        cur = jnp.where(oh, neg, cur)
        idx_rows.append(ik)
    chosen = jnp.where(cur == neg, jnp.where(cur0 > neg, 1.0, 0.0), 0.0)
    sk = jnp.concatenate(s_rows, axis=0)
    w_ref[...] = sk / jnp.sum(sk, axis=0, keepdims=True) * ROUTED_SCALE
    idx_ref[...] = jnp.concatenate(idx_rows, axis=0).astype(jnp.int32)

    before = _dot(chosen.astype(BF16), tri_sc[...]) + carry_sc[...]
    rank_rows = [jnp.sum(jnp.where(eidx == ik, before, 0.0), axis=0, keepdims=True) for ik in idx_rows]
    rank_ref[...] = jnp.concatenate(rank_rows, axis=0).astype(jnp.int32)
    carry_sc[...] = carry_sc[...] + jnp.sum(chosen, axis=1, keepdims=True)
    cnt_ref[...] = carry_sc[...]


def _route_call(scores_t, router_bias, tn=512):
    bsz, ne, seq = scores_t.shape
    tok = lambda b, i: (b, 0, i)
    return pl.pallas_call(
        _route_kernel,
        grid=(bsz, seq // tn),
        in_specs=[
            pl.BlockSpec((None, ne, tn), tok),
            pl.BlockSpec((ne, 1), lambda b, i: (0, 0)),
        ],
        out_specs=[
            pl.BlockSpec((None, TOP_K, tn), tok),
            pl.BlockSpec((None, TOP_K, tn), tok),
            pl.BlockSpec((None, TOP_K, tn), tok),
            pl.BlockSpec((ne, 1), lambda b, i: (0, 0)),
        ],
        out_shape=[
            jax.ShapeDtypeStruct((bsz, TOP_K, seq), jnp.int32),
            jax.ShapeDtypeStruct((bsz, TOP_K, seq), F32),
            jax.ShapeDtypeStruct((bsz, TOP_K, seq), jnp.int32),
            jax.ShapeDtypeStruct((ne, 1), F32),
        ],
        scratch_shapes=[pltpu.VMEM((tn, tn), BF16), pltpu.VMEM((ne, 1), F32)],
        compiler_params=pltpu.CompilerParams(
            dimension_semantics=("arbitrary", "arbitrary"), vmem_limit_bytes=VMEM_LIMIT),
        name="route",
    )(scores_t, router_bias.reshape(ne, 1))


def _dispatch_tables(idx, rank, counts):
    ne = counts.shape[0]
    n_tok = idx.shape[0] * idx.shape[2]
    counts = counts.astype(jnp.int32)
    pcounts = (counts + MOE_BLOCK - 1) // MOE_BLOCK * MOE_BLOCK
    pstarts = jnp.concatenate([jnp.zeros((1,), jnp.int32), jnp.cumsum(pcounts).astype(jnp.int32)])
    n_blocks = -(-(n_tok * TOP_K + ne * (MOE_BLOCK - 1)) // MOE_BLOCK)
    dest = _dest_call(pstarts[:ne], idx, rank)
    chunk_row = jnp.arange(n_blocks, dtype=jnp.int32)[:, None] * MOE_BLOCK
    owner = jnp.logical_and(pstarts[None, :ne] <= chunk_row, chunk_row < pstarts[None, 1:])
    left = jnp.sum(jnp.where(owner, counts[None, :] - (chunk_row - pstarts[None, :ne]), 0), axis=1)
    chunk_valid = jnp.clip(left, 0, MOE_BLOCK).astype(jnp.int32)
    return dest, pstarts // MOE_BLOCK, counts, chunk_valid, n_blocks


def _dest_kernel(ps_ref, idx_ref, rank_ref, o_ref):
    idx = idx_ref[...]

    def body(e, acc):
        return jnp.where(idx == e, ps_ref[e], acc)

    o_ref[...] = rank_ref[...] + lax.fori_loop(0, ps_ref.shape[0], body, jnp.zeros(idx.shape, jnp.int32))


def _dest_call(pstarts, idx, rank, tn=2048):
    bsz, kk, seq = idx.shape
    tok = lambda b, i, ps: (b, 0, i)
    grid_spec = pltpu.PrefetchScalarGridSpec(
        num_scalar_prefetch=1,
        grid=(bsz, seq // tn),
        in_specs=[pl.BlockSpec((None, kk, tn), tok), pl.BlockSpec((None, kk, tn), tok)],
        out_specs=pl.BlockSpec((None, kk, tn), tok),
    )
    return pl.pallas_call(
        _dest_kernel,
        grid_spec=grid_spec,
        out_shape=jax.ShapeDtypeStruct(idx.shape, jnp.int32),
        compiler_params=pltpu.CompilerParams(dimension_semantics=("parallel", "parallel")),
        name="dest",
    )(pstarts, idx, rank)


def _scatter_rows(rows, dest, n_out):
    n_tok, width = rows.shape
    k_slots = dest.shape[0]
    win = LANES
    mesh = plsc.VectorSubcoreMesh(core_axis_name="c", subcore_axis_name="s")

    @pl.kernel(out_type=jax.ShapeDtypeStruct((n_out, width), rows.dtype), mesh=mesh, scratch_types=[])
    def scatter(rows_hbm, dest_hbm, out_hbm):
        def body(rows_vmem, idx_vmem):
            pltpu.sync_copy(rows_vmem, out_hbm.at[idx_vmem.at[0]])

        pltpu.emit_pipeline(
            body,
            grid=(n_tok // win, k_slots),
            in_specs=[
                pl.BlockSpec((win, width), lambda i, k: (i, 0)),
                pl.BlockSpec((1, win), lambda i, k: (k, i)),
            ],
            out_specs=[],
            core_axis_name=("c", "s"),
            dimension_semantics=(pltpu.PARALLEL, pltpu.ARBITRARY),
        )(rows_hbm, dest_hbm)

    return scatter(rows, dest)


def _gather_rows(rows, idx):
    n = idx.shape[0]
    width = rows.shape[1]
    win = LANES
    mesh = plsc.VectorSubcoreMesh(core_axis_name="c", subcore_axis_name="s")

    @pl.kernel(out_type=jax.ShapeDtypeStruct((n, width), rows.dtype), mesh=mesh, scratch_types=[])
    def gather(rows_hbm, idx_hbm, out_hbm):
        def body(idx_vmem, out_vmem):
            pltpu.sync_copy(rows_hbm.at[idx_vmem.at[0]], out_vmem)

        pltpu.emit_pipeline(
            body,
            grid=(n // win,),
            in_specs=[pl.BlockSpec((1, win), lambda i: (0, i))],
            out_specs=[pl.BlockSpec((win, width), lambda i: (i, 0))],
            core_axis_name=("c", "s"),
            dimension_semantics=(pltpu.PARALLEL,),
        )(idx_hbm, out_hbm)

    return gather(rows, idx.reshape(1, n))


def kernel(x, c, w_ada, b_ada, norm1_g, w_in, w_out, lambda_q1, lambda_k1, lambda_q2, lambda_k2, subln_g,
           ssm_a_re, ssm_a_im, ssm_log_step, ssm_b_re, ssm_b_im, ssm_c_re, ssm_c_im, ssm_d, w_glu, b_glu,
           norm2_g, w_router, router_bias, w_gate, w_up, w_down, ws_gate, ws_up, ws_down, final_g):
    bsz, seq, dm = x.shape
    n_tok = bsz * seq
    aw = N_ATTN_HEADS * V_HEAD_DIM

    mod3 = _mod_call(c, w_ada[0], b_ada[0]).reshape(bsz, N_ADA, dm)

    wqkv = w_in[0][:, :3 * aw].astype(BF16)
    wut = w_in[0][:, 3 * aw:].T.astype(BF16)
    q, k, v, ut = _inproj_call(x, mod3, norm1_g[0], wqkv, wut)

    lam = (jnp.exp(jnp.sum(lambda_q1[0] * lambda_k1[0])) - jnp.exp(jnp.sum(lambda_q2[0] * lambda_k2[0]))
           + LAM_INIT).reshape(1)
    attn = _attn_call(lam, q, k, v, subln_g[0])

    yt = _ssm_call(ut, *_ssm_operators(ssm_a_re[0], ssm_a_im[0], ssm_log_step[0], ssm_b_re[0], ssm_b_im[0],
                                       ssm_c_re[0], ssm_c_im[0]))

    x1, h2, scores_t = _mid_call(
        x, attn, yt, ut, mod3, ssm_d[0], w_glu[0].T.astype(BF16), b_glu[0],
        w_out[0][:aw].astype(BF16), w_out[0][aw:].astype(BF16), norm2_g[0], w_router[0].T)

    idx, w, rank, counts = _route_call(scores_t, router_bias[0])
    dest, chunk_start, counts, chunk_valid, n_blocks = _dispatch_tables(idx, rank, counts.reshape(-1))
    n_rows = n_blocks * MOE_BLOCK

    dest_k = jnp.swapaxes(dest, 0, 1).reshape(TOP_K, n_tok)
    dest_half = jnp.concatenate([dest_k, dest_k + n_rows], axis=1)
    xs = _scatter_rows(h2.reshape(2 * n_tok, dm // 4), dest_half, 2 * n_rows).reshape(2, n_rows, dm // 4)

    out = _moe_call(chunk_start, counts, chunk_valid, xs, w_gate[0], w_up[0], w_down[0])
    src = jnp.concatenate([dest_k.reshape(-1), (dest_k + n_rows).reshape(-1)])
    rows = _gather_rows(out.reshape(2 * n_rows, dm // 4), src)
    return _final_call(x1, h2, rows.reshape(2, TOP_K, n_tok, dm // 4), w, mod3, final_g,
                       ws_gate[0].astype(BF16), ws_up[0].astype(BF16), ws_down[0].astype(BF16))
```

```python
import functools
import math

import jax
import jax.numpy as jnp
from jax import lax
from jax.experimental import pallas as pl
from jax.experimental.pallas import tpu as pltpu
from jax.experimental.pallas import tpu_sc as plsc

F32 = jnp.float32
BF16 = jnp.bfloat16

N_ATTN_HEADS = 4
ATTN_HEAD_DIM = 64
V_HEAD_DIM = 128
SSM_GROUP = 16
N_SSM_GROUPS = 32
SSM_STATE = 64
N_EXPERTS = 256
TOP_K = 8
N_EXPERT_GROUPS = 8
TOPK_GROUPS = 4
ROUTED_SCALE = 2.5
NORM_EPS = 1e-6
SUBLN_EPS = 1e-5
N_ADA = 6
LAM_INIT = 0.8 - 0.6 * math.exp(-0.3 * 0)

LANES = 128
MOE_BLOCK = 256
MOE_SLOTS = 8
MOE_EXPERTS_PER_STEP = 2
MOE_COPY_PARTS = 4
NEG_BIG = -1e30
LOG2_E = math.log2(math.e)
VMEM_LIMIT = 48 * 1024 * 1024


def _split_bf16(a):
    hi = a.astype(BF16)
    lo = (a - hi.astype(F32)).astype(BF16)
    return hi, lo


def _dot(a, b):
    return jnp.dot(a, b, preferred_element_type=F32)


def _dot_nt(a, b):
    return lax.dot_general(a, b, (((1,), (1,)), ((), ())), preferred_element_type=F32)


def _dot3(a, b):
    ah, al = _split_bf16(a)
    bh, bl = _split_bf16(b)
    return _dot(ah, bh) + _dot(ah, bl) + _dot(al, bh)


def _dot3_nt(a, b):
    ah, al = _split_bf16(a)
    bh, bl = _split_bf16(b)
    return _dot_nt(ah, bh) + _dot_nt(ah, bl) + _dot_nt(al, bh)


def _silu(x):
    return x * jax.nn.sigmoid(x)


def _gelu_tanh(x):
    c = math.sqrt(2.0 / math.pi)
    return 0.5 * x * (1.0 + jnp.tanh(c * (x + 0.044715 * (x * x * x))))


def _pack_rows(x):
    bits = lax.bitcast_convert_type(x.astype(BF16).astype(F32), jnp.uint32)
    half = bits.shape[1] // 2
    packed = (bits[:, :half] >> 16) | (bits[:, half:] & jnp.uint32(0xFFFF0000))
    return packed[:, :half // 2], packed[:, half // 2:]


def _unpack_plane(xu):
    return (lax.bitcast_convert_type(xu << 16, F32),
            lax.bitcast_convert_type(xu & jnp.uint32(0xFFFF0000), F32))


def _mod_kernel(c_ref, w_ref, b_ref, o_ref):
    cond = _silu(c_ref[...])
    o_ref[...] = _dot3(cond, w_ref[...]) + b_ref[...]


def _mod_call(c, w_ada, b_ada):
    bsz, dm = c.shape
    n_out = w_ada.shape[1]
    tn = 1024
    return pl.pallas_call(
        _mod_kernel,
        grid=(n_out // tn,),
        in_specs=[
            pl.BlockSpec((bsz, dm), lambda j: (0, 0)),
            pl.BlockSpec((dm, tn), lambda j: (0, j)),
            pl.BlockSpec((1, tn), lambda j: (0, j)),
        ],
        out_specs=pl.BlockSpec((bsz, tn), lambda j: (0, j)),
        out_shape=jax.ShapeDtypeStruct((bsz, n_out), F32),
        compiler_params=pltpu.CompilerParams(vmem_limit_bytes=VMEM_LIMIT),
        name="mod",
    )(c, w_ada, b_ada.reshape(1, n_out))


def _inproj_kernel(x_ref, mod_ref, g_ref, wqkv_ref, wut_ref, q_ref, k_ref, v_ref, ut_ref, *, aw):
    x = x_ref[...]
    ms = jnp.mean(x * x, axis=-1, keepdims=True)
    h = x * lax.rsqrt(ms + NORM_EPS) * g_ref[...]
    h = h * (1.0 + mod_ref[1:2, :]) + mod_ref[0:1, :]
    hb = h.astype(BF16)
    qkv = _dot(hb, wqkv_ref[...])
    q_ref[...] = (qkv[:, :aw] * (LOG2_E * ATTN_HEAD_DIM ** -0.5)).astype(BF16)
    k_ref[...] = qkv[:, aw:2 * aw].astype(BF16)
    v_ref[...] = qkv[:, 2 * aw:].astype(BF16)
    ut_ref[...] = _dot_nt(wut_ref[...], hb)


def _inproj_call(x, mod3, norm_g, wqkv, wut, tm=512):
    bsz, seq, dm = x.shape
    aw = wqkv.shape[1] // 3
    sw = wut.shape[0]
    row = lambda b, i: (b, i, 0)
    return pl.pallas_call(
        functools.partial(_inproj_kernel, aw=aw),
        grid=(bsz, seq // tm),
        in_specs=[
            pl.BlockSpec((None, tm, dm), row),
            pl.BlockSpec((None, N_ADA, dm), lambda b, i: (b, 0, 0)),
            pl.BlockSpec((1, dm), lambda b, i: (0, 0)),
            pl.BlockSpec(wqkv.shape, lambda b, i: (0, 0)),
            pl.BlockSpec(wut.shape, lambda b, i: (0, 0)),
        ],
        out_specs=[
            pl.BlockSpec((None, tm, aw), row),
            pl.BlockSpec((None, tm, aw), row),
            pl.BlockSpec((None, tm, aw), row),
            pl.BlockSpec((None, sw, tm), lambda b, i: (b, 0, i)),
        ],
        out_shape=[
            jax.ShapeDtypeStruct((bsz, seq, aw), BF16),
            jax.ShapeDtypeStruct((bsz, seq, aw), BF16),
            jax.ShapeDtypeStruct((bsz, seq, aw), BF16),
            jax.ShapeDtypeStruct((bsz, sw, seq), F32),
        ],
        compiler_params=pltpu.CompilerParams(
            dimension_semantics=("parallel", "parallel"), vmem_limit_bytes=VMEM_LIMIT),
        name="inproj",
    )(x, mod3, norm_g.reshape(1, dm), wqkv, wut)


def _attn_kernel(lam_ref, q_ref, k_ref, v_ref, g_ref, o_ref, m_sc, l_sc, acc_sc, *, tq):
    i = pl.program_id(2)
    th = tq // 2
    q = q_ref[...]
    lane = lax.broadcasted_iota(jnp.int32, q.shape, 1)
    zero = jnp.zeros_like(q)
    qa = jnp.where(lane < ATTN_HEAD_DIM, q, zero)
    qb = jnp.where(lane >= ATTN_HEAD_DIM, q, zero)
    q2 = jnp.concatenate([qa[:th], qb[:th], qa[th:], qb[th:]], axis=0)
    m_sc[...] = jnp.full(m_sc.shape, NEG_BIG, F32)
    l_sc[...] = jnp.zeros(l_sc.shape, F32)
    acc_sc[...] = jnp.zeros(acc_sc.shape, F32)

    def step(j, r0, nr, nc, qpos0):
        start = pl.multiple_of(j * tq, tq)
        kt = k_ref[pl.ds(start, nc), :]
        vt = v_ref[pl.ds(start, nc), :]
        s = _dot_nt(q2[r0:r0 + nr], kt)
        if qpos0 is not None:
            r = lax.broadcasted_iota(jnp.int32, s.shape, 0)
            c = lax.broadcasted_iota(jnp.int32, s.shape, 1)
            s = jnp.where(c <= qpos0 + (r & (th - 1)), s, NEG_BIG)
        m_prev = m_sc[r0:r0 + nr, :]
        m_new = jnp.maximum(m_prev, jnp.max(s, axis=-1, keepdims=True))
        alpha = jnp.exp2(m_prev - m_new)
        p = jnp.exp2(s - jnp.concatenate([m_new] * (nc // LANES), axis=1))
        psum = p[:, :LANES]
        for c0 in range(LANES, nc, LANES):
            psum = psum + p[:, c0:c0 + LANES]
        l_sc[r0:r0 + nr, :] = alpha * l_sc[r0:r0 + nr, :] + psum
        acc_sc[r0:r0 + nr, :] = alpha * acc_sc[r0:r0 + nr, :] + _dot(p.astype(BF16), vt)
        m_sc[r0:r0 + nr, :] = m_new

    def body(jj, carry):
        step(2 * jj, 0, 2 * tq, tq, None)
        step(2 * jj + 1, 0, 2 * tq, tq, None)
        return carry

    lax.fori_loop(0, i // 2, body, 0)

    @pl.when(i % 2 == 1)
    def _():
        step(i - 1, 0, 2 * tq, tq, None)

    step(i, 0, tq, th, 0)
    step(i, tq, tq, tq, th)

    o_all = acc_sc[...] / jnp.sum(l_sc[...], axis=-1, keepdims=True)
    o0 = jnp.concatenate([o_all[:th], o_all[tq:tq + th]], axis=0)
    o1 = jnp.concatenate([o_all[th:tq], o_all[tq + th:]], axis=0)
    o = o0 - lam_ref[0] * o1
    ms = jnp.mean(o * o, axis=-1, keepdims=True)
    o = o * lax.rsqrt(ms + SUBLN_EPS) * g_ref[...] * (1.0 - LAM_INIT)
    o_ref[...] = o.astype(o_ref.dtype)


def _attn_call(lam, q, k, v, subln_g, tq=1024):
    bsz, seq, aw = q.shape
    nh = aw // V_HEAD_DIM
    qmap = lambda b, h, i: (b, i, h)
    kvmap = lambda b, h, i: (b, 0, h)
    return pl.pallas_call(
        functools.partial(_attn_kernel, tq=tq),
        grid=(bsz, nh, seq // tq),
        in_specs=[
            pl.BlockSpec(memory_space=pltpu.SMEM),
            pl.BlockSpec((None, tq, V_HEAD_DIM), qmap),
            pl.BlockSpec((None, seq, V_HEAD_DIM), kvmap),
            pl.BlockSpec((None, seq, V_HEAD_DIM), kvmap),
            pl.BlockSpec((1, V_HEAD_DIM), lambda b, h, i: (0, 0)),
        ],
        out_specs=pl.BlockSpec((None, tq, V_HEAD_DIM), qmap),
        out_shape=jax.ShapeDtypeStruct((bsz, seq, aw), BF16),
        scratch_shapes=[pltpu.VMEM((2 * tq, V_HEAD_DIM), F32)] * 3,
        compiler_params=pltpu.CompilerParams(
            dimension_semantics=("parallel", "parallel", "parallel"), vmem_limit_bytes=VMEM_LIMIT),
        name="attn",
    )(lam, q, k, v, subln_g.reshape(1, V_HEAD_DIM))


def _ssm_operators(a_re, a_im, log_step, b_re, b_im, c_re, c_im):
    t = LANES
    hi = lax.Precision.HIGHEST
    lam = lax.complex(jnp.minimum(a_re, -1e-4), a_im)
    delta = jnp.exp(log_step)[:, None]
    lam_bar = jnp.exp(lam * delta)
    bbar = ((lam_bar - 1.0) / lam)[:, :, None] * lax.complex(b_re, b_im)
    cmat = lax.complex(c_re, c_im)
    ld = lam * delta
    tau = jnp.arange(t + 1, dtype=F32)
    pw = jnp.exp(ld[:, :, None] * tau)
    cb = cmat[:, None, :, :] * jnp.swapaxes(bbar, 1, 2)[:, :, None, :]
    g = a_re.shape[0]
    cb = cb.reshape(g, SSM_GROUP * SSM_GROUP, SSM_STATE)
    cbcat = jnp.concatenate([cb.real, -cb.imag], axis=-1)
    pcat = jnp.concatenate([pw.real[:, :, :t], pw.imag[:, :, :t]], axis=1)
    kmat = jnp.einsum('gxp,gpt->gxt', cbcat, pcat, precision=hi)
    prev = jnp.swapaxes(pw[:, :, t - 1::-1], 1, 2)
    arev = jnp.concatenate([prev.real, prev.imag], axis=-1)
    bt = jnp.swapaxes(bbar, 1, 2)
    brow = jnp.stack([jnp.concatenate([bt.real, bt.real], axis=-1),
                      jnp.concatenate([-bt.imag, bt.imag], axis=-1)], axis=2).reshape(g, 2 * SSM_GROUP, 2 * SSM_STATE)
    a1 = jnp.concatenate([pw.real[:, :, 1:], pw.imag[:, :, 1:]], axis=1)
    ct = jnp.swapaxes(cmat, 1, 2)
    ccol = jnp.stack([jnp.concatenate([ct.real, -ct.real], axis=1),
                      jnp.concatenate([-ct.imag, -ct.imag], axis=1)], axis=-1).reshape(g, 2 * SSM_STATE, 2 * SSM_GROUP)
    rows = []
    for i in range(6):
        d = jnp.exp(ld * float(t * (1 << i)))
        rows.append(jnp.concatenate([d.real, d.real], axis=-1))
        rows.append(jnp.concatenate([-d.imag, d.imag], axis=-1))
    rows += [jnp.zeros_like(rows[0])] * 4
    dpow = jnp.stack(rows, axis=1)
    return kmat, arev, brow, a1, ccol, dpow


def _ssm_kernel(u_ref, k_ref, arev_ref, brow_ref, a1_ref, ccol_ref, dp_ref, y_ref, m_sc, ws_sc, wc_sc):
    bsz, nch, n_chunk, t = u_ref.shape
    row = lax.broadcasted_iota(jnp.int32, (t, t), 0)
    col = lax.broadcasted_iota(jnp.int32, (t, t), 1)
    causal = col >= row
    half = arev_ref.shape[1] // 2

    arev = arev_ref[...]
    arev_sw = pltpu.roll(arev, half, 1)
    a1 = a1_ref[...]
    a1_sw = pltpu.roll(a1, half, 0)
    for c in range(nch):
        ws_sc[c * t:(c + 1) * t, :] = (arev * brow_ref[2 * c:2 * c + 1, :]
                                       + arev_sw * brow_ref[2 * c + 1:2 * c + 2, :]).astype(BF16)
        wc_sc[:, c * t:(c + 1) * t] = (a1 * ccol_ref[:, 2 * c:2 * c + 1]
                                       + a1_sw * ccol_ref[:, 2 * c + 1:2 * c + 2]).astype(BF16)

    uflat = jnp.concatenate(
        [jnp.concatenate([u_ref[b, ci] for b in range(bsz)], axis=0) for ci in range(nch)],
        axis=1).astype(BF16)

    y = None
    for c0 in range(0, nch, 2):
        for ci in (c0, c0 + 1):
            for co in range(nch):
                kb = jnp.broadcast_to(k_ref[ci * nch + co:ci * nch + co + 1, :], (t, t))
                kb = pltpu.roll(kb, 0, 1, stride=1, stride_axis=0)
                m_sc[ci * t:(ci + 1) * t, co * t:(co + 1) * t] = jnp.where(causal, kb, 0.0).astype(BF16)
        part = _dot(uflat[:, c0 * t:(c0 + 2) * t], m_sc[c0 * t:(c0 + 2) * t, :])
        y = part if y is None else y + part
    z = _dot(uflat, ws_sc[...])
    kidx = lax.broadcasted_iota(jnp.int32, z.shape, 0) & (n_chunk - 1)
    shift = 1
    i = 0
    while shift < n_chunk:
        zs = jnp.where(kidx >= shift, pltpu.roll(z, shift, 0), 0.0)
        z = z + zs * dp_ref[2 * i:2 * i + 1, :] + pltpu.roll(zs, half, 1) * dp_ref[2 * i + 1:2 * i + 2, :]
        shift *= 2
        i += 1
    xin = jnp.where(kidx >= 1, pltpu.roll(z, 1, 0), 0.0)
    xh, xl = _split_bf16(xin)
    wc = wc_sc[...]
    y = y + _dot(xh, wc) + _dot(xl, wc)
    for b in range(bsz):
        for co in range(nch):
            y_ref[b, co] = y[b * n_chunk:(b + 1) * n_chunk, co * t:(co + 1) * t]


def _ssm_call(ut, kmat, arev, brow, a1, ccol, dpow):
    bsz, sw, seq = ut.shape
    n_groups = sw // SSM_GROUP
    n_chunk = seq // LANES
    assert n_chunk & (n_chunk - 1) == 0 and n_chunk <= 64
    u4 = ut.reshape(bsz, sw, n_chunk, LANES)
    blk = (bsz, SSM_GROUP, n_chunk, LANES)
    gmap = lambda g: (0, g, 0, 0)
    pmap = lambda g: (g, 0, 0)
    y4 = pl.pallas_call(
        _ssm_kernel,
        grid=(n_groups,),
        in_specs=[
            pl.BlockSpec(blk, gmap),
            pl.BlockSpec((None,) + kmat.shape[1:], pmap),
            pl.BlockSpec((None,) + arev.shape[1:], pmap),
            pl.BlockSpec((None,) + brow.shape[1:], pmap),
            pl.BlockSpec((None,) + a1.shape[1:], pmap),
            pl.BlockSpec((None,) + ccol.shape[1:], pmap),
            pl.BlockSpec((None,) + dpow.shape[1:], pmap),
        ],
        out_specs=pl.BlockSpec(blk, gmap),
        out_shape=jax.ShapeDtypeStruct(u4.shape, F32),
        scratch_shapes=[pltpu.VMEM((SSM_GROUP * LANES, SSM_GROUP * LANES), BF16),
                        pltpu.VMEM((SSM_GROUP * LANES, 2 * SSM_STATE), BF16),
                        pltpu.VMEM((2 * SSM_STATE, SSM_GROUP * LANES), BF16)],
        compiler_params=pltpu.CompilerParams(
            dimension_semantics=("parallel",), vmem_limit_bytes=VMEM_LIMIT),
        name="ssm",
    )(u4, kmat, arev, brow, a1, ccol, dpow)
    return y4


def _mid_kernel(x_ref, attn_ref, yt_ref, ut_ref, mod_ref, dsk_ref, wglut_ref, bglu_ref, wo1_ref, wo2_ref,
                g2_ref, wrt_ref, x1_ref, h2_ref, lg_ref):
    yt = jnp.concatenate([yt_ref[:, j, :] for j in range(yt_ref.shape[1])], axis=1)
    gt = _gelu_tanh(yt + dsk_ref[...] * ut_ref[...])
    zt = _dot(wglut_ref[...], gt.astype(BF16)) + bglu_ref[...]
    st = gt * jax.nn.sigmoid(zt)
    s = st.T.astype(BF16)
    mix = _dot(attn_ref[...], wo1_ref[...]) + _dot(s, wo2_ref[...])
    x1 = x_ref[...] + mod_ref[2:3, :] * mix
    ms = jnp.mean(x1 * x1, axis=-1, keepdims=True)
    h2 = x1 * lax.rsqrt(ms + NORM_EPS) * g2_ref[...]
    h2 = h2 * (1.0 + mod_ref[4:5, :]) + mod_ref[3:4, :]
    x1_ref[...] = x1
    h2_ref[0], h2_ref[1] = _pack_rows(h2)
    lg_ref[...] = jax.nn.sigmoid(_dot3_nt(wrt_ref[...], h2))


def _mid_call(x, attn, y4, ut, mod3, dsk, wglut, bglu, wo1, wo2, g2, wrt, tm=1024):
    bsz, seq, dm = x.shape
    aw = attn.shape[2]
    sw = y4.shape[1]
    ne = wrt.shape[0]
    row = lambda b, i: (b, i, 0)
    colm = lambda b, i: (b, 0, i)
    full = lambda a: pl.BlockSpec(a.shape, lambda b, i: (0,) * a.ndim)
    dsk = dsk.reshape(sw, 1)
    bglu = bglu.reshape(sw, 1)
    g2 = g2.reshape(1, dm)
    return pl.pallas_call(
        _mid_kernel,
        grid=(bsz, seq // tm),
        in_specs=[
            pl.BlockSpec((None, tm, dm), row),
            pl.BlockSpec((None, tm, aw), row),
            pl.BlockSpec((None, sw, tm // LANES, LANES), lambda b, i: (b, 0, i, 0)),
            pl.BlockSpec((None, sw, tm), colm),
            pl.BlockSpec((None, N_ADA, dm), lambda b, i: (b, 0, 0)),
            full(dsk), full(wglut), full(bglu), full(wo1), full(wo2), full(g2), full(wrt),
        ],
        out_specs=[
            pl.BlockSpec((None, tm, dm), row),
            pl.BlockSpec((2, None, tm, dm // 4), lambda b, i: (0, b, i, 0)),
            pl.BlockSpec((None, ne, tm), colm),
        ],
        out_shape=[
            jax.ShapeDtypeStruct((bsz, seq, dm), F32),
            jax.ShapeDtypeStruct((2, bsz, seq, dm // 4), jnp.uint32),
            jax.ShapeDtypeStruct((bsz, ne, seq), F32),
        ],
        compiler_params=pltpu.CompilerParams(
            dimension_semantics=("parallel", "parallel"), vmem_limit_bytes=VMEM_LIMIT),
        name="mid",
    )(x, attn, y4, ut, mod3, dsk, wglut, bglu, wo1, wo2, g2, wrt)


def _moe_kernel(cs_ref, cnt_ref, nv_ref, xs_hbm, wg_ref, wu_ref, wd_ref, out_hbm, xbuf, obuf, isem, osem):
    step = pl.program_id(0)
    n_steps = pl.num_programs(0)
    per_step = wg_ref.shape[0]
    c_end = cs_ref[n_steps * per_step]
    n_slots, _, ch, q = xbuf.shape

    n_parts = MOE_COPY_PARTS
    pr = ch // n_parts
    row_queue = 1

    def pieces(c, slot, j, kind):
        rows = pl.ds(j * pr, pr)
        hbm_rows = pl.ds(c * ch + j * pr, pr)
        if kind == "fetch":
            return pltpu.make_async_copy(xs_hbm.at[:, hbm_rows, :], xbuf.at[slot, :, rows, :], isem.at[slot])
        return pltpu.make_async_copy(obuf.at[slot, :, rows, :], out_hbm.at[:, hbm_rows, :], osem.at[slot])

    def copy_op(c, slot, kind, op):
        for j in range(n_parts):
            @pl.when(nv_ref[c] > j * pr)
            def _(j=j):
                cp = pieces(c, slot, j, kind)
                cp.start(priority=row_queue) if op == "start" else cp.wait()

    look = n_slots - 2

    @pl.when(step == 0)
    def _():
        for c in range(look):
            @pl.when(c < c_end)
            def _():
                copy_op(c, c, "fetch", "start")

    def arrive(c, n):
        for i in range(n):
            copy_op(c + i, lax.rem(c + i, n_slots), "fetch", "wait")
        for i in range(n):
            ahead = c + look + i

            @pl.when(ahead < c_end)
            def _():
                copy_op(ahead, lax.rem(ahead, n_slots), "fetch", "start")
        for i in range(n):
            @pl.when(c + i >= n_slots)
            def _():
                copy_op(c + i - n_slots, lax.rem(c + i, n_slots), "drain", "wait")

    def compute(c, n, n_valid, sub):
        slots = [lax.rem(c + i, n_slots) for i in range(n)]
        valid = lax.broadcasted_iota(jnp.int32, (n * ch, q), 0) < n_valid
        gate = up = None
        for h in range(2):
            lo, hi = _unpack_plane(jnp.concatenate([xbuf[s, h] for s in slots], axis=0))
            lo = jnp.where(valid, lo, 0.0).astype(BF16)
            hi = jnp.where(valid, hi, 0.0).astype(BF16)
            for xpart, k0 in ((lo, h * q), (hi, (2 + h) * q)):
                g = _dot(xpart, wg_ref[sub, k0:k0 + q, :].astype(BF16))
                u = _dot(xpart, wu_ref[sub, k0:k0 + q, :].astype(BF16))
                gate = g if gate is None else gate + g
                up = u if up is None else up + u
        hb = _silu(gate) * up
        p0, p1 = _pack_rows(_dot(hb.astype(BF16), wd_ref[sub].astype(BF16)))
        for i, s in enumerate(slots):
            obuf[s, 0] = p0[i * ch:(i + 1) * ch]
            obuf[s, 1] = p1[i * ch:(i + 1) * ch]

    def depart(c, n):
        for i in range(n):
            copy_op(c + i, lax.rem(c + i, n_slots), "drain", "start")

    def expert(sub):
        e = step * per_step + sub
        c0, c1 = cs_ref[e], cs_ref[e + 1]

        @pl.when(c1 > c0)
        def _():
            n_pairs = (c1 - c0) // 2

            def pair(i, carry):
                c = c0 + 2 * i
                arrive(c, 2)
                compute(c, 2, cnt_ref[e] - (c - c0) * ch, sub)
                depart(c, 2)
                return carry

            lax.fori_loop(0, n_pairs, pair, 0)

            @pl.when(c0 + 2 * n_pairs < c1)
            def _():
                arrive(c1 - 1, 1)
                compute(c1 - 1, 1, cnt_ref[e] - (c1 - 1 - c0) * ch, sub)
                depart(c1 - 1, 1)

    for sub in range(per_step):
        expert(sub)

    @pl.when(step == n_steps - 1)
    def _():
        for back in range(n_slots, 0, -1):
            @pl.when(c_end >= back)
            def _():
                copy_op(c_end - back, lax.rem(c_end - back, n_slots), "drain", "wait")


def _moe_call(chunk_start, counts, chunk_valid, xs, w_gate, w_up, w_down):
    n_rows = xs.shape[1]
    n_exp, dm, de = w_gate.shape
    grid_spec = pltpu.PrefetchScalarGridSpec(
        num_scalar_prefetch=3,
        grid=(n_exp // MOE_EXPERTS_PER_STEP,),
        in_specs=[
            pl.BlockSpec(memory_space=pl.ANY),
            pl.BlockSpec((MOE_EXPERTS_PER_STEP, dm, de), lambda s, cs, cnt, nv: (s, 0, 0)),
            pl.BlockSpec((MOE_EXPERTS_PER_STEP, dm, de), lambda s, cs, cnt, nv: (s, 0, 0)),
            pl.BlockSpec((MOE_EXPERTS_PER_STEP, de, dm), lambda s, cs, cnt, nv: (s, 0, 0)),
        ],
        out_specs=pl.BlockSpec(memory_space=pl.ANY),
        scratch_shapes=[
            pltpu.VMEM((MOE_SLOTS, 2, MOE_BLOCK, dm // 4), jnp.uint32),
            pltpu.VMEM((MOE_SLOTS, 2, MOE_BLOCK, dm // 4), jnp.uint32),
            pltpu.SemaphoreType.DMA((MOE_SLOTS,)), pltpu.SemaphoreType.DMA((MOE_SLOTS,)),
        ],
    )
    return pl.pallas_call(
        _moe_kernel,
        grid_spec=grid_spec,
        out_shape=jax.ShapeDtypeStruct(xs.shape, jnp.uint32),
        compiler_params=pltpu.CompilerParams(
            dimension_semantics=("arbitrary",), vmem_limit_bytes=VMEM_LIMIT),
        name="moe",
    )(chunk_start, counts, chunk_valid, xs, w_gate, w_up, w_down)


def _final_kernel(x1_ref, h2_ref, r_ref, w_ref, mod_ref, g_ref, wsg_ref, wsu_ref, wsd_ref, o_ref):
    w = w_ref[...]
    groups = [None] * 4
    hparts = [None] * 4
    for h in range(2):
        hparts[h], hparts[2 + h] = _unpack_plane(h2_ref[h])
        for k in range(r_ref.shape[1]):
            lo, hi = _unpack_plane(r_ref[h, k])
            wk = w[:, k:k + 1]
            groups[h] = wk * lo if groups[h] is None else groups[h] + wk * lo
            groups[2 + h] = wk * hi if groups[2 + h] is None else groups[2 + h] + wk * hi
    routed = jnp.concatenate(groups, axis=1)
    hb = jnp.concatenate(hparts, axis=1).astype(BF16)
    sh = _silu(_dot(hb, wsg_ref[...])) * _dot(hb, wsu_ref[...])
    shared = _dot(sh.astype(BF16), wsd_ref[...])
    x = x1_ref[...] + mod_ref[5:6, :] * (routed + shared)
    ms = jnp.mean(x * x, axis=-1, keepdims=True)
    o_ref[...] = x * lax.rsqrt(ms + NORM_EPS) * g_ref[...]


def _final_call(x1, h2, rows, w, mod3, final_g, wsg, wsu, wsd, tm=256):
    bsz, seq, dm = x1.shape
    kk, q = rows.shape[1], rows.shape[3]
    nt = seq // tm
    row = lambda b, i: (b, i, 0)
    full = lambda a: pl.BlockSpec(a.shape, lambda b, i: (0,) * a.ndim)
    return pl.pallas_call(
        _final_kernel,
        grid=(bsz, nt),
        in_specs=[
            pl.BlockSpec((None, tm, dm), row),
            pl.BlockSpec((2, None, tm, q), lambda b, i: (0, b, i, 0)),
            pl.BlockSpec((2, kk, tm, q), lambda b, i: (0, 0, b * nt + i, 0)),
            pl.BlockSpec((None, tm, kk), row),
            pl.BlockSpec((None, N_ADA, dm), lambda b, i: (b, 0, 0)),
            pl.BlockSpec((1, dm), lambda b, i: (0, 0)),
            full(wsg), full(wsu), full(wsd),
        ],
        out_specs=pl.BlockSpec((None, tm, dm), row),
        out_shape=jax.ShapeDtypeStruct((bsz, seq, dm), F32),
        compiler_params=pltpu.CompilerParams(
            dimension_semantics=("parallel", "parallel"), vmem_limit_bytes=VMEM_LIMIT),
        name="final",
    )(x1, h2, rows, w, mod3, final_g.reshape(1, dm), wsg, wsu, wsd)


def _route_kernel(sc_ref, bias_ref, idx_ref, w_ref, rank_ref, cnt_ref, tri_sc, carry_sc):
    ne, tn = sc_ref.shape
    gsz = ne // N_EXPERT_GROUPS
    neg = -jnp.inf
    first = jnp.logical_and(pl.program_id(0) == 0, pl.program_id(1) == 0)

    @pl.when(first)
    def _():
        r = lax.broadcasted_iota(jnp.int32, (tn, tn), 0)
        c = lax.broadcasted_iota(jnp.int32, (tn, tn), 1)
        tri_sc[...] = jnp.where(r < c, 1.0, 0.0).astype(BF16)
        carry_sc[...] = jnp.zeros(carry_sc.shape, F32)

    s = sc_ref[...]
    sel = s + bias_ref[...]
    gs = []
    for g in range(N_EXPERT_GROUPS):
        blk = sel[g * gsz:(g + 1) * gsz]
        m1 = jnp.max(blk, axis=0, keepdims=True)
        eq = blk == m1
        n_eq = jnp.sum(jnp.where(eq, 1.0, 0.0), axis=0, keepdims=True)
        m2 = jnp.max(jnp.where(eq, neg, blk), axis=0, keepdims=True)
        gs.append(m1 + jnp.where(n_eq >= 2.0, m1, m2))
    gs = jnp.concatenate(gs, axis=0)
    gi = lax.broadcasted_iota(jnp.int32, gs.shape, 0)
    beaten = jnp.zeros(gs.shape, F32)
    for gp in range(N_EXPERT_GROUPS):
        other = gs[gp:gp + 1]
        wins = jnp.where(other > gs, 1.0, jnp.where(other == gs, jnp.where(gi > gp, 1.0, 0.0), 0.0))
        beaten = beaten + wins
    gadd = jnp.where(beaten < float(TOPK_GROUPS), 0.0, neg)
    cur = jnp.concatenate(
        [sel[g * gsz:(g + 1) * gsz] + gadd[g:g + 1] for g in range(N_EXPERT_GROUPS)], axis=0)

    eidx = lax.broadcasted_iota(jnp.int32, (ne, tn), 0).astype(F32)
    cur0 = cur
    idx_rows, s_rows = [], []
    for _ in range(TOP_K):
        m = jnp.max(cur, axis=0, keepdims=True)
        ik = jnp.min(jnp.where(cur == m, eidx, float(ne)), axis=0, keepdims=True)
        oh = eidx == ik
        s_rows.append(jnp.sum(jnp.where(oh, s, 0.0), axis=0, keepdims=True))
        cur = jnp.where(oh, neg, cur)
        idx_rows.append(ik)
    chosen = jnp.where(cur == neg, jnp.where(cur0 > neg, 1.0, 0.0), 0.0)
    sk = jnp.concatenate(s_rows, axis=0)
    w_ref[...] = sk / jnp.sum(sk, axis=0, keepdims=True) * ROUTED_SCALE
    idx_ref[...] = jnp.concatenate(idx_rows, axis=0).astype(jnp.int32)

    before = _dot(chosen.astype(BF16), tri_sc[...]) + carry_sc[...]
    rank_rows = [jnp.sum(jnp.where(eidx == ik, before, 0.0), axis=0, keepdims=True) for ik in idx_rows]
    rank_ref[...] = jnp.concatenate(rank_rows, axis=0).astype(jnp.int32)
    carry_sc[...] = carry_sc[...] + jnp.sum(chosen, axis=1, keepdims=True)
    cnt_ref[...] = carry_sc[...]


def _route_call(scores_t, router_bias, tn=512):
    bsz, ne, seq = scores_t.shape
    tok = lambda b, i: (b, 0, i)
    return pl.pallas_call(
        _route_kernel,
        grid=(bsz, seq // tn),
        in_specs=[
            pl.BlockSpec((None, ne, tn), tok),
            pl.BlockSpec((ne, 1), lambda b, i: (0, 0)),
        ],
        out_specs=[
            pl.BlockSpec((None, TOP_K, tn), tok),
            pl.BlockSpec((None, TOP_K, tn), tok),
            pl.BlockSpec((None, TOP_K, tn), tok),
            pl.BlockSpec((ne, 1), lambda b, i: (0, 0)),
        ],
        out_shape=[
            jax.ShapeDtypeStruct((bsz, TOP_K, seq), jnp.int32),
            jax.ShapeDtypeStruct((bsz, TOP_K, seq), F32),
            jax.ShapeDtypeStruct((bsz, TOP_K, seq), jnp.int32),
            jax.ShapeDtypeStruct((ne, 1), F32),
        ],
        scratch_shapes=[pltpu.VMEM((tn, tn), BF16), pltpu.VMEM((ne, 1), F32)],
        compiler_params=pltpu.CompilerParams(
            dimension_semantics=("arbitrary", "arbitrary"), vmem_limit_bytes=VMEM_LIMIT),
        name="route",
    )(scores_t, router_bias.reshape(ne, 1))


def _dispatch_tables(idx, rank, counts):
    ne = counts.shape[0]
    n_tok = idx.shape[0] * idx.shape[2]
    counts = counts.astype(jnp.int32)
    pcounts = (counts + MOE_BLOCK - 1) // MOE_BLOCK * MOE_BLOCK
    pstarts = jnp.concatenate([jnp.zeros((1,), jnp.int32), jnp.cumsum(pcounts).astype(jnp.int32)])
    n_blocks = -(-(n_tok * TOP_K + ne * (MOE_BLOCK - 1)) // MOE_BLOCK)
    dest = _dest_call(pstarts[:ne], idx, rank)
    chunk_row = jnp.arange(n_blocks, dtype=jnp.int32)[:, None] * MOE_BLOCK
    owner = jnp.logical_and(pstarts[None, :ne] <= chunk_row, chunk_row < pstarts[None, 1:])
    left = jnp.sum(jnp.where(owner, counts[None, :] - (chunk_row - pstarts[None, :ne]), 0), axis=1)
    chunk_valid = jnp.clip(left, 0, MOE_BLOCK).astype(jnp.int32)
    return dest, pstarts // MOE_BLOCK, counts, chunk_valid, n_blocks


def _dest_kernel(ps_ref, idx_ref, rank_ref, o_ref):
    idx = idx_ref[...]

    def body(e, acc):
        return jnp.where(idx == e, ps_ref[e], acc)

    o_ref[...] = rank_ref[...] + lax.fori_loop(0, ps_ref.shape[0], body, jnp.zeros(idx.shape, jnp.int32))


def _dest_call(pstarts, idx, rank, tn=2048):
    bsz, kk, seq = idx.shape
    tok = lambda b, i, ps: (b, 0, i)
    grid_spec = pltpu.PrefetchScalarGridSpec(
        num_scalar_prefetch=1,
        grid=(bsz, seq // tn),
        in_specs=[pl.BlockSpec((None, kk, tn), tok), pl.BlockSpec((None, kk, tn), tok)],
        out_specs=pl.BlockSpec((None, kk, tn), tok),
    )
    return pl.pallas_call(
        _dest_kernel,
        grid_spec=grid_spec,
        out_shape=jax.ShapeDtypeStruct(idx.shape, jnp.int32),
        compiler_params=pltpu.CompilerParams(dimension_semantics=("parallel", "parallel")),
        name="dest",
    )(pstarts, idx, rank)


def _scatter_rows(rows, dest, n_out):
    n_tok, width = rows.shape
    k_slots = dest.shape[0]
    win = LANES
    mesh = plsc.VectorSubcoreMesh(core_axis_name="c", subcore_axis_name="s")

    @pl.kernel(out_type=jax.ShapeDtypeStruct((n_out, width), rows.dtype), mesh=mesh, scratch_types=[])
    def scatter(rows_hbm, dest_hbm, out_hbm):
        def body(rows_vmem, idx_vmem):
            pltpu.sync_copy(rows_vmem, out_hbm.at[idx_vmem.at[0]])

        pltpu.emit_pipeline(
            body,
            grid=(n_tok // win, k_slots),
            in_specs=[
                pl.BlockSpec((win, width), lambda i, k: (i, 0)),
                pl.BlockSpec((1, win), lambda i, k: (k, i)),
            ],
            out_specs=[],
            core_axis_name=("c", "s"),
            dimension_semantics=(pltpu.PARALLEL, pltpu.ARBITRARY),
        )(rows_hbm, dest_hbm)

    return scatter(rows, dest)


def _gather_rows(rows, idx):
    n = idx.shape[0]
    width = rows.shape[1]
    win = LANES
    mesh = plsc.VectorSubcoreMesh(core_axis_name="c", subcore_axis_name="s")

    @pl.kernel(out_type=jax.ShapeDtypeStruct((n, width), rows.dtype), mesh=mesh, scratch_types=[])
    def gather(rows_hbm, idx_hbm, out_hbm):
        def body(idx_vmem, out_vmem):
            pltpu.sync_copy(rows_hbm.at[idx_vmem.at[0]], out_vmem)

        pltpu.emit_pipeline(
            body,
            grid=(n // win,),
            in_specs=[pl.BlockSpec((1, win), lambda i: (0, i))],
            out_specs=[pl.BlockSpec((win, width), lambda i: (i, 0))],
            core_axis_name=("c", "s"),
            dimension_semantics=(pltpu.PARALLEL,),
        )(idx_hbm, out_hbm)

    return gather(rows, idx.reshape(1, n))


def kernel(x, c, w_ada, b_ada, norm1_g, w_in, w_out, lambda_q1, lambda_k1, lambda_q2, lambda_k2, subln_g,
           ssm_a_re, ssm_a_im, ssm_log_step, ssm_b_re, ssm_b_im, ssm_c_re, ssm_c_im, ssm_d, w_glu, b_glu,
           norm2_g, w_router, router_bias, w_gate, w_up, w_down, ws_gate, ws_up, ws_down, final_g):
    bsz, seq, dm = x.shape
    n_tok = bsz * seq
    aw = N_ATTN_HEADS * V_HEAD_DIM

    mod3 = _mod_call(c, w_ada[0], b_ada[0]).reshape(bsz, N_ADA, dm)

    wqkv = w_in[0][:, :3 * aw].astype(BF16)
    wut = w_in[0][:, 3 * aw:].T.astype(BF16)
    q, k, v, ut = _inproj_call(x, mod3, norm1_g[0], wqkv, wut)

    lam = (jnp.exp(jnp.sum(lambda_q1[0] * lambda_k1[0])) - jnp.exp(jnp.sum(lambda_q2[0] * lambda_k2[0]))
           + LAM_INIT).reshape(1)
    attn = _attn_call(lam, q, k, v, subln_g[0])

    yt = _ssm_call(ut, *_ssm_operators(ssm_a_re[0], ssm_a_im[0], ssm_log_step[0], ssm_b_re[0], ssm_b_im[0],
                                       ssm_c_re[0], ssm_c_im[0]))

    x1, h2, scores_t = _mid_call(
        x, attn, yt, ut, mod3, ssm_d[0], w_glu[0].T.astype(BF16), b_glu[0],
        w_out[0][:aw].astype(BF16), w_out[0][aw:].astype(BF16), norm2_g[0], w_router[0].T)

    idx, w, rank, counts = _route_call(scores_t, router_bias[0])
    dest, chunk_start, counts, chunk_valid, n_blocks = _dispatch_tables(idx, rank, counts.reshape(-1))
    n_rows = n_blocks * MOE_BLOCK

    dest_k = jnp.swapaxes(dest, 0, 1).reshape(TOP_K, n_tok)
    dest_half = jnp.concatenate([dest_k, dest_k + n_rows], axis=1)
    xs = _scatter_rows(h2.reshape(2 * n_tok, dm // 4), dest_half, 2 * n_rows).reshape(2, n_rows, dm // 4)

    out = _moe_call(chunk_start, counts, chunk_valid, xs, w_gate[0], w_up[0], w_down[0])
    src = jnp.concatenate([dest_k.reshape(-1), (dest_k + n_rows).reshape(-1)])
    rows = _gather_rows(out.reshape(2 * n_rows, dm // 4), src)
    return _final_call(x1, h2, rows.reshape(2, TOP_K, n_tok, dm // 4), jnp.swapaxes(w, 1, 2), mod3, final_g,
                       ws_gate[0].astype(BF16), ws_up[0].astype(BF16), ws_down[0].astype(BF16))
```

```python
import functools
import math

import jax
import jax.numpy as jnp
from jax import lax
from jax.experimental import pallas as pl
from jax.experimental.pallas import tpu as pltpu
from jax.experimental.pallas import tpu_sc as plsc

F32 = jnp.float32
BF16 = jnp.bfloat16

N_ATTN_HEADS = 4
ATTN_HEAD_DIM = 64
V_HEAD_DIM = 128
SSM_GROUP = 16
N_SSM_GROUPS = 32
SSM_STATE = 64
N_EXPERTS = 256
TOP_K = 8
N_EXPERT_GROUPS = 8
TOPK_GROUPS = 4
ROUTED_SCALE = 2.5
NORM_EPS = 1e-6
SUBLN_EPS = 1e-5
N_ADA = 6
LAM_INIT = 0.8 - 0.6 * math.exp(-0.3 * 0)

LANES = 128
MOE_BLOCK = 256
MOE_SLOTS = 12
MOE_EXPERTS_PER_STEP = 2
MOE_COPY_PARTS = 4
NEG_BIG = -1e30
LOG2_E = math.log2(math.e)
VMEM_LIMIT = 48 * 1024 * 1024


def _split_bf16(a):
    hi = a.astype(BF16)
    lo = (a - hi.astype(F32)).astype(BF16)
    return hi, lo


def _dot(a, b):
    return jnp.dot(a, b, preferred_element_type=F32)


def _dot_nt(a, b):
    return lax.dot_general(a, b, (((1,), (1,)), ((), ())), preferred_element_type=F32)


def _dot3(a, b):
    ah, al = _split_bf16(a)
    bh, bl = _split_bf16(b)
    return _dot(ah, bh) + _dot(ah, bl) + _dot(al, bh)


def _dot3_nt(a, b):
    ah, al = _split_bf16(a)
    bh, bl = _split_bf16(b)
    return _dot_nt(ah, bh) + _dot_nt(ah, bl) + _dot_nt(al, bh)


def _silu(x):
    return x * jax.nn.sigmoid(x)


def _gelu_tanh(x):
    c = math.sqrt(2.0 / math.pi)
    return 0.5 * x * (1.0 + jnp.tanh(c * (x + 0.044715 * (x * x * x))))


def _pack_rows(x):
    bits = lax.bitcast_convert_type(x.astype(BF16).astype(F32), jnp.uint32)
    half = bits.shape[1] // 2
    packed = (bits[:, :half] >> 16) | (bits[:, half:] & jnp.uint32(0xFFFF0000))
    return packed[:, :half // 2], packed[:, half // 2:]


def _unpack_plane(xu):
    return (lax.bitcast_convert_type(xu << 16, F32),
            lax.bitcast_convert_type(xu & jnp.uint32(0xFFFF0000), F32))


def _mod_kernel(c_ref, w_ref, b_ref, o_ref):
    cond = _silu(c_ref[...])
    o_ref[...] = _dot3(cond, w_ref[...]) + b_ref[...]


def _mod_call(c, w_ada, b_ada):
    bsz, dm = c.shape
    n_out = w_ada.shape[1]
    tn = 1024
    return pl.pallas_call(
        _mod_kernel,
        grid=(n_out // tn,),
        in_specs=[
            pl.BlockSpec((bsz, dm), lambda j: (0, 0)),
            pl.BlockSpec((dm, tn), lambda j: (0, j)),
            pl.BlockSpec((1, tn), lambda j: (0, j)),
        ],
        out_specs=pl.BlockSpec((bsz, tn), lambda j: (0, j)),
        out_shape=jax.ShapeDtypeStruct((bsz, n_out), F32),
        compiler_params=pltpu.CompilerParams(vmem_limit_bytes=VMEM_LIMIT),
        name="mod",
    )(c, w_ada, b_ada.reshape(1, n_out))


def _inproj_kernel(x_ref, mod_ref, g_ref, wqkv_ref, wut_ref, q_ref, k_ref, v_ref, ut_ref, *, aw):
    x = x_ref[...]
    ms = jnp.mean(x * x, axis=-1, keepdims=True)
    h = x * lax.rsqrt(ms + NORM_EPS) * g_ref[...]
    h = h * (1.0 + mod_ref[1:2, :]) + mod_ref[0:1, :]
    hb = h.astype(BF16)
    qkv = _dot(hb, wqkv_ref[...])
    q_ref[...] = (qkv[:, :aw] * (LOG2_E * ATTN_HEAD_DIM ** -0.5)).astype(BF16)
    k_ref[...] = qkv[:, aw:2 * aw].astype(BF16)
    v_ref[...] = qkv[:, 2 * aw:].astype(BF16)
    ut_ref[...] = _dot_nt(wut_ref[...], hb)


def _inproj_call(x, mod3, norm_g, wqkv, wut, tm=512):
    bsz, seq, dm = x.shape
    aw = wqkv.shape[1] // 3
    sw = wut.shape[0]
    row = lambda b, i: (b, i, 0)
    return pl.pallas_call(
        functools.partial(_inproj_kernel, aw=aw),
        grid=(bsz, seq // tm),
        in_specs=[
            pl.BlockSpec((None, tm, dm), row),
            pl.BlockSpec((None, N_ADA, dm), lambda b, i: (b, 0, 0)),
            pl.BlockSpec((1, dm), lambda b, i: (0, 0)),
            pl.BlockSpec(wqkv.shape, lambda b, i: (0, 0)),
            pl.BlockSpec(wut.shape, lambda b, i: (0, 0)),
        ],
        out_specs=[
            pl.BlockSpec((None, tm, aw), row),
            pl.BlockSpec((None, tm, aw), row),
            pl.BlockSpec((None, tm, aw), row),
            pl.BlockSpec((None, sw, tm), lambda b, i: (b, 0, i)),
        ],
        out_shape=[
            jax.ShapeDtypeStruct((bsz, seq, aw), BF16),
            jax.ShapeDtypeStruct((bsz, seq, aw), BF16),
            jax.ShapeDtypeStruct((bsz, seq, aw), BF16),
            jax.ShapeDtypeStruct((bsz, sw, seq), F32),
        ],
        compiler_params=pltpu.CompilerParams(
            dimension_semantics=("parallel", "parallel"), vmem_limit_bytes=VMEM_LIMIT),
        name="inproj",
    )(x, mod3, norm_g.reshape(1, dm), wqkv, wut)


def _attn_kernel(lam_ref, q_ref, k_ref, v_ref, g_ref, o_ref, m_sc, l_sc, acc_sc, *, tq):
    i = pl.program_id(2)
    th = tq // 2
    q = q_ref[...]
    lane = lax.broadcasted_iota(jnp.int32, q.shape, 1)
    zero = jnp.zeros_like(q)
    qa = jnp.where(lane < ATTN_HEAD_DIM, q, zero)
    qb = jnp.where(lane >= ATTN_HEAD_DIM, q, zero)
    q2 = jnp.concatenate([qa[:th], qb[:th], qa[th:], qb[th:]], axis=0)
    m_sc[...] = jnp.full(m_sc.shape, NEG_BIG, F32)
    l_sc[...] = jnp.zeros(l_sc.shape, F32)
    acc_sc[...] = jnp.zeros(acc_sc.shape, F32)

    def step(j, r0, nr, nc, qpos0):
        start = pl.multiple_of(j * tq, tq)
        kt = k_ref[pl.ds(start, nc), :]
        vt = v_ref[pl.ds(start, nc), :]
        s = _dot_nt(q2[r0:r0 + nr], kt)
        if qpos0 is not None:
            r = lax.broadcasted_iota(jnp.int32, s.shape, 0)
            c = lax.broadcasted_iota(jnp.int32, s.shape, 1)
            s = jnp.where(c <= qpos0 + (r & (th - 1)), s, NEG_BIG)
        m_prev = m_sc[r0:r0 + nr, :]
        m_new = jnp.maximum(m_prev, jnp.max(s, axis=-1, keepdims=True))
        alpha = jnp.exp2(m_prev - m_new)
        p = jnp.exp2(s - jnp.concatenate([m_new] * (nc // LANES), axis=1))
        psum = p[:, :LANES]
        for c0 in range(LANES, nc, LANES):
            psum = psum + p[:, c0:c0 + LANES]
        l_sc[r0:r0 + nr, :] = alpha * l_sc[r0:r0 + nr, :] + psum
        acc_sc[r0:r0 + nr, :] = alpha * acc_sc[r0:r0 + nr, :] + _dot(p.astype(BF16), vt)
        m_sc[r0:r0 + nr, :] = m_new

    def body(jj, carry):
        step(2 * jj, 0, 2 * tq, tq, None)
        step(2 * jj + 1, 0, 2 * tq, tq, None)
        return carry

    lax.fori_loop(0, i // 2, body, 0)

    @pl.when(i % 2 == 1)
    def _():
        step(i - 1, 0, 2 * tq, tq, None)

    step(i, 0, tq, th, 0)
    step(i, tq, tq, tq, th)

    o_all = acc_sc[...] / jnp.sum(l_sc[...], axis=-1, keepdims=True)
    o0 = jnp.concatenate([o_all[:th], o_all[tq:tq + th]], axis=0)
    o1 = jnp.concatenate([o_all[th:tq], o_all[tq + th:]], axis=0)
    o = o0 - lam_ref[0] * o1
    ms = jnp.mean(o * o, axis=-1, keepdims=True)
    o = o * lax.rsqrt(ms + SUBLN_EPS) * g_ref[...] * (1.0 - LAM_INIT)
    o_ref[...] = o.astype(o_ref.dtype)


def _attn_call(lam, q, k, v, subln_g, tq=1024):
    bsz, seq, aw = q.shape
    nh = aw // V_HEAD_DIM
    qmap = lambda b, h, i: (b, i, h)
    kvmap = lambda b, h, i: (b, 0, h)
    return pl.pallas_call(
        functools.partial(_attn_kernel, tq=tq),
        grid=(bsz, nh, seq // tq),
        in_specs=[
            pl.BlockSpec(memory_space=pltpu.SMEM),
            pl.BlockSpec((None, tq, V_HEAD_DIM), qmap),
            pl.BlockSpec((None, seq, V_HEAD_DIM), kvmap),
            pl.BlockSpec((None, seq, V_HEAD_DIM), kvmap),
            pl.BlockSpec((1, V_HEAD_DIM), lambda b, h, i: (0, 0)),
        ],
        out_specs=pl.BlockSpec((None, tq, V_HEAD_DIM), qmap),
        out_shape=jax.ShapeDtypeStruct((bsz, seq, aw), BF16),
        scratch_shapes=[pltpu.VMEM((2 * tq, V_HEAD_DIM), F32)] * 3,
        compiler_params=pltpu.CompilerParams(
            dimension_semantics=("parallel", "parallel", "parallel"), vmem_limit_bytes=VMEM_LIMIT),
        name="attn",
    )(lam, q, k, v, subln_g.reshape(1, V_HEAD_DIM))


def _ssm_operators(a_re, a_im, log_step, b_re, b_im, c_re, c_im):
    t = LANES
    hi = lax.Precision.HIGHEST
    lam = lax.complex(jnp.minimum(a_re, -1e-4), a_im)
    delta = jnp.exp(log_step)[:, None]
    lam_bar = jnp.exp(lam * delta)
    bbar = ((lam_bar - 1.0) / lam)[:, :, None] * lax.complex(b_re, b_im)
    cmat = lax.complex(c_re, c_im)
    ld = lam * delta
    tau = jnp.arange(t + 1, dtype=F32)
    pw = jnp.exp(ld[:, :, None] * tau)
    cb = cmat[:, None, :, :] * jnp.swapaxes(bbar, 1, 2)[:, :, None, :]
    g = a_re.shape[0]
    cb = cb.reshape(g, SSM_GROUP * SSM_GROUP, SSM_STATE)
    cbcat = jnp.concatenate([cb.real, -cb.imag], axis=-1)
    pcat = jnp.concatenate([pw.real[:, :, :t], pw.imag[:, :, :t]], axis=1)
    kmat = jnp.einsum('gxp,gpt->gxt', cbcat, pcat, precision=hi)
    prev = jnp.swapaxes(pw[:, :, t - 1::-1], 1, 2)
    arev = jnp.concatenate([prev.real, prev.imag], axis=-1)
    bt = jnp.swapaxes(bbar, 1, 2)
    brow = jnp.stack([jnp.concatenate([bt.real, bt.real], axis=-1),
                      jnp.concatenate([-bt.imag, bt.imag], axis=-1)], axis=2).reshape(g, 2 * SSM_GROUP, 2 * SSM_STATE)
    a1 = jnp.concatenate([pw.real[:, :, 1:], pw.imag[:, :, 1:]], axis=1)
    ct = jnp.swapaxes(cmat, 1, 2)
    ccol = jnp.stack([jnp.concatenate([ct.real, -ct.real], axis=1),
                      jnp.concatenate([-ct.imag, -ct.imag], axis=1)], axis=-1).reshape(g, 2 * SSM_STATE, 2 * SSM_GROUP)
    rows = []
    for i in range(6):
        d = jnp.exp(ld * float(t * (1 << i)))
        rows.append(jnp.concatenate([d.real, d.real], axis=-1))
        rows.append(jnp.concatenate([-d.imag, d.imag], axis=-1))
    rows += [jnp.zeros_like(rows[0])] * 4
    dpow = jnp.stack(rows, axis=1)
    return kmat, arev, brow, a1, ccol, dpow


def _ssm_kernel(u_ref, k_ref, arev_ref, brow_ref, a1_ref, ccol_ref, dp_ref, y_ref, m_sc, ws_sc, wc_sc):
    bsz, nch, n_chunk, t = u_ref.shape
    row = lax.broadcasted_iota(jnp.int32, (t, t), 0)
    col = lax.broadcasted_iota(jnp.int32, (t, t), 1)
    causal = col >= row
    half = arev_ref.shape[1] // 2

    arev = arev_ref[...]
    arev_sw = pltpu.roll(arev, half, 1)
    a1 = a1_ref[...]
    a1_sw = pltpu.roll(a1, half, 0)
    for c in range(nch):
        ws_sc[c * t:(c + 1) * t, :] = (arev * brow_ref[2 * c:2 * c + 1, :]
                                       + arev_sw * brow_ref[2 * c + 1:2 * c + 2, :]).astype(BF16)
        wc_sc[:, c * t:(c + 1) * t] = (a1 * ccol_ref[:, 2 * c:2 * c + 1]
                                       + a1_sw * ccol_ref[:, 2 * c + 1:2 * c + 2]).astype(BF16)

    uflat = jnp.concatenate(
        [jnp.concatenate([u_ref[b, ci] for b in range(bsz)], axis=0) for ci in range(nch)],
        axis=1).astype(BF16)

    y = None
    for c0 in range(0, nch, 2):
        for ci in (c0, c0 + 1):
            for co in range(nch):
                kb = jnp.broadcast_to(k_ref[ci * nch + co:ci * nch + co + 1, :], (t, t))
                kb = pltpu.roll(kb, 0, 1, stride=1, stride_axis=0)
                m_sc[ci * t:(ci + 1) * t, co * t:(co + 1) * t] = jnp.where(causal, kb, 0.0).astype(BF16)
        part = _dot(uflat[:, c0 * t:(c0 + 2) * t], m_sc[c0 * t:(c0 + 2) * t, :])
        y = part if y is None else y + part
    z = _dot(uflat, ws_sc[...])
    kidx = lax.broadcasted_iota(jnp.int32, z.shape, 0) & (n_chunk - 1)
    shift = 1
    i = 0
    while shift < n_chunk:
        zs = jnp.where(kidx >= shift, pltpu.roll(z, shift, 0), 0.0)
        z = z + zs * dp_ref[2 * i:2 * i + 1, :] + pltpu.roll(zs, half, 1) * dp_ref[2 * i + 1:2 * i + 2, :]
        shift *= 2
        i += 1
    xin = jnp.where(kidx >= 1, pltpu.roll(z, 1, 0), 0.0)
    xh, xl = _split_bf16(xin)
    wc = wc_sc[...]
    y = y + _dot(xh, wc) + _dot(xl, wc)
    for b in range(bsz):
        for co in range(nch):
            y_ref[b, co] = y[b * n_chunk:(b + 1) * n_chunk, co * t:(co + 1) * t]


def _ssm_call(ut, kmat, arev, brow, a1, ccol, dpow):
    bsz, sw, seq = ut.shape
    n_groups = sw // SSM_GROUP
    n_chunk = seq // LANES
    assert n_chunk & (n_chunk - 1) == 0 and n_chunk <= 64
    u4 = ut.reshape(bsz, sw, n_chunk, LANES)
    blk = (bsz, SSM_GROUP, n_chunk, LANES)
    gmap = lambda g: (0, g, 0, 0)
    pmap = lambda g: (g, 0, 0)
    y4 = pl.pallas_call(
        _ssm_kernel,
        grid=(n_groups,),
        in_specs=[
            pl.BlockSpec(blk, gmap),
            pl.BlockSpec((None,) + kmat.shape[1:], pmap),
            pl.BlockSpec((None,) + arev.shape[1:], pmap),
            pl.BlockSpec((None,) + brow.shape[1:], pmap),
            pl.BlockSpec((None,) + a1.shape[1:], pmap),
            pl.BlockSpec((None,) + ccol.shape[1:], pmap),
            pl.BlockSpec((None,) + dpow.shape[1:], pmap),
        ],
        out_specs=pl.BlockSpec(blk, gmap),
        out_shape=jax.ShapeDtypeStruct(u4.shape, F32),
        scratch_shapes=[pltpu.VMEM((SSM_GROUP * LANES, SSM_GROUP * LANES), BF16),
                        pltpu.VMEM((SSM_GROUP * LANES, 2 * SSM_STATE), BF16),
                        pltpu.VMEM((2 * SSM_STATE, SSM_GROUP * LANES), BF16)],
        compiler_params=pltpu.CompilerParams(
            dimension_semantics=("parallel",), vmem_limit_bytes=VMEM_LIMIT),
        name="ssm",
    )(u4, kmat, arev, brow, a1, ccol, dpow)
    return y4


def _mid_kernel(x_ref, attn_ref, yt_ref, ut_ref, mod_ref, dsk_ref, wglut_ref, bglu_ref, wo1_ref, wo2_ref,
                g2_ref, wrt_ref, x1_ref, h2_ref, lg_ref):
    yt = jnp.concatenate([yt_ref[:, j, :] for j in range(yt_ref.shape[1])], axis=1)
    gt = _gelu_tanh(yt + dsk_ref[...] * ut_ref[...])
    zt = _dot(wglut_ref[...], gt.astype(BF16)) + bglu_ref[...]
    st = gt * jax.nn.sigmoid(zt)
    s = st.T.astype(BF16)
    mix = _dot(attn_ref[...], wo1_ref[...]) + _dot(s, wo2_ref[...])
    x1 = x_ref[...] + mod_ref[2:3, :] * mix
    ms = jnp.mean(x1 * x1, axis=-1, keepdims=True)
    h2 = x1 * lax.rsqrt(ms + NORM_EPS) * g2_ref[...]
    h2 = h2 * (1.0 + mod_ref[4:5, :]) + mod_ref[3:4, :]
    x1_ref[...] = x1
    h2_ref[0], h2_ref[1] = _pack_rows(h2)
    lg_ref[...] = jax.nn.sigmoid(_dot3_nt(wrt_ref[...], h2))


def _mid_call(x, attn, y4, ut, mod3, dsk, wglut, bglu, wo1, wo2, g2, wrt, tm=1024):
    bsz, seq, dm = x.shape
    aw = attn.shape[2]
    sw = y4.shape[1]
    ne = wrt.shape[0]
    row = lambda b, i: (b, i, 0)
    colm = lambda b, i: (b, 0, i)
    full = lambda a: pl.BlockSpec(a.shape, lambda b, i: (0,) * a.ndim)
    dsk = dsk.reshape(sw, 1)
    bglu = bglu.reshape(sw, 1)
    g2 = g2.reshape(1, dm)
    return pl.pallas_call(
        _mid_kernel,
        grid=(bsz, seq // tm),
        in_specs=[
            pl.BlockSpec((None, tm, dm), row),
            pl.BlockSpec((None, tm, aw), row),
            pl.BlockSpec((None, sw, tm // LANES, LANES), lambda b, i: (b, 0, i, 0)),
            pl.BlockSpec((None, sw, tm), colm),
            pl.BlockSpec((None, N_ADA, dm), lambda b, i: (b, 0, 0)),
            full(dsk), full(wglut), full(bglu), full(wo1), full(wo2), full(g2), full(wrt),
        ],
        out_specs=[
            pl.BlockSpec((None, tm, dm), row),
            pl.BlockSpec((2, None, tm, dm // 4), lambda b, i: (0, b, i, 0)),
            pl.BlockSpec((None, ne, tm), colm),
        ],
        out_shape=[
            jax.ShapeDtypeStruct((bsz, seq, dm), F32),
            jax.ShapeDtypeStruct((2, bsz, seq, dm // 4), jnp.uint32),
            jax.ShapeDtypeStruct((bsz, ne, seq), F32),
        ],
        compiler_params=pltpu.CompilerParams(
            dimension_semantics=("parallel", "parallel"), vmem_limit_bytes=VMEM_LIMIT),
        name="mid",
    )(x, attn, y4, ut, mod3, dsk, wglut, bglu, wo1, wo2, g2, wrt)


def _moe_kernel(cs_ref, cnt_ref, nv_ref, xs_hbm, wg_ref, wu_ref, wd_ref, out_hbm, xbuf, obuf, isem, osem):
    step = pl.program_id(0)
    n_steps = pl.num_programs(0)
    per_step = wg_ref.shape[0]
    c_end = cs_ref[n_steps * per_step]
    n_slots, _, ch, q = xbuf.shape

    n_parts = MOE_COPY_PARTS
    pr = ch // n_parts
    row_queue = 1

    def pieces(c, slot, j, kind):
        rows = pl.ds(j * pr, pr)
        hbm_rows = pl.ds(c * ch + j * pr, pr)
        if kind == "fetch":
            return pltpu.make_async_copy(xs_hbm.at[:, hbm_rows, :], xbuf.at[slot, :, rows, :], isem.at[slot])
        return pltpu.make_async_copy(obuf.at[slot, :, rows, :], out_hbm.at[:, hbm_rows, :], osem.at[slot])

    def copy_op(c, slot, kind, op):
        for j in range(n_parts):
            @pl.when(nv_ref[c] > j * pr)
            def _(j=j):
                cp = pieces(c, slot, j, kind)
                cp.start(priority=row_queue) if op == "start" else cp.wait()

    look = n_slots - 2

    @pl.when(step == 0)
    def _():
        for c in range(look):
            @pl.when(c < c_end)
            def _():
                copy_op(c, c, "fetch", "start")

    def arrive(c, n):
        for i in range(n):
            copy_op(c + i, lax.rem(c + i, n_slots), "fetch", "wait")
        for i in range(n):
            ahead = c + look + i

            @pl.when(ahead < c_end)
            def _():
                copy_op(ahead, lax.rem(ahead, n_slots), "fetch", "start")
        for i in range(n):
            @pl.when(c + i >= n_slots)
            def _():
                copy_op(c + i - n_slots, lax.rem(c + i, n_slots), "drain", "wait")

    def compute(c, n, n_valid, sub):
        slots = [lax.rem(c + i, n_slots) for i in range(n)]
        valid = lax.broadcasted_iota(jnp.int32, (n * ch, q), 0) < n_valid
        gate = up = None
        for h in range(2):
            lo, hi = _unpack_plane(jnp.concatenate([xbuf[s, h] for s in slots], axis=0))
            lo = jnp.where(valid, lo, 0.0).astype(BF16)
            hi = jnp.where(valid, hi, 0.0).astype(BF16)
            for xpart, k0 in ((lo, h * q), (hi, (2 + h) * q)):
                g = _dot(xpart, wg_ref[sub, k0:k0 + q, :].astype(BF16))
                u = _dot(xpart, wu_ref[sub, k0:k0 + q, :].astype(BF16))
                gate = g if gate is None else gate + g
                up = u if up is None else up + u
        hb = _silu(gate) * up
        p0, p1 = _pack_rows(_dot(hb.astype(BF16), wd_ref[sub].astype(BF16)))
        for i, s in enumerate(slots):
            obuf[s, 0] = p0[i * ch:(i + 1) * ch]
            obuf[s, 1] = p1[i * ch:(i + 1) * ch]

    def depart(c, n):
        for i in range(n):
            copy_op(c + i, lax.rem(c + i, n_slots), "drain", "start")

    def expert(sub):
        e = step * per_step + sub
        c0, c1 = cs_ref[e], cs_ref[e + 1]

        @pl.when(c1 > c0)
        def _():
            n_pairs = (c1 - c0) // 2

            def pair(i, carry):
                c = c0 + 2 * i
                arrive(c, 2)
                compute(c, 2, cnt_ref[e] - (c - c0) * ch, sub)
                depart(c, 2)
                return carry

            lax.fori_loop(0, n_pairs, pair, 0)

            @pl.when(c0 + 2 * n_pairs < c1)
            def _():
                arrive(c1 - 1, 1)
                compute(c1 - 1, 1, cnt_ref[e] - (c1 - 1 - c0) * ch, sub)
                depart(c1 - 1, 1)

    for sub in range(per_step):
        expert(sub)

    @pl.when(step == n_steps - 1)
    def _():
        for back in range(n_slots, 0, -1):
            @pl.when(c_end >= back)
            def _():
                copy_op(c_end - back, lax.rem(c_end - back, n_slots), "drain", "wait")


def _moe_call(chunk_start, counts, chunk_valid, xs, w_gate, w_up, w_down):
    n_rows = xs.shape[1]
    n_exp, dm, de = w_gate.shape
    grid_spec = pltpu.PrefetchScalarGridSpec(
        num_scalar_prefetch=3,
        grid=(n_exp // MOE_EXPERTS_PER_STEP,),
        in_specs=[
            pl.BlockSpec(memory_space=pl.ANY),
            pl.BlockSpec((MOE_EXPERTS_PER_STEP, dm, de), lambda s, cs, cnt, nv: (s, 0, 0)),
            pl.BlockSpec((MOE_EXPERTS_PER_STEP, dm, de), lambda s, cs, cnt, nv: (s, 0, 0)),
            pl.BlockSpec((MOE_EXPERTS_PER_STEP, de, dm), lambda s, cs, cnt, nv: (s, 0, 0)),
        ],
        out_specs=pl.BlockSpec(memory_space=pl.ANY),
        scratch_shapes=[
            pltpu.VMEM((MOE_SLOTS, 2, MOE_BLOCK, dm // 4), jnp.uint32),
            pltpu.VMEM((MOE_SLOTS, 2, MOE_BLOCK, dm // 4), jnp.uint32),
            pltpu.SemaphoreType.DMA((MOE_SLOTS,)), pltpu.SemaphoreType.DMA((MOE_SLOTS,)),
        ],
    )
    return pl.pallas_call(
        _moe_kernel,
        grid_spec=grid_spec,
        out_shape=jax.ShapeDtypeStruct(xs.shape, jnp.uint32),
        compiler_params=pltpu.CompilerParams(
            dimension_semantics=("arbitrary",), vmem_limit_bytes=VMEM_LIMIT),
        name="moe",
    )(chunk_start, counts, chunk_valid, xs, w_gate, w_up, w_down)


def _final_kernel(x1_ref, h2_ref, r_ref, w_ref, mod_ref, g_ref, wsg_ref, wsu_ref, wsd_ref, o_ref):
    w = w_ref[...]
    groups = [None] * 4
    hparts = [None] * 4
    for h in range(2):
        hparts[h], hparts[2 + h] = _unpack_plane(h2_ref[h])
        for k in range(r_ref.shape[1]):
            lo, hi = _unpack_plane(r_ref[h, k])
            wk = w[:, k:k + 1]
            groups[h] = wk * lo if groups[h] is None else groups[h] + wk * lo
            groups[2 + h] = wk * hi if groups[2 + h] is None else groups[2 + h] + wk * hi
    routed = jnp.concatenate(groups, axis=1)
    hb = jnp.concatenate(hparts, axis=1).astype(BF16)
    sh = _silu(_dot(hb, wsg_ref[...])) * _dot(hb, wsu_ref[...])
    shared = _dot(sh.astype(BF16), wsd_ref[...])
    x = x1_ref[...] + mod_ref[5:6, :] * (routed + shared)
    ms = jnp.mean(x * x, axis=-1, keepdims=True)
    o_ref[...] = x * lax.rsqrt(ms + NORM_EPS) * g_ref[...]


def _final_call(x1, h2, rows, w, mod3, final_g, wsg, wsu, wsd, tm=256):
    bsz, seq, dm = x1.shape
    kk, q = rows.shape[1], rows.shape[3]
    nt = seq // tm
    row = lambda b, i: (b, i, 0)
    full = lambda a: pl.BlockSpec(a.shape, lambda b, i: (0,) * a.ndim)
    return pl.pallas_call(
        _final_kernel,
        grid=(bsz, nt),
        in_specs=[
            pl.BlockSpec((None, tm, dm), row),
            pl.BlockSpec((2, None, tm, q), lambda b, i: (0, b, i, 0)),
            pl.BlockSpec((2, kk, tm, q), lambda b, i: (0, 0, b * nt + i, 0)),
            pl.BlockSpec((None, tm, kk), row),
            pl.BlockSpec((None, N_ADA, dm), lambda b, i: (b, 0, 0)),
            pl.BlockSpec((1, dm), lambda b, i: (0, 0)),
            full(wsg), full(wsu), full(wsd),
        ],
        out_specs=pl.BlockSpec((None, tm, dm), row),
        out_shape=jax.ShapeDtypeStruct((bsz, seq, dm), F32),
        compiler_params=pltpu.CompilerParams(
            dimension_semantics=("parallel", "parallel"), vmem_limit_bytes=VMEM_LIMIT),
        name="final",
    )(x1, h2, rows, w, mod3, final_g.reshape(1, dm), wsg, wsu, wsd)


def _route_kernel(sc_ref, bias_ref, idx_ref, w_ref, rank_ref, cnt_ref, tri_sc, carry_sc):
    ne, tn = sc_ref.shape
    gsz = ne // N_EXPERT_GROUPS
    neg = -jnp.inf
    first = jnp.logical_and(pl.program_id(0) == 0, pl.program_id(1) == 0)

    @pl.when(first)
    def _():
        r = lax.broadcasted_iota(jnp.int32, (tn, tn), 0)
        c = lax.broadcasted_iota(jnp.int32, (tn, tn), 1)
        tri_sc[...] = jnp.where(r < c, 1.0, 0.0).astype(BF16)
        carry_sc[...] = jnp.zeros(carry_sc.shape, F32)

    s = sc_ref[...]
    sel = s + bias_ref[...]
    gs = []
    for g in range(N_EXPERT_GROUPS):
        blk = sel[g * gsz:(g + 1) * gsz]
        m1 = jnp.max(blk, axis=0, keepdims=True)
        eq = blk == m1
        n_eq = jnp.sum(jnp.where(eq, 1.0, 0.0), axis=0, keepdims=True)
        m2 = jnp.max(jnp.where(eq, neg, blk), axis=0, keepdims=True)
        gs.append(m1 + jnp.where(n_eq >= 2.0, m1, m2))
    gs = jnp.concatenate(gs, axis=0)
    gi = lax.broadcasted_iota(jnp.int32, gs.shape, 0)
    beaten = jnp.zeros(gs.shape, F32)
    for gp in range(N_EXPERT_GROUPS):
        other = gs[gp:gp + 1]
        wins = jnp.where(other > gs, 1.0, jnp.where(other == gs, jnp.where(gi > gp, 1.0, 0.0), 0.0))
        beaten = beaten + wins
    gadd = jnp.where(beaten < float(TOPK_GROUPS), 0.0, neg)
    cur = jnp.concatenate(
        [sel[g * gsz:(g + 1) * gsz] + gadd[g:g + 1] for g in range(N_EXPERT_GROUPS)], axis=0)

    eidx = lax.broadcasted_iota(jnp.int32, (ne, tn), 0).astype(F32)
    cur0 = cur
    idx_rows, s_rows = [], []
    for _ in range(TOP_K):
        m = jnp.max(cur, axis=0, keepdims=True)
        ik = jnp.min(jnp.where(cur == m, eidx, float(ne)), axis=0, keepdims=True)
        oh = eidx == ik
        s_rows.append(jnp.sum(jnp.where(oh, s, 0.0), axis=0, keepdims=True))
        cur = jnp.where(oh, neg, cur)
        idx_rows.append(ik)
    chosen = jnp.where(cur == neg, jnp.where(cur0 > neg, 1.0, 0.0), 0.0)
    sk = jnp.concatenate(s_rows, axis=0)
    w_ref[...] = sk / jnp.sum(sk, axis=0, keepdims=True) * ROUTED_SCALE
    idx_ref[...] = jnp.concatenate(idx_rows, axis=0).astype(jnp.int32)

    before = _dot(chosen.astype(BF16), tri_sc[...]) + carry_sc[...]
    rank_rows = [jnp.sum(jnp.where(eidx == ik, before, 0.0), axis=0, keepdims=True) for ik in idx_rows]
    rank_ref[...] = jnp.concatenate(rank_rows, axis=0).astype(jnp.int32)
    carry_sc[...] = carry_sc[...] + jnp.sum(chosen, axis=1, keepdims=True)
    cnt_ref[...] = carry_sc[...]


def _route_call(scores_t, router_bias, tn=512):
    bsz, ne, seq = scores_t.shape
    tok = lambda b, i: (b, 0, i)
    return pl.pallas_call(
        _route_kernel,
        grid=(bsz, seq // tn),
        in_specs=[
            pl.BlockSpec((None, ne, tn), tok),
            pl.BlockSpec((ne, 1), lambda b, i: (0, 0)),
        ],
        out_specs=[
            pl.BlockSpec((None, TOP_K, tn), tok),
            pl.BlockSpec((None, TOP_K, tn), tok),
            pl.BlockSpec((None, TOP_K, tn), tok),
            pl.BlockSpec((ne, 1), lambda b, i: (0, 0)),
        ],
        out_shape=[
            jax.ShapeDtypeStruct((bsz, TOP_K, seq), jnp.int32),
            jax.ShapeDtypeStruct((bsz, TOP_K, seq), F32),
            jax.ShapeDtypeStruct((bsz, TOP_K, seq), jnp.int32),
            jax.ShapeDtypeStruct((ne, 1), F32),
        ],
        scratch_shapes=[pltpu.VMEM((tn, tn), BF16), pltpu.VMEM((ne, 1), F32)],
        compiler_params=pltpu.CompilerParams(
            dimension_semantics=("arbitrary", "arbitrary"), vmem_limit_bytes=VMEM_LIMIT),
        name="route",
    )(scores_t, router_bias.reshape(ne, 1))


def _dispatch_tables(idx, rank, counts):
    ne = counts.shape[0]
    n_tok = idx.shape[0] * idx.shape[2]
    counts = counts.astype(jnp.int32)
    pcounts = (counts + MOE_BLOCK - 1) // MOE_BLOCK * MOE_BLOCK
    pstarts = jnp.concatenate([jnp.zeros((1,), jnp.int32), jnp.cumsum(pcounts).astype(jnp.int32)])
    n_blocks = -(-(n_tok * TOP_K + ne * (MOE_BLOCK - 1)) // MOE_BLOCK)
    dest = _dest_call(pstarts[:ne], idx, rank)
    chunk_row = jnp.arange(n_blocks, dtype=jnp.int32)[:, None] * MOE_BLOCK
    owner = jnp.logical_and(pstarts[None, :ne] <= chunk_row, chunk_row < pstarts[None, 1:])
    left = jnp.sum(jnp.where(owner, counts[None, :] - (chunk_row - pstarts[None, :ne]), 0), axis=1)
    chunk_valid = jnp.clip(left, 0, MOE_BLOCK).astype(jnp.int32)
    return dest, pstarts // MOE_BLOCK, counts, chunk_valid, n_blocks


def _dest_kernel(ps_ref, idx_ref, rank_ref, o_ref):
    idx = idx_ref[...]

    def body(e, acc):
        return jnp.where(idx == e, ps_ref[e], acc)

    o_ref[...] = rank_ref[...] + lax.fori_loop(0, ps_ref.shape[0], body, jnp.zeros(idx.shape, jnp.int32))


def _dest_call(pstarts, idx, rank, tn=2048):
    bsz, kk, seq = idx.shape
    tok = lambda b, i, ps: (b, 0, i)
    grid_spec = pltpu.PrefetchScalarGridSpec(
        num_scalar_prefetch=1,
        grid=(bsz, seq // tn),
        in_specs=[pl.BlockSpec((None, kk, tn), tok), pl.BlockSpec((None, kk, tn), tok)],
        out_specs=pl.BlockSpec((None, kk, tn), tok),
    )
    return pl.pallas_call(
        _dest_kernel,
        grid_spec=grid_spec,
        out_shape=jax.ShapeDtypeStruct(idx.shape, jnp.int32),
        compiler_params=pltpu.CompilerParams(dimension_semantics=("parallel", "parallel")),
        name="dest",
    )(pstarts, idx, rank)


def _scatter_rows(rows, dest, n_out):
    n_tok, width = rows.shape
    k_slots = dest.shape[0]
    win = LANES
    mesh = plsc.VectorSubcoreMesh(core_axis_name="c", subcore_axis_name="s")

    @pl.kernel(out_type=jax.ShapeDtypeStruct((n_out, width), rows.dtype), mesh=mesh, scratch_types=[])
    def scatter(rows_hbm, dest_hbm, out_hbm):
        def body(rows_vmem, idx_vmem):
            pltpu.sync_copy(rows_vmem, out_hbm.at[idx_vmem.at[0]])

        pltpu.emit_pipeline(
            body,
            grid=(n_tok // win, k_slots),
            in_specs=[
                pl.BlockSpec((win, width), lambda i, k: (i, 0)),
                pl.BlockSpec((1, win), lambda i, k: (k, i)),
            ],
            out_specs=[],
            core_axis_name=("c", "s"),
            dimension_semantics=(pltpu.PARALLEL, pltpu.ARBITRARY),
        )(rows_hbm, dest_hbm)

    return scatter(rows, dest)


def _gather_rows(rows, idx):
    n = idx.shape[0]
    width = rows.shape[1]
    win = LANES
    mesh = plsc.VectorSubcoreMesh(core_axis_name="c", subcore_axis_name="s")

    @pl.kernel(out_type=jax.ShapeDtypeStruct((n, width), rows.dtype), mesh=mesh, scratch_types=[])
    def gather(rows_hbm, idx_hbm, out_hbm):
        def body(idx_vmem, out_vmem):
            pltpu.sync_copy(rows_hbm.at[idx_vmem.at[0]], out_vmem)

        pltpu.emit_pipeline(
            body,
            grid=(n // win,),
            in_specs=[pl.BlockSpec((1, win), lambda i: (0, i))],
            out_specs=[pl.BlockSpec((win, width), lambda i: (i, 0))],
            core_axis_name=("c", "s"),
            dimension_semantics=(pltpu.PARALLEL,),
        )(idx_hbm, out_hbm)

    return gather(rows, idx.reshape(1, n))


def kernel(x, c, w_ada, b_ada, norm1_g, w_in, w_out, lambda_q1, lambda_k1, lambda_q2, lambda_k2, subln_g,
           ssm_a_re, ssm_a_im, ssm_log_step, ssm_b_re, ssm_b_im, ssm_c_re, ssm_c_im, ssm_d, w_glu, b_glu,
           norm2_g, w_router, router_bias, w_gate, w_up, w_down, ws_gate, ws_up, ws_down, final_g):
    bsz, seq, dm = x.shape
    n_tok = bsz * seq
    aw = N_ATTN_HEADS * V_HEAD_DIM

    mod3 = _mod_call(c, w_ada[0], b_ada[0]).reshape(bsz, N_ADA, dm)

    wqkv = w_in[0][:, :3 * aw].astype(BF16)
    wut = w_in[0][:, 3 * aw:].T.astype(BF16)
    q, k, v, ut = _inproj_call(x, mod3, norm1_g[0], wqkv, wut)

    lam = (jnp.exp(jnp.sum(lambda_q1[0] * lambda_k1[0])) - jnp.exp(jnp.sum(lambda_q2[0] * lambda_k2[0]))
           + LAM_INIT).reshape(1)
    attn = _attn_call(lam, q, k, v, subln_g[0])

    yt = _ssm_call(ut, *_ssm_operators(ssm_a_re[0], ssm_a_im[0], ssm_log_step[0], ssm_b_re[0], ssm_b_im[0],
                                       ssm_c_re[0], ssm_c_im[0]))

    x1, h2, scores_t = _mid_call(
        x, attn, yt, ut, mod3, ssm_d[0], w_glu[0].T.astype(BF16), b_glu[0],
        w_out[0][:aw].astype(BF16), w_out[0][aw:].astype(BF16), norm2_g[0], w_router[0].T)

    idx, w, rank, counts = _route_call(scores_t, router_bias[0])
    dest, chunk_start, counts, chunk_valid, n_blocks = _dispatch_tables(idx, rank, counts.reshape(-1))
    n_rows = n_blocks * MOE_BLOCK

    dest_k = jnp.swapaxes(dest, 0, 1).reshape(TOP_K, n_tok)
    dest_half = jnp.concatenate([dest_k, dest_k + n_rows], axis=1)
    xs = _scatter_rows(h2.reshape(2 * n_tok, dm // 4), dest_half, 2 * n_rows).reshape(2, n_rows, dm // 4)

    out = _moe_call(chunk_start, counts, chunk_valid, xs, w_gate[0], w_up[0], w_down[0])
    src = jnp.concatenate([dest_k.reshape(-1), (dest_k + n_rows).reshape(-1)])
    rows = _gather_rows(out.reshape(2 * n_rows, dm // 4), src)
    return _final_call(x1, h2, rows.reshape(2, TOP_K, n_tok, dm // 4), jnp.swapaxes(w, 1, 2), mod3, final_g,
                       ws_gate[0].astype(BF16), ws_up[0].astype(BF16), ws_down[0].astype(BF16))
```

```python
import functools
import math

import jax
import jax.numpy as jnp
from jax import lax
from jax.experimental import pallas as pl
from jax.experimental.pallas import tpu as pltpu
from jax.experimental.pallas import tpu_sc as plsc

F32 = jnp.float32
BF16 = jnp.bfloat16

N_ATTN_HEADS = 4
ATTN_HEAD_DIM = 64
V_HEAD_DIM = 128
SSM_GROUP = 16
N_SSM_GROUPS = 32
SSM_STATE = 64
N_EXPERTS = 256
TOP_K = 8
N_EXPERT_GROUPS = 8
TOPK_GROUPS = 4
ROUTED_SCALE = 2.5
NORM_EPS = 1e-6
SUBLN_EPS = 1e-5
N_ADA = 6
LAM_INIT = 0.8 - 0.6 * math.exp(-0.3 * 0)

LANES = 128
MOE_BLOCK = 256
MOE_SLOTS = 12
MOE_EXPERTS_PER_STEP = 2
MOE_COPY_PARTS = 4
NEG_BIG = -1e30
LOG2_E = math.log2(math.e)
VMEM_LIMIT = 48 * 1024 * 1024


def _split_bf16(a):
    hi = a.astype(BF16)
    lo = (a - hi.astype(F32)).astype(BF16)
    return hi, lo


def _dot(a, b):
    return jnp.dot(a, b, preferred_element_type=F32)


def _dot_nt(a, b):
    return lax.dot_general(a, b, (((1,), (1,)), ((), ())), preferred_element_type=F32)


def _dot3(a, b):
    ah, al = _split_bf16(a)
    bh, bl = _split_bf16(b)
    return _dot(ah, bh) + _dot(ah, bl) + _dot(al, bh)


def _dot3_nt(a, b):
    ah, al = _split_bf16(a)
    bh, bl = _split_bf16(b)
    return _dot_nt(ah, bh) + _dot_nt(ah, bl) + _dot_nt(al, bh)


def _silu(x):
    return x * jax.nn.sigmoid(x)


def _gelu_tanh(x):
    c = math.sqrt(2.0 / math.pi)
    return 0.5 * x * (1.0 + jnp.tanh(c * (x + 0.044715 * (x * x * x))))


def _pack_rows(x):
    bits = lax.bitcast_convert_type(x.astype(BF16).astype(F32), jnp.uint32)
    half = bits.shape[1] // 2
    packed = (bits[:, :half] >> 16) | (bits[:, half:] & jnp.uint32(0xFFFF0000))
    return packed[:, :half // 2], packed[:, half // 2:]


def _unpack_plane(xu):
    return (lax.bitcast_convert_type(xu << 16, F32),
            lax.bitcast_convert_type(xu & jnp.uint32(0xFFFF0000), F32))


def _mod_kernel(c_ref, w_ref, b_ref, o_ref):
    cond = _silu(c_ref[...])
    o_ref[...] = _dot3(cond, w_ref[...]) + b_ref[...]


def _mod_call(c, w_ada, b_ada):
    bsz, dm = c.shape
    n_out = w_ada.shape[1]
    tn = 1024
    return pl.pallas_call(
        _mod_kernel,
        grid=(n_out // tn,),
        in_specs=[
            pl.BlockSpec((bsz, dm), lambda j: (0, 0)),
            pl.BlockSpec((dm, tn), lambda j: (0, j)),
            pl.BlockSpec((1, tn), lambda j: (0, j)),
        ],
        out_specs=pl.BlockSpec((bsz, tn), lambda j: (0, j)),
        out_shape=jax.ShapeDtypeStruct((bsz, n_out), F32),
        compiler_params=pltpu.CompilerParams(vmem_limit_bytes=VMEM_LIMIT),
        name="mod",
    )(c, w_ada, b_ada.reshape(1, n_out))


def _inproj_kernel(x_ref, mod_ref, g_ref, wqkv_ref, wut_ref, q_ref, k_ref, v_ref, ut_ref, *, aw):
    x = x_ref[...]
    ms = jnp.mean(x * x, axis=-1, keepdims=True)
    h = x * lax.rsqrt(ms + NORM_EPS) * g_ref[...]
    h = h * (1.0 + mod_ref[1:2, :]) + mod_ref[0:1, :]
    hb = h.astype(BF16)
    qkv = _dot(hb, wqkv_ref[...])
    q_ref[...] = (qkv[:, :aw] * (LOG2_E * ATTN_HEAD_DIM ** -0.5)).astype(BF16)
    k_ref[...] = qkv[:, aw:2 * aw].astype(BF16)
    v_ref[...] = qkv[:, 2 * aw:].astype(BF16)
    ut_ref[...] = _dot_nt(wut_ref[...], hb)


def _inproj_call(x, mod3, norm_g, wqkv, wut, tm=512):
    bsz, seq, dm = x.shape
    aw = wqkv.shape[1] // 3
    sw = wut.shape[0]
    row = lambda b, i: (b, i, 0)
    return pl.pallas_call(
        functools.partial(_inproj_kernel, aw=aw),
        grid=(bsz, seq // tm),
        in_specs=[
            pl.BlockSpec((None, tm, dm), row),
            pl.BlockSpec((None, N_ADA, dm), lambda b, i: (b, 0, 0)),
            pl.BlockSpec((1, dm), lambda b, i: (0, 0)),
            pl.BlockSpec(wqkv.shape, lambda b, i: (0, 0)),
            pl.BlockSpec(wut.shape, lambda b, i: (0, 0)),
        ],
        out_specs=[
            pl.BlockSpec((None, tm, aw), row),
            pl.BlockSpec((None, tm, aw), row),
            pl.BlockSpec((None, tm, aw), row),
            pl.BlockSpec((None, sw, tm), lambda b, i: (b, 0, i)),
        ],
        out_shape=[
            jax.ShapeDtypeStruct((bsz, seq, aw), BF16),
            jax.ShapeDtypeStruct((bsz, seq, aw), BF16),
            jax.ShapeDtypeStruct((bsz, seq, aw), BF16),
            jax.ShapeDtypeStruct((bsz, sw, seq), F32),
        ],
        compiler_params=pltpu.CompilerParams(
            dimension_semantics=("parallel", "parallel"), vmem_limit_bytes=VMEM_LIMIT),
        name="inproj",
    )(x, mod3, norm_g.reshape(1, dm), wqkv, wut)


def _attn_kernel(lam_ref, q_ref, k_ref, v_ref, g_ref, o_ref, m_sc, l_sc, acc_sc, *, tq):
    i = pl.program_id(2)
    th = tq // 2
    q = q_ref[...]
    lane = lax.broadcasted_iota(jnp.int32, q.shape, 1)
    zero = jnp.zeros_like(q)
    qa = jnp.where(lane < ATTN_HEAD_DIM, q, zero)
    qb = jnp.where(lane >= ATTN_HEAD_DIM, q, zero)
    q2 = jnp.concatenate([qa[:th], qb[:th], qa[th:], qb[th:]], axis=0)
    m_sc[...] = jnp.full(m_sc.shape, NEG_BIG, F32)
    l_sc[...] = jnp.zeros(l_sc.shape, F32)
    acc_sc[...] = jnp.zeros(acc_sc.shape, F32)

    def step(j, r0, nr, nc, qpos0):
        start = pl.multiple_of(j * tq, tq)
        kt = k_ref[pl.ds(start, nc), :]
        vt = v_ref[pl.ds(start, nc), :]
        s = _dot_nt(q2[r0:r0 + nr], kt)
        if qpos0 is not None:
            r = lax.broadcasted_iota(jnp.int32, s.shape, 0)
            c = lax.broadcasted_iota(jnp.int32, s.shape, 1)
            s = jnp.where(c <= qpos0 + (r & (th - 1)), s, NEG_BIG)
        m_prev = m_sc[r0:r0 + nr, :]
        m_new = jnp.maximum(m_prev, jnp.max(s, axis=-1, keepdims=True))
        alpha = jnp.exp2(m_prev - m_new)
        p = jnp.exp2(s - jnp.concatenate([m_new] * (nc // LANES), axis=1))
        psum = p[:, :LANES]
        for c0 in range(LANES, nc, LANES):
            psum = psum + p[:, c0:c0 + LANES]
        l_sc[r0:r0 + nr, :] = alpha * l_sc[r0:r0 + nr, :] + psum
        acc_sc[r0:r0 + nr, :] = alpha * acc_sc[r0:r0 + nr, :] + _dot(p.astype(BF16), vt)
        m_sc[r0:r0 + nr, :] = m_new

    def body(jj, carry):
        step(2 * jj, 0, 2 * tq, tq, None)
        step(2 * jj + 1, 0, 2 * tq, tq, None)
        return carry

    lax.fori_loop(0, i // 2, body, 0)

    @pl.when(i % 2 == 1)
    def _():
        step(i - 1, 0, 2 * tq, tq, None)

    step(i, 0, tq, th, 0)
    step(i, tq, tq, tq, th)

    o_all = acc_sc[...] / jnp.sum(l_sc[...], axis=-1, keepdims=True)
    o0 = jnp.concatenate([o_all[:th], o_all[tq:tq + th]], axis=0)
    o1 = jnp.concatenate([o_all[th:tq], o_all[tq + th:]], axis=0)
    o = o0 - lam_ref[0] * o1
    ms = jnp.mean(o * o, axis=-1, keepdims=True)
    o = o * lax.rsqrt(ms + SUBLN_EPS) * g_ref[...] * (1.0 - LAM_INIT)
    o_ref[...] = o.astype(o_ref.dtype)


def _attn_call(lam, q, k, v, subln_g, tq=1024):
    bsz, seq, aw = q.shape
    nh = aw // V_HEAD_DIM
    qmap = lambda b, h, i: (b, i, h)
    kvmap = lambda b, h, i: (b, 0, h)
    return pl.pallas_call(
        functools.partial(_attn_kernel, tq=tq),
        grid=(bsz, nh, seq // tq),
        in_specs=[
            pl.BlockSpec(memory_space=pltpu.SMEM),
            pl.BlockSpec((None, tq, V_HEAD_DIM), qmap),
            pl.BlockSpec((None, seq, V_HEAD_DIM), kvmap),
            pl.BlockSpec((None, seq, V_HEAD_DIM), kvmap),
            pl.BlockSpec((1, V_HEAD_DIM), lambda b, h, i: (0, 0)),
        ],
        out_specs=pl.BlockSpec((None, tq, V_HEAD_DIM), qmap),
        out_shape=jax.ShapeDtypeStruct((bsz, seq, aw), BF16),
        scratch_shapes=[pltpu.VMEM((2 * tq, V_HEAD_DIM), F32)] * 3,
        compiler_params=pltpu.CompilerParams(
            dimension_semantics=("parallel", "parallel", "parallel"), vmem_limit_bytes=VMEM_LIMIT),
        name="attn",
    )(lam, q, k, v, subln_g.reshape(1, V_HEAD_DIM))


def _ssm_operators(a_re, a_im, log_step, b_re, b_im, c_re, c_im):
    t = LANES
    hi = lax.Precision.HIGHEST
    lam = lax.complex(jnp.minimum(a_re, -1e-4), a_im)
    delta = jnp.exp(log_step)[:, None]
    lam_bar = jnp.exp(lam * delta)
    bbar = ((lam_bar - 1.0) / lam)[:, :, None] * lax.complex(b_re, b_im)
    cmat = lax.complex(c_re, c_im)
    ld = lam * delta
    tau = jnp.arange(t + 1, dtype=F32)
    pw = jnp.exp(ld[:, :, None] * tau)
    cb = cmat[:, None, :, :] * jnp.swapaxes(bbar, 1, 2)[:, :, None, :]
    g = a_re.shape[0]
    cb = cb.reshape(g, SSM_GROUP * SSM_GROUP, SSM_STATE)
    cbcat = jnp.concatenate([cb.real, -cb.imag], axis=-1)
    pcat = jnp.concatenate([pw.real[:, :, :t], pw.imag[:, :, :t]], axis=1)
    kmat = jnp.einsum('gxp,gpt->gxt', cbcat, pcat, precision=hi)
    prev = jnp.swapaxes(pw[:, :, t - 1::-1], 1, 2)
    arev = jnp.concatenate([prev.real, prev.imag], axis=-1)
    bt = jnp.swapaxes(bbar, 1, 2)
    brow = jnp.stack([jnp.concatenate([bt.real, bt.real], axis=-1),
                      jnp.concatenate([-bt.imag, bt.imag], axis=-1)], axis=2).reshape(g, 2 * SSM_GROUP, 2 * SSM_STATE)
    a1 = jnp.concatenate([pw.real[:, :, 1:], pw.imag[:, :, 1:]], axis=1)
    ct = jnp.swapaxes(cmat, 1, 2)
    ccol = jnp.stack([jnp.concatenate([ct.real, -ct.real], axis=1),
                      jnp.concatenate([-ct.imag, -ct.imag], axis=1)], axis=-1).reshape(g, 2 * SSM_STATE, 2 * SSM_GROUP)
    rows = []
    for i in range(6):
        d = jnp.exp(ld * float(t * (1 << i)))
        rows.append(jnp.concatenate([d.real, d.real], axis=-1))
        rows.append(jnp.concatenate([-d.imag, d.imag], axis=-1))
    rows += [jnp.zeros_like(rows[0])] * 4
    dpow = jnp.stack(rows, axis=1)
    return kmat, arev, brow, a1, ccol, dpow


def _ssm_kernel(u_ref, k_ref, arev_ref, brow_ref, a1_ref, ccol_ref, dp_ref, y_ref, m_sc, ws_sc, wc_sc):
    bsz, nch, n_chunk, t = u_ref.shape
    row = lax.broadcasted_iota(jnp.int32, (t, t), 0)
    col = lax.broadcasted_iota(jnp.int32, (t, t), 1)
    causal = col >= row
    half = arev_ref.shape[1] // 2

    arev = arev_ref[...]
    arev_sw = pltpu.roll(arev, half, 1)
    a1 = a1_ref[...]
    a1_sw = pltpu.roll(a1, half, 0)
    for c in range(nch):
        ws_sc[c * t:(c + 1) * t, :] = (arev * brow_ref[2 * c:2 * c + 1, :]
                                       + arev_sw * brow_ref[2 * c + 1:2 * c + 2, :]).astype(BF16)
        wc_sc[:, c * t:(c + 1) * t] = (a1 * ccol_ref[:, 2 * c:2 * c + 1]
                                       + a1_sw * ccol_ref[:, 2 * c + 1:2 * c + 2]).astype(BF16)

    uflat = jnp.concatenate(
        [jnp.concatenate([u_ref[b, ci] for b in range(bsz)], axis=0) for ci in range(nch)],
        axis=1).astype(BF16)

    y = None
    for c0 in range(0, nch, 2):
        for ci in (c0, c0 + 1):
            for co in range(nch):
                kb = jnp.broadcast_to(k_ref[ci * nch + co:ci * nch + co + 1, :], (t, t))
                kb = pltpu.roll(kb, 0, 1, stride=1, stride_axis=0)
                m_sc[ci * t:(ci + 1) * t, co * t:(co + 1) * t] = jnp.where(causal, kb, 0.0).astype(BF16)
        part = _dot(uflat[:, c0 * t:(c0 + 2) * t], m_sc[c0 * t:(c0 + 2) * t, :])
        y = part if y is None else y + part
    z = _dot(uflat, ws_sc[...])
    kidx = lax.broadcasted_iota(jnp.int32, z.shape, 0) & (n_chunk - 1)
    shift = 1
    i = 0
    while shift < n_chunk:
        zs = jnp.where(kidx >= shift, pltpu.roll(z, shift, 0), 0.0)
        z = z + zs * dp_ref[2 * i:2 * i + 1, :] + pltpu.roll(zs, half, 1) * dp_ref[2 * i + 1:2 * i + 2, :]
        shift *= 2
        i += 1
    xin = jnp.where(kidx >= 1, pltpu.roll(z, 1, 0), 0.0)
    xh, xl = _split_bf16(xin)
    wc = wc_sc[...]
    y = y + _dot(xh, wc) + _dot(xl, wc)
    for b in range(bsz):
        for co in range(nch):
            y_ref[b, co] = y[b * n_chunk:(b + 1) * n_chunk, co * t:(co + 1) * t]


def _ssm_call(ut, kmat, arev, brow, a1, ccol, dpow):
    bsz, sw, seq = ut.shape
    n_groups = sw // SSM_GROUP
    n_chunk = seq // LANES
    assert n_chunk & (n_chunk - 1) == 0 and n_chunk <= 64
    u4 = ut.reshape(bsz, sw, n_chunk, LANES)
    blk = (bsz, SSM_GROUP, n_chunk, LANES)
    gmap = lambda g: (0, g, 0, 0)
    pmap = lambda g: (g, 0, 0)
    y4 = pl.pallas_call(
        _ssm_kernel,
        grid=(n_groups,),
        in_specs=[
            pl.BlockSpec(blk, gmap),
            pl.BlockSpec((None,) + kmat.shape[1:], pmap),
            pl.BlockSpec((None,) + arev.shape[1:], pmap),
            pl.BlockSpec((None,) + brow.shape[1:], pmap),
            pl.BlockSpec((None,) + a1.shape[1:], pmap),
            pl.BlockSpec((None,) + ccol.shape[1:], pmap),
            pl.BlockSpec((None,) + dpow.shape[1:], pmap),
        ],
        out_specs=pl.BlockSpec(blk, gmap),
        out_shape=jax.ShapeDtypeStruct(u4.shape, F32),
        scratch_shapes=[pltpu.VMEM((SSM_GROUP * LANES, SSM_GROUP * LANES), BF16),
                        pltpu.VMEM((SSM_GROUP * LANES, 2 * SSM_STATE), BF16),
                        pltpu.VMEM((2 * SSM_STATE, SSM_GROUP * LANES), BF16)],
        compiler_params=pltpu.CompilerParams(
            dimension_semantics=("parallel",), vmem_limit_bytes=VMEM_LIMIT),
        name="ssm",
    )(u4, kmat, arev, brow, a1, ccol, dpow)
    return y4


def _mid_kernel(x_ref, attn_ref, yt_ref, ut_ref, mod_ref, dsk_ref, wglut_ref, bglu_ref, wo1_ref, wo2_ref,
                g2_ref, wrt_ref, x1_ref, h2_ref, lg_ref):
    yt = jnp.concatenate([yt_ref[:, j, :] for j in range(yt_ref.shape[1])], axis=1)
    gt = _gelu_tanh(yt + dsk_ref[...] * ut_ref[...])
    zt = _dot(wglut_ref[...], gt.astype(BF16)) + bglu_ref[...]
    st = gt * jax.nn.sigmoid(zt)
    s = st.T.astype(BF16)
    mix = _dot(attn_ref[...], wo1_ref[...]) + _dot(s, wo2_ref[...])
    x1 = x_ref[...] + mod_ref[2:3, :] * mix
    ms = jnp.mean(x1 * x1, axis=-1, keepdims=True)
    h2 = x1 * lax.rsqrt(ms + NORM_EPS) * g2_ref[...]
    h2 = h2 * (1.0 + mod_ref[4:5, :]) + mod_ref[3:4, :]
    x1_ref[...] = x1
    h2_ref[0], h2_ref[1] = _pack_rows(h2)
    lg_ref[...] = jax.nn.sigmoid(_dot3_nt(wrt_ref[...], h2))


def _mid_call(x, attn, y4, ut, mod3, dsk, wglut, bglu, wo1, wo2, g2, wrt, tm=1024):
    bsz, seq, dm = x.shape
    aw = attn.shape[2]
    sw = y4.shape[1]
    ne = wrt.shape[0]
    row = lambda b, i: (b, i, 0)
    colm = lambda b, i: (b, 0, i)
    full = lambda a: pl.BlockSpec(a.shape, lambda b, i: (0,) * a.ndim)
    dsk = dsk.reshape(sw, 1)
    bglu = bglu.reshape(sw, 1)
    g2 = g2.reshape(1, dm)
    return pl.pallas_call(
        _mid_kernel,
        grid=(bsz, seq // tm),
        in_specs=[
            pl.BlockSpec((None, tm, dm), row),
            pl.BlockSpec((None, tm, aw), row),
            pl.BlockSpec((None, sw, tm // LANES, LANES), lambda b, i: (b, 0, i, 0)),
            pl.BlockSpec((None, sw, tm), colm),
            pl.BlockSpec((None, N_ADA, dm), lambda b, i: (b, 0, 0)),
            full(dsk), full(wglut), full(bglu), full(wo1), full(wo2), full(g2), full(wrt),
        ],
        out_specs=[
            pl.BlockSpec((None, tm, dm), row),
            pl.BlockSpec((2, None, tm, dm // 4), lambda b, i: (0, b, i, 0)),
            pl.BlockSpec((None, ne, tm), colm),
        ],
        out_shape=[
            jax.ShapeDtypeStruct((bsz, seq, dm), F32),
            jax.ShapeDtypeStruct((2, bsz, seq, dm // 4), jnp.uint32),
            jax.ShapeDtypeStruct((bsz, ne, seq), F32),
        ],
        compiler_params=pltpu.CompilerParams(
            dimension_semantics=("parallel", "parallel"), vmem_limit_bytes=VMEM_LIMIT),
        name="mid",
    )(x, attn, y4, ut, mod3, dsk, wglut, bglu, wo1, wo2, g2, wrt)


def _moe_kernel(cs_ref, cnt_ref, nv_ref, xs_hbm, wg_ref, wu_ref, wd_ref, out_hbm, xbuf, obuf, isem, osem):
    step = pl.program_id(0)
    n_steps = pl.num_programs(0)
    per_step = wg_ref.shape[0]
    c_end = cs_ref[n_steps * per_step]
    n_slots, _, ch, q = xbuf.shape

    n_parts = MOE_COPY_PARTS
    pr = ch // n_parts
    row_queue = 1

    def pieces(c, slot, j, kind):
        rows = pl.ds(j * pr, pr)
        hbm_rows = pl.ds(c * ch + j * pr, pr)
        if kind == "fetch":
            return pltpu.make_async_copy(xs_hbm.at[:, hbm_rows, :], xbuf.at[slot, :, rows, :], isem.at[slot])
        return pltpu.make_async_copy(obuf.at[slot, :, rows, :], out_hbm.at[:, hbm_rows, :], osem.at[slot])

    def copy_op(c, slot, kind, op):
        for j in range(n_parts):
            @pl.when(nv_ref[c] > j * pr)
            def _(j=j):
                cp = pieces(c, slot, j, kind)
                cp.start(priority=row_queue) if op == "start" else cp.wait()

    look = n_slots - 2

    @pl.when(step == 0)
    def _():
        for c in range(look):
            @pl.when(c < c_end)
            def _():
                copy_op(c, c, "fetch", "start")

    def arrive(c, n):
        for i in range(n):
            copy_op(c + i, lax.rem(c + i, n_slots), "fetch", "wait")
        for i in range(n):
            ahead = c + look + i

            @pl.when(ahead < c_end)
            def _():
                copy_op(ahead, lax.rem(ahead, n_slots), "fetch", "start")
        for i in range(n):
            @pl.when(c + i >= n_slots)
            def _():
                copy_op(c + i - n_slots, lax.rem(c + i, n_slots), "drain", "wait")

    def compute(c, n, n_valid, sub):
        slots = [lax.rem(c + i, n_slots) for i in range(n)]
        valid = lax.broadcasted_iota(jnp.int32, (n * ch, q), 0) < n_valid
        gate = up = None
        for h in range(2):
            lo, hi = _unpack_plane(jnp.concatenate([xbuf[s, h] for s in slots], axis=0))
            lo = jnp.where(valid, lo, 0.0).astype(BF16)
            hi = jnp.where(valid, hi, 0.0).astype(BF16)
            for xpart, k0 in ((lo, h * q), (hi, (2 + h) * q)):
                g = _dot(xpart, wg_ref[sub, k0:k0 + q, :].astype(BF16))
                u = _dot(xpart, wu_ref[sub, k0:k0 + q, :].astype(BF16))
                gate = g if gate is None else gate + g
                up = u if up is None else up + u
        hb = _silu(gate) * up
        p0, p1 = _pack_rows(_dot(hb.astype(BF16), wd_ref[sub].astype(BF16)))
        for i, s in enumerate(slots):
            obuf[s, 0] = p0[i * ch:(i + 1) * ch]
            obuf[s, 1] = p1[i * ch:(i + 1) * ch]

    def depart(c, n):
        for i in range(n):
            copy_op(c + i, lax.rem(c + i, n_slots), "drain", "start")

    def expert(sub):
        e = step * per_step + sub
        c0, c1 = cs_ref[e], cs_ref[e + 1]

        @pl.when(c1 > c0)
        def _():
            n_pairs = (c1 - c0) // 2

            def pair(i, carry):
                c = c0 + 2 * i
                arrive(c, 2)
                compute(c, 2, cnt_ref[e] - (c - c0) * ch, sub)
                depart(c, 2)
                return carry

            lax.fori_loop(0, n_pairs, pair, 0)

            @pl.when(c0 + 2 * n_pairs < c1)
            def _():
                arrive(c1 - 1, 1)
                compute(c1 - 1, 1, cnt_ref[e] - (c1 - 1 - c0) * ch, sub)
                depart(c1 - 1, 1)

    for sub in range(per_step):
        expert(sub)

    @pl.when(step == n_steps - 1)
    def _():
        for back in range(n_slots, 0, -1):
            @pl.when(c_end >= back)
            def _():
                copy_op(c_end - back, lax.rem(c_end - back, n_slots), "drain", "wait")


def _moe_call(chunk_start, counts, chunk_valid, xs, w_gate, w_up, w_down):
    n_rows = xs.shape[1]
    n_exp, dm, de = w_gate.shape
    grid_spec = pltpu.PrefetchScalarGridSpec(
        num_scalar_prefetch=3,
        grid=(n_exp // MOE_EXPERTS_PER_STEP,),
        in_specs=[
            pl.BlockSpec(memory_space=pl.ANY),
            pl.BlockSpec((MOE_EXPERTS_PER_STEP, dm, de), lambda s, cs, cnt, nv: (s, 0, 0)),
            pl.BlockSpec((MOE_EXPERTS_PER_STEP, dm, de), lambda s, cs, cnt, nv: (s, 0, 0)),
            pl.BlockSpec((MOE_EXPERTS_PER_STEP, de, dm), lambda s, cs, cnt, nv: (s, 0, 0)),
        ],
        out_specs=pl.BlockSpec(memory_space=pl.ANY),
        scratch_shapes=[
            pltpu.VMEM((MOE_SLOTS, 2, MOE_BLOCK, dm // 4), jnp.uint32),
            pltpu.VMEM((MOE_SLOTS, 2, MOE_BLOCK, dm // 4), jnp.uint32),
            pltpu.SemaphoreType.DMA((MOE_SLOTS,)), pltpu.SemaphoreType.DMA((MOE_SLOTS,)),
        ],
    )
    return pl.pallas_call(
        _moe_kernel,
        grid_spec=grid_spec,
        out_shape=jax.ShapeDtypeStruct(xs.shape, jnp.uint32),
        compiler_params=pltpu.CompilerParams(
            dimension_semantics=("arbitrary",), vmem_limit_bytes=VMEM_LIMIT),
        name="moe",
    )(chunk_start, counts, chunk_valid, xs, w_gate, w_up, w_down)


def _final_kernel(x1_ref, h2_ref, r_ref, w_ref, mod_ref, g_ref, wsg_ref, wsu_ref, wsd_ref, o_ref):
    w = w_ref[...]
    groups = [None] * 4
    hparts = [None] * 4
    for h in range(2):
        hparts[h], hparts[2 + h] = _unpack_plane(h2_ref[h])
        for k in range(r_ref.shape[1]):
            lo, hi = _unpack_plane(r_ref[h, k])
            wk = w[:, k:k + 1]
            groups[h] = wk * lo if groups[h] is None else groups[h] + wk * lo
            groups[2 + h] = wk * hi if groups[2 + h] is None else groups[2 + h] + wk * hi
    routed = jnp.concatenate(groups, axis=1)
    hb = jnp.concatenate(hparts, axis=1).astype(BF16)
    sh = _silu(_dot(hb, wsg_ref[...])) * _dot(hb, wsu_ref[...])
    shared = _dot(sh.astype(BF16), wsd_ref[...])
    x = x1_ref[...] + mod_ref[5:6, :] * (routed + shared)
    ms = jnp.mean(x * x, axis=-1, keepdims=True)
    o_ref[...] = x * lax.rsqrt(ms + NORM_EPS) * g_ref[...]


def _final_call(x1, h2, rows, w, mod3, final_g, wsg, wsu, wsd, tm=512):
    bsz, seq, dm = x1.shape
    kk, q = rows.shape[1], rows.shape[3]
    nt = seq // tm
    row = lambda b, i: (b, i, 0)
    full = lambda a: pl.BlockSpec(a.shape, lambda b, i: (0,) * a.ndim)
    return pl.pallas_call(
        _final_kernel,
        grid=(bsz, nt),
        in_specs=[
            pl.BlockSpec((None, tm, dm), row),
            pl.BlockSpec((2, None, tm, q), lambda b, i: (0, b, i, 0)),
            pl.BlockSpec((2, kk, tm, q), lambda b, i: (0, 0, b * nt + i, 0)),
            pl.BlockSpec((None, tm, kk), row),
            pl.BlockSpec((None, N_ADA, dm), lambda b, i: (b, 0, 0)),
            pl.BlockSpec((1, dm), lambda b, i: (0, 0)),
            full(wsg), full(wsu), full(wsd),
        ],
        out_specs=pl.BlockSpec((None, tm, dm), row),
        out_shape=jax.ShapeDtypeStruct((bsz, seq, dm), F32),
        compiler_params=pltpu.CompilerParams(
            dimension_semantics=("parallel", "parallel"), vmem_limit_bytes=VMEM_LIMIT),
        name="final",
    )(x1, h2, rows, w, mod3, final_g.reshape(1, dm), wsg, wsu, wsd)


def _route_kernel(sc_ref, bias_ref, idx_ref, w_ref, rank_ref, cnt_ref, tri_sc, carry_sc):
    ne, tn = sc_ref.shape
    gsz = ne // N_EXPERT_GROUPS
    neg = -jnp.inf
    first = jnp.logical_and(pl.program_id(0) == 0, pl.program_id(1) == 0)

    @pl.when(first)
    def _():
        r = lax.broadcasted_iota(jnp.int32, (tn, tn), 0)
        c = lax.broadcasted_iota(jnp.int32, (tn, tn), 1)
        tri_sc[...] = jnp.where(r < c, 1.0, 0.0).astype(BF16)
        carry_sc[...] = jnp.zeros(carry_sc.shape, F32)

    s = sc_ref[...]
    sel = s + bias_ref[...]
    gs = []
    for g in range(N_EXPERT_GROUPS):
        blk = sel[g * gsz:(g + 1) * gsz]
        m1 = jnp.max(blk, axis=0, keepdims=True)
        eq = blk == m1
        n_eq = jnp.sum(jnp.where(eq, 1.0, 0.0), axis=0, keepdims=True)
        m2 = jnp.max(jnp.where(eq, neg, blk), axis=0, keepdims=True)
        gs.append(m1 + jnp.where(n_eq >= 2.0, m1, m2))
    gs = jnp.concatenate(gs, axis=0)
    gi = lax.broadcasted_iota(jnp.int32, gs.shape, 0)
    beaten = jnp.zeros(gs.shape, F32)
    for gp in range(N_EXPERT_GROUPS):
        other = gs[gp:gp + 1]
        wins = jnp.where(other > gs, 1.0, jnp.where(other == gs, jnp.where(gi > gp, 1.0, 0.0), 0.0))
        beaten = beaten + wins
    gadd = jnp.where(beaten < float(TOPK_GROUPS), 0.0, neg)
    cur = jnp.concatenate(
        [sel[g * gsz:(g + 1) * gsz] + gadd[g:g + 1] for g in range(N_EXPERT_GROUPS)], axis=0)

    eidx = lax.broadcasted_iota(jnp.int32, (ne, tn), 0).astype(F32)
    cur0 = cur
    idx_rows, s_rows = [], []
    for _ in range(TOP_K):
        m = jnp.max(cur, axis=0, keepdims=True)
        ik = jnp.min(jnp.where(cur == m, eidx, float(ne)), axis=0, keepdims=True)
        oh = eidx == ik
        s_rows.append(jnp.sum(jnp.where(oh, s, 0.0), axis=0, keepdims=True))
        cur = jnp.where(oh, neg, cur)
        idx_rows.append(ik)
    chosen = jnp.where(cur == neg, jnp.where(cur0 > neg, 1.0, 0.0), 0.0)
    sk = jnp.concatenate(s_rows, axis=0)
    w_ref[...] = sk / jnp.sum(sk, axis=0, keepdims=True) * ROUTED_SCALE
    idx_ref[...] = jnp.concatenate(idx_rows, axis=0).astype(jnp.int32)

    before = _dot(chosen.astype(BF16), tri_sc[...]) + carry_sc[...]
    rank_rows = [jnp.sum(jnp.where(eidx == ik, before, 0.0), axis=0, keepdims=True) for ik in idx_rows]
    rank_ref[...] = jnp.concatenate(rank_rows, axis=0).astype(jnp.int32)
    carry_sc[...] = carry_sc[...] + jnp.sum(chosen, axis=1, keepdims=True)
    cnt_ref[...] = carry_sc[...]


def _route_call(scores_t, router_bias, tn=512):
    bsz, ne, seq = scores_t.shape
    tok = lambda b, i: (b, 0, i)
    return pl.pallas_call(
        _route_kernel,
        grid=(bsz, seq // tn),
        in_specs=[
            pl.BlockSpec((None, ne, tn), tok),
            pl.BlockSpec((ne, 1), lambda b, i: (0, 0)),
        ],
        out_specs=[
            pl.BlockSpec((None, TOP_K, tn), tok),
            pl.BlockSpec((None, TOP_K, tn), tok),
            pl.BlockSpec((None, TOP_K, tn), tok),
            pl.BlockSpec((ne, 1), lambda b, i: (0, 0)),
        ],
        out_shape=[
            jax.ShapeDtypeStruct((bsz, TOP_K, seq), jnp.int32),
            jax.ShapeDtypeStruct((bsz, TOP_K, seq), F32),
            jax.ShapeDtypeStruct((bsz, TOP_K, seq), jnp.int32),
            jax.ShapeDtypeStruct((ne, 1), F32),
        ],
        scratch_shapes=[pltpu.VMEM((tn, tn), BF16), pltpu.VMEM((ne, 1), F32)],
        compiler_params=pltpu.CompilerParams(
            dimension_semantics=("arbitrary", "arbitrary"), vmem_limit_bytes=VMEM_LIMIT),
        name="route",
    )(scores_t, router_bias.reshape(ne, 1))


def _dispatch_tables(idx, rank, counts):
    ne = counts.shape[0]
    n_tok = idx.shape[0] * idx.shape[2]
    counts = counts.astype(jnp.int32)
    pcounts = (counts + MOE_BLOCK - 1) // MOE_BLOCK * MOE_BLOCK
    pstarts = jnp.concatenate([jnp.zeros((1,), jnp.int32), jnp.cumsum(pcounts).astype(jnp.int32)])
    n_blocks = -(-(n_tok * TOP_K + ne * (MOE_BLOCK - 1)) // MOE_BLOCK)
    dest = _dest_call(pstarts[:ne], idx, rank)
    chunk_row = jnp.arange(n_blocks, dtype=jnp.int32)[:, None] * MOE_BLOCK
    owner = jnp.logical_and(pstarts[None, :ne] <= chunk_row, chunk_row < pstarts[None, 1:])
    left = jnp.sum(jnp.where(owner, counts[None, :] - (chunk_row - pstarts[None, :ne]), 0), axis=1)
    chunk_valid = jnp.clip(left, 0, MOE_BLOCK).astype(jnp.int32)
    return dest, pstarts // MOE_BLOCK, counts, chunk_valid, n_blocks


def _dest_kernel(ps_ref, idx_ref, rank_ref, o_ref):
    idx = idx_ref[...]

    def body(e, acc):
        return jnp.where(idx == e, ps_ref[e], acc)

    o_ref[...] = rank_ref[...] + lax.fori_loop(0, ps_ref.shape[0], body, jnp.zeros(idx.shape, jnp.int32))


def _dest_call(pstarts, idx, rank, tn=2048):
    bsz, kk, seq = idx.shape
    tok = lambda b, i, ps: (b, 0, i)
    grid_spec = pltpu.PrefetchScalarGridSpec(
        num_scalar_prefetch=1,
        grid=(bsz, seq // tn),
        in_specs=[pl.BlockSpec((None, kk, tn), tok), pl.BlockSpec((None, kk, tn), tok)],
        out_specs=pl.BlockSpec((None, kk, tn), tok),
    )
    return pl.pallas_call(
        _dest_kernel,
        grid_spec=grid_spec,
        out_shape=jax.ShapeDtypeStruct(idx.shape, jnp.int32),
        compiler_params=pltpu.CompilerParams(dimension_semantics=("parallel", "parallel")),
        name="dest",
    )(pstarts, idx, rank)


def _scatter_rows(rows, dest, n_out):
    n_tok, width = rows.shape
    k_slots = dest.shape[0]
    win = LANES
    mesh = plsc.VectorSubcoreMesh(core_axis_name="c", subcore_axis_name="s")

    @pl.kernel(out_type=jax.ShapeDtypeStruct((n_out, width), rows.dtype), mesh=mesh, scratch_types=[])
    def scatter(rows_hbm, dest_hbm, out_hbm):
        def body(rows_vmem, idx_vmem):
            pltpu.sync_copy(rows_vmem, out_hbm.at[idx_vmem.at[0]])

        pltpu.emit_pipeline(
            body,
            grid=(n_tok // win, k_slots),
            in_specs=[
                pl.BlockSpec((win, width), lambda i, k: (i, 0)),
                pl.BlockSpec((1, win), lambda i, k: (k, i)),
            ],
            out_specs=[],
            core_axis_name=("c", "s"),
            dimension_semantics=(pltpu.PARALLEL, pltpu.ARBITRARY),
        )(rows_hbm, dest_hbm)

    return scatter(rows, dest)


def _gather_rows(rows, idx):
    n = idx.shape[0]
    width = rows.shape[1]
    win = LANES
    mesh = plsc.VectorSubcoreMesh(core_axis_name="c", subcore_axis_name="s")

    @pl.kernel(out_type=jax.ShapeDtypeStruct((n, width), rows.dtype), mesh=mesh, scratch_types=[])
    def gather(rows_hbm, idx_hbm, out_hbm):
        def body(idx_vmem, out_vmem):
            pltpu.sync_copy(rows_hbm.at[idx_vmem.at[0]], out_vmem)

        pltpu.emit_pipeline(
            body,
            grid=(n // win,),
            in_specs=[pl.BlockSpec((1, win), lambda i: (0, i))],
            out_specs=[pl.BlockSpec((win, width), lambda i: (i, 0))],
            core_axis_name=("c", "s"),
            dimension_semantics=(pltpu.PARALLEL,),
        )(idx_hbm, out_hbm)

    return gather(rows, idx.reshape(1, n))


def kernel(x, c, w_ada, b_ada, norm1_g, w_in, w_out, lambda_q1, lambda_k1, lambda_q2, lambda_k2, subln_g,
           ssm_a_re, ssm_a_im, ssm_log_step, ssm_b_re, ssm_b_im, ssm_c_re, ssm_c_im, ssm_d, w_glu, b_glu,
           norm2_g, w_router, router_bias, w_gate, w_up, w_down, ws_gate, ws_up, ws_down, final_g):
    bsz, seq, dm = x.shape
    n_tok = bsz * seq
    aw = N_ATTN_HEADS * V_HEAD_DIM

    mod3 = _mod_call(c, w_ada[0], b_ada[0]).reshape(bsz, N_ADA, dm)

    wqkv = w_in[0][:, :3 * aw].astype(BF16)
    wut = w_in[0][:, 3 * aw:].T.astype(BF16)
    q, k, v, ut = _inproj_call(x, mod3, norm1_g[0], wqkv, wut)

    lam = (jnp.exp(jnp.sum(lambda_q1[0] * lambda_k1[0])) - jnp.exp(jnp.sum(lambda_q2[0] * lambda_k2[0]))
           + LAM_INIT).reshape(1)
    attn = _attn_call(lam, q, k, v, subln_g[0])

    yt = _ssm_call(ut, *_ssm_operators(ssm_a_re[0], ssm_a_im[0], ssm_log_step[0], ssm_b_re[0], ssm_b_im[0],
                                       ssm_c_re[0], ssm_c_im[0]))

    x1, h2, scores_t = _mid_call(
        x, attn, yt, ut, mod3, ssm_d[0], w_glu[0].T.astype(BF16), b_glu[0],
        w_out[0][:aw].astype(BF16), w_out[0][aw:].astype(BF16), norm2_g[0], w_router[0].T)

    idx, w, rank, counts = _route_call(scores_t, router_bias[0])
    dest, chunk_start, counts, chunk_valid, n_blocks = _dispatch_tables(idx, rank, counts.reshape(-1))
    n_rows = n_blocks * MOE_BLOCK

    dest_k = jnp.swapaxes(dest, 0, 1).reshape(TOP_K, n_tok)
    dest_half = jnp.concatenate([dest_k, dest_k + n_rows], axis=1)
    xs = _scatter_rows(h2.reshape(2 * n_tok, dm // 4), dest_half, 2 * n_rows).reshape(2, n_rows, dm // 4)

    out = _moe_call(chunk_start, counts, chunk_valid, xs, w_gate[0], w_up[0], w_down[0])
    src = jnp.concatenate([dest_k.reshape(-1), (dest_k + n_rows).reshape(-1)])
    rows = _gather_rows(out.reshape(2 * n_rows, dm // 4), src)
    return _final_call(x1, h2, rows.reshape(2, TOP_K, n_tok, dm // 4), jnp.swapaxes(w, 1, 2), mod3, final_g,
                       ws_gate[0].astype(BF16), ws_up[0].astype(BF16), ws_down[0].astype(BF16))
```
